```python
import math
import functools
import jax
import jax.numpy as jnp
from jax import lax
import numpy as np

D_MODEL = 2048
BATCH = 8
SEQ = 4096
DEPTH = 2
DEC_BATCH = 32
DEC_SEQ = 64
PAST_LEN = 1024

CHUNK = 64
Q_BLOCK = 128
HEAD_DIM = 128
N_BRANCH = 4
BRANCH_W = 512
H_MLA = 4
Q_LORA = 384
KV_LORA = 256
NOPE_DIM = 128
ROPE_DIM = 64
V_DIM = 128
ROPE_THETA = 10000.0
H_SB = 4
H_BAND = 4
N_PREV_CHUNKS = 8
BAND_CHUNKS = N_PREV_CHUNKS + 1
REL_CLIP = 128
H_DSA = 4
H_IDX = 16
D_IDX = 64
TOPK_MAX = 256
T5_BUCKETS = 32
T5_MAX_DIST = 128
D_FF = 5632
N_EXPERTS = 8
TOP_K_EXPERTS = 2
D_FF_EXPERT = 2816
N_DENSE = (DEPTH + 1) // 2
N_MOE = DEPTH // 2
PLE_DIM = 256
EPS = 1e-6
IN_SIZES = (Q_LORA, KV_LORA, ROPE_DIM,
            3 * H_SB * HEAD_DIM, 3 * H_BAND * HEAD_DIM, 3 * H_DSA * HEAD_DIM,
            H_IDX * D_IDX, D_IDX, H_IDX, N_BRANCH * D_MODEL)
IN_TOTAL = sum(IN_SIZES)

kernel_name = 'hybrid_chunk_streaming_encoder_step'


def rms_norm(x, g):
    xf = x.astype(jnp.float32)
    y = xf * lax.rsqrt(jnp.mean(xf * xf, axis=-1, keepdims=True) + EPS)
    return (y * g.astype(jnp.float32)).astype(x.dtype)


def rope(x, pos):
    half = x.shape[-1] // 2
    inv = ROPE_THETA ** (-jnp.arange(half, dtype=jnp.float32) / half)
    ang = pos.astype(jnp.float32)[:, None] * inv[None, :]
    cos = jnp.cos(ang)[None, :, None, :]
    sin = jnp.sin(ang)[None, :, None, :]
    xf = x.astype(jnp.float32)
    x1, x2 = xf[..., :half], xf[..., half:]
    return jnp.concatenate([x1 * cos - x2 * sin, x1 * sin + x2 * cos], axis=-1).astype(x.dtype)


def chunk_limit(q_pos):
    return (q_pos // CHUNK + 1) * CHUNK


def t5_bucket(rel):
    half = T5_BUCKETS // 2
    max_exact = half // 2
    n = jnp.abs(rel)
    nf = jnp.maximum(n, 1).astype(jnp.float32)
    large = max_exact + (jnp.log(nf / max_exact) / math.log(T5_MAX_DIST / max_exact)
                         * (half - max_exact)).astype(jnp.int32)
    large = jnp.minimum(large, half - 1)
    return jnp.where(rel > 0, half, 0) + jnp.where(n < max_exact, n, large)


def split_columns(a, sizes):
    out, start = [], 0
    for s in sizes:
        out.append(a[..., start:start + s])
        start += s
    return out


def split_heads3(a, n_heads):
    b, s, _ = a.shape
    a = a.reshape(b, s, 3, n_heads, HEAD_DIM)
    return a[:, :, 0], a[:, :, 1], a[:, :, 2]


def sweep_query_blocks(fn, qs, q_pos):
    q_len = q_pos.shape[0]
    if q_len <= Q_BLOCK or q_len % Q_BLOCK:
        return fn(qs, q_pos)
    nb = q_len // Q_BLOCK
    qs_b = tuple(jnp.moveaxis(a.reshape(a.shape[0], nb, Q_BLOCK, *a.shape[2:]), 1, 0) for a in qs)
    out = lax.map(lambda args: fn(args[0], args[1]), (qs_b, q_pos.reshape(nb, Q_BLOCK)))
    out = jnp.moveaxis(out, 0, 1)
    return out.reshape(out.shape[0], q_len, *out.shape[3:])


def mla_attention(q_nope, q_rope, c_kv_all, k_rope_all, w_ukv, q_pos, k_pos):
    b, l, _ = c_kv_all.shape
    kv = (c_kv_all @ w_ukv).reshape(b, l, H_MLA, NOPE_DIM + V_DIM)
    k_nope, v = kv[..., :NOPE_DIM], kv[..., NOPE_DIM:]
    scale = (NOPE_DIM + ROPE_DIM) ** -0.5

    def block(qs, qp):
        qn, qr = qs
        logits = (jnp.einsum('bqhd,bkhd->bhqk', qn, k_nope).astype(jnp.float32)
                  + jnp.einsum('bqhr,bkr->bhqk', qr, k_rope_all).astype(jnp.float32)) * scale
        mask = k_pos[None, :] < chunk_limit(qp)[:, None]
        p = jax.nn.softmax(jnp.where(mask, logits, -jnp.inf), axis=-1)
        return jnp.einsum('bhqk,bkhd->bqhd', p.astype(v.dtype), v)

    return sweep_query_blocks(block, (q_nope, q_rope), q_pos)


def stick_breaking_attention(q, k_all, v_all, q_pos, k_pos):
    scale = HEAD_DIM ** -0.5

    def block(qs, qp):
        (qb,) = qs
        z = jnp.einsum('bqhd,bkhd->bhqk', qb, k_all).astype(jnp.float32) * scale
        strict = k_pos[None, :] < qp[:, None]
        log_not = jnp.where(strict, jax.nn.log_sigmoid(-z), 0.0)
        after = lax.cumsum(log_not, axis=3, reverse=True) - log_not
        a = jnp.where(strict, jnp.exp(jax.nn.log_sigmoid(z) + after), 0.0)
        return jnp.einsum('bhqk,bkhd->bqhd', a.astype(v_all.dtype), v_all)

    return sweep_query_blocks(block, (q,), q_pos)


def band_attend(q, k, v, q_pos, k_pos, rel_bias):
    logits = jnp.einsum('bqhd,bkhd->bhqk', q, k).astype(jnp.float32) * HEAD_DIM ** -0.5
    rel = jnp.clip(q_pos[:, None] - k_pos[None, :], -REL_CLIP, REL_CLIP) + REL_CLIP
    logits = logits + jnp.transpose(rel_bias[rel], (2, 0, 1))[None].astype(jnp.float32)
    cq = (q_pos // CHUNK)[:, None]
    ck = (k_pos // CHUNK)[None, :]
    mask = (k_pos[None, :] >= 0) & (ck <= cq) & (ck >= cq - N_PREV_CHUNKS)
    p = jax.nn.softmax(jnp.where(mask, logits, -jnp.inf), axis=-1)
    return jnp.einsum('bhqk,bkhd->bqhd', p.astype(v.dtype), v)


def band_attention_prompt(q, k, v, rel_bias):
    b, s = q.shape[:2]
    pad = N_PREV_CHUNKS * CHUNK
    kp = jnp.pad(k, ((0, 0), (pad, 0), (0, 0), (0, 0)))
    vp = jnp.pad(v, ((0, 0), (pad, 0), (0, 0), (0, 0)))

    def one_chunk(c):
        start = c * CHUNK
        qc = lax.dynamic_slice_in_dim(q, start, CHUNK, axis=1)
        kc = lax.dynamic_slice_in_dim(kp, start, BAND_CHUNKS * CHUNK, axis=1)
        vc = lax.dynamic_slice_in_dim(vp, start, BAND_CHUNKS * CHUNK, axis=1)
        q_pos = start + jnp.arange(CHUNK)
        k_pos = start - pad + jnp.arange(BAND_CHUNKS * CHUNK)
        return band_attend(qc, kc, vc, q_pos, k_pos, rel_bias)

    out = lax.map(one_chunk, jnp.arange(s // CHUNK))
    return jnp.moveaxis(out, 0, 1).reshape(b, s, H_BAND, HEAD_DIM)


def dsa_attention(q, q_idx, w_idx, k_all, v_all, k_idx_all, q_pos, k_pos, t5_table):
    topk = min(TOPK_MAX, k_all.shape[1] // 4)
    w_eff = w_idx.astype(jnp.float32) * (H_IDX ** -0.5 * D_IDX ** -0.5)
    scale = HEAD_DIM ** -0.5

    def block(qs, qp):
        qb, qib, wb = qs
        idx_logits = jnp.einsum('bqhd,bkd->bqhk', qib, k_idx_all).astype(jnp.float32)
        score = jnp.einsum('bqh,bqhk->bqk', wb, jax.nn.relu(idx_logits))
        limit = chunk_limit(qp)
        score = jnp.where((k_pos[None, :] < limit[:, None])[None], score, -jnp.inf)
        _, sel = lax.top_k(score, topk)
        sel_pos = k_pos[sel]
        valid = sel_pos < limit[None, :, None]
        k_sel = jax.vmap(lambda kk, ii: kk[ii])(k_all, sel)
        v_sel = jax.vmap(lambda vv, ii: vv[ii])(v_all, sel)
        logits = jnp.einsum('bqhd,bqkhd->bhqk', qb, k_sel).astype(jnp.float32) * scale
        bias = t5_table[t5_bucket(sel_pos - qp[None, :, None])]
        logits = logits + jnp.transpose(bias, (0, 3, 1, 2)).astype(jnp.float32)
        p = jax.nn.softmax(jnp.where(valid[:, None], logits, -jnp.inf), axis=-1)
        return jnp.einsum('bhqk,bqkhd->bqhd', p.astype(v_sel.dtype), v_sel)

    return sweep_query_blocks(block, (q, q_idx, w_eff), q_pos)


def mixing_block(u, pos, past, w_in, q_norm, w_uq, kv_norm, w_ukv, band_bias, t5_table, w_branch, w_out):
    b, s, _ = u.shape
    (c_q, c_kv, k_r, sb_qkv, bd_qkv, ds_qkv, idx_q, idx_k, idx_w, gates) = split_columns(u @ w_in, IN_SIZES)
    c_kv = rms_norm(c_kv, kv_norm)
    k_r = rope(k_r[:, :, None, :], pos)[:, :, 0, :]
    q = (rms_norm(c_q, q_norm) @ w_uq).reshape(b, s, H_MLA, NOPE_DIM + ROPE_DIM)
    q_nope, q_rope = q[..., :NOPE_DIM], rope(q[..., NOPE_DIM:], pos)
    sb_q, sb_k, sb_v = split_heads3(sb_qkv, H_SB)
    bd_q, bd_k, bd_v = split_heads3(bd_qkv, H_BAND)
    ds_q, ds_k, ds_v = split_heads3(ds_qkv, H_DSA)
    idx_q = idx_q.reshape(b, s, H_IDX, D_IDX)
    new = (c_kv, k_r, sb_k, sb_v, bd_k, bd_v, ds_k, ds_v, idx_k)
    if past is None:
        ckv_a, kr_a, sbk_a, sbv_a, dsk_a, dsv_a, kidx_a = c_kv, k_r, sb_k, sb_v, ds_k, ds_v, idx_k
        k_pos = pos
        o_bd = band_attention_prompt(bd_q, bd_k, bd_v, band_bias)
        keep = min(N_PREV_CHUNKS * CHUNK, s)
        rows = (c_kv, k_r, sb_k, sb_v, bd_k[:, s - keep:], bd_v[:, s - keep:], ds_k, ds_v, idx_k)
    else:
        past_len = past[0].shape[1]
        ckv_a, kr_a, sbk_a, sbv_a, bdk_a, bdv_a, dsk_a, dsv_a, kidx_a = (
            jnp.concatenate([a, n], axis=1) for a, n in zip(past, new))
        k_pos = jnp.arange(past_len + s)
        band_pos = jnp.arange(past_len - past[4].shape[1], past_len + s)
        o_bd = band_attend(bd_q, bdk_a, bdv_a, pos, band_pos, band_bias)
        rows = new
    o_mla = mla_attention(q_nope, q_rope, ckv_a, kr_a, w_ukv, pos, k_pos)
    o_sb = stick_breaking_attention(sb_q, sbk_a, sbv_a, pos, k_pos)
    o_ds = dsa_attention(ds_q, idx_q, idx_w, dsk_a, dsv_a, kidx_a, pos, k_pos, t5_table)
    g = jax.nn.sigmoid(gates.reshape(b, s, N_BRANCH, D_MODEL))
    branches = (o_mla, o_sb, o_bd, o_ds)
    merged = g[:, :, 0] * (branches[0].reshape(b, s, BRANCH_W) @ w_branch[0])
    for n in range(1, N_BRANCH):
        merged = merged + g[:, :, n] * (branches[n].reshape(b, s, BRANCH_W) @ w_branch[n])
    return merged @ w_out, rows


def swiglu(u, w1, w3, w2):
    return (jax.nn.silu(u @ w1) * (u @ w3)) @ w2


def moe_swiglu(u, router, w1, w3, w2):
    b, s, d = u.shape
    t = u.reshape(b * s, d)
    probs = jax.nn.softmax((t @ router).astype(jnp.float32), axis=-1)
    top_p, top_i = lax.top_k(probs, TOP_K_EXPERTS)
    top_p = top_p / jnp.sum(top_p, axis=-1, keepdims=True)
    gate = jnp.sum(jax.nn.one_hot(top_i, N_EXPERTS, dtype=jnp.float32) * top_p[..., None], axis=1)
    y = gate[:, 0:1].astype(t.dtype) * swiglu(t, w1[0], w3[0], w2[0])
    for e in range(1, N_EXPERTS):
        y = y + gate[:, e:e + 1].astype(t.dtype) * swiglu(t, w1[e], w3[e], w2[e])
    return y.reshape(b, s, d)


def layer_step(h, p_i, pos, past, mix_w, ffn, g_mix, g_ffn, g_ple, w_ple, w_ple_gate):
    mix, rows = mixing_block(rms_norm(h, g_mix), pos, past, *mix_w)
    h = h + mix
    h = h + ffn(rms_norm(h, g_ffn))
    h = h + (p_i @ w_ple) * jax.nn.sigmoid(rms_norm(h, g_ple) @ w_ple_gate)
    return h, rows


def setup_inputs(seed: int = 0) -> dict:
    key = jax.random.key(seed)
    keys = iter(jax.random.split(key, 48))

    def nrm(shape, scale=1.0):
        return jax.random.normal(next(keys), shape, jnp.float32) * scale

    def gain(shape):
        return 1.0 + 0.05 * nrm(shape)

    d = D_MODEL
    band_past = min(N_PREV_CHUNKS * CHUNK, PAST_LEN)
    return {
        'x_prompt': nrm((BATCH, SEQ, d)),
        'x_sample': nrm((DEC_BATCH, DEC_SEQ, d)),
        'p_prompt': nrm((DEPTH, BATCH, SEQ, PLE_DIM)),
        'p_sample': nrm((DEPTH, DEC_BATCH, DEC_SEQ, PLE_DIM)),
        'cache_mla_ckv': nrm((DEPTH, DEC_BATCH, PAST_LEN, KV_LORA)),
        'cache_mla_krope': nrm((DEPTH, DEC_BATCH, PAST_LEN, ROPE_DIM)),
        'cache_sb_k': nrm((DEPTH, DEC_BATCH, PAST_LEN, H_SB, HEAD_DIM)),
        'cache_sb_v': nrm((DEPTH, DEC_BATCH, PAST_LEN, H_SB, HEAD_DIM)),
        'cache_band_k': nrm((DEPTH, DEC_BATCH, band_past, H_BAND, HEAD_DIM)),
        'cache_band_v': nrm((DEPTH, DEC_BATCH, band_past, H_BAND, HEAD_DIM)),
        'cache_dsa_k': nrm((DEPTH, DEC_BATCH, PAST_LEN, H_DSA, HEAD_DIM)),
        'cache_dsa_v': nrm((DEPTH, DEC_BATCH, PAST_LEN, H_DSA, HEAD_DIM)),
        'cache_dsa_kidx': nrm((DEPTH, DEC_BATCH, PAST_LEN, D_IDX)),
        'norm_mix': gain((DEPTH, d)),
        'w_in': nrm((DEPTH, d, IN_TOTAL), d ** -0.5),
        'mla_q_norm': gain((DEPTH, Q_LORA)),
        'mla_w_uq': nrm((DEPTH, Q_LORA, H_MLA * (NOPE_DIM + ROPE_DIM)), Q_LORA ** -0.5),
        'mla_kv_norm': gain((DEPTH, KV_LORA)),
        'mla_w_ukv': nrm((DEPTH, KV_LORA, H_MLA * (NOPE_DIM + V_DIM)), KV_LORA ** -0.5),
        'band_rel_bias': nrm((DEPTH, 2 * REL_CLIP + 1, H_BAND), 0.5),
        't5_rel_bias': nrm((T5_BUCKETS, H_DSA), 0.5),
        'w_branch': nrm((DEPTH, N_BRANCH, BRANCH_W, d), BRANCH_W ** -0.5),
        'w_out': nrm((DEPTH, d, d), d ** -0.5),
        'norm_ffn': gain((DEPTH, d)),
        'ffn_w1': nrm((N_DENSE, d, D_FF), d ** -0.5),
        'ffn_w3': nrm((N_DENSE, d, D_FF), d ** -0.5),
        'ffn_w2': nrm((N_DENSE, D_FF, d), D_FF ** -0.5),
        'moe_router': nrm((N_MOE, d, N_EXPERTS), d ** -0.5),
        'moe_w1': nrm((N_MOE, N_EXPERTS, d, D_FF_EXPERT), d ** -0.5),
        'moe_w3': nrm((N_MOE, N_EXPERTS, d, D_FF_EXPERT), d ** -0.5),
        'moe_w2': nrm((N_MOE, N_EXPERTS, D_FF_EXPERT, d), D_FF_EXPERT ** -0.5),
        'norm_ple': gain((DEPTH, d)),
        'ple_w': nrm((DEPTH, PLE_DIM, d), PLE_DIM ** -0.5),
        'ple_gate_w': nrm((DEPTH, d, d), d ** -0.5),
        'norm_final': gain((d,)),
    }


def reference(x_prompt, x_sample, p_prompt, p_sample,
              cache_mla_ckv, cache_mla_krope, cache_sb_k, cache_sb_v,
              cache_band_k, cache_band_v, cache_dsa_k, cache_dsa_v, cache_dsa_kidx,
              norm_mix, w_in, mla_q_norm, mla_w_uq, mla_kv_norm, mla_w_ukv,
              band_rel_bias, t5_rel_bias, w_branch, w_out,
              norm_ffn, ffn_w1, ffn_w3, ffn_w2,
              moe_router, moe_w1, moe_w3, moe_w2,
              norm_ple, ple_w, ple_gate_w, norm_final):
    past_len = cache_sb_k.shape[2]
    pos_p = jnp.arange(x_prompt.shape[1])
    pos_s = past_len + jnp.arange(x_sample.shape[1])
    hp, hs = x_prompt, x_sample
    rows_p, rows_s = [], []
    for i in range(DEPTH):
        mix_w = (w_in[i], mla_q_norm[i], mla_w_uq[i], mla_kv_norm[i], mla_w_ukv[i],
                 band_rel_bias[i], t5_rel_bias, w_branch[i], w_out[i])
        j = i // 2
        if i % 2 == 0:
            ffn = functools.partial(swiglu, w1=ffn_w1[j], w3=ffn_w3[j], w2=ffn_w2[j])
        else:
            ffn = functools.partial(moe_swiglu, router=moe_router[j], w1=moe_w1[j], w3=moe_w3[j], w2=moe_w2[j])
        past_i = (cache_mla_ckv[i], cache_mla_krope[i], cache_sb_k[i], cache_sb_v[i],
                  cache_band_k[i], cache_band_v[i], cache_dsa_k[i], cache_dsa_v[i], cache_dsa_kidx[i])
        hp, rp = layer_step(hp, p_prompt[i], pos_p, None, mix_w, ffn,
                            norm_mix[i], norm_ffn[i], norm_ple[i], ple_w[i], ple_gate_w[i])
        hs, rs = layer_step(hs, p_sample[i], pos_s, past_i, mix_w, ffn,
                            norm_mix[i], norm_ffn[i], norm_ple[i], ple_w[i], ple_gate_w[i])
        rows_p.append(rp)
        rows_s.append(rs)
    y_prompt = rms_norm(hp, norm_final)
    y_sample = rms_norm(hs, norm_final)

    def stacked(rows, n):
        return jnp.stack([r[n] for r in rows], axis=0)

    mla_ckv_p, mla_ckv_s = stacked(rows_p, 0), stacked(rows_s, 0)
    mla_krope_p, mla_krope_s = stacked(rows_p, 1), stacked(rows_s, 1)
    sb_k_p, sb_k_s = stacked(rows_p, 2), stacked(rows_s, 2)
    sb_v_p, sb_v_s = stacked(rows_p, 3), stacked(rows_s, 3)
    band_k_p, band_k_s = stacked(rows_p, 4), stacked(rows_s, 4)
    band_v_p, band_v_s = stacked(rows_p, 5), stacked(rows_s, 5)
    dsa_k_p, dsa_k_s = stacked(rows_p, 6), stacked(rows_s, 6)
    dsa_v_p, dsa_v_s = stacked(rows_p, 7), stacked(rows_s, 7)
    dsa_kidx_p, dsa_kidx_s = stacked(rows_p, 8), stacked(rows_s, 8)
    return (y_prompt, y_sample,
            mla_ckv_p, mla_ckv_s, mla_krope_p, mla_krope_s,
            sb_k_p, sb_k_s, sb_v_p, sb_v_s,
            band_k_p, band_k_s, band_v_p, band_v_s,
            dsa_k_p, dsa_k_s, dsa_v_p, dsa_v_s, dsa_kidx_p, dsa_kidx_s)
```

```python
import functools
import math

import jax
import jax.numpy as jnp
from jax import lax
from jax.experimental import pallas as pl
from jax.experimental.pallas import tpu as pltpu

BF = jnp.bfloat16
F32 = jnp.float32

CHUNK = 64
CHUNK_SHIFT = 6
HEAD_DIM = 128
N_BRANCH = 4
H_MLA = 4
NOPE_DIM = 128
ROPE_DIM = 64
V_DIM = 128
ROPE_THETA = 10000.0
H_SB = 4
H_BAND = 4
N_PREV_CHUNKS = 8
REL_CLIP = 128
H_DSA = 4
H_IDX = 16
D_IDX = 64
TOPK_MAX = 256
T5_BUCKETS = 32
T5_MAX_DIST = 128
TOP_K_EXPERTS = 2
EPS = 1e-6

LANES = 128
VMEM_LIMIT = 56 * 1024 * 1024
NEG = -1e30
INT_MIN = -2147483648


def _cparams(n_axes):
    return pltpu.CompilerParams(dimension_semantics=("arbitrary",) * n_axes,
                                vmem_limit_bytes=VMEM_LIMIT)


def _tile(n, preferred):
    t = min(preferred, n)
    while n % t:
        t //= 2
    assert t == n or t % LANES == 0
    return t


def _dot(a, b):
    return jnp.dot(a, b, preferred_element_type=F32)


def _dot_nt(a, b):
    return lax.dot_general(a, b, (((1,), (1,)), ((), ())), preferred_element_type=F32)


def _sigmoid(x):
    return 1.0 / (1.0 + jnp.exp(-x))


def _rms(x, g):
    return x * lax.rsqrt(jnp.mean(x * x, axis=-1, keepdims=True) + EPS) * g


def _rmsnorm_kernel(x_ref, g_ref, o_ref):
    o_ref[...] = _rms(x_ref[...], g_ref[...]).astype(o_ref.dtype)


def rmsnorm(x, g, out_dtype, tm=512):
    t, d = x.shape
    tm = _tile(t, tm)
    return pl.pallas_call(
        _rmsnorm_kernel,
        grid=(t // tm,),
        in_specs=[pl.BlockSpec((tm, d), lambda i: (i, 0)),
                  pl.BlockSpec((1, d), lambda i: (0, 0))],
        out_specs=pl.BlockSpec((tm, d), lambda i: (i, 0)),
        out_shape=jax.ShapeDtypeStruct((t, d), out_dtype),
        compiler_params=_cparams(1),
        name="rmsnorm",
    )(x, g.reshape(1, d))


def _mm_kernel(*refs, has_res, gate_col, n_out):
    a_ref, w_ref = refs[0], refs[1]
    pos = 2
    res_ref = gate_ref = None
    if has_res:
        res_ref = refs[pos]
        pos += 1
    if gate_col is not None:
        gate_ref = refs[pos]
        pos += 1
    r = _dot(a_ref[...], w_ref[...])
    if gate_ref is not None:
        r = r * gate_ref[:, gate_col:gate_col + 1]
    if res_ref is not None:
        r = res_ref[...] + r
    for o_ref in refs[pos:pos + n_out]:
        o_ref[...] = r.astype(o_ref.dtype)


def matmul(a, w, out_dtypes, residual=None, gate=None, gate_col=None, tm=512, tn=512, name="mm"):
    m, k = a.shape
    n = w.shape[1]
    tm, tn = _tile(m, tm), _tile(n, tn)
    in_specs = [pl.BlockSpec((tm, k), lambda j, i: (i, 0)),
                pl.BlockSpec((k, tn), lambda j, i: (0, j))]
    args = [a, w]
    if residual is not None:
        in_specs.append(pl.BlockSpec((tm, tn), lambda j, i: (i, j)))
        args.append(residual)
    if gate is not None:
        in_specs.append(pl.BlockSpec((tm, gate.shape[1]), lambda j, i: (i, 0)))
        args.append(gate)
    outs = pl.pallas_call(
        functools.partial(_mm_kernel, has_res=residual is not None,
                          gate_col=gate_col if gate is not None else None, n_out=len(out_dtypes)),
        grid=(n // tn, m // tm),
        in_specs=in_specs,
        out_specs=[pl.BlockSpec((tm, tn), lambda j, i: (i, j)) for _ in out_dtypes],
        out_shape=[jax.ShapeDtypeStruct((m, n), dt) for dt in out_dtypes],
        compiler_params=_cparams(2),
        name=name,
    )(*args)
    return outs


def _swiglu_up_kernel(a_ref, w1_ref, w3_ref, o_ref):
    a = a_ref[...]
    x1 = _dot(a, w1_ref[...])
    x3 = _dot(a, w3_ref[...])
    o_ref[...] = (x1 * _sigmoid(x1) * x3).astype(o_ref.dtype)


def swiglu_up(a, w1, w3, tm=512, tn=512):
    m, k = a.shape
    n = w1.shape[1]
    tm, tn = _tile(m, tm), _tile(n, tn)
    return pl.pallas_call(
        _swiglu_up_kernel,
        grid=(n // tn, m // tm),
        in_specs=[pl.BlockSpec((tm, k), lambda j, i: (i, 0)),
                  pl.BlockSpec((k, tn), lambda j, i: (0, j)),
                  pl.BlockSpec((k, tn), lambda j, i: (0, j))],
        out_specs=pl.BlockSpec((tm, tn), lambda j, i: (i, j)),
        out_shape=jax.ShapeDtypeStruct((m, n), BF),
        compiler_params=_cparams(2),
        name="swiglu_up",
    )(a, w1, w3)


def _gate_merge_kernel(u_ref, o0_ref, o1_ref, o2_ref, o3_ref, wg_ref, wb_ref, out_ref):
    u = u_ref[...]
    acc = None
    for b, o_ref in enumerate((o0_ref, o1_ref, o2_ref, o3_ref)):
        t = _sigmoid(_dot(u, wg_ref[b])) * _dot(o_ref[...], wb_ref[b])
        acc = t if acc is None else acc + t
    out_ref[...] = acc.astype(out_ref.dtype)


def gate_merge(u, branches, wg, wb, tm=512, tn=256):
    t, d = u.shape
    bw = branches[0].shape[1]
    n = wg.shape[2]
    tm = min(tm, t)
    return pl.pallas_call(
        _gate_merge_kernel,
        grid=(n // tn, t // tm),
        in_specs=[pl.BlockSpec((tm, d), lambda j, i: (i, 0))]
        + [pl.BlockSpec((tm, bw), lambda j, i: (i, 0)) for _ in range(N_BRANCH)]
        + [pl.BlockSpec((N_BRANCH, d, tn), lambda j, i: (0, 0, j)),
           pl.BlockSpec((N_BRANCH, bw, tn), lambda j, i: (0, 0, j))],
        out_specs=pl.BlockSpec((tm, tn), lambda j, i: (i, j)),
        out_shape=jax.ShapeDtypeStruct((t, n), BF),
        compiler_params=_cparams(2),
        name="gate_merge",
    )(u, *branches, wg, wb)


def _ple_kernel(h_ref, p_ref, un_ref, wp_ref, wg_ref, o_ref):
    o_ref[...] = h_ref[...] + _dot(p_ref[...], wp_ref[...]) * _sigmoid(_dot(un_ref[...], wg_ref[...]))


def ple_update(h, p, un, wp, wg, tm=512, tn=512):
    t, d = h.shape
    tm = min(tm, t)
    return pl.pallas_call(
        _ple_kernel,
        grid=(d // tn, t // tm),
        in_specs=[pl.BlockSpec((tm, tn), lambda j, i: (i, j)),
                  pl.BlockSpec((tm, p.shape[1]), lambda j, i: (i, 0)),
                  pl.BlockSpec((tm, d), lambda j, i: (i, 0)),
                  pl.BlockSpec((p.shape[1], tn), lambda j, i: (0, j)),
                  pl.BlockSpec((d, tn), lambda j, i: (0, j))],
        out_specs=pl.BlockSpec((tm, tn), lambda j, i: (i, j)),
        out_shape=jax.ShapeDtypeStruct((t, d), F32),
        compiler_params=_cparams(2),
        name="ple_update",
    )(h, p, un, wp, wg)


def _router_kernel(un_ref, wr_ref, g_ref, *, n_experts):
    logits = _dot(un_ref[...], wr_ref[...])
    lane = lax.broadcasted_iota(jnp.int32, logits.shape, 1).astype(F32)
    real = lane < n_experts
    logits = jnp.where(real, logits, NEG)
    e = jnp.where(real, jnp.exp(logits - jnp.max(logits, axis=1, keepdims=True)), 0.0)
    probs = e / jnp.sum(e, axis=1, keepdims=True)
    p1 = jnp.max(probs, axis=1, keepdims=True)
    i1 = jnp.min(jnp.where(probs == p1, lane, float(LANES)), axis=1, keepdims=True)
    first = lane == i1
    rest = jnp.where(first | ~real, -1.0, probs)
    p2 = jnp.max(rest, axis=1, keepdims=True)
    i2 = jnp.min(jnp.where(rest == p2, lane, float(LANES)), axis=1, keepdims=True)
    second = lane == i2
    denom = p1 + p2
    g_ref[...] = jnp.where(first, p1 / denom, 0.0) + jnp.where(second, p2 / denom, 0.0)


def router_gate(un, wr_pad, n_experts, tm=512):
    t, d = un.shape
    tm = min(tm, t)
    return pl.pallas_call(
        functools.partial(_router_kernel, n_experts=n_experts),
        grid=(t // tm,),
        in_specs=[pl.BlockSpec((tm, d), lambda i: (i, 0)),
                  pl.BlockSpec((d, LANES), lambda i: (0, 0))],
        out_specs=pl.BlockSpec((tm, LANES), lambda i: (i, 0)),
        out_shape=jax.ShapeDtypeStruct((t, LANES), F32),
        compiler_params=_cparams(1),
        name="router_gate",
    )(un, wr_pad)


Q_LORA_OFF = 0


def _proj_misc_kernel(u_ref, wm_ref, qn_ref, kvn_ref, wqa_ref, wqb_ref, cs_ref,
                      ckv_ref, krp_ref, kidx_ref, idxw_ref, qcat_ref, *, q_lora, kv_lora):
    x = _dot(u_ref[...], wm_ref[...])
    cos = cs_ref[:, 0:LANES]
    sin = cs_ref[:, LANES:2 * LANES]
    o = q_lora
    ckv_ref[...] = _rms(x[:, o:o + kv_lora], kvn_ref[...])
    o += kv_lora
    krp_ref[...] = x[:, o:o + LANES] * cos + x[:, o + LANES:o + 2 * LANES] * sin
    o += 2 * LANES
    kidx_ref[...] = x[:, o:o + LANES]
    idxw_ref[...] = x[:, o + LANES:o + 2 * LANES]
    cqn = _rms(x[:, 0:q_lora], qn_ref[...]).astype(BF)
    qa = _dot(cqn, wqa_ref[...])
    qb = _dot(cqn, wqb_ref[...])
    for h in range(H_MLA):
        lo = 2 * h * LANES
        qcat_ref[:, lo:lo + LANES] = qa[:, lo:lo + LANES].astype(BF)
        qcat_ref[:, lo + LANES:lo + 2 * LANES] = (
            qa[:, lo + LANES:lo + 2 * LANES] * cos + qb[:, h * LANES:(h + 1) * LANES] * sin).astype(BF)


def proj_misc(u, wm, qn, kvn, wqa, wqb, cs, tm=512):
    t, d = u.shape
    tm = min(tm, t)
    q_lora, kv_lora = qn.shape[1], kvn.shape[1]
    n_pos_blocks = cs.shape[0] // tm
    row = lambda n: pl.BlockSpec((tm, n), lambda i: (i, 0))
    full = lambda a: pl.BlockSpec(a.shape, lambda i: (0, 0))
    return pl.pallas_call(
        functools.partial(_proj_misc_kernel, q_lora=q_lora, kv_lora=kv_lora),
        grid=(t // tm,),
        in_specs=[row(d), full(wm), full(qn), full(kvn), full(wqa), full(wqb),
                  pl.BlockSpec((tm, 2 * LANES), lambda i: (i % n_pos_blocks, 0))],
        out_specs=[row(kv_lora), row(LANES), row(LANES), row(LANES), row(2 * LANES * H_MLA)],
        out_shape=[jax.ShapeDtypeStruct((t, kv_lora), F32),
                   jax.ShapeDtypeStruct((t, LANES), F32),
                   jax.ShapeDtypeStruct((t, LANES), F32),
                   jax.ShapeDtypeStruct((t, LANES), F32),
                   jax.ShapeDtypeStruct((t, 2 * LANES * H_MLA), BF)],
        compiler_params=_cparams(1),
        name="proj_misc",
    )(u, wm, qn, kvn, wqa, wqb, cs)


def _mla_kv_up_kernel(ckv_ref, krp_ref, wk_ref, wv_ref, kcat_ref, v_ref):
    c = ckv_ref[...].astype(BF)
    kn = _dot(c, wk_ref[...])
    krp = krp_ref[...].astype(BF)
    for h in range(H_MLA):
        kcat_ref[:, 2 * h * LANES:(2 * h + 1) * LANES] = kn[:, h * LANES:(h + 1) * LANES].astype(BF)
        kcat_ref[:, (2 * h + 1) * LANES:(2 * h + 2) * LANES] = krp
    v_ref[...] = _dot(c, wv_ref[...]).astype(BF)


def mla_kv_up(ckv, krp, wk, wv, tm=512):
    t, c = ckv.shape
    tm = min(tm, t)
    row = lambda n: pl.BlockSpec((tm, n), lambda i: (i, 0))
    full = lambda a: pl.BlockSpec(a.shape, lambda i: (0, 0))
    return pl.pallas_call(
        _mla_kv_up_kernel,
        grid=(t // tm,),
        in_specs=[row(c), row(LANES), full(wk), full(wv)],
        out_specs=[row(2 * LANES * H_MLA), row(V_DIM * H_MLA)],
        out_shape=[jax.ShapeDtypeStruct((t, 2 * LANES * H_MLA), BF),
                   jax.ShapeDtypeStruct((t, V_DIM * H_MLA), BF)],
        compiler_params=_cparams(1),
        name="mla_kv_up",
    )(ckv, krp, wk, wv)


def _mla_attn_kernel(q_ref, k_ref, v_ref, o_ref, m_ref, l_ref, acc_ref, *, tq, tk, q_off, scale):
    q0 = q_off + pl.program_id(2) * tq
    qpos = q0 + lax.broadcasted_iota(jnp.int32, (tq, 1), 0)
    limit = (lax.shift_right_arithmetic(qpos, CHUNK_SHIFT) + 1) * CHUNK
    last_limit = ((q0 + tq - 1) // CHUNK + 1) * CHUNK
    n_chunks = (last_limit + tk - 1) // tk
    m_ref[...] = jnp.full(m_ref.shape, NEG, F32)
    l_ref[...] = jnp.zeros(l_ref.shape, F32)
    acc_ref[...] = jnp.zeros(acc_ref.shape, F32)
    q = q_ref[...]

    def body(kc, carry):
        ks = pl.multiple_of(kc * tk, tk)
        s = _dot_nt(q, k_ref[pl.ds(ks, tk), :]) * scale
        kpos = ks + lax.broadcasted_iota(jnp.int32, (1, tk), 1)
        s = jnp.where(kpos < limit, s, NEG)
        m_prev = m_ref[...]
        m_new = jnp.maximum(m_prev, jnp.max(s, axis=1, keepdims=True))
        p = jnp.exp(s - m_new)
        alpha = jnp.exp(m_prev - m_new)
        l_ref[...] = alpha * l_ref[...] + jnp.sum(p, axis=1, keepdims=True)
        acc_ref[...] = alpha * acc_ref[...] + _dot(p.astype(BF), v_ref[pl.ds(ks, tk), :])
        m_ref[...] = m_new
        return carry

    lax.fori_loop(0, n_chunks, body, 0)
    o_ref[...] = (acc_ref[...] / l_ref[...]).astype(o_ref.dtype)


def mla_attention(qcat, kcat, v, q_off, tq, tk):
    b, sq, _ = qcat.shape
    sk = kcat.shape[1]
    scale = (NOPE_DIM + ROPE_DIM) ** -0.5
    return pl.pallas_call(
        functools.partial(_mla_attn_kernel, tq=tq, tk=tk, q_off=q_off, scale=scale),
        grid=(b, H_MLA, sq // tq),
        in_specs=[pl.BlockSpec((None, tq, 2 * LANES), lambda bi, h, qi: (bi, qi, h)),
                  pl.BlockSpec((None, sk, 2 * LANES), lambda bi, h, qi: (bi, 0, h)),
                  pl.BlockSpec((None, sk, V_DIM), lambda bi, h, qi: (bi, 0, h))],
        out_specs=pl.BlockSpec((None, tq, V_DIM), lambda bi, h, qi: (bi, qi, h)),
        out_shape=jax.ShapeDtypeStruct((b, sq, H_MLA * V_DIM), BF),
        scratch_shapes=[pltpu.VMEM((tq, 1), F32), pltpu.VMEM((tq, 1), F32), pltpu.VMEM((tq, V_DIM), F32)],
        compiler_params=_cparams(3),
        name="mla_attention",
    )(qcat, kcat, v)


def _split3(x):
    hi = x.astype(BF)
    r = x - hi.astype(F32)
    mid = r.astype(BF)
    lo = (r - mid.astype(F32)).astype(BF)
    return hi, mid, lo


def _sb_attn_kernel(q_ref, k_ref, v_ref, tri_ref, o_ref, acc_ref, carry_ref, *, tq, tk, q_off, scale):
    q0 = q_off + pl.program_id(2) * tq
    qpos = q0 + lax.broadcasted_iota(jnp.int32, (tq, 1), 0)
    n_chunks = (q0 + tq - 1 + tk - 1) // tk
    acc_ref[...] = jnp.zeros(acc_ref.shape, F32)
    carry_ref[...] = jnp.zeros(carry_ref.shape, F32)
    q = q_ref[...]
    tri = tri_ref[...]

    def body(it, carry):
        kc = n_chunks - 1 - it
        ks = pl.multiple_of(kc * tk, tk)
        z = _dot_nt(q, k_ref[pl.ds(ks, tk), :]) * scale
        kpos = ks + lax.broadcasted_iota(jnp.int32, (1, tk), 1)
        strict = kpos < qpos
        sp = jnp.maximum(z, 0.0) + jnp.log(1.0 + jnp.exp(-jnp.abs(z)))
        log_not = jnp.where(strict, -sp, 0.0)
        hi, mid, lo = _split3(log_not)
        sums = _dot(hi, tri) + _dot(mid, tri) + _dot(lo, tri)
        after = sums[:, 0:tk] + carry_ref[...]
        a = jnp.where(strict, jnp.exp((z - sp) + after), 0.0)
        acc_ref[...] += _dot(a.astype(BF), v_ref[pl.ds(ks, tk), :])
        carry_ref[...] += sums[:, tk:2 * tk]
        return carry

    lax.fori_loop(0, n_chunks, body, 0)
    o_ref[...] = acc_ref[...].astype(o_ref.dtype)


def sb_attention(q, k, v, q_off, tq, tk=LANES):
    b, sq, _ = q.shape
    sk = k.shape[1]
    j = lax.broadcasted_iota(jnp.int32, (tk, tk), 0)
    s = lax.broadcasted_iota(jnp.int32, (tk, tk), 1)
    tri = jnp.concatenate([(j > s).astype(BF), jnp.ones((tk, tk), BF)], axis=1)
    return pl.pallas_call(
        functools.partial(_sb_attn_kernel, tq=tq, tk=tk, q_off=q_off, scale=HEAD_DIM ** -0.5),
        grid=(b, H_SB, sq // tq),
        in_specs=[pl.BlockSpec((None, tq, HEAD_DIM), lambda bi, h, qi: (bi, qi, h)),
                  pl.BlockSpec((None, sk, HEAD_DIM), lambda bi, h, qi: (bi, 0, h)),
                  pl.BlockSpec((None, sk, HEAD_DIM), lambda bi, h, qi: (bi, 0, h)),
                  pl.BlockSpec((tk, 2 * tk), lambda bi, h, qi: (0, 0))],
        out_specs=pl.BlockSpec((None, tq, HEAD_DIM), lambda bi, h, qi: (bi, qi, h)),
        out_shape=jax.ShapeDtypeStruct((b, sq, H_SB * HEAD_DIM), BF),
        scratch_shapes=[pltpu.VMEM((tq, HEAD_DIM), F32), pltpu.VMEM((tq, tk), F32)],
        compiler_params=_cparams(3),
        name="sb_attention",
    )(q, k, v, tri)


def _band_attn_kernel(q_ref, k_ref, v_ref, bm_ref, o_ref, *, tq, win, kpos_base, scale):
    w0 = pl.multiple_of(pl.program_id(2) * tq, tq)
    k = k_ref[pl.ds(w0, win), :]
    s = _dot_nt(q_ref[...], k) * scale + bm_ref[...]
    kpos = kpos_base + w0 + lax.broadcasted_iota(jnp.int32, (1, win), 1)
    s = jnp.where(kpos >= 0, s, NEG)
    p = jnp.exp(s - jnp.max(s, axis=1, keepdims=True))
    denom = jnp.sum(p, axis=1, keepdims=True)
    o = _dot(p.astype(BF), v_ref[pl.ds(w0, win), :])
    o_ref[...] = (o / denom).astype(o_ref.dtype)


def _band_window(tq):
    return -(-(tq + N_PREV_CHUNKS * CHUNK) // LANES) * LANES


def band_bias_mask(rel_bias, tq):
    win = _band_window(tq)
    i = jnp.arange(tq)[:, None]
    j = jnp.arange(win)[None, :]
    rel = jnp.clip(i + N_PREV_CHUNKS * CHUNK - j, -REL_CLIP, REL_CLIP) + REL_CLIP
    ci, cj = i // CHUNK, j // CHUNK
    inside = (cj >= ci) & (cj <= ci + N_PREV_CHUNKS)
    bias = jnp.transpose(rel_bias.astype(F32)[rel], (2, 0, 1))
    return jnp.where(inside[None], bias, NEG)


def band_attention(q, k_pad, v_pad, bias_mask, tq, kpos_base):
    b, sq, _ = q.shape
    skp = k_pad.shape[1]
    win = _band_window(tq)
    assert skp >= sq - tq + win
    return pl.pallas_call(
        functools.partial(_band_attn_kernel, tq=tq, win=win, kpos_base=kpos_base, scale=HEAD_DIM ** -0.5),
        grid=(b, H_BAND, sq // tq),
        in_specs=[pl.BlockSpec((None, tq, HEAD_DIM), lambda bi, h, qi: (bi, qi, h)),
                  pl.BlockSpec((None, skp, HEAD_DIM), lambda bi, h, qi: (bi, 0, h)),
                  pl.BlockSpec((None, skp, HEAD_DIM), lambda bi, h, qi: (bi, 0, h)),
                  pl.BlockSpec((None, tq, win), lambda bi, h, qi: (h, 0, 0))],
        out_specs=pl.BlockSpec((None, tq, HEAD_DIM), lambda bi, h, qi: (bi, qi, h)),
        out_shape=jax.ShapeDtypeStruct((b, sq, H_BAND * HEAD_DIM), BF),
        compiler_params=_cparams(3),
        name="band_attention",
    )(q, k_pad, v_pad, bias_mask)


def _sortable(x):
    i = lax.bitcast_convert_type(x, jnp.int32)
    return i ^ (lax.shift_right_arithmetic(i, 31) & 0x7FFFFFFF)


def _dsa_kernel(qi2_ref, w_ref, klo_ref, khi_ref, q_ref, k_ref, v_ref, bnear_ref, bfar_ref, tri_ref,
                o_ref, key_ref, m_ref, l_ref, acc_ref, eqc_ref, *, tq, tk, q_off, topk, scale, w_scale):
    q0 = q_off + pl.program_id(1) * tq
    qpos = q0 + lax.broadcasted_iota(jnp.int32, (tq, 1), 0)
    limit = (lax.shift_right_arithmetic(qpos, CHUNK_SHIFT) + 1) * CHUNK
    last_limit = ((q0 + tq - 1) // CHUNK + 1) * CHUNK
    n_chunks = (last_limit + tk - 1) // tk
    diag = q0 // tk

    w = w_ref[...] * w_scale

    def score_body(kc, carry):
        ks = pl.multiple_of(kc * tk, tk)
        klo = klo_ref[pl.ds(ks, tk), :]
        khi = khi_ref[pl.ds(ks, tk), :]
        acc = jnp.zeros((tq, tk), F32)
        for pair in range(H_IDX // 2):
            q2 = qi2_ref[:, pair * LANES:(pair + 1) * LANES]
            acc = acc + w[:, 2 * pair:2 * pair + 1] * jnp.maximum(_dot_nt(q2, klo), 0.0)
            acc = acc + w[:, 2 * pair + 1:2 * pair + 2] * jnp.maximum(_dot_nt(q2, khi), 0.0)
        kpos = ks + lax.broadcasted_iota(jnp.int32, (1, tk), 1)
        acc = jnp.where(kpos < limit, acc + 0.0, -jnp.inf)
        key_ref[kc] = _sortable(acc)
        return carry

    lax.fori_loop(0, n_chunks, score_body, 0)

    def count_ge(cand):
        def body(kc, cnt):
            return cnt + jnp.where(key_ref[kc] >= cand, 1.0, 0.0)
        cnt = lax.fori_loop(0, n_chunks, body, jnp.zeros((tq, tk), F32))
        return jnp.sum(cnt, axis=1, keepdims=True)

    def bit_body(it, thr):
        cand = thr + lax.shift_left(jnp.int32(1), 31 - it)
        return jnp.where(count_ge(cand) >= topk, cand, thr)

    thr = lax.fori_loop(0, 32, bit_body, jnp.full((tq, 1), INT_MIN, jnp.int32))
    n_above = count_ge(thr + 1)
    n_ties_kept = topk - n_above

    m_ref[...] = jnp.full(m_ref.shape, NEG, F32)
    l_ref[...] = jnp.zeros(l_ref.shape, F32)
    acc_ref[...] = jnp.zeros(acc_ref.shape, F32)
    eqc_ref[...] = jnp.zeros(eqc_ref.shape, F32)
    tri = tri_ref[...]

    def attend(kc, bias_of_head):
        ks = pl.multiple_of(kc * tk, tk)
        key = key_ref[kc]
        eq = key == thr
        counts = _dot(jnp.where(eq, 1.0, 0.0).astype(BF), tri)
        rank = counts[:, 0:tk] + eqc_ref[...]
        eqc_ref[...] += counts[:, tk:2 * tk]
        kpos = ks + lax.broadcasted_iota(jnp.int32, (1, tk), 1)
        sel = ((key > thr) | (eq & (rank <= n_ties_kept))) & (kpos < limit)
        for h in range(H_DSA):
            hs = slice(h * HEAD_DIM, (h + 1) * HEAD_DIM)
            s = _dot_nt(q_ref[:, hs], k_ref[pl.ds(ks, tk), hs]) * scale + bias_of_head(h)
            s = jnp.where(sel, s, NEG)
            m_prev = m_ref[h]
            m_new = jnp.maximum(m_prev, jnp.max(s, axis=1, keepdims=True))
            p = jnp.where(sel, jnp.exp(s - m_new), 0.0)
            alpha = jnp.exp(m_prev - m_new)
            l_ref[h] = alpha * l_ref[h] + jnp.sum(p, axis=1, keepdims=True)
            acc_ref[h] = alpha * acc_ref[h] + _dot(p.astype(BF), v_ref[pl.ds(ks, tk), hs])
            m_ref[h] = m_new

    def far_body(kc, carry):
        attend(kc, lambda h: bfar_ref[h])
        return carry

    lax.fori_loop(0, jnp.maximum(diag - 1, 0), far_body, 0)

    @pl.when(diag >= 1)
    def _():
        attend(diag - 1, lambda h: bnear_ref[1, h])

    attend(diag, lambda h: bnear_ref[0, h])

    for h in range(H_DSA):
        o_ref[:, h * HEAD_DIM:(h + 1) * HEAD_DIM] = (acc_ref[h] / l_ref[h]).astype(o_ref.dtype)


def t5_bucket(rel):
    half = T5_BUCKETS // 2
    max_exact = half // 2
    n = jnp.abs(rel)
    nf = jnp.maximum(n, 1).astype(F32)
    large = max_exact + (jnp.log(nf / max_exact) / math.log(T5_MAX_DIST / max_exact)
                         * (half - max_exact)).astype(jnp.int32)
    large = jnp.minimum(large, half - 1)
    return jnp.where(rel > 0, half, 0) + jnp.where(n < max_exact, n, large)


def dsa_bias_tables(t5_table, tq, tk):
    i = jnp.arange(tq)[:, None]
    j = jnp.arange(tk)[None, :]
    near = jnp.stack([jnp.transpose(t5_table.astype(F32)[t5_bucket(j - i - d * tk)], (2, 0, 1))
                      for d in range(2)], axis=0)
    far_rel = -jnp.ones((1, tk), jnp.int32) * (2 * tk)
    far = jnp.transpose(t5_table.astype(F32)[t5_bucket(far_rel)], (2, 0, 1))
    return near, far


def dsa_attention(qidx2, idxw, klo, khi, q, k, v, t5_table, q_off, n_keys, tq, tk=LANES):
    b, sq, _ = q.shape
    sk = k.shape[1]
    assert tk >= T5_MAX_DIST and q_off % tk == 0 and (tq == tk or sq == tq) and sk % tk == 0
    topk = min(TOPK_MAX, n_keys // 4)
    near, far = dsa_bias_tables(t5_table, tq, tk)
    jj = lax.broadcasted_iota(jnp.int32, (tk, tk), 0)
    ss = lax.broadcasted_iota(jnp.int32, (tk, tk), 1)
    tri = jnp.concatenate([(jj <= ss).astype(BF), jnp.ones((tk, tk), BF)], axis=1)
    whole = lambda n: pl.BlockSpec((None, sk, n), lambda bi, qi: (bi, 0, 0))
    tile = lambda n: pl.BlockSpec((None, tq, n), lambda bi, qi: (bi, qi, 0))
    const = lambda a: pl.BlockSpec(a.shape, lambda bi, qi: (0,) * a.ndim)
    return pl.pallas_call(
        functools.partial(_dsa_kernel, tq=tq, tk=tk, q_off=q_off, topk=topk,
                          scale=HEAD_DIM ** -0.5, w_scale=H_IDX ** -0.5 * D_IDX ** -0.5),
        grid=(b, sq // tq),
        in_specs=[tile(qidx2.shape[2]), tile(LANES), whole(LANES), whole(LANES),
                  tile(H_DSA * HEAD_DIM), whole(H_DSA * HEAD_DIM), whole(H_DSA * HEAD_DIM),
                  const(near), const(far), const(tri)],
        out_specs=tile(H_DSA * HEAD_DIM),
        out_shape=jax.ShapeDtypeStruct((b, sq, H_DSA * HEAD_DIM), BF),
        scratch_shapes=[pltpu.VMEM((sk // tk, tq, tk), jnp.int32),
                        pltpu.VMEM((H_DSA, tq, 1), F32), pltpu.VMEM((H_DSA, tq, 1), F32),
                        pltpu.VMEM((H_DSA, tq, HEAD_DIM), F32), pltpu.VMEM((tq, tk), F32)],
        compiler_params=_cparams(2),
        name="dsa_attention",
    )(qidx2, idxw, klo, khi, q, k, v, near, far, tri)


def _pad_cols(a, n):
    return jnp.pad(a, ((0, 0), (0, n - a.shape[1])))


def _rotate_half_cols(w):
    half = w.shape[1] // 2
    return jnp.concatenate([-w[:, half:], w[:, :half]], axis=1)


def prepare_layer_weights(w_in, q_norm, w_uq, kv_norm, w_ukv, w_branch):
    d = w_in.shape[0]
    q_lora, kv_lora = q_norm.shape[0], kv_norm.shape[0]
    sizes = (q_lora, kv_lora, ROPE_DIM, 3 * H_SB * HEAD_DIM, 3 * H_BAND * HEAD_DIM, 3 * H_DSA * HEAD_DIM,
             H_IDX * D_IDX, D_IDX, H_IDX, N_BRANCH * d)
    cols, start = [], 0
    for s in sizes:
        cols.append(w_in[:, start:start + s])
        start += s
    w_cq, w_ckv, w_kr, w_sb, w_bd, w_ds, w_iq, w_ik, w_iw, w_g = cols
    wm = jnp.concatenate([w_cq, w_ckv, _pad_cols(w_kr, LANES), _pad_cols(_rotate_half_cols(w_kr), LANES),
                          _pad_cols(w_ik, LANES), _pad_cols(w_iw, LANES)], axis=1).astype(BF)
    qa, qb = [], []
    hd = NOPE_DIM + ROPE_DIM
    for h in range(H_MLA):
        wh = w_uq[:, h * hd:(h + 1) * hd]
        qa += [wh[:, :NOPE_DIM], _pad_cols(wh[:, NOPE_DIM:], LANES)]
        qb.append(_pad_cols(_rotate_half_cols(wh[:, NOPE_DIM:]), LANES))
    wqa = jnp.concatenate(qa, axis=1).astype(BF)
    wqb = jnp.concatenate(qb, axis=1).astype(BF)
    kvd = NOPE_DIM + V_DIM
    wk = jnp.concatenate([w_ukv[:, h * kvd:h * kvd + NOPE_DIM] for h in range(H_MLA)], axis=1).astype(BF)
    wv = jnp.concatenate([w_ukv[:, h * kvd + NOPE_DIM:(h + 1) * kvd] for h in range(H_MLA)], axis=1).astype(BF)
    wg = jnp.transpose(w_g.reshape(d, N_BRANCH, d), (1, 0, 2)).astype(BF)
    return dict(wm=wm, wqa=wqa, wqb=wqb, wk=wk, wv=wv,
                w_sb=w_sb.astype(BF), w_bd=w_bd.astype(BF), w_ds=w_ds.astype(BF), w_iq=w_iq.astype(BF),
                wg=wg, wb=w_branch.astype(BF),
                qn=q_norm.reshape(1, -1), kvn=kv_norm.reshape(1, -1))


def rope_table(pos):
    half = ROPE_DIM // 2
    inv = ROPE_THETA ** (-jnp.arange(half, dtype=F32) / half)
    ang = pos.astype(F32)[:, None] * inv[None, :]
    cos, sin = jnp.cos(ang), jnp.sin(ang)
    z = jnp.zeros((pos.shape[0], LANES - ROPE_DIM), F32)
    return jnp.concatenate([cos, cos, z, sin, sin, z], axis=1)


def _kidx_pair(kidx):
    kb = kidx.astype(BF)
    return (jnp.pad(kb, ((0, 0), (0, 0), (0, LANES - D_IDX))),
            jnp.pad(kb, ((0, 0), (0, 0), (LANES - D_IDX, 0))))


def _with_past(past, new, pad_to):
    b = new.shape[0]
    a = jnp.concatenate([past.reshape(b, past.shape[1], -1).astype(BF), new.astype(BF)], axis=1)
    return jnp.pad(a, ((0, 0), (0, pad_to - a.shape[1]), (0, 0)))


def mixing_block(u, bsz, seq, lw, cs, past, band_bias, t5_table, w_out, h):
    t = bsz * seq
    ckv, krp, kidxp, idxw, qcat = proj_misc(u, lw["wm"], lw["qn"], lw["kvn"], lw["wqa"], lw["wqb"], cs)
    hw = H_SB * HEAD_DIM

    def qkv(w):
        q, = matmul(u, w[:, :hw], (BF,), name="proj_q")
        k32, kbf = matmul(u, w[:, hw:2 * hw], (F32, BF), name="proj_k")
        v32, vbf = matmul(u, w[:, 2 * hw:], (F32, BF), name="proj_v")
        return q, k32, kbf, v32, vbf

    sb_q, sb_k, sb_kb, sb_v, sb_vb = qkv(lw["w_sb"])
    bd_q, bd_k, bd_kb, bd_v, bd_vb = qkv(lw["w_bd"])
    ds_q, ds_k, ds_kb, ds_v, ds_vb = qkv(lw["w_ds"])
    idx_q, = matmul(u, lw["w_iq"], (BF,), name="proj_idxq")
    kr = krp[:, :ROPE_DIM]
    kidx = kidxp[:, :D_IDX]
    r3 = lambda a: a.reshape(bsz, seq, -1)
    band_pad = N_PREV_CHUNKS * CHUNK

    if past is None:
        kcat, vmla = mla_kv_up(ckv, krp, lw["wk"], lw["wv"])
        o_mla = mla_attention(r3(qcat), r3(kcat), r3(vmla), 0, 256, 256)
        o_sb = sb_attention(r3(sb_q), r3(sb_kb), r3(sb_vb), 0, 256)
        front = ((0, 0), (band_pad, 0), (0, 0))
        o_bd = band_attention(r3(bd_q), jnp.pad(r3(bd_kb), front), jnp.pad(r3(bd_vb), front),
                              band_bias_mask(band_bias, 256), 256, -band_pad)
        klo, khi = _kidx_pair(r3(kidx))
        o_ds = dsa_attention(r3(idx_q), r3(idxw), klo, khi, r3(ds_q), r3(ds_kb), r3(ds_vb), t5_table,
                             0, seq, LANES)
    else:
        (p_ckv, p_kr, p_sbk, p_sbv, p_bdk, p_bdv, p_dsk, p_dsv, p_kidx) = past
        past_len = p_sbk.shape[1]
        total = past_len + seq
        pad_to = -(-total // 384) * 384
        ckv_all = jnp.concatenate([p_ckv, r3(ckv)], axis=1)
        krp_all = jnp.concatenate([jnp.pad(p_kr, ((0, 0), (0, 0), (0, LANES - ROPE_DIM))), r3(krp)], axis=1)
        rows = pad_to - total
        ckv_all = jnp.pad(ckv_all, ((0, 0), (0, rows), (0, 0))).reshape(bsz * pad_to, -1)
        krp_all = jnp.pad(krp_all, ((0, 0), (0, rows), (0, 0))).reshape(bsz * pad_to, -1)
        kcat, vmla = mla_kv_up(ckv_all, krp_all, lw["wk"], lw["wv"], tm=pad_to)
        o_mla = mla_attention(r3(qcat), kcat.reshape(bsz, pad_to, -1), vmla.reshape(bsz, pad_to, -1),
                              past_len, seq, 384)
        o_sb = sb_attention(r3(sb_q), _with_past(p_sbk, r3(sb_kb), pad_to), _with_past(p_sbv, r3(sb_vb), pad_to),
                            past_len, seq)
        band_len = _band_window(seq)
        o_bd = band_attention(r3(bd_q), _with_past(p_bdk, r3(bd_kb), band_len), _with_past(p_bdv, r3(bd_vb), band_len),
                              band_bias_mask(band_bias, seq), seq, past_len - p_bdk.shape[1])
        kidx_all = jnp.pad(jnp.concatenate([p_kidx, r3(kidx)], axis=1), ((0, 0), (0, rows), (0, 0)))
        klo, khi = _kidx_pair(kidx_all)
        o_ds = dsa_attention(r3(idx_q), r3(idxw), klo, khi, r3(ds_q),
                             _with_past(p_dsk, r3(ds_kb), pad_to), _with_past(p_dsv, r3(ds_vb), pad_to),
                             t5_table, past_len, total, seq)

    f2 = lambda a: a.reshape(t, -1)
    merged = gate_merge(u, [f2(o_mla), f2(o_sb), f2(o_bd), f2(o_ds)], lw["wg"], lw["wb"])
    h_new, = matmul(merged, w_out, (F32,), residual=h, name="mix_out")
    return h_new, (ckv, kr, sb_k, sb_v, bd_k, bd_v, ds_k, ds_v, kidx)


def dense_ffn(h, un, w1, w3, w2):
    act = swiglu_up(un, w1, w3)
    out, = matmul(act, w2, (F32,), residual=h, name="ffn_down")
    return out


def moe_ffn(h, un, router_pad, w1, w3, w2):
    n_experts = w1.shape[0]
    gate = router_gate(un, router_pad, n_experts)
    for e in range(n_experts):
        act = swiglu_up(un, w1[e], w3[e])
        h, = matmul(act, w2[e], (F32,), residual=h, gate=gate, gate_col=e, name="moe_down")
    return h


def layer_step(h, p, bsz, seq, cs, past, lw, band_bias, t5_table, fw):
    u = rmsnorm(h, fw["g_mix"], BF)
    h, rows = mixing_block(u, bsz, seq, lw, cs, past, band_bias, t5_table, fw["w_out"], h)
    un = rmsnorm(h, fw["g_ffn"], BF)
    if fw["moe"]:
        h = moe_ffn(h, un, fw["router"], fw["w1"], fw["w3"], fw["w2"])
    else:
        h = dense_ffn(h, un, fw["w1"], fw["w3"], fw["w2"])
    un = rmsnorm(h, fw["g_ple"], BF)
    h = ple_update(h, p, un, fw["ple_w"], fw["ple_gate_w"])
    return h, rows


def kernel(x_prompt, x_sample, p_prompt, p_sample, cache_mla_ckv, cache_mla_krope, cache_sb_k, cache_sb_v, cache_band_k, cache_band_v, cache_dsa_k, cache_dsa_v, cache_dsa_kidx, norm_mix, w_in, mla_q_norm, mla_w_uq, mla_kv_norm, mla_w_ukv, band_rel_bias, t5_rel_bias, w_branch, w_out, norm_ffn, ffn_w1, ffn_w3, ffn_w2, moe_router, moe_w1, moe_w3, moe_w2, norm_ple, ple_w, ple_gate_w, norm_final):
    depth = w_in.shape[0]
    bp, sp, d = x_prompt.shape
    bs, ss, _ = x_sample.shape
    past_len = cache_sb_k.shape[2]
    tm = 512
    cs_p = rope_table(jnp.arange(sp))
    cs_s = rope_table(past_len + (jnp.arange(tm) % ss))
    hp = x_prompt.reshape(bp * sp, d)
    hs = x_sample.reshape(bs * ss, d)
    rows_p, rows_s = [], []
    for i in range(depth):
        lw = prepare_layer_weights(w_in[i], mla_q_norm[i], mla_w_uq[i], mla_kv_norm[i], mla_w_ukv[i], w_branch[i])
        j = i // 2
        fw = dict(g_mix=norm_mix[i], g_ffn=norm_ffn[i], g_ple=norm_ple[i], w_out=w_out[i].astype(BF),
                  ple_w=ple_w[i].astype(BF), ple_gate_w=ple_gate_w[i].astype(BF), moe=i % 2 == 1)
        if i % 2 == 0:
            fw.update(w1=ffn_w1[j].astype(BF), w3=ffn_w3[j].astype(BF), w2=ffn_w2[j].astype(BF))
        else:
            fw.update(router=_pad_cols(moe_router[j], LANES).astype(BF),
                      w1=moe_w1[j].astype(BF), w3=moe_w3[j].astype(BF), w2=moe_w2[j].astype(BF))
        past_i = (cache_mla_ckv[i], cache_mla_krope[i], cache_sb_k[i], cache_sb_v[i], cache_band_k[i],
                  cache_band_v[i], cache_dsa_k[i], cache_dsa_v[i], cache_dsa_kidx[i])
        hp, rp = layer_step(hp, p_prompt[i].reshape(bp * sp, -1).astype(BF), bp, sp, cs_p, None,
                            lw, band_rel_bias[i], t5_rel_bias, fw)
        hs, rs = layer_step(hs, p_sample[i].reshape(bs * ss, -1).astype(BF), bs, ss, cs_s, past_i,
                            lw, band_rel_bias[i], t5_rel_bias, fw)
        rows_p.append(rp)
        rows_s.append(rs)
    ones = norm_final
    y_prompt = rmsnorm(hp, ones, F32).reshape(bp, sp, d)
    y_sample = rmsnorm(hs, ones, F32).reshape(bs, ss, d)

    keep = min(N_PREV_CHUNKS * CHUNK, sp)

    def stacked(rows, n, bsz, seq, heads=None, tail=None):
        out = []
        for r in rows:
            a = r[n].reshape(bsz, seq, -1)
            if tail is not None:
                a = a[:, seq - tail:]
            if heads is not None:
                a = a.reshape(a.shape[0], a.shape[1], heads, HEAD_DIM)
            out.append(a)
        return jnp.stack(out, axis=0)

    res = [y_prompt, y_sample]
    for n, heads in ((0, None), (1, None), (2, H_SB), (3, H_SB), (4, H_BAND), (5, H_BAND),
                     (6, H_DSA), (7, H_DSA), (8, None)):
        tail = keep if n in (4, 5) else None
        res.append(stacked(rows_p, n, bp, sp, heads, tail))
        res.append(stacked(rows_s, n, bs, ss, heads))
    return tuple(res)
```

```python
import functools
import math

import jax
import jax.numpy as jnp
from jax import lax
from jax.experimental import pallas as pl
from jax.experimental.pallas import tpu as pltpu

BF = jnp.bfloat16
F32 = jnp.float32

CHUNK = 64
CHUNK_SHIFT = 6
HEAD_DIM = 128
N_BRANCH = 4
H_MLA = 4
NOPE_DIM = 128
ROPE_DIM = 64
V_DIM = 128
ROPE_THETA = 10000.0
H_SB = 4
H_BAND = 4
N_PREV_CHUNKS = 8
REL_CLIP = 128
H_DSA = 4
H_IDX = 16
D_IDX = 64
TOPK_MAX = 256
T5_BUCKETS = 32
T5_MAX_DIST = 128
TOP_K_EXPERTS = 2
EPS = 1e-6

LANES = 128
PROMPT_TQ = 256
PROMPT_TK = 512
SAMPLE_TK = 384
VMEM_LIMIT = 56 * 1024 * 1024
NEG = -1e30
INT_MIN = -2147483648


def _cparams(n_axes):
    return pltpu.CompilerParams(dimension_semantics=("arbitrary",) * n_axes,
                                vmem_limit_bytes=VMEM_LIMIT)


def _tile(n, preferred):
    t = min(preferred, n)
    while n % t:
        t //= 2
    assert t == n or t % LANES == 0
    return t


def _dot(a, b):
    return jnp.dot(a, b, preferred_element_type=F32)


def _dot_nt(a, b):
    return lax.dot_general(a, b, (((1,), (1,)), ((), ())), preferred_element_type=F32)


def _sigmoid(x):
    return 1.0 / (1.0 + jnp.exp(-x))


def _rms(x, g):
    return x * lax.rsqrt(jnp.mean(x * x, axis=-1, keepdims=True) + EPS) * g


def _rmsnorm_kernel(x_ref, g_ref, o_ref):
    o_ref[...] = _rms(x_ref[...], g_ref[...]).astype(o_ref.dtype)


def rmsnorm(x, g, out_dtype, tm=512):
    t, d = x.shape
    tm = _tile(t, tm)
    return pl.pallas_call(
        _rmsnorm_kernel,
        grid=(t // tm,),
        in_specs=[pl.BlockSpec((tm, d), lambda i: (i, 0)),
                  pl.BlockSpec((1, d), lambda i: (0, 0))],
        out_specs=pl.BlockSpec((tm, d), lambda i: (i, 0)),
        out_shape=jax.ShapeDtypeStruct((t, d), out_dtype),
        compiler_params=_cparams(1),
        name="rmsnorm",
    )(x, g.reshape(1, d))


def _mm_kernel(*refs, has_res, gate_col, n_out):
    a_ref, w_ref = refs[0], refs[1]
    pos = 2
    res_ref = gate_ref = None
    if has_res:
        res_ref = refs[pos]
        pos += 1
    if gate_col is not None:
        gate_ref = refs[pos]
        pos += 1
    r = _dot(a_ref[...], w_ref[...])
    if gate_ref is not None:
        r = r * gate_ref[:, gate_col:gate_col + 1]
    if res_ref is not None:
        r = res_ref[...] + r
    for o_ref in refs[pos:pos + n_out]:
        o_ref[...] = r.astype(o_ref.dtype)


def matmul(a, w, out_dtypes, residual=None, gate=None, gate_col=None, tm=512, tn=512, name="mm"):
    m, k = a.shape
    n = w.shape[1]
    tm, tn = _tile(m, tm), _tile(n, tn)
    in_specs = [pl.BlockSpec((tm, k), lambda j, i: (i, 0)),
                pl.BlockSpec((k, tn), lambda j, i: (0, j))]
    args = [a, w]
    if residual is not None:
        in_specs.append(pl.BlockSpec((tm, tn), lambda j, i: (i, j)))
        args.append(residual)
    if gate is not None:
        in_specs.append(pl.BlockSpec((tm, gate.shape[1]), lambda j, i: (i, 0)))
        args.append(gate)
    outs = pl.pallas_call(
        functools.partial(_mm_kernel, has_res=residual is not None,
                          gate_col=gate_col if gate is not None else None, n_out=len(out_dtypes)),
        grid=(n // tn, m // tm),
        in_specs=in_specs,
        out_specs=[pl.BlockSpec((tm, tn), lambda j, i: (i, j)) for _ in out_dtypes],
        out_shape=[jax.ShapeDtypeStruct((m, n), dt) for dt in out_dtypes],
        compiler_params=_cparams(2),
        name=name,
    )(*args)
    return outs


def _swiglu_up_kernel(a_ref, w1_ref, w3_ref, o_ref):
    a = a_ref[...]
    x1 = _dot(a, w1_ref[...])
    x3 = _dot(a, w3_ref[...])
    o_ref[...] = (x1 * _sigmoid(x1) * x3).astype(o_ref.dtype)


def swiglu_up(a, w1, w3, tm=512, tn=512):
    m, k = a.shape
    n = w1.shape[1]
    tm, tn = _tile(m, tm), _tile(n, tn)
    return pl.pallas_call(
        _swiglu_up_kernel,
        grid=(n // tn, m // tm),
        in_specs=[pl.BlockSpec((tm, k), lambda j, i: (i, 0)),
                  pl.BlockSpec((k, tn), lambda j, i: (0, j)),
                  pl.BlockSpec((k, tn), lambda j, i: (0, j))],
        out_specs=pl.BlockSpec((tm, tn), lambda j, i: (i, j)),
        out_shape=jax.ShapeDtypeStruct((m, n), BF),
        compiler_params=_cparams(2),
        name="swiglu_up",
    )(a, w1, w3)


def _gate_merge_kernel(u_ref, o0_ref, o1_ref, o2_ref, o3_ref, wg_ref, wb_ref, out_ref):
    u = u_ref[...]
    acc = None
    for b, o_ref in enumerate((o0_ref, o1_ref, o2_ref, o3_ref)):
        t = _sigmoid(_dot(u, wg_ref[b])) * _dot(o_ref[...], wb_ref[b])
        acc = t if acc is None else acc + t
    out_ref[...] = acc.astype(out_ref.dtype)


def gate_merge(u, branches, wg, wb, tm=512, tn=256):
    t, d = u.shape
    bw = branches[0].shape[1]
    n = wg.shape[2]
    tm = min(tm, t)
    return pl.pallas_call(
        _gate_merge_kernel,
        grid=(n // tn, t // tm),
        in_specs=[pl.BlockSpec((tm, d), lambda j, i: (i, 0))]
        + [pl.BlockSpec((tm, bw), lambda j, i: (i, 0)) for _ in range(N_BRANCH)]
        + [pl.BlockSpec((N_BRANCH, d, tn), lambda j, i: (0, 0, j)),
           pl.BlockSpec((N_BRANCH, bw, tn), lambda j, i: (0, 0, j))],
        out_specs=pl.BlockSpec((tm, tn), lambda j, i: (i, j)),
        out_shape=jax.ShapeDtypeStruct((t, n), BF),
        compiler_params=_cparams(2),
        name="gate_merge",
    )(u, *branches, wg, wb)


def _ple_kernel(h_ref, p_ref, un_ref, wp_ref, wg_ref, o_ref):
    o_ref[...] = h_ref[...] + _dot(p_ref[...], wp_ref[...]) * _sigmoid(_dot(un_ref[...], wg_ref[...]))


def ple_update(h, p, un, wp, wg, tm=512, tn=512):
    t, d = h.shape
    tm = min(tm, t)
    return pl.pallas_call(
        _ple_kernel,
        grid=(d // tn, t // tm),
        in_specs=[pl.BlockSpec((tm, tn), lambda j, i: (i, j)),
                  pl.BlockSpec((tm, p.shape[1]), lambda j, i: (i, 0)),
                  pl.BlockSpec((tm, d), lambda j, i: (i, 0)),
                  pl.BlockSpec((p.shape[1], tn), lambda j, i: (0, j)),
                  pl.BlockSpec((d, tn), lambda j, i: (0, j))],
        out_specs=pl.BlockSpec((tm, tn), lambda j, i: (i, j)),
        out_shape=jax.ShapeDtypeStruct((t, d), F32),
        compiler_params=_cparams(2),
        name="ple_update",
    )(h, p, un, wp, wg)


def _router_kernel(un_ref, wr_ref, g_ref, *, n_experts):
    logits = _dot(un_ref[...], wr_ref[...])
    lane = lax.broadcasted_iota(jnp.int32, logits.shape, 1).astype(F32)
    real = lane < n_experts
    logits = jnp.where(real, logits, NEG)
    e = jnp.where(real, jnp.exp(logits - jnp.max(logits, axis=1, keepdims=True)), 0.0)
    probs = e / jnp.sum(e, axis=1, keepdims=True)
    p1 = jnp.max(probs, axis=1, keepdims=True)
    i1 = jnp.min(jnp.where(probs == p1, lane, float(LANES)), axis=1, keepdims=True)
    first = lane == i1
    rest = jnp.where(first | ~real, -1.0, probs)
    p2 = jnp.max(rest, axis=1, keepdims=True)
    i2 = jnp.min(jnp.where(rest == p2, lane, float(LANES)), axis=1, keepdims=True)
    second = lane == i2
    denom = p1 + p2
    g_ref[...] = jnp.where(first, p1 / denom, 0.0) + jnp.where(second, p2 / denom, 0.0)


def router_gate(un, wr_pad, n_experts, tm=512):
    t, d = un.shape
    tm = min(tm, t)
    return pl.pallas_call(
        functools.partial(_router_kernel, n_experts=n_experts),
        grid=(t // tm,),
        in_specs=[pl.BlockSpec((tm, d), lambda i: (i, 0)),
                  pl.BlockSpec((d, LANES), lambda i: (0, 0))],
        out_specs=pl.BlockSpec((tm, LANES), lambda i: (i, 0)),
        out_shape=jax.ShapeDtypeStruct((t, LANES), F32),
        compiler_params=_cparams(1),
        name="router_gate",
    )(un, wr_pad)


Q_LORA_OFF = 0


def _proj_misc_kernel(u_ref, wm_ref, qn_ref, kvn_ref, wqa_ref, wqb_ref, cs_ref,
                      ckv_ref, krp_ref, kidx_ref, idxw_ref, qcat_ref, *, q_lora, kv_lora):
    x = _dot(u_ref[...], wm_ref[...])
    cos = cs_ref[:, 0:LANES]
    sin = cs_ref[:, LANES:2 * LANES]
    o = q_lora
    ckv_ref[...] = _rms(x[:, o:o + kv_lora], kvn_ref[...])
    o += kv_lora
    krp_ref[...] = x[:, o:o + LANES] * cos + x[:, o + LANES:o + 2 * LANES] * sin
    o += 2 * LANES
    kidx_ref[...] = x[:, o:o + LANES]
    idxw_ref[...] = x[:, o + LANES:o + 2 * LANES]
    cqn = _rms(x[:, 0:q_lora], qn_ref[...]).astype(BF)
    qa = _dot(cqn, wqa_ref[...])
    qb = _dot(cqn, wqb_ref[...])
    for h in range(H_MLA):
        lo = 2 * h * LANES
        qcat_ref[:, lo:lo + LANES] = qa[:, lo:lo + LANES].astype(BF)
        qcat_ref[:, lo + LANES:lo + 2 * LANES] = (
            qa[:, lo + LANES:lo + 2 * LANES] * cos + qb[:, h * LANES:(h + 1) * LANES] * sin).astype(BF)


def proj_misc(u, wm, qn, kvn, wqa, wqb, cs, tm=512):
    t, d = u.shape
    tm = min(tm, t)
    q_lora, kv_lora = qn.shape[1], kvn.shape[1]
    n_pos_blocks = cs.shape[0] // tm
    row = lambda n: pl.BlockSpec((tm, n), lambda i: (i, 0))
    full = lambda a: pl.BlockSpec(a.shape, lambda i: (0, 0))
    return pl.pallas_call(
        functools.partial(_proj_misc_kernel, q_lora=q_lora, kv_lora=kv_lora),
        grid=(t // tm,),
        in_specs=[row(d), full(wm), full(qn), full(kvn), full(wqa), full(wqb),
                  pl.BlockSpec((tm, 2 * LANES), lambda i: (i % n_pos_blocks, 0))],
        out_specs=[row(kv_lora), row(LANES), row(LANES), row(LANES), row(2 * LANES * H_MLA)],
        out_shape=[jax.ShapeDtypeStruct((t, kv_lora), F32),
                   jax.ShapeDtypeStruct((t, LANES), F32),
                   jax.ShapeDtypeStruct((t, LANES), F32),
                   jax.ShapeDtypeStruct((t, LANES), F32),
                   jax.ShapeDtypeStruct((t, 2 * LANES * H_MLA), BF)],
        compiler_params=_cparams(1),
        name="proj_misc",
    )(u, wm, qn, kvn, wqa, wqb, cs)


def _mla_kv_up_kernel(ckv_ref, krp_ref, wk_ref, wv_ref, kcat_ref, v_ref):
    c = ckv_ref[...].astype(BF)
    kn = _dot(c, wk_ref[...])
    krp = krp_ref[...].astype(BF)
    for h in range(H_MLA):
        kcat_ref[:, 2 * h * LANES:(2 * h + 1) * LANES] = kn[:, h * LANES:(h + 1) * LANES].astype(BF)
        kcat_ref[:, (2 * h + 1) * LANES:(2 * h + 2) * LANES] = krp
    v_ref[...] = _dot(c, wv_ref[...]).astype(BF)


def mla_kv_up(ckv, krp, wk, wv, tm=512):
    t, c = ckv.shape
    tm = min(tm, t)
    row = lambda n: pl.BlockSpec((tm, n), lambda i: (i, 0))
    full = lambda a: pl.BlockSpec(a.shape, lambda i: (0, 0))
    return pl.pallas_call(
        _mla_kv_up_kernel,
        grid=(t // tm,),
        in_specs=[row(c), row(LANES), full(wk), full(wv)],
        out_specs=[row(2 * LANES * H_MLA), row(V_DIM * H_MLA)],
        out_shape=[jax.ShapeDtypeStruct((t, 2 * LANES * H_MLA), BF),
                   jax.ShapeDtypeStruct((t, V_DIM * H_MLA), BF)],
        compiler_params=_cparams(1),
        name="mla_kv_up",
    )(ckv, krp, wk, wv)


def _mla_attn_kernel(q_ref, k_ref, v_ref, o_ref, m_ref, l_ref, acc_ref, *, tq, tk, q_off, scale):
    q0 = q_off + pl.program_id(2) * tq
    qpos = q0 + lax.broadcasted_iota(jnp.int32, (tq, 1), 0)
    limit = (lax.shift_right_arithmetic(qpos, CHUNK_SHIFT) + 1) * CHUNK
    last_limit = ((q0 + tq - 1) // CHUNK + 1) * CHUNK
    n_chunks = (last_limit + tk - 1) // tk
    m_ref[...] = jnp.full(m_ref.shape, NEG, F32)
    l_ref[...] = jnp.zeros(l_ref.shape, F32)
    acc_ref[...] = jnp.zeros(acc_ref.shape, F32)
    q = q_ref[...]

    def body(kc, carry):
        ks = pl.multiple_of(kc * tk, tk)
        s = _dot_nt(q, k_ref[pl.ds(ks, tk), :]) * scale
        kpos = ks + lax.broadcasted_iota(jnp.int32, (1, tk), 1)
        s = jnp.where(kpos < limit, s, NEG)
        m_prev = m_ref[...]
        m_new = jnp.maximum(m_prev, jnp.max(s, axis=1, keepdims=True))
        p = jnp.exp(s - m_new)
        alpha = jnp.exp(m_prev - m_new)
        l_ref[...] = alpha * l_ref[...] + jnp.sum(p, axis=1, keepdims=True)
        acc_ref[...] = alpha * acc_ref[...] + _dot(p.astype(BF), v_ref[pl.ds(ks, tk), :])
        m_ref[...] = m_new
        return carry

    lax.fori_loop(0, n_chunks, body, 0)
    o_ref[...] = (acc_ref[...] / l_ref[...]).astype(o_ref.dtype)


def mla_attention(qcat, kcat, v, q_off, tq, tk):
    b, sq, _ = qcat.shape
    sk = kcat.shape[1]
    scale = (NOPE_DIM + ROPE_DIM) ** -0.5
    return pl.pallas_call(
        functools.partial(_mla_attn_kernel, tq=tq, tk=tk, q_off=q_off, scale=scale),
        grid=(b, H_MLA, sq // tq),
        in_specs=[pl.BlockSpec((None, tq, 2 * LANES), lambda bi, h, qi: (bi, qi, h)),
                  pl.BlockSpec((None, sk, 2 * LANES), lambda bi, h, qi: (bi, 0, h)),
                  pl.BlockSpec((None, sk, V_DIM), lambda bi, h, qi: (bi, 0, h))],
        out_specs=pl.BlockSpec((None, tq, V_DIM), lambda bi, h, qi: (bi, qi, h)),
        out_shape=jax.ShapeDtypeStruct((b, sq, H_MLA * V_DIM), BF),
        scratch_shapes=[pltpu.VMEM((tq, 1), F32), pltpu.VMEM((tq, 1), F32), pltpu.VMEM((tq, V_DIM), F32)],
        compiler_params=_cparams(3),
        name="mla_attention",
    )(qcat, kcat, v)


def _split3(x):
    hi = x.astype(BF)
    r = x - hi.astype(F32)
    mid = r.astype(BF)
    lo = (r - mid.astype(F32)).astype(BF)
    return hi, mid, lo


def _sb_attn_kernel(q_ref, k_ref, v_ref, tri_ref, o_ref, acc_ref, carry_ref, *, tq, tk, q_off, scale):
    q0 = q_off + pl.program_id(2) * tq
    qpos = q0 + lax.broadcasted_iota(jnp.int32, (tq, 1), 0)
    n_chunks = (q0 + tq - 1 + tk - 1) // tk
    acc_ref[...] = jnp.zeros(acc_ref.shape, F32)
    carry_ref[...] = jnp.zeros(carry_ref.shape, F32)
    q = q_ref[...]
    tri = tri_ref[...]

    def body(it, carry):
        kc = n_chunks - 1 - it
        ks = pl.multiple_of(kc * tk, tk)
        z = _dot_nt(q, k_ref[pl.ds(ks, tk), :]) * scale
        kpos = ks + lax.broadcasted_iota(jnp.int32, (1, tk), 1)
        strict = kpos < qpos
        sp = jnp.maximum(z, 0.0) + jnp.log(1.0 + jnp.exp(-jnp.abs(z)))
        log_not = jnp.where(strict, -sp, 0.0)
        hi, mid, lo = _split3(log_not)
        later = carry_ref[...]
        after = [None] * (tk // LANES)
        for blk in reversed(range(tk // LANES)):
            sl = slice(blk * LANES, (blk + 1) * LANES)
            sums = _dot(hi[:, sl], tri) + _dot(mid[:, sl], tri) + _dot(lo[:, sl], tri)
            after[blk] = sums[:, 0:LANES] + later
            later = later + sums[:, LANES:2 * LANES]
        carry_ref[...] = later
        a = jnp.where(strict, jnp.exp((z - sp) + jnp.concatenate(after, axis=1)), 0.0)
        acc_ref[...] += _dot(a.astype(BF), v_ref[pl.ds(ks, tk), :])
        return carry

    lax.fori_loop(0, n_chunks, body, 0)
    o_ref[...] = acc_ref[...].astype(o_ref.dtype)


def sb_attention(q, k, v, q_off, tq, tk):
    b, sq, _ = q.shape
    sk = k.shape[1]
    assert sk % tk == 0 and tk % LANES == 0
    j = lax.broadcasted_iota(jnp.int32, (LANES, LANES), 0)
    s = lax.broadcasted_iota(jnp.int32, (LANES, LANES), 1)
    tri = jnp.concatenate([(j > s).astype(BF), jnp.ones((LANES, LANES), BF)], axis=1)
    return pl.pallas_call(
        functools.partial(_sb_attn_kernel, tq=tq, tk=tk, q_off=q_off, scale=HEAD_DIM ** -0.5),
        grid=(b, H_SB, sq // tq),
        in_specs=[pl.BlockSpec((None, tq, HEAD_DIM), lambda bi, h, qi: (bi, qi, h)),
                  pl.BlockSpec((None, sk, HEAD_DIM), lambda bi, h, qi: (bi, 0, h)),
                  pl.BlockSpec((None, sk, HEAD_DIM), lambda bi, h, qi: (bi, 0, h)),
                  pl.BlockSpec((LANES, 2 * LANES), lambda bi, h, qi: (0, 0))],
        out_specs=pl.BlockSpec((None, tq, HEAD_DIM), lambda bi, h, qi: (bi, qi, h)),
        out_shape=jax.ShapeDtypeStruct((b, sq, H_SB * HEAD_DIM), BF),
        scratch_shapes=[pltpu.VMEM((tq, HEAD_DIM), F32), pltpu.VMEM((tq, LANES), F32)],
        compiler_params=_cparams(3),
        name="sb_attention",
    )(q, k, v, tri)


def _band_attn_kernel(q_ref, k_ref, v_ref, bm_ref, o_ref, *, tq, win, kpos_base, scale):
    w0 = pl.multiple_of(pl.program_id(2) * tq, tq)
    k = k_ref[pl.ds(w0, win), :]
    s = _dot_nt(q_ref[...], k) * scale + bm_ref[...]
    kpos = kpos_base + w0 + lax.broadcasted_iota(jnp.int32, (1, win), 1)
    s = jnp.where(kpos >= 0, s, NEG)
    p = jnp.exp(s - jnp.max(s, axis=1, keepdims=True))
    denom = jnp.sum(p, axis=1, keepdims=True)
    o = _dot(p.astype(BF), v_ref[pl.ds(w0, win), :])
    o_ref[...] = (o / denom).astype(o_ref.dtype)


def _band_window(tq):
    return -(-(tq + N_PREV_CHUNKS * CHUNK) // LANES) * LANES


def band_bias_mask(rel_bias, tq):
    win = _band_window(tq)
    i = jnp.arange(tq)[:, None]
    j = jnp.arange(win)[None, :]
    rel = jnp.clip(i + N_PREV_CHUNKS * CHUNK - j, -REL_CLIP, REL_CLIP) + REL_CLIP
    ci, cj = i // CHUNK, j // CHUNK
    inside = (cj >= ci) & (cj <= ci + N_PREV_CHUNKS)
    return jnp.where(inside[None], _table_lookup(rel_bias, rel), NEG)


def band_attention(q, k_pad, v_pad, bias_mask, tq, kpos_base):
    b, sq, _ = q.shape
    skp = k_pad.shape[1]
    win = _band_window(tq)
    assert skp >= sq - tq + win
    return pl.pallas_call(
        functools.partial(_band_attn_kernel, tq=tq, win=win, kpos_base=kpos_base, scale=HEAD_DIM ** -0.5),
        grid=(b, H_BAND, sq // tq),
        in_specs=[pl.BlockSpec((None, tq, HEAD_DIM), lambda bi, h, qi: (bi, qi, h)),
                  pl.BlockSpec((None, skp, HEAD_DIM), lambda bi, h, qi: (bi, 0, h)),
                  pl.BlockSpec((None, skp, HEAD_DIM), lambda bi, h, qi: (bi, 0, h)),
                  pl.BlockSpec((None, tq, win), lambda bi, h, qi: (h, 0, 0))],
        out_specs=pl.BlockSpec((None, tq, HEAD_DIM), lambda bi, h, qi: (bi, qi, h)),
        out_shape=jax.ShapeDtypeStruct((b, sq, H_BAND * HEAD_DIM), BF),
        compiler_params=_cparams(3),
        name="band_attention",
    )(q, k_pad, v_pad, bias_mask)


def _sortable(x):
    i = lax.bitcast_convert_type(x, jnp.int32)
    return i ^ (lax.shift_right_arithmetic(i, 31) & 0x7FFFFFFF)


def _dsa_kernel(qi2_ref, w_ref, klo_ref, khi_ref, q_ref, k_ref, v_ref, bnear_ref, bfar_ref, tri_ref,
                o_ref, key_ref, wb_ref, m_ref, l_ref, acc_ref, eqc_ref,
                *, tq, tk, big, q_off, topk, scale, w_scale):
    wide = big * tk
    q0 = q_off + pl.program_id(1) * tq
    qpos = q0 + lax.broadcasted_iota(jnp.int32, (tq, 1), 0)
    limit = (lax.shift_right_arithmetic(qpos, CHUNK_SHIFT) + 1) * CHUNK
    last_limit = ((q0 + tq - 1) // CHUNK + 1) * CHUNK
    n_wide = (last_limit + wide - 1) // wide
    diag = q0 // tk

    w = w_ref[...] * w_scale
    for h in range(H_IDX):
        wb_ref[h] = jnp.broadcast_to(w[:, h:h + 1], (tq, tk))

    def score_body(c, carry):
        ks = pl.multiple_of(c * wide, wide)
        klo = klo_ref[pl.ds(ks, wide), :]
        khi = khi_ref[pl.ds(ks, wide), :]
        acc = [jnp.zeros((tq, tk), F32) for _ in range(big)]
        for pair in range(H_IDX // 2):
            q2 = qi2_ref[:, pair * LANES:(pair + 1) * LANES]
            l0 = jnp.maximum(_dot_nt(q2, klo), 0.0)
            l1 = jnp.maximum(_dot_nt(q2, khi), 0.0)
            for blk in range(big):
                sl = slice(blk * tk, (blk + 1) * tk)
                acc[blk] = acc[blk] + wb_ref[2 * pair] * l0[:, sl] + wb_ref[2 * pair + 1] * l1[:, sl]
        for blk in range(big):
            kpos = ks + blk * tk + lax.broadcasted_iota(jnp.int32, (1, tk), 1)
            key_ref[c * big + blk] = _sortable(jnp.where(kpos < limit, acc[blk] + 0.0, -jnp.inf))
        return carry

    lax.fori_loop(0, n_wide, score_body, 0)

    def count_ge(cand):
        def body(c, cnt):
            for blk in range(big):
                cnt = cnt + jnp.where(key_ref[c * big + blk] >= cand, 1.0, 0.0)
            return cnt
        cnt = lax.fori_loop(0, n_wide, body, jnp.zeros((tq, tk), F32))
        return jnp.sum(cnt, axis=1, keepdims=True)

    def bit_body(it, thr):
        cand = thr + lax.shift_left(jnp.int32(1), 31 - it)
        return jnp.where(count_ge(cand) >= topk, cand, thr)

    thr = lax.fori_loop(0, 32, bit_body, jnp.full((tq, 1), INT_MIN, jnp.int32))
    n_above = count_ge(thr + 1)
    n_ties_kept = topk - n_above

    m_ref[...] = jnp.full(m_ref.shape, NEG, F32)
    l_ref[...] = jnp.zeros(l_ref.shape, F32)
    acc_ref[...] = jnp.zeros(acc_ref.shape, F32)
    eqc_ref[...] = jnp.zeros(eqc_ref.shape, F32)
    tri = tri_ref[...]

    def attend(first_blk, n_blk, bias_of_head, below_tile):
        width = n_blk * tk
        ks = pl.multiple_of(first_blk * tk, tk)
        key = jnp.concatenate([key_ref[first_blk + i] for i in range(n_blk)], axis=1)
        eq = key == thr
        eq_bf = jnp.where(eq, 1.0, 0.0).astype(BF)
        seen = eqc_ref[...]
        rank = [None] * n_blk
        for i in range(n_blk):
            counts = _dot(eq_bf[:, i * tk:(i + 1) * tk], tri)
            rank[i] = counts[:, 0:tk] + seen
            seen = seen + counts[:, tk:2 * tk]
        eqc_ref[...] = seen
        sel = (key > thr) | (eq & (jnp.concatenate(rank, axis=1) <= n_ties_kept))
        if not below_tile:
            kpos = ks + lax.broadcasted_iota(jnp.int32, (1, width), 1)
            sel = sel & (kpos < limit)
        for h in range(H_DSA):
            hs = slice(h * HEAD_DIM, (h + 1) * HEAD_DIM)
            s = _dot_nt(q_ref[:, hs], k_ref[pl.ds(ks, width), hs]) * scale + bias_of_head(h)
            s = jnp.where(sel, s, NEG)
            m_prev = m_ref[h]
            m_new = jnp.maximum(m_prev, jnp.max(s, axis=1, keepdims=True))
            p = jnp.where(sel, jnp.exp(s - m_new), 0.0)
            alpha = jnp.exp(m_prev - m_new)
            l_ref[h] = alpha * l_ref[h] + jnp.sum(p, axis=1, keepdims=True)
            acc_ref[h] = alpha * acc_ref[h] + _dot(p.astype(BF), v_ref[pl.ds(ks, width), hs])
            m_ref[h] = m_new

    far_bias = lambda h: bfar_ref[h][:, 0:1]
    n_far_wide = jnp.maximum(q0 - tk, 0) // wide

    def far_wide_body(c, carry):
        attend(c * big, big, far_bias, True)
        return carry

    lax.fori_loop(0, n_far_wide, far_wide_body, 0)

    def far_body(kc, carry):
        attend(kc, 1, far_bias, True)
        return carry

    lax.fori_loop(n_far_wide * big, jnp.maximum(diag - 1, n_far_wide * big), far_body, 0)

    @pl.when(diag >= 1)
    def _():
        attend(diag - 1, 1, lambda h: bnear_ref[1, h], True)

    attend(diag, 1, lambda h: bnear_ref[0, h], False)

    for h in range(H_DSA):
        o_ref[:, h * HEAD_DIM:(h + 1) * HEAD_DIM] = (acc_ref[h] / l_ref[h]).astype(o_ref.dtype)


def t5_bucket(rel):
    half = T5_BUCKETS // 2
    max_exact = half // 2
    n = jnp.abs(rel)
    nf = jnp.maximum(n, 1).astype(F32)
    large = max_exact + (jnp.log(nf / max_exact) / math.log(T5_MAX_DIST / max_exact)
                         * (half - max_exact)).astype(jnp.int32)
    large = jnp.minimum(large, half - 1)
    return jnp.where(rel > 0, half, 0) + jnp.where(n < max_exact, n, large)


def _table_lookup(table, idx):
    onehot = jax.nn.one_hot(idx, table.shape[0], dtype=F32)
    out = jnp.einsum("...n,nh->...h", onehot, table.astype(F32), precision=lax.Precision.HIGHEST)
    return jnp.moveaxis(out, -1, 0)


def dsa_bias_tables(t5_table, tq, tk=LANES):
    i = jnp.arange(tq)[:, None]
    j = jnp.arange(tk)[None, :]
    near = jnp.stack([_table_lookup(t5_table, t5_bucket(j - i - d * tk)) for d in range(2)], axis=0)
    far_rel = -jnp.ones((1, tk), jnp.int32) * (2 * tk)
    far = _table_lookup(t5_table, t5_bucket(far_rel))
    return near, far


def dsa_attention(qidx2, idxw, klo, khi, q, k, v, bias_tables, q_off, n_keys, tq, big, tk=LANES):
    b, sq, _ = q.shape
    sk = k.shape[1]
    assert tk >= T5_MAX_DIST and q_off % tk == 0 and (tq == tk or sq == tq) and sk % (big * tk) == 0
    topk = min(TOPK_MAX, n_keys // 4)
    near, far = bias_tables
    jj = lax.broadcasted_iota(jnp.int32, (tk, tk), 0)
    ss = lax.broadcasted_iota(jnp.int32, (tk, tk), 1)
    tri = jnp.concatenate([(jj <= ss).astype(BF), jnp.ones((tk, tk), BF)], axis=1)
    whole = lambda n: pl.BlockSpec((None, sk, n), lambda bi, qi: (bi, 0, 0))
    tile = lambda n: pl.BlockSpec((None, tq, n), lambda bi, qi: (bi, qi, 0))
    const = lambda a: pl.BlockSpec(a.shape, lambda bi, qi: (0,) * a.ndim)
    return pl.pallas_call(
        functools.partial(_dsa_kernel, tq=tq, tk=tk, big=big, q_off=q_off, topk=topk,
                          scale=HEAD_DIM ** -0.5, w_scale=H_IDX ** -0.5 * D_IDX ** -0.5),
        grid=(b, sq // tq),
        in_specs=[tile(qidx2.shape[2]), tile(LANES), whole(LANES), whole(LANES),
                  tile(H_DSA * HEAD_DIM), whole(H_DSA * HEAD_DIM), whole(H_DSA * HEAD_DIM),
                  const(near), const(far), const(tri)],
        out_specs=tile(H_DSA * HEAD_DIM),
        out_shape=jax.ShapeDtypeStruct((b, sq, H_DSA * HEAD_DIM), BF),
        scratch_shapes=[pltpu.VMEM((sk // tk, tq, tk), jnp.int32), pltpu.VMEM((H_IDX, tq, tk), F32),
                        pltpu.VMEM((H_DSA, tq, 1), F32), pltpu.VMEM((H_DSA, tq, 1), F32),
                        pltpu.VMEM((H_DSA, tq, HEAD_DIM), F32), pltpu.VMEM((tq, tk), F32)],
        compiler_params=_cparams(2),
        name="dsa_attention",
    )(qidx2, idxw, klo, khi, q, k, v, near, far, tri)


def _pad_cols(a, n):
    return jnp.pad(a, ((0, 0), (0, n - a.shape[1])))


def _rotate_half_cols(w):
    half = w.shape[1] // 2
    return jnp.concatenate([-w[:, half:], w[:, :half]], axis=1)


def prepare_layer_weights(w_in, q_norm, w_uq, kv_norm, w_ukv, w_branch):
    d = w_in.shape[0]
    q_lora, kv_lora = q_norm.shape[0], kv_norm.shape[0]
    sizes = (q_lora, kv_lora, ROPE_DIM, 3 * H_SB * HEAD_DIM, 3 * H_BAND * HEAD_DIM, 3 * H_DSA * HEAD_DIM,
             H_IDX * D_IDX, D_IDX, H_IDX, N_BRANCH * d)
    cols, start = [], 0
    for s in sizes:
        cols.append(w_in[:, start:start + s])
        start += s
    w_cq, w_ckv, w_kr, w_sb, w_bd, w_ds, w_iq, w_ik, w_iw, w_g = cols
    wm = jnp.concatenate([w_cq, w_ckv, _pad_cols(w_kr, LANES), _pad_cols(_rotate_half_cols(w_kr), LANES),
                          _pad_cols(w_ik, LANES), _pad_cols(w_iw, LANES)], axis=1).astype(BF)
    qa, qb = [], []
    hd = NOPE_DIM + ROPE_DIM
    for h in range(H_MLA):
        wh = w_uq[:, h * hd:(h + 1) * hd]
        qa += [wh[:, :NOPE_DIM], _pad_cols(wh[:, NOPE_DIM:], LANES)]
        qb.append(_pad_cols(_rotate_half_cols(wh[:, NOPE_DIM:]), LANES))
    wqa = jnp.concatenate(qa, axis=1).astype(BF)
    wqb = jnp.concatenate(qb, axis=1).astype(BF)
    kvd = NOPE_DIM + V_DIM
    wk = jnp.concatenate([w_ukv[:, h * kvd:h * kvd + NOPE_DIM] for h in range(H_MLA)], axis=1).astype(BF)
    wv = jnp.concatenate([w_ukv[:, h * kvd + NOPE_DIM:(h + 1) * kvd] for h in range(H_MLA)], axis=1).astype(BF)
    wg = jnp.transpose(w_g.reshape(d, N_BRANCH, d), (1, 0, 2)).astype(BF)
    return dict(wm=wm, wqa=wqa, wqb=wqb, wk=wk, wv=wv,
                w_sb=w_sb.astype(BF), w_bd=w_bd.astype(BF), w_ds=w_ds.astype(BF), w_iq=w_iq.astype(BF),
                wg=wg, wb=w_branch.astype(BF),
                qn=q_norm.reshape(1, -1), kvn=kv_norm.reshape(1, -1))


def rope_table(pos):
    half = ROPE_DIM // 2
    inv = ROPE_THETA ** (-jnp.arange(half, dtype=F32) / half)
    ang = pos.astype(F32)[:, None] * inv[None, :]
    cos, sin = jnp.cos(ang), jnp.sin(ang)
    z = jnp.zeros((pos.shape[0], LANES - ROPE_DIM), F32)
    return jnp.concatenate([cos, cos, z, sin, sin, z], axis=1)


def _kidx_pair(kidx):
    kb = kidx.astype(BF)
    return (jnp.pad(kb, ((0, 0), (0, 0), (0, LANES - D_IDX))),
            jnp.pad(kb, ((0, 0), (0, 0), (LANES - D_IDX, 0))))


def _with_past(past, new, pad_to):
    b = new.shape[0]
    a = jnp.concatenate([past.reshape(b, past.shape[1], -1).astype(BF), new.astype(BF)], axis=1)
    return jnp.pad(a, ((0, 0), (0, pad_to - a.shape[1]), (0, 0)))


def mixing_block(u, bsz, seq, lw, cs, past, band_mask, dsa_tables, w_out, h):
    t = bsz * seq
    ckv, krp, kidxp, idxw, qcat = proj_misc(u, lw["wm"], lw["qn"], lw["kvn"], lw["wqa"], lw["wqb"], cs)
    hw = H_SB * HEAD_DIM

    def qkv(w):
        q, = matmul(u, w[:, :hw], (BF,), name="proj_q")
        k32, kbf = matmul(u, w[:, hw:2 * hw], (F32, BF), name="proj_k")
        v32, vbf = matmul(u, w[:, 2 * hw:], (F32, BF), name="proj_v")
        return q, k32, kbf, v32, vbf

    sb_q, sb_k, sb_kb, sb_v, sb_vb = qkv(lw["w_sb"])
    bd_q, bd_k, bd_kb, bd_v, bd_vb = qkv(lw["w_bd"])
    ds_q, ds_k, ds_kb, ds_v, ds_vb = qkv(lw["w_ds"])
    idx_q, = matmul(u, lw["w_iq"], (BF,), name="proj_idxq")
    kr = krp[:, :ROPE_DIM]
    kidx = kidxp[:, :D_IDX]
    r3 = lambda a: a.reshape(bsz, seq, -1)
    band_pad = N_PREV_CHUNKS * CHUNK

    if past is None:
        kcat, vmla = mla_kv_up(ckv, krp, lw["wk"], lw["wv"])
        o_mla = mla_attention(r3(qcat), r3(kcat), r3(vmla), 0, PROMPT_TQ, PROMPT_TK)
        o_sb = sb_attention(r3(sb_q), r3(sb_kb), r3(sb_vb), 0, PROMPT_TQ, PROMPT_TK)
        front = ((0, 0), (band_pad, 0), (0, 0))
        o_bd = band_attention(r3(bd_q), jnp.pad(r3(bd_kb), front), jnp.pad(r3(bd_vb), front),
                              band_mask, PROMPT_TQ, -band_pad)
        klo, khi = _kidx_pair(r3(kidx))
        o_ds = dsa_attention(r3(idx_q), r3(idxw), klo, khi, r3(ds_q), r3(ds_kb), r3(ds_vb), dsa_tables,
                             0, seq, LANES, PROMPT_TK // LANES)
    else:
        (p_ckv, p_kr, p_sbk, p_sbv, p_bdk, p_bdv, p_dsk, p_dsv, p_kidx) = past
        past_len = p_sbk.shape[1]
        total = past_len + seq
        pad_to = -(-total // SAMPLE_TK) * SAMPLE_TK
        ckv_all = jnp.concatenate([p_ckv, r3(ckv)], axis=1)
        krp_all = jnp.concatenate([jnp.pad(p_kr, ((0, 0), (0, 0), (0, LANES - ROPE_DIM))), r3(krp)], axis=1)
        rows = pad_to - total
        ckv_all = jnp.pad(ckv_all, ((0, 0), (0, rows), (0, 0))).reshape(bsz * pad_to, -1)
        krp_all = jnp.pad(krp_all, ((0, 0), (0, rows), (0, 0))).reshape(bsz * pad_to, -1)
        kcat, vmla = mla_kv_up(ckv_all, krp_all, lw["wk"], lw["wv"], tm=pad_to)
        o_mla = mla_attention(r3(qcat), kcat.reshape(bsz, pad_to, -1), vmla.reshape(bsz, pad_to, -1),
                              past_len, seq, SAMPLE_TK)
        o_sb = sb_attention(r3(sb_q), _with_past(p_sbk, r3(sb_kb), pad_to), _with_past(p_sbv, r3(sb_vb), pad_to),
                            past_len, seq, SAMPLE_TK)
        band_len = _band_window(seq)
        o_bd = band_attention(r3(bd_q), _with_past(p_bdk, r3(bd_kb), band_len), _with_past(p_bdv, r3(bd_vb), band_len),
                              band_mask, seq, past_len - p_bdk.shape[1])
        kidx_all = jnp.pad(jnp.concatenate([p_kidx, r3(kidx)], axis=1), ((0, 0), (0, rows), (0, 0)))
        klo, khi = _kidx_pair(kidx_all)
        o_ds = dsa_attention(r3(idx_q), r3(idxw), klo, khi, r3(ds_q),
                             _with_past(p_dsk, r3(ds_kb), pad_to), _with_past(p_dsv, r3(ds_vb), pad_to),
                             dsa_tables, past_len, total, seq, SAMPLE_TK // LANES)

    f2 = lambda a: a.reshape(t, -1)
    merged = gate_merge(u, [f2(o_mla), f2(o_sb), f2(o_bd), f2(o_ds)], lw["wg"], lw["wb"])
    h_new, = matmul(merged, w_out, (F32,), residual=h, name="mix_out")
    return h_new, (ckv, kr, sb_k, sb_v, bd_k, bd_v, ds_k, ds_v, kidx)


def dense_ffn(h, un, w1, w3, w2):
    act = swiglu_up(un, w1, w3)
    out, = matmul(act, w2, (F32,), residual=h, name="ffn_down")
    return out


def moe_ffn(h, un, router_pad, w1, w3, w2):
    n_experts = w1.shape[0]
    gate = router_gate(un, router_pad, n_experts)
    for e in range(n_experts):
        act = swiglu_up(un, w1[e], w3[e])
        h, = matmul(act, w2[e], (F32,), residual=h, gate=gate, gate_col=e, name="moe_down")
    return h


def layer_step(h, p, bsz, seq, cs, past, lw, band_mask, dsa_tables, fw):
    u = rmsnorm(h, fw["g_mix"], BF)
    h, rows = mixing_block(u, bsz, seq, lw, cs, past, band_mask, dsa_tables, fw["w_out"], h)
    un = rmsnorm(h, fw["g_ffn"], BF)
    if fw["moe"]:
        h = moe_ffn(h, un, fw["router"], fw["w1"], fw["w3"], fw["w2"])
    else:
        h = dense_ffn(h, un, fw["w1"], fw["w3"], fw["w2"])
    un = rmsnorm(h, fw["g_ple"], BF)
    h = ple_update(h, p, un, fw["ple_w"], fw["ple_gate_w"])
    return h, rows


def kernel(x_prompt, x_sample, p_prompt, p_sample, cache_mla_ckv, cache_mla_krope, cache_sb_k, cache_sb_v, cache_band_k, cache_band_v, cache_dsa_k, cache_dsa_v, cache_dsa_kidx, norm_mix, w_in, mla_q_norm, mla_w_uq, mla_kv_norm, mla_w_ukv, band_rel_bias, t5_rel_bias, w_branch, w_out, norm_ffn, ffn_w1, ffn_w3, ffn_w2, moe_router, moe_w1, moe_w3, moe_w2, norm_ple, ple_w, ple_gate_w, norm_final):
    depth = w_in.shape[0]
    bp, sp, d = x_prompt.shape
    bs, ss, _ = x_sample.shape
    past_len = cache_sb_k.shape[2]
    tm = 512
    cs_p = rope_table(jnp.arange(sp))
    cs_s = rope_table(past_len + (jnp.arange(tm) % ss))
    dsa_tables_p = dsa_bias_tables(t5_rel_bias, LANES)
    dsa_tables_s = dsa_bias_tables(t5_rel_bias, ss)
    hp = x_prompt.reshape(bp * sp, d)
    hs = x_sample.reshape(bs * ss, d)
    rows_p, rows_s = [], []
    for i in range(depth):
        lw = prepare_layer_weights(w_in[i], mla_q_norm[i], mla_w_uq[i], mla_kv_norm[i], mla_w_ukv[i], w_branch[i])
        j = i // 2
        fw = dict(g_mix=norm_mix[i], g_ffn=norm_ffn[i], g_ple=norm_ple[i], w_out=w_out[i].astype(BF),
                  ple_w=ple_w[i].astype(BF), ple_gate_w=ple_gate_w[i].astype(BF), moe=i % 2 == 1)
        if i % 2 == 0:
            fw.update(w1=ffn_w1[j].astype(BF), w3=ffn_w3[j].astype(BF), w2=ffn_w2[j].astype(BF))
        else:
            fw.update(router=_pad_cols(moe_router[j], LANES).astype(BF),
                      w1=moe_w1[j].astype(BF), w3=moe_w3[j].astype(BF), w2=moe_w2[j].astype(BF))
        past_i = (cache_mla_ckv[i], cache_mla_krope[i], cache_sb_k[i], cache_sb_v[i], cache_band_k[i],
                  cache_band_v[i], cache_dsa_k[i], cache_dsa_v[i], cache_dsa_kidx[i])
        hp, rp = layer_step(hp, p_prompt[i].reshape(bp * sp, -1).astype(BF), bp, sp, cs_p, None,
                            lw, band_bias_mask(band_rel_bias[i], PROMPT_TQ), dsa_tables_p, fw)
        hs, rs = layer_step(hs, p_sample[i].reshape(bs * ss, -1).astype(BF), bs, ss, cs_s, past_i,
                            lw, band_bias_mask(band_rel_bias[i], ss), dsa_tables_s, fw)
        rows_p.append(rp)
        rows_s.append(rs)
    ones = norm_final
    y_prompt = rmsnorm(hp, ones, F32).reshape(bp, sp, d)
    y_sample = rmsnorm(hs, ones, F32).reshape(bs, ss, d)

    keep = min(N_PREV_CHUNKS * CHUNK, sp)

    def stacked(rows, n, bsz, seq, heads=None, tail=None):
        out = []
        for r in rows:
            a = r[n].reshape(bsz, seq, -1)
            if tail is not None:
                a = a[:, seq - tail:]
            if heads is not None:
                a = a.reshape(a.shape[0], a.shape[1], heads, HEAD_DIM)
            out.append(a)
        return jnp.stack(out, axis=0)

    res = [y_prompt, y_sample]
    for n, heads in ((0, None), (1, None), (2, H_SB), (3, H_SB), (4, H_BAND), (5, H_BAND),
                     (6, H_DSA), (7, H_DSA), (8, None)):
        tail = keep if n in (4, 5) else None
        res.append(stacked(rows_p, n, bp, sp, heads, tail))
        res.append(stacked(rows_s, n, bs, ss, heads))
    return tuple(res)
```

```python
import functools
import math

import jax
import jax.numpy as jnp
from jax import lax
from jax.experimental import pallas as pl
from jax.experimental.pallas import tpu as pltpu

BF = jnp.bfloat16
F32 = jnp.float32

CHUNK = 64
CHUNK_SHIFT = 6
HEAD_DIM = 128
N_BRANCH = 4
H_MLA = 4
NOPE_DIM = 128
ROPE_DIM = 64
V_DIM = 128
ROPE_THETA = 10000.0
H_SB = 4
H_BAND = 4
N_PREV_CHUNKS = 8
REL_CLIP = 128
H_DSA = 4
H_IDX = 16
D_IDX = 64
TOPK_MAX = 256
T5_BUCKETS = 32
T5_MAX_DIST = 128
TOP_K_EXPERTS = 2
EPS = 1e-6

LANES = 128
PROMPT_TQ = 256
PROMPT_TK = 512
SAMPLE_TK = 384
VMEM_LIMIT = 56 * 1024 * 1024
NEG = -1e30
INT_MIN = -2147483648


def _cparams(n_axes):
    return pltpu.CompilerParams(dimension_semantics=("arbitrary",) * n_axes,
                                vmem_limit_bytes=VMEM_LIMIT)


def _tile(n, preferred):
    if n <= preferred:
        return n
    t = preferred - preferred % LANES
    while n % t:
        t -= LANES
    assert t > 0
    return t


def _dot(a, b):
    return jnp.dot(a, b, preferred_element_type=F32)


def _dot_nt(a, b):
    return lax.dot_general(a, b, (((1,), (1,)), ((), ())), preferred_element_type=F32)


def _sigmoid(x):
    return 1.0 / (1.0 + jnp.exp(-x))


def _rms(x, g):
    return x * lax.rsqrt(jnp.mean(x * x, axis=-1, keepdims=True) + EPS) * g


def _rmsnorm_kernel(x_ref, g_ref, o_ref):
    o_ref[...] = _rms(x_ref[...], g_ref[...]).astype(o_ref.dtype)


def rmsnorm(x, g, out_dtype, tm=512):
    t, d = x.shape
    tm = _tile(t, tm)
    return pl.pallas_call(
        _rmsnorm_kernel,
        grid=(t // tm,),
        in_specs=[pl.BlockSpec((tm, d), lambda i: (i, 0)),
                  pl.BlockSpec((1, d), lambda i: (0, 0))],
        out_specs=pl.BlockSpec((tm, d), lambda i: (i, 0)),
        out_shape=jax.ShapeDtypeStruct((t, d), out_dtype),
        compiler_params=_cparams(1),
        name="rmsnorm",
    )(x, g.reshape(1, d))


def _mm_kernel(*refs, has_res, gate_col, n_out):
    a_ref, w_ref = refs[0], refs[1]
    pos = 2
    res_ref = gate_ref = None
    if has_res:
        res_ref = refs[pos]
        pos += 1
    if gate_col is not None:
        gate_ref = refs[pos]
        pos += 1
    r = _dot(a_ref[...], w_ref[...])
    if gate_ref is not None:
        r = r * gate_ref[:, gate_col:gate_col + 1]
    if res_ref is not None:
        r = res_ref[...] + r
    for o_ref in refs[pos:pos + n_out]:
        o_ref[...] = r.astype(o_ref.dtype)


WEIGHT_BLOCK_BYTES = 6 * 1024 * 1024


def matmul(a, w, out_dtypes, residual=None, gate=None, gate_col=None, tm=512, name="mm"):
    m, k = a.shape
    n = w.shape[1]
    tm, tn = _tile(m, tm), _tile(n, WEIGHT_BLOCK_BYTES // (2 * k))
    in_specs = [pl.BlockSpec((tm, k), lambda j, i: (i, 0)),
                pl.BlockSpec((k, tn), lambda j, i: (0, j))]
    args = [a, w]
    if residual is not None:
        in_specs.append(pl.BlockSpec((tm, tn), lambda j, i: (i, j)))
        args.append(residual)
    if gate is not None:
        in_specs.append(pl.BlockSpec((tm, gate.shape[1]), lambda j, i: (i, 0)))
        args.append(gate)
    outs = pl.pallas_call(
        functools.partial(_mm_kernel, has_res=residual is not None,
                          gate_col=gate_col if gate is not None else None, n_out=len(out_dtypes)),
        grid=(n // tn, m // tm),
        in_specs=in_specs,
        out_specs=[pl.BlockSpec((tm, tn), lambda j, i: (i, j)) for _ in out_dtypes],
        out_shape=[jax.ShapeDtypeStruct((m, n), dt) for dt in out_dtypes],
        compiler_params=_cparams(2),
        name=name,
    )(*args)
    return outs


def _swiglu_up_kernel(a_ref, w1_ref, w3_ref, o_ref):
    a = a_ref[...]
    x1 = _dot(a, w1_ref[...])
    x3 = _dot(a, w3_ref[...])
    o_ref[...] = (x1 * _sigmoid(x1) * x3).astype(o_ref.dtype)


def swiglu_up(a, w1, w3, tm=512):
    m, k = a.shape
    n = w1.shape[1]
    tm, tn = _tile(m, tm), _tile(n, WEIGHT_BLOCK_BYTES // (2 * k))
    return pl.pallas_call(
        _swiglu_up_kernel,
        grid=(n // tn, m // tm),
        in_specs=[pl.BlockSpec((tm, k), lambda j, i: (i, 0)),
                  pl.BlockSpec((k, tn), lambda j, i: (0, j)),
                  pl.BlockSpec((k, tn), lambda j, i: (0, j))],
        out_specs=pl.BlockSpec((tm, tn), lambda j, i: (i, j)),
        out_shape=jax.ShapeDtypeStruct((m, n), BF),
        compiler_params=_cparams(2),
        name="swiglu_up",
    )(a, w1, w3)


def _gate_merge_kernel(u_ref, o0_ref, o1_ref, o2_ref, o3_ref, wg_ref, wb_ref, out_ref):
    u = u_ref[...]
    acc = None
    for b, o_ref in enumerate((o0_ref, o1_ref, o2_ref, o3_ref)):
        t = _sigmoid(_dot(u, wg_ref[b])) * _dot(o_ref[...], wb_ref[b])
        acc = t if acc is None else acc + t
    out_ref[...] = acc.astype(out_ref.dtype)


def gate_merge(u, branches, wg, wb, tm=512, tn=512):
    t, d = u.shape
    bw = branches[0].shape[1]
    n = wg.shape[2]
    tm = min(tm, t)
    return pl.pallas_call(
        _gate_merge_kernel,
        grid=(n // tn, t // tm),
        in_specs=[pl.BlockSpec((tm, d), lambda j, i: (i, 0))]
        + [pl.BlockSpec((tm, bw), lambda j, i: (i, 0)) for _ in range(N_BRANCH)]
        + [pl.BlockSpec((N_BRANCH, d, tn), lambda j, i: (0, 0, j)),
           pl.BlockSpec((N_BRANCH, bw, tn), lambda j, i: (0, 0, j))],
        out_specs=pl.BlockSpec((tm, tn), lambda j, i: (i, j)),
        out_shape=jax.ShapeDtypeStruct((t, n), BF),
        compiler_params=_cparams(2),
        name="gate_merge",
    )(u, *branches, wg, wb)


def _ple_kernel(h_ref, p_ref, un_ref, wp_ref, wg_ref, o_ref):
    o_ref[...] = h_ref[...] + _dot(p_ref[...], wp_ref[...]) * _sigmoid(_dot(un_ref[...], wg_ref[...]))


def ple_update(h, p, un, wp, wg, tm=512, tn=1024):
    t, d = h.shape
    tm = min(tm, t)
    return pl.pallas_call(
        _ple_kernel,
        grid=(d // tn, t // tm),
        in_specs=[pl.BlockSpec((tm, tn), lambda j, i: (i, j)),
                  pl.BlockSpec((tm, p.shape[1]), lambda j, i: (i, 0)),
                  pl.BlockSpec((tm, d), lambda j, i: (i, 0)),
                  pl.BlockSpec((p.shape[1], tn), lambda j, i: (0, j)),
                  pl.BlockSpec((d, tn), lambda j, i: (0, j))],
        out_specs=pl.BlockSpec((tm, tn), lambda j, i: (i, j)),
        out_shape=jax.ShapeDtypeStruct((t, d), F32),
        compiler_params=_cparams(2),
        name="ple_update",
    )(h, p, un, wp, wg)


def _router_kernel(un_ref, wr_ref, g_ref, *, n_experts):
    logits = _dot(un_ref[...], wr_ref[...])
    lane = lax.broadcasted_iota(jnp.int32, logits.shape, 1).astype(F32)
    real = lane < n_experts
    logits = jnp.where(real, logits, NEG)
    e = jnp.where(real, jnp.exp(logits - jnp.max(logits, axis=1, keepdims=True)), 0.0)
    probs = e / jnp.sum(e, axis=1, keepdims=True)
    p1 = jnp.max(probs, axis=1, keepdims=True)
    i1 = jnp.min(jnp.where(probs == p1, lane, float(LANES)), axis=1, keepdims=True)
    first = lane == i1
    rest = jnp.where(first | ~real, -1.0, probs)
    p2 = jnp.max(rest, axis=1, keepdims=True)
    i2 = jnp.min(jnp.where(rest == p2, lane, float(LANES)), axis=1, keepdims=True)
    second = lane == i2
    denom = p1 + p2
    g_ref[...] = jnp.where(first, p1 / denom, 0.0) + jnp.where(second, p2 / denom, 0.0)


def router_gate(un, wr_pad, n_experts, tm=512):
    t, d = un.shape
    tm = min(tm, t)
    return pl.pallas_call(
        functools.partial(_router_kernel, n_experts=n_experts),
        grid=(t // tm,),
        in_specs=[pl.BlockSpec((tm, d), lambda i: (i, 0)),
                  pl.BlockSpec((d, LANES), lambda i: (0, 0))],
        out_specs=pl.BlockSpec((tm, LANES), lambda i: (i, 0)),
        out_shape=jax.ShapeDtypeStruct((t, LANES), F32),
        compiler_params=_cparams(1),
        name="router_gate",
    )(un, wr_pad)


Q_LORA_OFF = 0


def _proj_misc_kernel(u_ref, wm_ref, qn_ref, kvn_ref, wqa_ref, wqb_ref, cs_ref,
                      ckv_ref, krp_ref, kidx_ref, idxw_ref, qcat_ref, *, q_lora, kv_lora):
    x = _dot(u_ref[...], wm_ref[...])
    cos = cs_ref[:, 0:LANES]
    sin = cs_ref[:, LANES:2 * LANES]
    o = q_lora
    ckv_ref[...] = _rms(x[:, o:o + kv_lora], kvn_ref[...])
    o += kv_lora
    krp_ref[...] = x[:, o:o + LANES] * cos + x[:, o + LANES:o + 2 * LANES] * sin
    o += 2 * LANES
    kidx_ref[...] = x[:, o:o + LANES]
    idxw_ref[...] = x[:, o + LANES:o + 2 * LANES]
    cqn = _rms(x[:, 0:q_lora], qn_ref[...]).astype(BF)
    qa = _dot(cqn, wqa_ref[...])
    qb = _dot(cqn, wqb_ref[...])
    for h in range(H_MLA):
        lo = 2 * h * LANES
        qcat_ref[:, lo:lo + LANES] = qa[:, lo:lo + LANES].astype(BF)
        qcat_ref[:, lo + LANES:lo + 2 * LANES] = (
            qa[:, lo + LANES:lo + 2 * LANES] * cos + qb[:, h * LANES:(h + 1) * LANES] * sin).astype(BF)


def proj_misc(u, wm, qn, kvn, wqa, wqb, cs, tm=512):
    t, d = u.shape
    tm = min(tm, t)
    q_lora, kv_lora = qn.shape[1], kvn.shape[1]
    n_pos_blocks = cs.shape[0] // tm
    row = lambda n: pl.BlockSpec((tm, n), lambda i: (i, 0))
    full = lambda a: pl.BlockSpec(a.shape, lambda i: (0, 0))
    return pl.pallas_call(
        functools.partial(_proj_misc_kernel, q_lora=q_lora, kv_lora=kv_lora),
        grid=(t // tm,),
        in_specs=[row(d), full(wm), full(qn), full(kvn), full(wqa), full(wqb),
                  pl.BlockSpec((tm, 2 * LANES), lambda i: (i % n_pos_blocks, 0))],
        out_specs=[row(kv_lora), row(LANES), row(LANES), row(LANES), row(2 * LANES * H_MLA)],
        out_shape=[jax.ShapeDtypeStruct((t, kv_lora), F32),
                   jax.ShapeDtypeStruct((t, LANES), F32),
                   jax.ShapeDtypeStruct((t, LANES), F32),
                   jax.ShapeDtypeStruct((t, LANES), F32),
                   jax.ShapeDtypeStruct((t, 2 * LANES * H_MLA), BF)],
        compiler_params=_cparams(1),
        name="proj_misc",
    )(u, wm, qn, kvn, wqa, wqb, cs)


def _mla_kv_up_kernel(ckv_ref, krp_ref, wk_ref, wv_ref, kcat_ref, v_ref):
    c = ckv_ref[...].astype(BF)
    kn = _dot(c, wk_ref[...])
    krp = krp_ref[...].astype(BF)
    for h in range(H_MLA):
        kcat_ref[:, 2 * h * LANES:(2 * h + 1) * LANES] = kn[:, h * LANES:(h + 1) * LANES].astype(BF)
        kcat_ref[:, (2 * h + 1) * LANES:(2 * h + 2) * LANES] = krp
    v_ref[...] = _dot(c, wv_ref[...]).astype(BF)


def mla_kv_up(ckv, krp, wk, wv, tm=512):
    t, c = ckv.shape
    tm = min(tm, t)
    row = lambda n: pl.BlockSpec((tm, n), lambda i: (i, 0))
    full = lambda a: pl.BlockSpec(a.shape, lambda i: (0, 0))
    return pl.pallas_call(
        _mla_kv_up_kernel,
        grid=(t // tm,),
        in_specs=[row(c), row(LANES), full(wk), full(wv)],
        out_specs=[row(2 * LANES * H_MLA), row(V_DIM * H_MLA)],
        out_shape=[jax.ShapeDtypeStruct((t, 2 * LANES * H_MLA), BF),
                   jax.ShapeDtypeStruct((t, V_DIM * H_MLA), BF)],
        compiler_params=_cparams(1),
        name="mla_kv_up",
    )(ckv, krp, wk, wv)


def _mla_attn_kernel(qt_ref, k_ref, vt_ref, o_ref, *, tq, tk, q_off, scale, heads):
    q0 = q_off + pl.program_id(2) * tq
    qpos = q0 + lax.broadcasted_iota(jnp.int32, (1, tq), 1)
    limit = (lax.shift_right_arithmetic(qpos, CHUNK_SHIFT) + 1) * CHUNK
    last_limit = ((q0 + tq - 1) // CHUNK + 1) * CHUNK
    n_chunks = (last_limit + tk - 1) // tk

    def body(kc, carry):
        ks = pl.multiple_of(kc * tk, tk)
        kpos = ks + lax.broadcasted_iota(jnp.int32, (tk, tq), 0)
        visible = kpos < limit
        scores = [_dot(k_ref[pl.ds(ks, tk), 2 * h * LANES:2 * (h + 1) * LANES], qt_ref[h]) for h in range(heads)]
        probs, stats = [], []
        for h in range(heads):
            m_prev, l_prev, _ = carry[h]
            s = jnp.where(visible, scores[h] * scale, NEG)
            m_new = jnp.maximum(m_prev, jnp.max(s, axis=0, keepdims=True))
            p = jnp.exp(s - m_new)
            alpha = jnp.exp(m_prev - m_new)
            stats.append((m_new, alpha * l_prev + jnp.sum(p, axis=0, keepdims=True), alpha))
            probs.append(p.astype(BF))
        out = []
        for h in range(heads):
            m_new, l_new, alpha = stats[h]
            out.append((m_new, l_new, alpha * carry[h][2] + _dot(vt_ref[h, kc], probs[h])))
        return tuple(out)

    init = tuple((jnp.full((1, tq), NEG, F32), jnp.zeros((1, tq), F32), jnp.zeros((V_DIM, tq), F32))
                 for _ in range(heads))
    final = lax.fori_loop(0, n_chunks, body, init)
    for h in range(heads):
        _, l_fin, acc = final[h]
        o_ref[:, h * V_DIM:(h + 1) * V_DIM] = (acc / l_fin).T.astype(o_ref.dtype)


def _chunked_transpose(v, heads, tk):
    b, s, hd = v.shape
    d = hd // heads
    return jnp.transpose(v.reshape(b, s // tk, tk, heads, d), (0, 3, 1, 4, 2))


def mla_attention(qcat, kcat, v, q_off, tq, tk, heads=4):
    b, sq, _ = qcat.shape
    sk = kcat.shape[1]
    scale = (NOPE_DIM + ROPE_DIM) ** -0.5
    qt = jnp.transpose(qcat.reshape(b, sq, H_MLA, 2 * LANES), (0, 2, 3, 1))
    vt = _chunked_transpose(v, H_MLA, tk)
    return pl.pallas_call(
        functools.partial(_mla_attn_kernel, tq=tq, tk=tk, q_off=q_off, scale=scale, heads=heads),
        grid=(b, H_MLA // heads, sq // tq),
        in_specs=[pl.BlockSpec((None, heads, 2 * LANES, tq), lambda bi, h, qi: (bi, h, 0, qi)),
                  pl.BlockSpec((None, sk, heads * 2 * LANES), lambda bi, h, qi: (bi, 0, h)),
                  pl.BlockSpec((None, heads, sk // tk, V_DIM, tk), lambda bi, h, qi: (bi, h, 0, 0, 0))],
        out_specs=pl.BlockSpec((None, tq, heads * V_DIM), lambda bi, h, qi: (bi, qi, h)),
        out_shape=jax.ShapeDtypeStruct((b, sq, H_MLA * V_DIM), BF),
        compiler_params=_cparams(3),
        name="mla_attention",
    )(qt, kcat, vt)


def _split3(x):
    hi = x.astype(BF)
    r = x - hi.astype(F32)
    mid = r.astype(BF)
    lo = (r - mid.astype(F32)).astype(BF)
    return hi, mid, lo


def _sb_attn_kernel(q_ref, k_ref, v_ref, tri_ref, o_ref, *, tq, tk, q_off, scale, heads):
    q0 = q_off + pl.program_id(2) * tq
    qpos = q0 + lax.broadcasted_iota(jnp.int32, (tq, 1), 0)
    n_chunks = (q0 + tq - 1 + tk - 1) // tk
    tri2 = tri_ref[...]
    tri = tri2[0:LANES]
    n_blk = tk // LANES
    hs = [slice(h * HEAD_DIM, (h + 1) * HEAD_DIM) for h in range(heads)]

    def step(kc, carry, diagonal):
        ks = pl.multiple_of(kc * tk, tk)
        if diagonal:
            strict = ks + lax.broadcasted_iota(jnp.int32, (1, tk), 1) < qpos
            causal = lambda x: jnp.where(strict, x, 0.0)
        else:
            causal = lambda x: x
        z = [_dot_nt(q_ref[:, hs[h]], k_ref[pl.ds(ks, tk), hs[h]]) for h in range(heads)]
        log_beta, parts = [], []
        for h in range(heads):
            zh = z[h] * scale
            sp = jnp.maximum(zh, 0.0) + jnp.log(1.0 + jnp.exp(-jnp.abs(zh)))
            log_beta.append(zh - sp)
            parts.append(_split3(causal(-sp)))
        after, later = [], []
        for h in range(heads):
            hi, mid, lo = parts[h]
            run = carry[h][1]
            blocks = [None] * n_blk
            for blk in reversed(range(n_blk)):
                sl = slice(blk * LANES, (blk + 1) * LANES)
                sums = _dot(jnp.concatenate([hi[:, sl], mid[:, sl]], axis=1), tri2) + _dot(lo[:, sl], tri)
                blocks[blk] = sums[:, 0:LANES] + run
                run = run + sums[:, LANES:2 * LANES]
            after.append(jnp.concatenate(blocks, axis=1))
            later.append(run)
        weights = [causal(jnp.exp(log_beta[h] + after[h])).astype(BF) for h in range(heads)]
        return tuple((carry[h][0] + _dot(weights[h], v_ref[pl.ds(ks, tk), hs[h]]), later[h]) for h in range(heads))

    n_below = q0 // tk
    init = tuple((jnp.zeros((tq, HEAD_DIM), F32), jnp.zeros((tq, LANES), F32)) for _ in range(heads))
    state = lax.fori_loop(0, n_chunks - n_below, lambda it, c: step(n_chunks - 1 - it, c, True), init)
    final = lax.fori_loop(0, n_below, lambda it, c: step(n_below - 1 - it, c, False), state)
    for h in range(heads):
        o_ref[:, hs[h]] = final[h][0].astype(o_ref.dtype)


def sb_attention(q, k, v, q_off, tq, tk, heads=2):
    b, sq, _ = q.shape
    sk = k.shape[1]
    assert sk % tk == 0 and tk % LANES == 0 and H_SB % heads == 0
    j = lax.broadcasted_iota(jnp.int32, (LANES, LANES), 0)
    s = lax.broadcasted_iota(jnp.int32, (LANES, LANES), 1)
    tri = jnp.concatenate([(j > s).astype(BF), jnp.ones((LANES, LANES), BF)], axis=1)
    tri = jnp.concatenate([tri, tri], axis=0)
    width = heads * HEAD_DIM
    return pl.pallas_call(
        functools.partial(_sb_attn_kernel, tq=tq, tk=tk, q_off=q_off, scale=HEAD_DIM ** -0.5, heads=heads),
        grid=(b, H_SB // heads, sq // tq),
        in_specs=[pl.BlockSpec((None, tq, width), lambda bi, h, qi: (bi, qi, h)),
                  pl.BlockSpec((None, sk, width), lambda bi, h, qi: (bi, 0, h)),
                  pl.BlockSpec((None, sk, width), lambda bi, h, qi: (bi, 0, h)),
                  pl.BlockSpec((2 * LANES, 2 * LANES), lambda bi, h, qi: (0, 0))],
        out_specs=pl.BlockSpec((None, tq, width), lambda bi, h, qi: (bi, qi, h)),
        out_shape=jax.ShapeDtypeStruct((b, sq, H_SB * HEAD_DIM), BF),
        compiler_params=_cparams(3),
        name="sb_attention",
    )(q, k, v, tri)


def _band_attn_kernel(q_ref, k_ref, v_ref, bm_ref, o_ref, *, tq, win, kpos_base, scale):
    w0 = pl.multiple_of(pl.program_id(2) * tq, tq)
    k = k_ref[pl.ds(w0, win), :]
    s = _dot_nt(q_ref[...], k) * scale + bm_ref[...]
    kpos = kpos_base + w0 + lax.broadcasted_iota(jnp.int32, (1, win), 1)
    s = jnp.where(kpos >= 0, s, NEG)
    p = jnp.exp(s - jnp.max(s, axis=1, keepdims=True))
    denom = jnp.sum(p, axis=1, keepdims=True)
    o = _dot(p.astype(BF), v_ref[pl.ds(w0, win), :])
    o_ref[...] = (o / denom).astype(o_ref.dtype)


def _band_window(tq):
    return -(-(tq + N_PREV_CHUNKS * CHUNK) // LANES) * LANES


def band_bias_mask(rel_bias, tq):
    win = _band_window(tq)
    i = jnp.arange(tq)[:, None]
    j = jnp.arange(win)[None, :]
    rel = jnp.clip(i + N_PREV_CHUNKS * CHUNK - j, -REL_CLIP, REL_CLIP) + REL_CLIP
    ci, cj = i // CHUNK, j // CHUNK
    inside = (cj >= ci) & (cj <= ci + N_PREV_CHUNKS)
    return jnp.where(inside[None], _table_lookup(rel_bias, rel), NEG)


def band_attention(q, k_pad, v_pad, bias_mask, tq, kpos_base):
    b, sq, _ = q.shape
    skp = k_pad.shape[1]
    win = _band_window(tq)
    assert skp >= sq - tq + win
    return pl.pallas_call(
        functools.partial(_band_attn_kernel, tq=tq, win=win, kpos_base=kpos_base, scale=HEAD_DIM ** -0.5),
        grid=(b, H_BAND, sq // tq),
        in_specs=[pl.BlockSpec((None, tq, HEAD_DIM), lambda bi, h, qi: (bi, qi, h)),
                  pl.BlockSpec((None, skp, HEAD_DIM), lambda bi, h, qi: (bi, 0, h)),
                  pl.BlockSpec((None, skp, HEAD_DIM), lambda bi, h, qi: (bi, 0, h)),
                  pl.BlockSpec((None, tq, win), lambda bi, h, qi: (h, 0, 0))],
        out_specs=pl.BlockSpec((None, tq, HEAD_DIM), lambda bi, h, qi: (bi, qi, h)),
        out_shape=jax.ShapeDtypeStruct((b, sq, H_BAND * HEAD_DIM), BF),
        compiler_params=_cparams(3),
        name="band_attention",
    )(q, k_pad, v_pad, bias_mask)


def _sortable(x):
    i = lax.bitcast_convert_type(x, jnp.int32)
    return i ^ (lax.shift_right_arithmetic(i, 31) & 0x7FFFFFFF)


def _dsa_kernel(qi2_ref, w_ref, klo_ref, khi_ref, q_ref, k_ref, v_ref, bnear_ref, bfar_ref, tri_ref,
                o_ref, key_ref, wb_ref, m_ref, l_ref, acc_ref, eqc_ref,
                *, tq, tk, big, q_off, topk, scale, w_scale):
    wide = big * tk
    q0 = q_off + pl.program_id(1) * tq
    qpos = q0 + lax.broadcasted_iota(jnp.int32, (tq, 1), 0)
    limit = (lax.shift_right_arithmetic(qpos, CHUNK_SHIFT) + 1) * CHUNK
    last_limit = ((q0 + tq - 1) // CHUNK + 1) * CHUNK
    n_wide = (last_limit + wide - 1) // wide
    diag = q0 // tk

    w = w_ref[...] * w_scale
    for h in range(H_IDX):
        wb_ref[h] = jnp.broadcast_to(w[:, h:h + 1], (tq, tk))

    def score_body(c, carry):
        ks = pl.multiple_of(c * wide, wide)
        klo = klo_ref[pl.ds(ks, wide), :]
        khi = khi_ref[pl.ds(ks, wide), :]
        acc = [jnp.zeros((tq, tk), F32) for _ in range(big)]
        for pair in range(H_IDX // 2):
            q2 = qi2_ref[:, pair * LANES:(pair + 1) * LANES]
            l0 = jnp.maximum(_dot_nt(q2, klo), 0.0)
            l1 = jnp.maximum(_dot_nt(q2, khi), 0.0)
            for blk in range(big):
                sl = slice(blk * tk, (blk + 1) * tk)
                acc[blk] = acc[blk] + wb_ref[2 * pair] * l0[:, sl] + wb_ref[2 * pair + 1] * l1[:, sl]
        for blk in range(big):
            kpos = ks + blk * tk + lax.broadcasted_iota(jnp.int32, (1, tk), 1)
            key_ref[c * big + blk] = _sortable(jnp.where(kpos < limit, acc[blk] + 0.0, -jnp.inf))
        return carry

    lax.fori_loop(0, n_wide, score_body, 0)

    def count_ge(cand):
        def body(c, cnt):
            for blk in range(big):
                cnt = cnt + jnp.where(key_ref[c * big + blk] >= cand, 1.0, 0.0)
            return cnt
        cnt = lax.fori_loop(0, n_wide, body, jnp.zeros((tq, tk), F32))
        return jnp.sum(cnt, axis=1, keepdims=True)

    def bit_body(it, thr):
        cand = thr + lax.shift_left(jnp.int32(1), 31 - it)
        return jnp.where(count_ge(cand) >= topk, cand, thr)

    thr = lax.fori_loop(0, 32, bit_body, jnp.full((tq, 1), INT_MIN, jnp.int32))
    n_above = count_ge(thr + 1)
    n_ties_kept = topk - n_above

    m_ref[...] = jnp.full(m_ref.shape, NEG, F32)
    l_ref[...] = jnp.zeros(l_ref.shape, F32)
    acc_ref[...] = jnp.zeros(acc_ref.shape, F32)
    eqc_ref[...] = jnp.zeros(eqc_ref.shape, F32)
    tri = tri_ref[...]

    def attend(first_blk, n_blk, bias_of_head, below_tile):
        width = n_blk * tk
        ks = pl.multiple_of(first_blk * tk, tk)
        key = jnp.concatenate([key_ref[first_blk + i] for i in range(n_blk)], axis=1)
        eq = key == thr
        eq_bf = jnp.where(eq, 1.0, 0.0).astype(BF)
        seen = eqc_ref[...]
        rank = [None] * n_blk
        for i in range(n_blk):
            counts = _dot(eq_bf[:, i * tk:(i + 1) * tk], tri)
            rank[i] = counts[:, 0:tk] + seen
            seen = seen + counts[:, tk:2 * tk]
        eqc_ref[...] = seen
        sel = (key > thr) | (eq & (jnp.concatenate(rank, axis=1) <= n_ties_kept))
        if not below_tile:
            kpos = ks + lax.broadcasted_iota(jnp.int32, (1, width), 1)
            sel = sel & (kpos < limit)
        hs = [slice(h * HEAD_DIM, (h + 1) * HEAD_DIM) for h in range(H_DSA)]
        scores = [_dot_nt(q_ref[:, hs[h]], k_ref[pl.ds(ks, width), hs[h]]) for h in range(H_DSA)]
        probs, alphas = [], []
        for h in range(H_DSA):
            s = jnp.where(sel, scores[h] * scale + bias_of_head(h), NEG)
            m_prev = m_ref[h]
            m_new = jnp.maximum(m_prev, jnp.max(s, axis=1, keepdims=True))
            p = jnp.where(sel, jnp.exp(s - m_new), 0.0)
            alpha = jnp.exp(m_prev - m_new)
            l_ref[h] = alpha * l_ref[h] + jnp.sum(p, axis=1, keepdims=True)
            m_ref[h] = m_new
            probs.append(p.astype(BF))
            alphas.append(alpha)
        for h in range(H_DSA):
            acc_ref[h] = alphas[h] * acc_ref[h] + _dot(probs[h], v_ref[pl.ds(ks, width), hs[h]])

    far_bias = lambda h: bfar_ref[h][:, 0:1]
    n_far_wide = jnp.maximum(q0 - tk, 0) // wide

    def far_wide_body(c, carry):
        attend(c * big, big, far_bias, True)
        return carry

    lax.fori_loop(0, n_far_wide, far_wide_body, 0)

    def far_body(kc, carry):
        attend(kc, 1, far_bias, True)
        return carry

    lax.fori_loop(n_far_wide * big, jnp.maximum(diag - 1, n_far_wide * big), far_body, 0)

    @pl.when(diag >= 1)
    def _():
        attend(diag - 1, 1, lambda h: bnear_ref[1, h], True)

    attend(diag, 1, lambda h: bnear_ref[0, h], False)

    for h in range(H_DSA):
        o_ref[:, h * HEAD_DIM:(h + 1) * HEAD_DIM] = (acc_ref[h] / l_ref[h]).astype(o_ref.dtype)


def t5_bucket(rel):
    half = T5_BUCKETS // 2
    max_exact = half // 2
    n = jnp.abs(rel)
    nf = jnp.maximum(n, 1).astype(F32)
    large = max_exact + (jnp.log(nf / max_exact) / math.log(T5_MAX_DIST / max_exact)
                         * (half - max_exact)).astype(jnp.int32)
    large = jnp.minimum(large, half - 1)
    return jnp.where(rel > 0, half, 0) + jnp.where(n < max_exact, n, large)


def _table_lookup(table, idx):
    onehot = jax.nn.one_hot(idx, table.shape[0], dtype=F32)
    out = jnp.einsum("...n,nh->...h", onehot, table.astype(F32), precision=lax.Precision.HIGHEST)
    return jnp.moveaxis(out, -1, 0)


def dsa_bias_tables(t5_table, tq, tk=LANES):
    i = jnp.arange(tq)[:, None]
    j = jnp.arange(tk)[None, :]
    near = jnp.stack([_table_lookup(t5_table, t5_bucket(j - i - d * tk)) for d in range(2)], axis=0)
    far_rel = -jnp.ones((1, tk), jnp.int32) * (2 * tk)
    far = _table_lookup(t5_table, t5_bucket(far_rel))
    return near, far


def dsa_attention(qidx2, idxw, klo, khi, q, k, v, bias_tables, q_off, n_keys, tq, big, tk=LANES):
    b, sq, _ = q.shape
    sk = k.shape[1]
    assert tk >= T5_MAX_DIST and q_off % tk == 0 and (tq == tk or sq == tq) and sk % (big * tk) == 0
    topk = min(TOPK_MAX, n_keys // 4)
    near, far = bias_tables
    jj = lax.broadcasted_iota(jnp.int32, (tk, tk), 0)
    ss = lax.broadcasted_iota(jnp.int32, (tk, tk), 1)
    tri = jnp.concatenate([(jj <= ss).astype(BF), jnp.ones((tk, tk), BF)], axis=1)
    whole = lambda n: pl.BlockSpec((None, sk, n), lambda bi, qi: (bi, 0, 0))
    tile = lambda n: pl.BlockSpec((None, tq, n), lambda bi, qi: (bi, qi, 0))
    const = lambda a: pl.BlockSpec(a.shape, lambda bi, qi: (0,) * a.ndim)
    return pl.pallas_call(
        functools.partial(_dsa_kernel, tq=tq, tk=tk, big=big, q_off=q_off, topk=topk,
                          scale=HEAD_DIM ** -0.5, w_scale=H_IDX ** -0.5 * D_IDX ** -0.5),
        grid=(b, sq // tq),
        in_specs=[tile(qidx2.shape[2]), tile(LANES), whole(LANES), whole(LANES),
                  tile(H_DSA * HEAD_DIM), whole(H_DSA * HEAD_DIM), whole(H_DSA * HEAD_DIM),
                  const(near), const(far), const(tri)],
        out_specs=tile(H_DSA * HEAD_DIM),
        out_shape=jax.ShapeDtypeStruct((b, sq, H_DSA * HEAD_DIM), BF),
        scratch_shapes=[pltpu.VMEM((sk // tk, tq, tk), jnp.int32), pltpu.VMEM((H_IDX, tq, tk), F32),
                        pltpu.VMEM((H_DSA, tq, 1), F32), pltpu.VMEM((H_DSA, tq, 1), F32),
                        pltpu.VMEM((H_DSA, tq, HEAD_DIM), F32), pltpu.VMEM((tq, tk), F32)],
        compiler_params=_cparams(2),
        name="dsa_attention",
    )(qidx2, idxw, klo, khi, q, k, v, near, far, tri)


def _pad_cols(a, n):
    return jnp.pad(a, ((0, 0), (0, n - a.shape[1])))


def _rotate_half_cols(w):
    half = w.shape[1] // 2
    return jnp.concatenate([-w[:, half:], w[:, :half]], axis=1)


def prepare_layer_weights(w_in, q_norm, w_uq, kv_norm, w_ukv, w_branch):
    d = w_in.shape[0]
    q_lora, kv_lora = q_norm.shape[0], kv_norm.shape[0]
    sizes = (q_lora, kv_lora, ROPE_DIM, 3 * H_SB * HEAD_DIM, 3 * H_BAND * HEAD_DIM, 3 * H_DSA * HEAD_DIM,
             H_IDX * D_IDX, D_IDX, H_IDX, N_BRANCH * d)
    cols, start = [], 0
    for s in sizes:
        cols.append(w_in[:, start:start + s])
        start += s
    w_cq, w_ckv, w_kr, w_sb, w_bd, w_ds, w_iq, w_ik, w_iw, w_g = cols
    wm = jnp.concatenate([w_cq, w_ckv, _pad_cols(w_kr, LANES), _pad_cols(_rotate_half_cols(w_kr), LANES),
                          _pad_cols(w_ik, LANES), _pad_cols(w_iw, LANES)], axis=1).astype(BF)
    qa, qb = [], []
    hd = NOPE_DIM + ROPE_DIM
    for h in range(H_MLA):
        wh = w_uq[:, h * hd:(h + 1) * hd]
        qa += [wh[:, :NOPE_DIM], _pad_cols(wh[:, NOPE_DIM:], LANES)]
        qb.append(_pad_cols(_rotate_half_cols(wh[:, NOPE_DIM:]), LANES))
    wqa = jnp.concatenate(qa, axis=1).astype(BF)
    wqb = jnp.concatenate(qb, axis=1).astype(BF)
    kvd = NOPE_DIM + V_DIM
    wk = jnp.concatenate([w_ukv[:, h * kvd:h * kvd + NOPE_DIM] for h in range(H_MLA)], axis=1).astype(BF)
    wv = jnp.concatenate([w_ukv[:, h * kvd + NOPE_DIM:(h + 1) * kvd] for h in range(H_MLA)], axis=1).astype(BF)
    wg = jnp.transpose(w_g.reshape(d, N_BRANCH, d), (1, 0, 2)).astype(BF)
    return dict(wm=wm, wqa=wqa, wqb=wqb, wk=wk, wv=wv,
                w_sb=w_sb.astype(BF), w_bd=w_bd.astype(BF), w_ds=w_ds.astype(BF), w_iq=w_iq.astype(BF),
                wg=wg, wb=w_branch.astype(BF),
                qn=q_norm.reshape(1, -1), kvn=kv_norm.reshape(1, -1))


def rope_table(pos):
    half = ROPE_DIM // 2
    inv = ROPE_THETA ** (-jnp.arange(half, dtype=F32) / half)
    ang = pos.astype(F32)[:, None] * inv[None, :]
    cos, sin = jnp.cos(ang), jnp.sin(ang)
    z = jnp.zeros((pos.shape[0], LANES - ROPE_DIM), F32)
    return jnp.concatenate([cos, cos, z, sin, sin, z], axis=1)


def _kidx_pair(kidx):
    kb = kidx.astype(BF)
    return (jnp.pad(kb, ((0, 0), (0, 0), (0, LANES - D_IDX))),
            jnp.pad(kb, ((0, 0), (0, 0), (LANES - D_IDX, 0))))


def _with_past(past, new, pad_to):
    b = new.shape[0]
    a = jnp.concatenate([past.reshape(b, past.shape[1], -1).astype(BF), new.astype(BF)], axis=1)
    return jnp.pad(a, ((0, 0), (0, pad_to - a.shape[1]), (0, 0)))


def mixing_block(u, bsz, seq, lw, cs, past, band_mask, dsa_tables, w_out, h):
    t = bsz * seq
    ckv, krp, kidxp, idxw, qcat = proj_misc(u, lw["wm"], lw["qn"], lw["kvn"], lw["wqa"], lw["wqb"], cs)
    hw = H_SB * HEAD_DIM

    def qkv(w):
        q, = matmul(u, w[:, :hw], (BF,), tm=1024, name="proj_q")
        k32, kbf = matmul(u, w[:, hw:2 * hw], (F32, BF), tm=1024, name="proj_k")
        v32, vbf = matmul(u, w[:, 2 * hw:], (F32, BF), tm=1024, name="proj_v")
        return q, k32, kbf, v32, vbf

    sb_q, sb_k, sb_kb, sb_v, sb_vb = qkv(lw["w_sb"])
    bd_q, bd_k, bd_kb, bd_v, bd_vb = qkv(lw["w_bd"])
    ds_q, ds_k, ds_kb, ds_v, ds_vb = qkv(lw["w_ds"])
    idx_q, = matmul(u, lw["w_iq"], (BF,), tm=1024, name="proj_idxq")
    kr = krp[:, :ROPE_DIM]
    kidx = kidxp[:, :D_IDX]
    r3 = lambda a: a.reshape(bsz, seq, -1)
    band_pad = N_PREV_CHUNKS * CHUNK

    if past is None:
        kcat, vmla = mla_kv_up(ckv, krp, lw["wk"], lw["wv"])
        o_mla = mla_attention(r3(qcat), r3(kcat), r3(vmla), 0, PROMPT_TQ, PROMPT_TK)
        o_sb = sb_attention(r3(sb_q), r3(sb_kb), r3(sb_vb), 0, PROMPT_TQ, PROMPT_TK)
        front = ((0, 0), (band_pad, 0), (0, 0))
        o_bd = band_attention(r3(bd_q), jnp.pad(r3(bd_kb), front), jnp.pad(r3(bd_vb), front),
                              band_mask, PROMPT_TQ, -band_pad)
        klo, khi = _kidx_pair(r3(kidx))
        o_ds = dsa_attention(r3(idx_q), r3(idxw), klo, khi, r3(ds_q), r3(ds_kb), r3(ds_vb), dsa_tables,
                             0, seq, LANES, PROMPT_TK // LANES)
    else:
        (p_ckv, p_kr, p_sbk, p_sbv, p_bdk, p_bdv, p_dsk, p_dsv, p_kidx) = past
        past_len = p_sbk.shape[1]
        total = past_len + seq
        pad_to = -(-total // SAMPLE_TK) * SAMPLE_TK
        ckv_all = jnp.concatenate([p_ckv, r3(ckv)], axis=1)
        krp_all = jnp.concatenate([jnp.pad(p_kr, ((0, 0), (0, 0), (0, LANES - ROPE_DIM))), r3(krp)], axis=1)
        rows = pad_to - total
        ckv_all = jnp.pad(ckv_all, ((0, 0), (0, rows), (0, 0))).reshape(bsz * pad_to, -1)
        krp_all = jnp.pad(krp_all, ((0, 0), (0, rows), (0, 0))).reshape(bsz * pad_to, -1)
        kcat, vmla = mla_kv_up(ckv_all, krp_all, lw["wk"], lw["wv"], tm=pad_to)
        o_mla = mla_attention(r3(qcat), kcat.reshape(bsz, pad_to, -1), vmla.reshape(bsz, pad_to, -1),
                              past_len, seq, SAMPLE_TK)
        o_sb = sb_attention(r3(sb_q), _with_past(p_sbk, r3(sb_kb), pad_to), _with_past(p_sbv, r3(sb_vb), pad_to),
                            past_len, seq, SAMPLE_TK)
        band_len = _band_window(seq)
        o_bd = band_attention(r3(bd_q), _with_past(p_bdk, r3(bd_kb), band_len), _with_past(p_bdv, r3(bd_vb), band_len),
                              band_mask, seq, past_len - p_bdk.shape[1])
        kidx_all = jnp.pad(jnp.concatenate([p_kidx, r3(kidx)], axis=1), ((0, 0), (0, rows), (0, 0)))
        klo, khi = _kidx_pair(kidx_all)
        o_ds = dsa_attention(r3(idx_q), r3(idxw), klo, khi, r3(ds_q),
                             _with_past(p_dsk, r3(ds_kb), pad_to), _with_past(p_dsv, r3(ds_vb), pad_to),
                             dsa_tables, past_len, total, seq, SAMPLE_TK // LANES)

    f2 = lambda a: a.reshape(t, -1)
    merged = gate_merge(u, [f2(o_mla), f2(o_sb), f2(o_bd), f2(o_ds)], lw["wg"], lw["wb"])
    h_new, = matmul(merged, w_out, (F32,), residual=h, name="mix_out")
    return h_new, (ckv, kr, sb_k, sb_v, bd_k, bd_v, ds_k, ds_v, kidx)


def dense_ffn(h, un, w1, w3, w2):
    act = swiglu_up(un, w1, w3)
    out, = matmul(act, w2, (F32,), residual=h, name="ffn_down")
    return out


def moe_ffn(h, un, router_pad, w1, w3, w2):
    n_experts = w1.shape[0]
    gate = router_gate(un, router_pad, n_experts)
    for e in range(n_experts):
        act = swiglu_up(un, w1[e], w3[e])
        h, = matmul(act, w2[e], (F32,), residual=h, gate=gate, gate_col=e, name="moe_down")
    return h


def layer_step(h, p, bsz, seq, cs, past, lw, band_mask, dsa_tables, fw):
    u = rmsnorm(h, fw["g_mix"], BF)
    h, rows = mixing_block(u, bsz, seq, lw, cs, past, band_mask, dsa_tables, fw["w_out"], h)
    un = rmsnorm(h, fw["g_ffn"], BF)
    if fw["moe"]:
        h = moe_ffn(h, un, fw["router"], fw["w1"], fw["w3"], fw["w2"])
    else:
        h = dense_ffn(h, un, fw["w1"], fw["w3"], fw["w2"])
    un = rmsnorm(h, fw["g_ple"], BF)
    h = ple_update(h, p, un, fw["ple_w"], fw["ple_gate_w"])
    return h, rows


def kernel(x_prompt, x_sample, p_prompt, p_sample, cache_mla_ckv, cache_mla_krope, cache_sb_k, cache_sb_v, cache_band_k, cache_band_v, cache_dsa_k, cache_dsa_v, cache_dsa_kidx, norm_mix, w_in, mla_q_norm, mla_w_uq, mla_kv_norm, mla_w_ukv, band_rel_bias, t5_rel_bias, w_branch, w_out, norm_ffn, ffn_w1, ffn_w3, ffn_w2, moe_router, moe_w1, moe_w3, moe_w2, norm_ple, ple_w, ple_gate_w, norm_final):
    depth = w_in.shape[0]
    bp, sp, d = x_prompt.shape
    bs, ss, _ = x_sample.shape
    past_len = cache_sb_k.shape[2]
    tm = 512
    cs_p = rope_table(jnp.arange(sp))
    cs_s = rope_table(past_len + (jnp.arange(tm) % ss))
    dsa_tables_p = dsa_bias_tables(t5_rel_bias, LANES)
    dsa_tables_s = dsa_bias_tables(t5_rel_bias, ss)
    hp = x_prompt.reshape(bp * sp, d)
    hs = x_sample.reshape(bs * ss, d)
    rows_p, rows_s = [], []
    for i in range(depth):
        lw = prepare_layer_weights(w_in[i], mla_q_norm[i], mla_w_uq[i], mla_kv_norm[i], mla_w_ukv[i], w_branch[i])
        j = i // 2
        fw = dict(g_mix=norm_mix[i], g_ffn=norm_ffn[i], g_ple=norm_ple[i], w_out=w_out[i].astype(BF),
                  ple_w=ple_w[i].astype(BF), ple_gate_w=ple_gate_w[i].astype(BF), moe=i % 2 == 1)
        if i % 2 == 0:
            fw.update(w1=ffn_w1[j].astype(BF), w3=ffn_w3[j].astype(BF), w2=ffn_w2[j].astype(BF))
        else:
            fw.update(router=_pad_cols(moe_router[j], LANES).astype(BF),
                      w1=moe_w1[j].astype(BF), w3=moe_w3[j].astype(BF), w2=moe_w2[j].astype(BF))
        past_i = (cache_mla_ckv[i], cache_mla_krope[i], cache_sb_k[i], cache_sb_v[i], cache_band_k[i],
                  cache_band_v[i], cache_dsa_k[i], cache_dsa_v[i], cache_dsa_kidx[i])
        hp, rp = layer_step(hp, p_prompt[i].reshape(bp * sp, -1).astype(BF), bp, sp, cs_p, None,
                            lw, band_bias_mask(band_rel_bias[i], PROMPT_TQ), dsa_tables_p, fw)
        hs, rs = layer_step(hs, p_sample[i].reshape(bs * ss, -1).astype(BF), bs, ss, cs_s, past_i,
                            lw, band_bias_mask(band_rel_bias[i], ss), dsa_tables_s, fw)
        rows_p.append(rp)
        rows_s.append(rs)
    ones = norm_final
    y_prompt = rmsnorm(hp, ones, F32).reshape(bp, sp, d)
    y_sample = rmsnorm(hs, ones, F32).reshape(bs, ss, d)

    keep = min(N_PREV_CHUNKS * CHUNK, sp)

    def stacked(rows, n, bsz, seq, heads=None, tail=None):
        out = []
        for r in rows:
            a = r[n].reshape(bsz, seq, -1)
            if tail is not None:
                a = a[:, seq - tail:]
            if heads is not None:
                a = a.reshape(a.shape[0], a.shape[1], heads, HEAD_DIM)
            out.append(a)
        return jnp.stack(out, axis=0)

    res = [y_prompt, y_sample]
    for n, heads in ((0, None), (1, None), (2, H_SB), (3, H_SB), (4, H_BAND), (5, H_BAND),
                     (6, H_DSA), (7, H_DSA), (8, None)):
        tail = keep if n in (4, 5) else None
        res.append(stacked(rows_p, n, bp, sp, heads, tail))
        res.append(stacked(rows_s, n, bs, ss, heads))
    return tuple(res)
```

```python
import functools
import math

import jax
import jax.numpy as jnp
from jax import lax
from jax.experimental import pallas as pl
from jax.experimental.pallas import tpu as pltpu

BF = jnp.bfloat16
F32 = jnp.float32

CHUNK = 64
CHUNK_SHIFT = 6
HEAD_DIM = 128
N_BRANCH = 4
H_MLA = 4
NOPE_DIM = 128
ROPE_DIM = 64
V_DIM = 128
ROPE_THETA = 10000.0
H_SB = 4
H_BAND = 4
N_PREV_CHUNKS = 8
REL_CLIP = 128
H_DSA = 4
H_IDX = 16
D_IDX = 64
TOPK_MAX = 256
T5_BUCKETS = 32
T5_MAX_DIST = 128
TOP_K_EXPERTS = 2
EPS = 1e-6

LANES = 128
PROMPT_TQ = 256
PROMPT_TK = 512
SAMPLE_TK = 384
VMEM_LIMIT = 56 * 1024 * 1024
NEG = -1e30
INT_MIN = -2147483648


def _cparams(n_axes):
    return pltpu.CompilerParams(dimension_semantics=("arbitrary",) * n_axes,
                                vmem_limit_bytes=VMEM_LIMIT)


def _tile(n, preferred):
    if n <= preferred:
        return n
    t = preferred - preferred % LANES
    while n % t:
        t -= LANES
    assert t > 0
    return t


def _dot(a, b):
    return jnp.dot(a, b, preferred_element_type=F32)


def _dot_nt(a, b):
    return lax.dot_general(a, b, (((1,), (1,)), ((), ())), preferred_element_type=F32)


def _sigmoid(x):
    return 1.0 / (1.0 + jnp.exp(-x))


def _rms(x, g):
    return x * lax.rsqrt(jnp.mean(x * x, axis=-1, keepdims=True) + EPS) * g


def _rmsnorm_kernel(x_ref, g_ref, o_ref):
    o_ref[...] = _rms(x_ref[...], g_ref[...]).astype(o_ref.dtype)


def rmsnorm(x, g, out_dtype, tm=512):
    t, d = x.shape
    tm = _tile(t, tm)
    return pl.pallas_call(
        _rmsnorm_kernel,
        grid=(t // tm,),
        in_specs=[pl.BlockSpec((tm, d), lambda i: (i, 0)),
                  pl.BlockSpec((1, d), lambda i: (0, 0))],
        out_specs=pl.BlockSpec((tm, d), lambda i: (i, 0)),
        out_shape=jax.ShapeDtypeStruct((t, d), out_dtype),
        compiler_params=_cparams(1),
        name="rmsnorm",
    )(x, g.reshape(1, d))


def _mm_kernel(*refs, has_res, gate_col, n_out):
    a_ref, w_ref = refs[0], refs[1]
    pos = 2
    res_ref = gate_ref = None
    if has_res:
        res_ref = refs[pos]
        pos += 1
    if gate_col is not None:
        gate_ref = refs[pos]
        pos += 1
    r = _dot(a_ref[...], w_ref[...])
    if gate_ref is not None:
        r = r * gate_ref[:, gate_col:gate_col + 1]
    if res_ref is not None:
        r = res_ref[...] + r
    for o_ref in refs[pos:pos + n_out]:
        o_ref[...] = r.astype(o_ref.dtype)


WEIGHT_BLOCK_BYTES = 6 * 1024 * 1024


def matmul(a, w, out_dtypes, residual=None, gate=None, gate_col=None, tm=512, name="mm"):
    m, k = a.shape
    n = w.shape[1]
    tm, tn = _tile(m, tm), _tile(n, WEIGHT_BLOCK_BYTES // (2 * k))
    in_specs = [pl.BlockSpec((tm, k), lambda j, i: (i, 0)),
                pl.BlockSpec((k, tn), lambda j, i: (0, j))]
    args = [a, w]
    if residual is not None:
        in_specs.append(pl.BlockSpec((tm, tn), lambda j, i: (i, j)))
        args.append(residual)
    if gate is not None:
        in_specs.append(pl.BlockSpec((tm, gate.shape[1]), lambda j, i: (i, 0)))
        args.append(gate)
    outs = pl.pallas_call(
        functools.partial(_mm_kernel, has_res=residual is not None,
                          gate_col=gate_col if gate is not None else None, n_out=len(out_dtypes)),
        grid=(n // tn, m // tm),
        in_specs=in_specs,
        out_specs=[pl.BlockSpec((tm, tn), lambda j, i: (i, j)) for _ in out_dtypes],
        out_shape=[jax.ShapeDtypeStruct((m, n), dt) for dt in out_dtypes],
        compiler_params=_cparams(2),
        name=name,
    )(*args)
    return outs


def _swiglu_up_kernel(a_ref, w1_ref, w3_ref, o_ref):
    a = a_ref[...]
    x1 = _dot(a, w1_ref[...])
    x3 = _dot(a, w3_ref[...])
    o_ref[...] = (x1 * _sigmoid(x1) * x3).astype(o_ref.dtype)


def swiglu_up(a, w1, w3, tm=512):
    m, k = a.shape
    n = w1.shape[1]
    tm, tn = _tile(m, tm), _tile(n, WEIGHT_BLOCK_BYTES // (2 * k))
    return pl.pallas_call(
        _swiglu_up_kernel,
        grid=(n // tn, m // tm),
        in_specs=[pl.BlockSpec((tm, k), lambda j, i: (i, 0)),
                  pl.BlockSpec((k, tn), lambda j, i: (0, j)),
                  pl.BlockSpec((k, tn), lambda j, i: (0, j))],
        out_specs=pl.BlockSpec((tm, tn), lambda j, i: (i, j)),
        out_shape=jax.ShapeDtypeStruct((m, n), BF),
        compiler_params=_cparams(2),
        name="swiglu_up",
    )(a, w1, w3)


def _gate_merge_kernel(u_ref, o0_ref, o1_ref, o2_ref, o3_ref, wg_ref, wb_ref, out_ref):
    u = u_ref[...]
    acc = None
    for b, o_ref in enumerate((o0_ref, o1_ref, o2_ref, o3_ref)):
        t = _sigmoid(_dot(u, wg_ref[b])) * _dot(o_ref[...], wb_ref[b])
        acc = t if acc is None else acc + t
    out_ref[...] = acc.astype(out_ref.dtype)


def gate_merge(u, branches, wg, wb, tm=512, tn=512):
    t, d = u.shape
    bw = branches[0].shape[1]
    n = wg.shape[2]
    tm = min(tm, t)
    return pl.pallas_call(
        _gate_merge_kernel,
        grid=(n // tn, t // tm),
        in_specs=[pl.BlockSpec((tm, d), lambda j, i: (i, 0))]
        + [pl.BlockSpec((tm, bw), lambda j, i: (i, 0)) for _ in range(N_BRANCH)]
        + [pl.BlockSpec((N_BRANCH, d, tn), lambda j, i: (0, 0, j)),
           pl.BlockSpec((N_BRANCH, bw, tn), lambda j, i: (0, 0, j))],
        out_specs=pl.BlockSpec((tm, tn), lambda j, i: (i, j)),
        out_shape=jax.ShapeDtypeStruct((t, n), BF),
        compiler_params=_cparams(2),
        name="gate_merge",
    )(u, *branches, wg, wb)


def _ple_kernel(h_ref, p_ref, un_ref, wp_ref, wg_ref, o_ref):
    o_ref[...] = h_ref[...] + _dot(p_ref[...], wp_ref[...]) * _sigmoid(_dot(un_ref[...], wg_ref[...]))


def ple_update(h, p, un, wp, wg, tm=512, tn=1024):
    t, d = h.shape
    tm = min(tm, t)
    return pl.pallas_call(
        _ple_kernel,
        grid=(d // tn, t // tm),
        in_specs=[pl.BlockSpec((tm, tn), lambda j, i: (i, j)),
                  pl.BlockSpec((tm, p.shape[1]), lambda j, i: (i, 0)),
                  pl.BlockSpec((tm, d), lambda j, i: (i, 0)),
                  pl.BlockSpec((p.shape[1], tn), lambda j, i: (0, j)),
                  pl.BlockSpec((d, tn), lambda j, i: (0, j))],
        out_specs=pl.BlockSpec((tm, tn), lambda j, i: (i, j)),
        out_shape=jax.ShapeDtypeStruct((t, d), F32),
        compiler_params=_cparams(2),
        name="ple_update",
    )(h, p, un, wp, wg)


def _router_kernel(un_ref, wr_ref, g_ref, *, n_experts):
    logits = _dot(un_ref[...], wr_ref[...])
    lane = lax.broadcasted_iota(jnp.int32, logits.shape, 1).astype(F32)
    real = lane < n_experts
    logits = jnp.where(real, logits, NEG)
    e = jnp.where(real, jnp.exp(logits - jnp.max(logits, axis=1, keepdims=True)), 0.0)
    probs = e / jnp.sum(e, axis=1, keepdims=True)
    p1 = jnp.max(probs, axis=1, keepdims=True)
    i1 = jnp.min(jnp.where(probs == p1, lane, float(LANES)), axis=1, keepdims=True)
    first = lane == i1
    rest = jnp.where(first | ~real, -1.0, probs)
    p2 = jnp.max(rest, axis=1, keepdims=True)
    i2 = jnp.min(jnp.where(rest == p2, lane, float(LANES)), axis=1, keepdims=True)
    second = lane == i2
    denom = p1 + p2
    g_ref[...] = jnp.where(first, p1 / denom, 0.0) + jnp.where(second, p2 / denom, 0.0)


def router_gate(un, wr_pad, n_experts, tm=512):
    t, d = un.shape
    tm = min(tm, t)
    return pl.pallas_call(
        functools.partial(_router_kernel, n_experts=n_experts),
        grid=(t // tm,),
        in_specs=[pl.BlockSpec((tm, d), lambda i: (i, 0)),
                  pl.BlockSpec((d, LANES), lambda i: (0, 0))],
        out_specs=pl.BlockSpec((tm, LANES), lambda i: (i, 0)),
        out_shape=jax.ShapeDtypeStruct((t, LANES), F32),
        compiler_params=_cparams(1),
        name="router_gate",
    )(un, wr_pad)


Q_LORA_OFF = 0


def _proj_misc_kernel(u_ref, wm_ref, qn_ref, kvn_ref, wqa_ref, wqb_ref, cs_ref,
                      ckv_ref, krp_ref, kidx_ref, idxw_ref, qcat_ref, *, q_lora, kv_lora):
    x = _dot(u_ref[...], wm_ref[...])
    cos = cs_ref[:, 0:LANES]
    sin = cs_ref[:, LANES:2 * LANES]
    o = q_lora
    ckv_ref[...] = _rms(x[:, o:o + kv_lora], kvn_ref[...])
    o += kv_lora
    krp_ref[...] = x[:, o:o + LANES] * cos + x[:, o + LANES:o + 2 * LANES] * sin
    o += 2 * LANES
    kidx_ref[...] = x[:, o:o + LANES]
    idxw_ref[...] = x[:, o + LANES:o + 2 * LANES]
    cqn = _rms(x[:, 0:q_lora], qn_ref[...]).astype(BF)
    qa = _dot(cqn, wqa_ref[...])
    qb = _dot(cqn, wqb_ref[...])
    for h in range(H_MLA):
        lo = 2 * h * LANES
        qcat_ref[:, lo:lo + LANES] = qa[:, lo:lo + LANES].astype(BF)
        qcat_ref[:, lo + LANES:lo + 2 * LANES] = (
            qa[:, lo + LANES:lo + 2 * LANES] * cos + qb[:, h * LANES:(h + 1) * LANES] * sin).astype(BF)


def proj_misc(u, wm, qn, kvn, wqa, wqb, cs, tm=512):
    t, d = u.shape
    tm = min(tm, t)
    q_lora, kv_lora = qn.shape[1], kvn.shape[1]
    n_pos_blocks = cs.shape[0] // tm
    row = lambda n: pl.BlockSpec((tm, n), lambda i: (i, 0))
    full = lambda a: pl.BlockSpec(a.shape, lambda i: (0, 0))
    return pl.pallas_call(
        functools.partial(_proj_misc_kernel, q_lora=q_lora, kv_lora=kv_lora),
        grid=(t // tm,),
        in_specs=[row(d), full(wm), full(qn), full(kvn), full(wqa), full(wqb),
                  pl.BlockSpec((tm, 2 * LANES), lambda i: (i % n_pos_blocks, 0))],
        out_specs=[row(kv_lora), row(LANES), row(LANES), row(LANES), row(2 * LANES * H_MLA)],
        out_shape=[jax.ShapeDtypeStruct((t, kv_lora), F32),
                   jax.ShapeDtypeStruct((t, LANES), F32),
                   jax.ShapeDtypeStruct((t, LANES), F32),
                   jax.ShapeDtypeStruct((t, LANES), F32),
                   jax.ShapeDtypeStruct((t, 2 * LANES * H_MLA), BF)],
        compiler_params=_cparams(1),
        name="proj_misc",
    )(u, wm, qn, kvn, wqa, wqb, cs)


def _mla_kv_up_kernel(ckv_ref, krp_ref, wk_ref, wv_ref, kcat_ref, v_ref):
    c = ckv_ref[...].astype(BF)
    kn = _dot(c, wk_ref[...])
    krp = krp_ref[...].astype(BF)
    for h in range(H_MLA):
        kcat_ref[:, 2 * h * LANES:(2 * h + 1) * LANES] = kn[:, h * LANES:(h + 1) * LANES].astype(BF)
        kcat_ref[:, (2 * h + 1) * LANES:(2 * h + 2) * LANES] = krp
    v_ref[...] = _dot(c, wv_ref[...]).astype(BF)


def mla_kv_up(ckv, krp, wk, wv, tm=512):
    t, c = ckv.shape
    tm = min(tm, t)
    row = lambda n: pl.BlockSpec((tm, n), lambda i: (i, 0))
    full = lambda a: pl.BlockSpec(a.shape, lambda i: (0, 0))
    return pl.pallas_call(
        _mla_kv_up_kernel,
        grid=(t // tm,),
        in_specs=[row(c), row(LANES), full(wk), full(wv)],
        out_specs=[row(2 * LANES * H_MLA), row(V_DIM * H_MLA)],
        out_shape=[jax.ShapeDtypeStruct((t, 2 * LANES * H_MLA), BF),
                   jax.ShapeDtypeStruct((t, V_DIM * H_MLA), BF)],
        compiler_params=_cparams(1),
        name="mla_kv_up",
    )(ckv, krp, wk, wv)


def _mla_attn_kernel(qt_ref, k_ref, vt_ref, o_ref, *, tq, tk, q_off, scale, heads):
    q0 = q_off + pl.program_id(2) * tq
    qpos = q0 + lax.broadcasted_iota(jnp.int32, (1, tq), 1)
    limit = (lax.shift_right_arithmetic(qpos, CHUNK_SHIFT) + 1) * CHUNK
    last_limit = ((q0 + tq - 1) // CHUNK + 1) * CHUNK
    n_chunks = (last_limit + tk - 1) // tk

    def body(kc, carry):
        ks = pl.multiple_of(kc * tk, tk)
        kpos = ks + lax.broadcasted_iota(jnp.int32, (tk, tq), 0)
        visible = kpos < limit
        scores = [_dot(k_ref[pl.ds(ks, tk), 2 * h * LANES:2 * (h + 1) * LANES], qt_ref[h]) for h in range(heads)]
        probs, stats = [], []
        for h in range(heads):
            m_prev, l_prev, _ = carry[h]
            s = jnp.where(visible, scores[h] * scale, NEG)
            m_new = jnp.maximum(m_prev, jnp.max(s, axis=0, keepdims=True))
            p = jnp.exp(s - m_new)
            alpha = jnp.exp(m_prev - m_new)
            stats.append((m_new, alpha * l_prev + jnp.sum(p, axis=0, keepdims=True), alpha))
            probs.append(p.astype(BF))
        out = []
        for h in range(heads):
            m_new, l_new, alpha = stats[h]
            out.append((m_new, l_new, alpha * carry[h][2] + _dot(vt_ref[h, kc], probs[h])))
        return tuple(out)

    init = tuple((jnp.full((1, tq), NEG, F32), jnp.zeros((1, tq), F32), jnp.zeros((V_DIM, tq), F32))
                 for _ in range(heads))
    final = lax.fori_loop(0, n_chunks, body, init)
    for h in range(heads):
        _, l_fin, acc = final[h]
        o_ref[:, h * V_DIM:(h + 1) * V_DIM] = (acc / l_fin).T.astype(o_ref.dtype)


def _chunked_transpose(v, heads, tk):
    b, s, hd = v.shape
    d = hd // heads
    return jnp.transpose(v.reshape(b, s // tk, tk, heads, d), (0, 3, 1, 4, 2))


def mla_attention(qcat, kcat, v, q_off, tq, tk, heads=4):
    b, sq, _ = qcat.shape
    sk = kcat.shape[1]
    scale = (NOPE_DIM + ROPE_DIM) ** -0.5
    qt = jnp.transpose(qcat.reshape(b, sq, H_MLA, 2 * LANES), (0, 2, 3, 1))
    vt = _chunked_transpose(v, H_MLA, tk)
    return pl.pallas_call(
        functools.partial(_mla_attn_kernel, tq=tq, tk=tk, q_off=q_off, scale=scale, heads=heads),
        grid=(b, H_MLA // heads, sq // tq),
        in_specs=[pl.BlockSpec((None, heads, 2 * LANES, tq), lambda bi, h, qi: (bi, h, 0, qi)),
                  pl.BlockSpec((None, sk, heads * 2 * LANES), lambda bi, h, qi: (bi, 0, h)),
                  pl.BlockSpec((None, heads, sk // tk, V_DIM, tk), lambda bi, h, qi: (bi, h, 0, 0, 0))],
        out_specs=pl.BlockSpec((None, tq, heads * V_DIM), lambda bi, h, qi: (bi, qi, h)),
        out_shape=jax.ShapeDtypeStruct((b, sq, H_MLA * V_DIM), BF),
        compiler_params=_cparams(3),
        name="mla_attention",
    )(qt, kcat, vt)


def _split3(x):
    hi = x.astype(BF)
    r = x - hi.astype(F32)
    mid = r.astype(BF)
    lo = (r - mid.astype(F32)).astype(BF)
    return hi, mid, lo


def _sb_attn_kernel(q_ref, k_ref, v_ref, tri_ref, o_ref, *, tq, tk, q_off, scale, heads):
    q0 = q_off + pl.program_id(2) * tq
    qpos = q0 + lax.broadcasted_iota(jnp.int32, (tq, 1), 0)
    n_chunks = (q0 + tq - 1 + tk - 1) // tk
    tri2 = tri_ref[...]
    tri = tri2[0:LANES]
    n_blk = tk // LANES
    hs = [slice(h * HEAD_DIM, (h + 1) * HEAD_DIM) for h in range(heads)]

    def step(kc, carry, diagonal):
        ks = pl.multiple_of(kc * tk, tk)
        if diagonal:
            strict = ks + lax.broadcasted_iota(jnp.int32, (1, tk), 1) < qpos
            causal = lambda x: jnp.where(strict, x, 0.0)
        else:
            causal = lambda x: x
        z = [_dot_nt(q_ref[:, hs[h]], k_ref[pl.ds(ks, tk), hs[h]]) for h in range(heads)]
        log_beta, parts = [], []
        for h in range(heads):
            zh = z[h] * scale
            sp = jnp.maximum(zh, 0.0) + jnp.log(1.0 + jnp.exp(-jnp.abs(zh)))
            log_beta.append(zh - sp)
            parts.append(_split3(causal(-sp)))
        after, later = [], []
        for h in range(heads):
            hi, mid, lo = parts[h]
            run = carry[h][1]
            blocks = [None] * n_blk
            for blk in reversed(range(n_blk)):
                sl = slice(blk * LANES, (blk + 1) * LANES)
                sums = _dot(jnp.concatenate([hi[:, sl], mid[:, sl]], axis=1), tri2) + _dot(lo[:, sl], tri)
                blocks[blk] = sums[:, 0:LANES] + run
                run = run + sums[:, LANES:2 * LANES]
            after.append(jnp.concatenate(blocks, axis=1))
            later.append(run)
        weights = [causal(jnp.exp(log_beta[h] + after[h])).astype(BF) for h in range(heads)]
        return tuple((carry[h][0] + _dot(weights[h], v_ref[pl.ds(ks, tk), hs[h]]), later[h]) for h in range(heads))

    n_below = q0 // tk
    init = tuple((jnp.zeros((tq, HEAD_DIM), F32), jnp.zeros((tq, LANES), F32)) for _ in range(heads))
    state = lax.fori_loop(0, n_chunks - n_below, lambda it, c: step(n_chunks - 1 - it, c, True), init)
    final = lax.fori_loop(0, n_below, lambda it, c: step(n_below - 1 - it, c, False), state)
    for h in range(heads):
        o_ref[:, hs[h]] = final[h][0].astype(o_ref.dtype)


def sb_attention(q, k, v, q_off, tq, tk, heads=2):
    b, sq, _ = q.shape
    sk = k.shape[1]
    assert sk % tk == 0 and tk % LANES == 0 and H_SB % heads == 0
    j = lax.broadcasted_iota(jnp.int32, (LANES, LANES), 0)
    s = lax.broadcasted_iota(jnp.int32, (LANES, LANES), 1)
    tri = jnp.concatenate([(j > s).astype(BF), jnp.ones((LANES, LANES), BF)], axis=1)
    tri = jnp.concatenate([tri, tri], axis=0)
    width = heads * HEAD_DIM
    return pl.pallas_call(
        functools.partial(_sb_attn_kernel, tq=tq, tk=tk, q_off=q_off, scale=HEAD_DIM ** -0.5, heads=heads),
        grid=(b, H_SB // heads, sq // tq),
        in_specs=[pl.BlockSpec((None, tq, width), lambda bi, h, qi: (bi, qi, h)),
                  pl.BlockSpec((None, sk, width), lambda bi, h, qi: (bi, 0, h)),
                  pl.BlockSpec((None, sk, width), lambda bi, h, qi: (bi, 0, h)),
                  pl.BlockSpec((2 * LANES, 2 * LANES), lambda bi, h, qi: (0, 0))],
        out_specs=pl.BlockSpec((None, tq, width), lambda bi, h, qi: (bi, qi, h)),
        out_shape=jax.ShapeDtypeStruct((b, sq, H_SB * HEAD_DIM), BF),
        compiler_params=_cparams(3),
        name="sb_attention",
    )(q, k, v, tri)


def _band_attn_kernel(q_ref, k_ref, v_ref, bm_ref, o_ref, *, tq, win, kpos_base, scale):
    w0 = pl.multiple_of(pl.program_id(2) * tq, tq)
    k = k_ref[pl.ds(w0, win), :]
    s = _dot_nt(q_ref[...], k) * scale + bm_ref[...]
    kpos = kpos_base + w0 + lax.broadcasted_iota(jnp.int32, (1, win), 1)
    s = jnp.where(kpos >= 0, s, NEG)
    p = jnp.exp(s - jnp.max(s, axis=1, keepdims=True))
    denom = jnp.sum(p, axis=1, keepdims=True)
    o = _dot(p.astype(BF), v_ref[pl.ds(w0, win), :])
    o_ref[...] = (o / denom).astype(o_ref.dtype)


def _band_window(tq):
    return -(-(tq + N_PREV_CHUNKS * CHUNK) // LANES) * LANES


def band_bias_mask(rel_bias, tq):
    win = _band_window(tq)
    i = jnp.arange(tq)[:, None]
    j = jnp.arange(win)[None, :]
    rel = jnp.clip(i + N_PREV_CHUNKS * CHUNK - j, -REL_CLIP, REL_CLIP) + REL_CLIP
    ci, cj = i // CHUNK, j // CHUNK
    inside = (cj >= ci) & (cj <= ci + N_PREV_CHUNKS)
    return jnp.where(inside[None], _table_lookup(rel_bias, rel), NEG)


def band_attention(q, k_pad, v_pad, bias_mask, tq, kpos_base):
    b, sq, _ = q.shape
    skp = k_pad.shape[1]
    win = _band_window(tq)
    assert skp >= sq - tq + win
    return pl.pallas_call(
        functools.partial(_band_attn_kernel, tq=tq, win=win, kpos_base=kpos_base, scale=HEAD_DIM ** -0.5),
        grid=(b, H_BAND, sq // tq),
        in_specs=[pl.BlockSpec((None, tq, HEAD_DIM), lambda bi, h, qi: (bi, qi, h)),
                  pl.BlockSpec((None, skp, HEAD_DIM), lambda bi, h, qi: (bi, 0, h)),
                  pl.BlockSpec((None, skp, HEAD_DIM), lambda bi, h, qi: (bi, 0, h)),
                  pl.BlockSpec((None, tq, win), lambda bi, h, qi: (h, 0, 0))],
        out_specs=pl.BlockSpec((None, tq, HEAD_DIM), lambda bi, h, qi: (bi, qi, h)),
        out_shape=jax.ShapeDtypeStruct((b, sq, H_BAND * HEAD_DIM), BF),
        compiler_params=_cparams(3),
        name="band_attention",
    )(q, k_pad, v_pad, bias_mask)


def _sortable(x):
    i = lax.bitcast_convert_type(x, jnp.int32)
    return i ^ (lax.shift_right_arithmetic(i, 31) & 0x7FFFFFFF)


def _dsa_kernel(qi2t_ref, wt_ref, klo_ref, khi_ref, qt_ref, k_ref, vt_ref, bnear_ref, bfar_ref, tri_ref,
                o_ref, key_ref, m_ref, l_ref, acc_ref, eqc_ref,
                *, tq, tk, big, q_off, topk, scale, w_scale):
    wide = big * tk
    near_after = -(-tq // tk)
    q0 = q_off + pl.program_id(1) * tq
    qpos = q0 + lax.broadcasted_iota(jnp.int32, (1, tq), 1)
    limit = (lax.shift_right_arithmetic(qpos, CHUNK_SHIFT) + 1) * CHUNK
    last_limit = ((q0 + tq - 1) // CHUNK + 1) * CHUNK
    n_wide = (last_limit + wide - 1) // wide
    diag = q0 // tk

    w = wt_ref[...] * w_scale

    def score_body(c, carry):
        ks = pl.multiple_of(c * wide, wide)
        klo = klo_ref[pl.ds(ks, wide), :]
        khi = khi_ref[pl.ds(ks, wide), :]
        acc = jnp.zeros((wide, tq), F32)
        for pair in range(H_IDX // 2):
            q2 = qi2t_ref[pair]
            acc = acc + w[2 * pair:2 * pair + 1] * jnp.maximum(_dot(klo, q2), 0.0)
            acc = acc + w[2 * pair + 1:2 * pair + 2] * jnp.maximum(_dot(khi, q2), 0.0)
        kpos = ks + lax.broadcasted_iota(jnp.int32, (wide, tq), 0)
        keys = _sortable(jnp.where(kpos < limit, acc + 0.0, -jnp.inf))
        for blk in range(big):
            key_ref[c * big + blk] = keys[blk * tk:(blk + 1) * tk]
        return carry

    lax.fori_loop(0, n_wide, score_body, 0)

    def count_ge(cand):
        def body(c, cnt):
            for blk in range(big):
                cnt = cnt + jnp.where(key_ref[c * big + blk] >= cand, 1.0, 0.0)
            return cnt
        cnt = lax.fori_loop(0, n_wide, body, jnp.zeros((tk, tq), F32))
        return jnp.sum(cnt, axis=0, keepdims=True)

    def bit_body(it, thr):
        cand = thr + lax.shift_left(jnp.int32(1), 31 - it)
        return jnp.where(count_ge(cand) >= topk, cand, thr)

    thr = lax.fori_loop(0, 32, bit_body, jnp.full((1, tq), INT_MIN, jnp.int32))
    n_above = count_ge(thr + 1)
    n_ties_kept = topk - n_above

    m_ref[...] = jnp.full(m_ref.shape, NEG, F32)
    l_ref[...] = jnp.zeros(l_ref.shape, F32)
    acc_ref[...] = jnp.zeros(acc_ref.shape, F32)
    eqc_ref[...] = jnp.zeros(eqc_ref.shape, F32)
    tri = tri_ref[...]
    hs = [slice(h * HEAD_DIM, (h + 1) * HEAD_DIM) for h in range(H_DSA)]

    def attend(first_blk, n_blk, bias_of_head, below_tile):
        width = n_blk * tk
        ks = pl.multiple_of(first_blk * tk, tk)
        key = jnp.concatenate([key_ref[first_blk + i] for i in range(n_blk)], axis=0)
        eq = key == thr
        eq_bf = jnp.where(eq, 1.0, 0.0).astype(BF)
        seen = eqc_ref[0:1, :]
        rank = [None] * n_blk
        for i in range(n_blk):
            counts = _dot(tri, eq_bf[i * tk:(i + 1) * tk])
            rank[i] = counts[0:tk] + seen
            seen = seen + counts[tk:tk + 1]
        eqc_ref[0:1, :] = seen
        sel = (key > thr) | (eq & (jnp.concatenate(rank, axis=0) <= n_ties_kept))
        if not below_tile:
            sel = sel & (ks + lax.broadcasted_iota(jnp.int32, (width, tq), 0) < limit)
        scores = [_dot(k_ref[pl.ds(ks, width), hs[h]], qt_ref[h]) for h in range(H_DSA)]
        probs, alphas = [], []
        for h in range(H_DSA):
            s = jnp.where(sel, scores[h] * scale + bias_of_head(h), NEG)
            m_prev = m_ref[h]
            m_new = jnp.maximum(m_prev, jnp.max(s, axis=0, keepdims=True))
            p = jnp.where(sel, jnp.exp(s - m_new), 0.0)
            alpha = jnp.exp(m_prev - m_new)
            l_ref[h] = alpha * l_ref[h] + jnp.sum(p, axis=0, keepdims=True)
            m_ref[h] = m_new
            probs.append(p.astype(BF))
            alphas.append(alpha)
        for h in range(H_DSA):
            vt = jnp.concatenate([vt_ref[h, first_blk + i] for i in range(n_blk)], axis=1)
            acc_ref[h] = alphas[h] * acc_ref[h] + _dot(vt, probs[h])

    far_bias = lambda h: bfar_ref[h][:, 0:1]
    n_far_wide = jnp.maximum(q0 - tk, 0) // wide

    def far_wide_body(c, carry):
        attend(c * big, big, far_bias, True)
        return carry

    lax.fori_loop(0, n_far_wide, far_wide_body, 0)

    def far_body(kc, carry):
        attend(kc, 1, far_bias, True)
        return carry

    lax.fori_loop(n_far_wide * big, jnp.maximum(diag - 1, n_far_wide * big), far_body, 0)

    @pl.when(diag >= 1)
    def _():
        attend(diag - 1, 1, lambda h: bnear_ref[0, h], True)

    for d in range(near_after):
        attend(diag + d, 1, lambda h, d=d: bnear_ref[d + 1, h], False)

    for h in range(H_DSA):
        o_ref[:, hs[h]] = (acc_ref[h] / l_ref[h]).T.astype(o_ref.dtype)


def t5_bucket(rel):
    half = T5_BUCKETS // 2
    max_exact = half // 2
    n = jnp.abs(rel)
    nf = jnp.maximum(n, 1).astype(F32)
    large = max_exact + (jnp.log(nf / max_exact) / math.log(T5_MAX_DIST / max_exact)
                         * (half - max_exact)).astype(jnp.int32)
    large = jnp.minimum(large, half - 1)
    return jnp.where(rel > 0, half, 0) + jnp.where(n < max_exact, n, large)


def _table_lookup(table, idx):
    onehot = jax.nn.one_hot(idx, table.shape[0], dtype=F32)
    out = jnp.einsum("...n,nh->...h", onehot, table.astype(F32), precision=lax.Precision.HIGHEST)
    return jnp.moveaxis(out, -1, 0)


def dsa_bias_tables(t5_table, tq, tk=LANES):
    j = jnp.arange(tk)[:, None]
    i = jnp.arange(tq)[None, :]
    near = jnp.stack([_table_lookup(t5_table, t5_bucket(d * tk + j - i)) for d in range(-1, -(-tq // tk))],
                     axis=0)
    far_rel = -jnp.ones((1, tk), jnp.int32) * (2 * tk)
    far = _table_lookup(t5_table, t5_bucket(far_rel))
    return near, far


def dsa_attention(idx_q, idxw, klo, khi, q, k, v, bias_tables, q_off, n_keys, tq, big, tk=LANES):
    b, sq, _ = q.shape
    sk = k.shape[1]
    assert tk >= T5_MAX_DIST and q_off % tk == 0 and (tq % tk == 0 or sq == tq) and sk % (big * tk) == 0
    topk = min(TOPK_MAX, n_keys // 4)
    near, far = bias_tables
    ss = lax.broadcasted_iota(jnp.int32, (tk + 16, tk), 0)
    jj = lax.broadcasted_iota(jnp.int32, (tk + 16, tk), 1)
    tri = ((jj <= ss) | (ss >= tk)).astype(BF)
    qi2t = jnp.transpose(idx_q.reshape(b, sq, H_IDX // 2, LANES), (0, 2, 3, 1))
    wt = jnp.transpose(idxw[:, :, :H_IDX], (0, 2, 1))
    qt = jnp.transpose(q.reshape(b, sq, H_DSA, HEAD_DIM), (0, 2, 3, 1))
    vt = _chunked_transpose(v, H_DSA, tk)
    whole = lambda n: pl.BlockSpec((None, sk, n), lambda bi, qi: (bi, 0, 0))
    heads_t = lambda h, d: pl.BlockSpec((None, h, d, tq), lambda bi, qi: (bi, 0, 0, qi))
    const = lambda a: pl.BlockSpec(a.shape, lambda bi, qi: (0,) * a.ndim)
    return pl.pallas_call(
        functools.partial(_dsa_kernel, tq=tq, tk=tk, big=big, q_off=q_off, topk=topk,
                          scale=HEAD_DIM ** -0.5, w_scale=H_IDX ** -0.5 * D_IDX ** -0.5),
        grid=(b, sq // tq),
        in_specs=[heads_t(H_IDX // 2, LANES), pl.BlockSpec((None, H_IDX, tq), lambda bi, qi: (bi, 0, qi)),
                  whole(LANES), whole(LANES), heads_t(H_DSA, HEAD_DIM), whole(H_DSA * HEAD_DIM),
                  pl.BlockSpec((None, H_DSA, sk // tk, HEAD_DIM, tk), lambda bi, qi: (bi, 0, 0, 0, 0)),
                  const(near), const(far), const(tri)],
        out_specs=pl.BlockSpec((None, tq, H_DSA * HEAD_DIM), lambda bi, qi: (bi, qi, 0)),
        out_shape=jax.ShapeDtypeStruct((b, sq, H_DSA * HEAD_DIM), BF),
        scratch_shapes=[pltpu.VMEM((sk // tk, tk, tq), jnp.int32),
                        pltpu.VMEM((H_DSA, 1, tq), F32), pltpu.VMEM((H_DSA, 1, tq), F32),
                        pltpu.VMEM((H_DSA, HEAD_DIM, tq), F32), pltpu.VMEM((8, tq), F32)],
        compiler_params=_cparams(2),
        name="dsa_attention",
    )(qi2t, wt, klo, khi, qt, k, vt, near, far, tri)


def _pad_cols(a, n):
    return jnp.pad(a, ((0, 0), (0, n - a.shape[1])))


def _rotate_half_cols(w):
    half = w.shape[1] // 2
    return jnp.concatenate([-w[:, half:], w[:, :half]], axis=1)


def prepare_layer_weights(w_in, q_norm, w_uq, kv_norm, w_ukv, w_branch):
    d = w_in.shape[0]
    q_lora, kv_lora = q_norm.shape[0], kv_norm.shape[0]
    sizes = (q_lora, kv_lora, ROPE_DIM, 3 * H_SB * HEAD_DIM, 3 * H_BAND * HEAD_DIM, 3 * H_DSA * HEAD_DIM,
             H_IDX * D_IDX, D_IDX, H_IDX, N_BRANCH * d)
    cols, start = [], 0
    for s in sizes:
        cols.append(w_in[:, start:start + s])
        start += s
    w_cq, w_ckv, w_kr, w_sb, w_bd, w_ds, w_iq, w_ik, w_iw, w_g = cols
    wm = jnp.concatenate([w_cq, w_ckv, _pad_cols(w_kr, LANES), _pad_cols(_rotate_half_cols(w_kr), LANES),
                          _pad_cols(w_ik, LANES), _pad_cols(w_iw, LANES)], axis=1).astype(BF)
    qa, qb = [], []
    hd = NOPE_DIM + ROPE_DIM
    for h in range(H_MLA):
        wh = w_uq[:, h * hd:(h + 1) * hd]
        qa += [wh[:, :NOPE_DIM], _pad_cols(wh[:, NOPE_DIM:], LANES)]
        qb.append(_pad_cols(_rotate_half_cols(wh[:, NOPE_DIM:]), LANES))
    wqa = jnp.concatenate(qa, axis=1).astype(BF)
    wqb = jnp.concatenate(qb, axis=1).astype(BF)
    kvd = NOPE_DIM + V_DIM
    wk = jnp.concatenate([w_ukv[:, h * kvd:h * kvd + NOPE_DIM] for h in range(H_MLA)], axis=1).astype(BF)
    wv = jnp.concatenate([w_ukv[:, h * kvd + NOPE_DIM:(h + 1) * kvd] for h in range(H_MLA)], axis=1).astype(BF)
    wg = jnp.transpose(w_g.reshape(d, N_BRANCH, d), (1, 0, 2)).astype(BF)
    return dict(wm=wm, wqa=wqa, wqb=wqb, wk=wk, wv=wv,
                w_sb=w_sb.astype(BF), w_bd=w_bd.astype(BF), w_ds=w_ds.astype(BF), w_iq=w_iq.astype(BF),
                wg=wg, wb=w_branch.astype(BF),
                qn=q_norm.reshape(1, -1), kvn=kv_norm.reshape(1, -1))


def rope_table(pos):
    half = ROPE_DIM // 2
    inv = ROPE_THETA ** (-jnp.arange(half, dtype=F32) / half)
    ang = pos.astype(F32)[:, None] * inv[None, :]
    cos, sin = jnp.cos(ang), jnp.sin(ang)
    z = jnp.zeros((pos.shape[0], LANES - ROPE_DIM), F32)
    return jnp.concatenate([cos, cos, z, sin, sin, z], axis=1)


def _kidx_pair(kidx):
    kb = kidx.astype(BF)
    return (jnp.pad(kb, ((0, 0), (0, 0), (0, LANES - D_IDX))),
            jnp.pad(kb, ((0, 0), (0, 0), (LANES - D_IDX, 0))))


def _with_past(past, new, pad_to):
    b = new.shape[0]
    a = jnp.concatenate([past.reshape(b, past.shape[1], -1).astype(BF), new.astype(BF)], axis=1)
    return jnp.pad(a, ((0, 0), (0, pad_to - a.shape[1]), (0, 0)))


def mixing_block(u, bsz, seq, lw, cs, past, band_mask, dsa_tables, w_out, h):
    t = bsz * seq
    ckv, krp, kidxp, idxw, qcat = proj_misc(u, lw["wm"], lw["qn"], lw["kvn"], lw["wqa"], lw["wqb"], cs)
    hw = H_SB * HEAD_DIM

    def qkv(w):
        q, = matmul(u, w[:, :hw], (BF,), tm=1024, name="proj_q")
        k32, kbf = matmul(u, w[:, hw:2 * hw], (F32, BF), tm=1024, name="proj_k")
        v32, vbf = matmul(u, w[:, 2 * hw:], (F32, BF), tm=1024, name="proj_v")
        return q, k32, kbf, v32, vbf

    sb_q, sb_k, sb_kb, sb_v, sb_vb = qkv(lw["w_sb"])
    bd_q, bd_k, bd_kb, bd_v, bd_vb = qkv(lw["w_bd"])
    ds_q, ds_k, ds_kb, ds_v, ds_vb = qkv(lw["w_ds"])
    idx_q, = matmul(u, lw["w_iq"], (BF,), tm=1024, name="proj_idxq")
    kr = krp[:, :ROPE_DIM]
    kidx = kidxp[:, :D_IDX]
    r3 = lambda a: a.reshape(bsz, seq, -1)
    band_pad = N_PREV_CHUNKS * CHUNK

    if past is None:
        kcat, vmla = mla_kv_up(ckv, krp, lw["wk"], lw["wv"])
        o_mla = mla_attention(r3(qcat), r3(kcat), r3(vmla), 0, PROMPT_TQ, PROMPT_TK)
        o_sb = sb_attention(r3(sb_q), r3(sb_kb), r3(sb_vb), 0, PROMPT_TQ, PROMPT_TK)
        front = ((0, 0), (band_pad, 0), (0, 0))
        o_bd = band_attention(r3(bd_q), jnp.pad(r3(bd_kb), front), jnp.pad(r3(bd_vb), front),
                              band_mask, PROMPT_TQ, -band_pad)
        klo, khi = _kidx_pair(r3(kidx))
        o_ds = dsa_attention(r3(idx_q), r3(idxw), klo, khi, r3(ds_q), r3(ds_kb), r3(ds_vb), dsa_tables,
                             0, seq, PROMPT_TQ, PROMPT_TK // LANES)
    else:
        (p_ckv, p_kr, p_sbk, p_sbv, p_bdk, p_bdv, p_dsk, p_dsv, p_kidx) = past
        past_len = p_sbk.shape[1]
        total = past_len + seq
        pad_to = -(-total // SAMPLE_TK) * SAMPLE_TK
        ckv_all = jnp.concatenate([p_ckv, r3(ckv)], axis=1)
        krp_all = jnp.concatenate([jnp.pad(p_kr, ((0, 0), (0, 0), (0, LANES - ROPE_DIM))), r3(krp)], axis=1)
        rows = pad_to - total
        ckv_all = jnp.pad(ckv_all, ((0, 0), (0, rows), (0, 0))).reshape(bsz * pad_to, -1)
        krp_all = jnp.pad(krp_all, ((0, 0), (0, rows), (0, 0))).reshape(bsz * pad_to, -1)
        kcat, vmla = mla_kv_up(ckv_all, krp_all, lw["wk"], lw["wv"], tm=pad_to)
        o_mla = mla_attention(r3(qcat), kcat.reshape(bsz, pad_to, -1), vmla.reshape(bsz, pad_to, -1),
                              past_len, seq, SAMPLE_TK)
        o_sb = sb_attention(r3(sb_q), _with_past(p_sbk, r3(sb_kb), pad_to), _with_past(p_sbv, r3(sb_vb), pad_to),
                            past_len, seq, SAMPLE_TK)
        band_len = _band_window(seq)
        o_bd = band_attention(r3(bd_q), _with_past(p_bdk, r3(bd_kb), band_len), _with_past(p_bdv, r3(bd_vb), band_len),
                              band_mask, seq, past_len - p_bdk.shape[1])
        kidx_all = jnp.pad(jnp.concatenate([p_kidx, r3(kidx)], axis=1), ((0, 0), (0, rows), (0, 0)))
        klo, khi = _kidx_pair(kidx_all)
        o_ds = dsa_attention(r3(idx_q), r3(idxw), klo, khi, r3(ds_q),
                             _with_past(p_dsk, r3(ds_kb), pad_to), _with_past(p_dsv, r3(ds_vb), pad_to),
                             dsa_tables, past_len, total, seq, SAMPLE_TK // LANES)

    f2 = lambda a: a.reshape(t, -1)
    merged = gate_merge(u, [f2(o_mla), f2(o_sb), f2(o_bd), f2(o_ds)], lw["wg"], lw["wb"])
    h_new, = matmul(merged, w_out, (F32,), residual=h, name="mix_out")
    return h_new, (ckv, kr, sb_k, sb_v, bd_k, bd_v, ds_k, ds_v, kidx)


def dense_ffn(h, un, w1, w3, w2):
    act = swiglu_up(un, w1, w3)
    out, = matmul(act, w2, (F32,), residual=h, name="ffn_down")
    return out


def moe_ffn(h, un, router_pad, w1, w3, w2):
    n_experts = w1.shape[0]
    gate = router_gate(un, router_pad, n_experts)
    for e in range(n_experts):
        act = swiglu_up(un, w1[e], w3[e])
        h, = matmul(act, w2[e], (F32,), residual=h, gate=gate, gate_col=e, name="moe_down")
    return h


def layer_step(h, p, bsz, seq, cs, past, lw, band_mask, dsa_tables, fw):
    u = rmsnorm(h, fw["g_mix"], BF)
    h, rows = mixing_block(u, bsz, seq, lw, cs, past, band_mask, dsa_tables, fw["w_out"], h)
    un = rmsnorm(h, fw["g_ffn"], BF)
    if fw["moe"]:
        h = moe_ffn(h, un, fw["router"], fw["w1"], fw["w3"], fw["w2"])
    else:
        h = dense_ffn(h, un, fw["w1"], fw["w3"], fw["w2"])
    un = rmsnorm(h, fw["g_ple"], BF)
    h = ple_update(h, p, un, fw["ple_w"], fw["ple_gate_w"])
    return h, rows


def kernel(x_prompt, x_sample, p_prompt, p_sample, cache_mla_ckv, cache_mla_krope, cache_sb_k, cache_sb_v, cache_band_k, cache_band_v, cache_dsa_k, cache_dsa_v, cache_dsa_kidx, norm_mix, w_in, mla_q_norm, mla_w_uq, mla_kv_norm, mla_w_ukv, band_rel_bias, t5_rel_bias, w_branch, w_out, norm_ffn, ffn_w1, ffn_w3, ffn_w2, moe_router, moe_w1, moe_w3, moe_w2, norm_ple, ple_w, ple_gate_w, norm_final):
    depth = w_in.shape[0]
    bp, sp, d = x_prompt.shape
    bs, ss, _ = x_sample.shape
    past_len = cache_sb_k.shape[2]
    tm = 512
    cs_p = rope_table(jnp.arange(sp))
    cs_s = rope_table(past_len + (jnp.arange(tm) % ss))
    dsa_tables_p = dsa_bias_tables(t5_rel_bias, PROMPT_TQ)
    dsa_tables_s = dsa_bias_tables(t5_rel_bias, ss)
    hp = x_prompt.reshape(bp * sp, d)
    hs = x_sample.reshape(bs * ss, d)
    rows_p, rows_s = [], []
    for i in range(depth):
        lw = prepare_layer_weights(w_in[i], mla_q_norm[i], mla_w_uq[i], mla_kv_norm[i], mla_w_ukv[i], w_branch[i])
        j = i // 2
        fw = dict(g_mix=norm_mix[i], g_ffn=norm_ffn[i], g_ple=norm_ple[i], w_out=w_out[i].astype(BF),
                  ple_w=ple_w[i].astype(BF), ple_gate_w=ple_gate_w[i].astype(BF), moe=i % 2 == 1)
        if i % 2 == 0:
            fw.update(w1=ffn_w1[j].astype(BF), w3=ffn_w3[j].astype(BF), w2=ffn_w2[j].astype(BF))
        else:
            fw.update(router=_pad_cols(moe_router[j], LANES).astype(BF),
                      w1=moe_w1[j].astype(BF), w3=moe_w3[j].astype(BF), w2=moe_w2[j].astype(BF))
        past_i = (cache_mla_ckv[i], cache_mla_krope[i], cache_sb_k[i], cache_sb_v[i], cache_band_k[i],
                  cache_band_v[i], cache_dsa_k[i], cache_dsa_v[i], cache_dsa_kidx[i])
        hp, rp = layer_step(hp, p_prompt[i].reshape(bp * sp, -1).astype(BF), bp, sp, cs_p, None,
                            lw, band_bias_mask(band_rel_bias[i], PROMPT_TQ), dsa_tables_p, fw)
        hs, rs = layer_step(hs, p_sample[i].reshape(bs * ss, -1).astype(BF), bs, ss, cs_s, past_i,
                            lw, band_bias_mask(band_rel_bias[i], ss), dsa_tables_s, fw)
        rows_p.append(rp)
        rows_s.append(rs)
    ones = norm_final
    y_prompt = rmsnorm(hp, ones, F32).reshape(bp, sp, d)
    y_sample = rmsnorm(hs, ones, F32).reshape(bs, ss, d)

    keep = min(N_PREV_CHUNKS * CHUNK, sp)

    def stacked(rows, n, bsz, seq, heads=None, tail=None):
        out = []
        for r in rows:
            a = r[n].reshape(bsz, seq, -1)
            if tail is not None:
                a = a[:, seq - tail:]
            if heads is not None:
                a = a.reshape(a.shape[0], a.shape[1], heads, HEAD_DIM)
            out.append(a)
        return jnp.stack(out, axis=0)

    res = [y_prompt, y_sample]
    for n, heads in ((0, None), (1, None), (2, H_SB), (3, H_SB), (4, H_BAND), (5, H_BAND),
                     (6, H_DSA), (7, H_DSA), (8, None)):
        tail = keep if n in (4, 5) else None
        res.append(stacked(rows_p, n, bp, sp, heads, tail))
        res.append(stacked(rows_s, n, bs, ss, heads))
    return tuple(res)
```

```python
import functools
import math

import jax
import jax.numpy as jnp
from jax import lax
from jax.experimental import pallas as pl
from jax.experimental.pallas import tpu as pltpu

BF = jnp.bfloat16
F32 = jnp.float32

CHUNK = 64
CHUNK_SHIFT = 6
HEAD_DIM = 128
N_BRANCH = 4
H_MLA = 4
NOPE_DIM = 128
ROPE_DIM = 64
V_DIM = 128
ROPE_THETA = 10000.0
H_SB = 4
H_BAND = 4
N_PREV_CHUNKS = 8
REL_CLIP = 128
H_DSA = 4
H_IDX = 16
D_IDX = 64
TOPK_MAX = 256
T5_BUCKETS = 32
T5_MAX_DIST = 128
TOP_K_EXPERTS = 2
EPS = 1e-6

LANES = 128
PROMPT_TQ = 256
PROMPT_TK = 512
SAMPLE_TK = 384
VMEM_LIMIT = 56 * 1024 * 1024
NEG = -1e30
INT_MIN = -2147483648


def _cparams(n_axes):
    return pltpu.CompilerParams(dimension_semantics=("arbitrary",) * n_axes,
                                vmem_limit_bytes=VMEM_LIMIT)


def _tile(n, preferred):
    if n <= preferred:
        return n
    t = preferred - preferred % LANES
    while n % t:
        t -= LANES
    assert t > 0
    return t


def _dot(a, b):
    return jnp.dot(a, b, preferred_element_type=F32)


def _dot_nt(a, b):
    return lax.dot_general(a, b, (((1,), (1,)), ((), ())), preferred_element_type=F32)


def _sigmoid(x):
    return 1.0 / (1.0 + jnp.exp(-x))


def _rms(x, g):
    return x * lax.rsqrt(jnp.mean(x * x, axis=-1, keepdims=True) + EPS) * g


def _rmsnorm_kernel(x_ref, g_ref, o_ref):
    o_ref[...] = _rms(x_ref[...], g_ref[...]).astype(o_ref.dtype)


def rmsnorm(x, g, out_dtype, tm=512):
    t, d = x.shape
    tm = _tile(t, tm)
    return pl.pallas_call(
        _rmsnorm_kernel,
        grid=(t // tm,),
        in_specs=[pl.BlockSpec((tm, d), lambda i: (i, 0)),
                  pl.BlockSpec((1, d), lambda i: (0, 0))],
        out_specs=pl.BlockSpec((tm, d), lambda i: (i, 0)),
        out_shape=jax.ShapeDtypeStruct((t, d), out_dtype),
        compiler_params=_cparams(1),
        name="rmsnorm",
    )(x, g.reshape(1, d))


def _mm_kernel(*refs, has_res, gate_col, n_out):
    a_ref, w_ref = refs[0], refs[1]
    pos = 2
    res_ref = gate_ref = None
    if has_res:
        res_ref = refs[pos]
        pos += 1
    if gate_col is not None:
        gate_ref = refs[pos]
        pos += 1
    r = _dot(a_ref[...], w_ref[...])
    if gate_ref is not None:
        r = r * gate_ref[:, gate_col:gate_col + 1]
    if res_ref is not None:
        r = res_ref[...] + r
    for o_ref in refs[pos:pos + n_out]:
        o_ref[...] = r.astype(o_ref.dtype)


WEIGHT_BLOCK_BYTES = 6 * 1024 * 1024


def matmul(a, w, out_dtypes, residual=None, gate=None, gate_col=None, tm=512, name="mm"):
    m, k = a.shape
    n = w.shape[1]
    tm, tn = _tile(m, tm), _tile(n, WEIGHT_BLOCK_BYTES // (2 * k))
    in_specs = [pl.BlockSpec((tm, k), lambda j, i: (i, 0)),
                pl.BlockSpec((k, tn), lambda j, i: (0, j))]
    args = [a, w]
    if residual is not None:
        in_specs.append(pl.BlockSpec((tm, tn), lambda j, i: (i, j)))
        args.append(residual)
    if gate is not None:
        in_specs.append(pl.BlockSpec((tm, gate.shape[1]), lambda j, i: (i, 0)))
        args.append(gate)
    outs = pl.pallas_call(
        functools.partial(_mm_kernel, has_res=residual is not None,
                          gate_col=gate_col if gate is not None else None, n_out=len(out_dtypes)),
        grid=(n // tn, m // tm),
        in_specs=in_specs,
        out_specs=[pl.BlockSpec((tm, tn), lambda j, i: (i, j)) for _ in out_dtypes],
        out_shape=[jax.ShapeDtypeStruct((m, n), dt) for dt in out_dtypes],
        compiler_params=_cparams(2),
        name=name,
    )(*args)
    return outs


def _swiglu_up_kernel(a_ref, w1_ref, w3_ref, o_ref):
    a = a_ref[...]
    x1 = _dot(a, w1_ref[...])
    x3 = _dot(a, w3_ref[...])
    o_ref[...] = (x1 * _sigmoid(x1) * x3).astype(o_ref.dtype)


def swiglu_up(a, w1, w3, tm=512):
    m, k = a.shape
    n = w1.shape[1]
    tm, tn = _tile(m, tm), _tile(n, WEIGHT_BLOCK_BYTES // (2 * k))
    return pl.pallas_call(
        _swiglu_up_kernel,
        grid=(n // tn, m // tm),
        in_specs=[pl.BlockSpec((tm, k), lambda j, i: (i, 0)),
                  pl.BlockSpec((k, tn), lambda j, i: (0, j)),
                  pl.BlockSpec((k, tn), lambda j, i: (0, j))],
        out_specs=pl.BlockSpec((tm, tn), lambda j, i: (i, j)),
        out_shape=jax.ShapeDtypeStruct((m, n), BF),
        compiler_params=_cparams(2),
        name="swiglu_up",
    )(a, w1, w3)


def _gate_merge_kernel(u_ref, o0_ref, o1_ref, o2_ref, o3_ref, wg_ref, wb_ref, out_ref):
    u = u_ref[...]
    acc = None
    for b, o_ref in enumerate((o0_ref, o1_ref, o2_ref, o3_ref)):
        t = _sigmoid(_dot(u, wg_ref[b])) * _dot(o_ref[...], wb_ref[b])
        acc = t if acc is None else acc + t
    out_ref[...] = acc.astype(out_ref.dtype)


def gate_merge(u, branches, wg, wb, tm=512, tn=512):
    t, d = u.shape
    bw = branches[0].shape[1]
    n = wg.shape[2]
    tm = min(tm, t)
    return pl.pallas_call(
        _gate_merge_kernel,
        grid=(n // tn, t // tm),
        in_specs=[pl.BlockSpec((tm, d), lambda j, i: (i, 0))]
        + [pl.BlockSpec((tm, bw), lambda j, i: (i, 0)) for _ in range(N_BRANCH)]
        + [pl.BlockSpec((N_BRANCH, d, tn), lambda j, i: (0, 0, j)),
           pl.BlockSpec((N_BRANCH, bw, tn), lambda j, i: (0, 0, j))],
        out_specs=pl.BlockSpec((tm, tn), lambda j, i: (i, j)),
        out_shape=jax.ShapeDtypeStruct((t, n), BF),
        compiler_params=_cparams(2),
        name="gate_merge",
    )(u, *branches, wg, wb)


def _ple_kernel(h_ref, p_ref, un_ref, wp_ref, wg_ref, o_ref):
    o_ref[...] = h_ref[...] + _dot(p_ref[...], wp_ref[...]) * _sigmoid(_dot(un_ref[...], wg_ref[...]))


def ple_update(h, p, un, wp, wg, tm=512, tn=1024):
    t, d = h.shape
    tm = min(tm, t)
    return pl.pallas_call(
        _ple_kernel,
        grid=(d // tn, t // tm),
        in_specs=[pl.BlockSpec((tm, tn), lambda j, i: (i, j)),
                  pl.BlockSpec((tm, p.shape[1]), lambda j, i: (i, 0)),
                  pl.BlockSpec((tm, d), lambda j, i: (i, 0)),
                  pl.BlockSpec((p.shape[1], tn), lambda j, i: (0, j)),
                  pl.BlockSpec((d, tn), lambda j, i: (0, j))],
        out_specs=pl.BlockSpec((tm, tn), lambda j, i: (i, j)),
        out_shape=jax.ShapeDtypeStruct((t, d), F32),
        compiler_params=_cparams(2),
        name="ple_update",
    )(h, p, un, wp, wg)


def _router_kernel(un_ref, wr_ref, g_ref, *, n_experts):
    logits = _dot(un_ref[...], wr_ref[...])
    lane = lax.broadcasted_iota(jnp.int32, logits.shape, 1).astype(F32)
    real = lane < n_experts
    logits = jnp.where(real, logits, NEG)
    e = jnp.where(real, jnp.exp(logits - jnp.max(logits, axis=1, keepdims=True)), 0.0)
    probs = e / jnp.sum(e, axis=1, keepdims=True)
    p1 = jnp.max(probs, axis=1, keepdims=True)
    i1 = jnp.min(jnp.where(probs == p1, lane, float(LANES)), axis=1, keepdims=True)
    first = lane == i1
    rest = jnp.where(first | ~real, -1.0, probs)
    p2 = jnp.max(rest, axis=1, keepdims=True)
    i2 = jnp.min(jnp.where(rest == p2, lane, float(LANES)), axis=1, keepdims=True)
    second = lane == i2
    denom = p1 + p2
    g_ref[...] = jnp.where(first, p1 / denom, 0.0) + jnp.where(second, p2 / denom, 0.0)


def router_gate(un, wr_pad, n_experts, tm=512):
    t, d = un.shape
    tm = min(tm, t)
    return pl.pallas_call(
        functools.partial(_router_kernel, n_experts=n_experts),
        grid=(t // tm,),
        in_specs=[pl.BlockSpec((tm, d), lambda i: (i, 0)),
                  pl.BlockSpec((d, LANES), lambda i: (0, 0))],
        out_specs=pl.BlockSpec((tm, LANES), lambda i: (i, 0)),
        out_shape=jax.ShapeDtypeStruct((t, LANES), F32),
        compiler_params=_cparams(1),
        name="router_gate",
    )(un, wr_pad)


Q_LORA_OFF = 0


def _proj_misc_kernel(u_ref, wm_ref, qn_ref, kvn_ref, wqa_ref, wqb_ref, cs_ref,
                      ckv_ref, krp_ref, kidx_ref, idxw_ref, qcat_ref, *, q_lora, kv_lora):
    x = _dot(u_ref[...], wm_ref[...])
    cos = cs_ref[:, 0:LANES]
    sin = cs_ref[:, LANES:2 * LANES]
    o = q_lora
    ckv_ref[...] = _rms(x[:, o:o + kv_lora], kvn_ref[...])
    o += kv_lora
    krp_ref[...] = x[:, o:o + LANES] * cos + x[:, o + LANES:o + 2 * LANES] * sin
    o += 2 * LANES
    kidx_ref[...] = x[:, o:o + LANES]
    idxw_ref[...] = x[:, o + LANES:o + 2 * LANES]
    cqn = _rms(x[:, 0:q_lora], qn_ref[...]).astype(BF)
    qa = _dot(cqn, wqa_ref[...])
    qb = _dot(cqn, wqb_ref[...])
    for h in range(H_MLA):
        lo = 2 * h * LANES
        qcat_ref[:, lo:lo + LANES] = qa[:, lo:lo + LANES].astype(BF)
        qcat_ref[:, lo + LANES:lo + 2 * LANES] = (
            qa[:, lo + LANES:lo + 2 * LANES] * cos + qb[:, h * LANES:(h + 1) * LANES] * sin).astype(BF)


def proj_misc(u, wm, qn, kvn, wqa, wqb, cs, tm=512):
    t, d = u.shape
    tm = min(tm, t)
    q_lora, kv_lora = qn.shape[1], kvn.shape[1]
    n_pos_blocks = cs.shape[0] // tm
    row = lambda n: pl.BlockSpec((tm, n), lambda i: (i, 0))
    full = lambda a: pl.BlockSpec(a.shape, lambda i: (0, 0))
    return pl.pallas_call(
        functools.partial(_proj_misc_kernel, q_lora=q_lora, kv_lora=kv_lora),
        grid=(t // tm,),
        in_specs=[row(d), full(wm), full(qn), full(kvn), full(wqa), full(wqb),
                  pl.BlockSpec((tm, 2 * LANES), lambda i: (i % n_pos_blocks, 0))],
        out_specs=[row(kv_lora), row(LANES), row(LANES), row(LANES), row(2 * LANES * H_MLA)],
        out_shape=[jax.ShapeDtypeStruct((t, kv_lora), F32),
                   jax.ShapeDtypeStruct((t, LANES), F32),
                   jax.ShapeDtypeStruct((t, LANES), F32),
                   jax.ShapeDtypeStruct((t, LANES), F32),
                   jax.ShapeDtypeStruct((t, 2 * LANES * H_MLA), BF)],
        compiler_params=_cparams(1),
        name="proj_misc",
    )(u, wm, qn, kvn, wqa, wqb, cs)


def _mla_kv_up_kernel(ckv_ref, krp_ref, wk_ref, wv_ref, kcat_ref, v_ref):
    c = ckv_ref[...].astype(BF)
    kn = _dot(c, wk_ref[...])
    krp = krp_ref[...].astype(BF)
    for h in range(H_MLA):
        kcat_ref[:, 2 * h * LANES:(2 * h + 1) * LANES] = kn[:, h * LANES:(h + 1) * LANES].astype(BF)
        kcat_ref[:, (2 * h + 1) * LANES:(2 * h + 2) * LANES] = krp
    v_ref[...] = _dot(c, wv_ref[...]).astype(BF)


def mla_kv_up(ckv, krp, wk, wv, tm=512):
    t, c = ckv.shape
    tm = min(tm, t)
    row = lambda n: pl.BlockSpec((tm, n), lambda i: (i, 0))
    full = lambda a: pl.BlockSpec(a.shape, lambda i: (0, 0))
    return pl.pallas_call(
        _mla_kv_up_kernel,
        grid=(t // tm,),
        in_specs=[row(c), row(LANES), full(wk), full(wv)],
        out_specs=[row(2 * LANES * H_MLA), row(V_DIM * H_MLA)],
        out_shape=[jax.ShapeDtypeStruct((t, 2 * LANES * H_MLA), BF),
                   jax.ShapeDtypeStruct((t, V_DIM * H_MLA), BF)],
        compiler_params=_cparams(1),
        name="mla_kv_up",
    )(ckv, krp, wk, wv)


def _mla_attn_kernel(qt_ref, k_ref, vt_ref, o_ref, *, tq, tk, q_off, scale, heads):
    q0 = q_off + pl.program_id(2) * tq
    qpos = q0 + lax.broadcasted_iota(jnp.int32, (1, tq), 1)
    limit = (lax.shift_right_arithmetic(qpos, CHUNK_SHIFT) + 1) * CHUNK
    last_limit = ((q0 + tq - 1) // CHUNK + 1) * CHUNK
    n_chunks = (last_limit + tk - 1) // tk

    def body(kc, carry):
        ks = pl.multiple_of(kc * tk, tk)
        kpos = ks + lax.broadcasted_iota(jnp.int32, (tk, tq), 0)
        visible = kpos < limit
        scores = [_dot(k_ref[pl.ds(ks, tk), 2 * h * LANES:2 * (h + 1) * LANES], qt_ref[h]) for h in range(heads)]
        probs, stats = [], []
        for h in range(heads):
            m_prev, l_prev, _ = carry[h]
            s = jnp.where(visible, scores[h] * scale, NEG)
            m_new = jnp.maximum(m_prev, jnp.max(s, axis=0, keepdims=True))
            p = jnp.exp(s - m_new)
            alpha = jnp.exp(m_prev - m_new)
            stats.append((m_new, alpha * l_prev + jnp.sum(p, axis=0, keepdims=True), alpha))
            probs.append(p.astype(BF))
        out = []
        for h in range(heads):
            m_new, l_new, alpha = stats[h]
            out.append((m_new, l_new, alpha * carry[h][2] + _dot(vt_ref[h, kc], probs[h])))
        return tuple(out)

    init = tuple((jnp.full((1, tq), NEG, F32), jnp.zeros((1, tq), F32), jnp.zeros((V_DIM, tq), F32))
                 for _ in range(heads))
    final = lax.fori_loop(0, n_chunks, body, init)
    for h in range(heads):
        _, l_fin, acc = final[h]
        o_ref[:, h * V_DIM:(h + 1) * V_DIM] = (acc / l_fin).T.astype(o_ref.dtype)


def _chunked_transpose(v, heads, tk):
    b, s, hd = v.shape
    d = hd // heads
    return jnp.transpose(v.reshape(b, s // tk, tk, heads, d), (0, 3, 1, 4, 2))


def mla_attention(qcat, kcat, v, q_off, tq, tk, heads=4):
    b, sq, _ = qcat.shape
    sk = kcat.shape[1]
    scale = (NOPE_DIM + ROPE_DIM) ** -0.5
    qt = jnp.transpose(qcat.reshape(b, sq, H_MLA, 2 * LANES), (0, 2, 3, 1))
    vt = _chunked_transpose(v, H_MLA, tk)
    return pl.pallas_call(
        functools.partial(_mla_attn_kernel, tq=tq, tk=tk, q_off=q_off, scale=scale, heads=heads),
        grid=(b, H_MLA // heads, sq // tq),
        in_specs=[pl.BlockSpec((None, heads, 2 * LANES, tq), lambda bi, h, qi: (bi, h, 0, qi)),
                  pl.BlockSpec((None, sk, heads * 2 * LANES), lambda bi, h, qi: (bi, 0, h)),
                  pl.BlockSpec((None, heads, sk // tk, V_DIM, tk), lambda bi, h, qi: (bi, h, 0, 0, 0))],
        out_specs=pl.BlockSpec((None, tq, heads * V_DIM), lambda bi, h, qi: (bi, qi, h)),
        out_shape=jax.ShapeDtypeStruct((b, sq, H_MLA * V_DIM), BF),
        compiler_params=_cparams(3),
        name="mla_attention",
    )(qt, kcat, vt)


def _split2(x):
    hi = x.astype(BF)
    return hi, (x - hi.astype(F32)).astype(BF)


def _sb_attn_kernel(q_ref, k_ref, v_ref, tri_ref, o_ref, *, tq, tk, q_off, scale, heads):
    q0 = q_off + pl.program_id(2) * tq
    qpos = q0 + lax.broadcasted_iota(jnp.int32, (tq, 1), 0)
    n_chunks = (q0 + tq - 1 + tk - 1) // tk
    tri2 = tri_ref[...]
    n_blk = tk // LANES
    hs = [slice(h * HEAD_DIM, (h + 1) * HEAD_DIM) for h in range(heads)]

    def step(kc, carry, diagonal):
        ks = pl.multiple_of(kc * tk, tk)
        if diagonal:
            strict = ks + lax.broadcasted_iota(jnp.int32, (1, tk), 1) < qpos
            causal = lambda x: jnp.where(strict, x, 0.0)
        else:
            causal = lambda x: x
        z = [_dot_nt(q_ref[:, hs[h]], k_ref[pl.ds(ks, tk), hs[h]]) for h in range(heads)]
        log_beta, parts = [], []
        for h in range(heads):
            zh = z[h] * scale
            sp = jnp.maximum(zh, 0.0) + jnp.log(1.0 + jnp.exp(-jnp.abs(zh)))
            log_beta.append(zh - sp)
            parts.append(_split2(causal(-sp)))
        after, later = [], []
        for h in range(heads):
            hi, lo = parts[h]
            run = carry[h][1]
            blocks = [None] * n_blk
            for blk in reversed(range(n_blk)):
                sl = slice(blk * LANES, (blk + 1) * LANES)
                sums = _dot(jnp.concatenate([hi[:, sl], lo[:, sl]], axis=1), tri2)
                blocks[blk] = sums[:, 0:LANES] + run
                run = run + sums[:, LANES:2 * LANES]
            after.append(jnp.concatenate(blocks, axis=1))
            later.append(run)
        weights = [causal(jnp.exp(log_beta[h] + after[h])).astype(BF) for h in range(heads)]
        return tuple((carry[h][0] + _dot(weights[h], v_ref[pl.ds(ks, tk), hs[h]]), later[h]) for h in range(heads))

    n_below = q0 // tk
    init = tuple((jnp.zeros((tq, HEAD_DIM), F32), jnp.zeros((tq, LANES), F32)) for _ in range(heads))
    state = lax.fori_loop(0, n_chunks - n_below, lambda it, c: step(n_chunks - 1 - it, c, True), init)
    final = lax.fori_loop(0, n_below, lambda it, c: step(n_below - 1 - it, c, False), state)
    for h in range(heads):
        o_ref[:, hs[h]] = final[h][0].astype(o_ref.dtype)


def sb_attention(q, k, v, q_off, tq, tk, heads=4):
    b, sq, _ = q.shape
    sk = k.shape[1]
    assert sk % tk == 0 and tk % LANES == 0 and H_SB % heads == 0
    j = lax.broadcasted_iota(jnp.int32, (LANES, LANES), 0)
    s = lax.broadcasted_iota(jnp.int32, (LANES, LANES), 1)
    tri = jnp.concatenate([(j > s).astype(BF), jnp.ones((LANES, LANES), BF)], axis=1)
    tri = jnp.concatenate([tri, tri], axis=0)
    width = heads * HEAD_DIM
    return pl.pallas_call(
        functools.partial(_sb_attn_kernel, tq=tq, tk=tk, q_off=q_off, scale=HEAD_DIM ** -0.5, heads=heads),
        grid=(b, H_SB // heads, sq // tq),
        in_specs=[pl.BlockSpec((None, tq, width), lambda bi, h, qi: (bi, qi, h)),
                  pl.BlockSpec((None, sk, width), lambda bi, h, qi: (bi, 0, h)),
                  pl.BlockSpec((None, sk, width), lambda bi, h, qi: (bi, 0, h)),
                  pl.BlockSpec((2 * LANES, 2 * LANES), lambda bi, h, qi: (0, 0))],
        out_specs=pl.BlockSpec((None, tq, width), lambda bi, h, qi: (bi, qi, h)),
        out_shape=jax.ShapeDtypeStruct((b, sq, H_SB * HEAD_DIM), BF),
        compiler_params=_cparams(3),
        name="sb_attention",
    )(q, k, v, tri)


def _band_attn_kernel(q_ref, k_ref, v_ref, bm_ref, o_ref, *, tq, win, kpos_base, scale):
    w0 = pl.multiple_of(pl.program_id(1) * tq, tq)
    kpos = kpos_base + w0 + lax.broadcasted_iota(jnp.int32, (1, win), 1)
    exists = kpos >= 0
    hs = [slice(h * HEAD_DIM, (h + 1) * HEAD_DIM) for h in range(H_BAND)]
    scores = [_dot_nt(q_ref[:, hs[h]], k_ref[pl.ds(w0, win), hs[h]]) for h in range(H_BAND)]
    probs, denoms = [], []
    for h in range(H_BAND):
        s = jnp.where(exists, scores[h] * scale + bm_ref[h], NEG)
        p = jnp.exp(s - jnp.max(s, axis=1, keepdims=True))
        denoms.append(jnp.sum(p, axis=1, keepdims=True))
        probs.append(p.astype(BF))
    for h in range(H_BAND):
        o = _dot(probs[h], v_ref[pl.ds(w0, win), hs[h]])
        o_ref[:, hs[h]] = (o / denoms[h]).astype(o_ref.dtype)


def _band_window(tq):
    return -(-(tq + N_PREV_CHUNKS * CHUNK) // LANES) * LANES


def band_bias_mask(rel_bias, tq):
    win = _band_window(tq)
    i = jnp.arange(tq)[:, None]
    j = jnp.arange(win)[None, :]
    rel = jnp.clip(i + N_PREV_CHUNKS * CHUNK - j, -REL_CLIP, REL_CLIP) + REL_CLIP
    ci, cj = i // CHUNK, j // CHUNK
    inside = (cj >= ci) & (cj <= ci + N_PREV_CHUNKS)
    return jnp.where(inside[None], _table_lookup(rel_bias, rel), NEG)


def band_attention(q, k_pad, v_pad, bias_mask, tq, kpos_base):
    b, sq, _ = q.shape
    skp = k_pad.shape[1]
    win = _band_window(tq)
    width = H_BAND * HEAD_DIM
    assert skp >= sq - tq + win
    return pl.pallas_call(
        functools.partial(_band_attn_kernel, tq=tq, win=win, kpos_base=kpos_base, scale=HEAD_DIM ** -0.5),
        grid=(b, sq // tq),
        in_specs=[pl.BlockSpec((None, tq, width), lambda bi, qi: (bi, qi, 0)),
                  pl.BlockSpec((None, skp, width), lambda bi, qi: (bi, 0, 0)),
                  pl.BlockSpec((None, skp, width), lambda bi, qi: (bi, 0, 0)),
                  pl.BlockSpec((H_BAND, tq, win), lambda bi, qi: (0, 0, 0))],
        out_specs=pl.BlockSpec((None, tq, width), lambda bi, qi: (bi, qi, 0)),
        out_shape=jax.ShapeDtypeStruct((b, sq, width), BF),
        compiler_params=_cparams(2),
        name="band_attention",
    )(q, k_pad, v_pad, bias_mask)


def _sortable(x):
    i = lax.bitcast_convert_type(x, jnp.int32)
    return i ^ (lax.shift_right_arithmetic(i, 31) & 0x7FFFFFFF)


def _dsa_kernel(qi2t_ref, wt_ref, klo_ref, khi_ref, qt_ref, k_ref, vt_ref, bnear_ref, bfar_ref, tri_ref,
                o_ref, key_ref, m_ref, l_ref, acc_ref, eqc_ref,
                *, tq, tk, big, q_off, topk, scale, w_scale):
    wide = big * tk
    near_after = -(-tq // tk)
    q0 = q_off + pl.program_id(1) * tq
    qpos = q0 + lax.broadcasted_iota(jnp.int32, (1, tq), 1)
    limit = (lax.shift_right_arithmetic(qpos, CHUNK_SHIFT) + 1) * CHUNK
    last_limit = ((q0 + tq - 1) // CHUNK + 1) * CHUNK
    n_wide = (last_limit + wide - 1) // wide
    diag = q0 // tk

    w = wt_ref[...] * w_scale

    def score_body(c, carry):
        ks = pl.multiple_of(c * wide, wide)
        klo = klo_ref[pl.ds(ks, wide), :]
        khi = khi_ref[pl.ds(ks, wide), :]
        acc = jnp.zeros((wide, tq), F32)
        for pair in range(H_IDX // 2):
            q2 = qi2t_ref[pair]
            acc = acc + w[2 * pair:2 * pair + 1] * jnp.maximum(_dot(klo, q2), 0.0)
            acc = acc + w[2 * pair + 1:2 * pair + 2] * jnp.maximum(_dot(khi, q2), 0.0)
        kpos = ks + lax.broadcasted_iota(jnp.int32, (wide, tq), 0)
        keys = _sortable(jnp.where(kpos < limit, acc + 0.0, -jnp.inf))
        for blk in range(big):
            key_ref[c * big + blk] = keys[blk * tk:(blk + 1) * tk]
        return carry

    lax.fori_loop(0, n_wide, score_body, 0)

    def count_ge(cand):
        def body(c, cnt):
            for blk in range(big):
                cnt = cnt + jnp.where(key_ref[c * big + blk] >= cand, 1.0, 0.0)
            return cnt
        cnt = lax.fori_loop(0, n_wide, body, jnp.zeros((tk, tq), F32))
        return jnp.sum(cnt, axis=0, keepdims=True)

    def bit_body(it, thr):
        cand = thr + lax.shift_left(jnp.int32(1), 31 - it)
        return jnp.where(count_ge(cand) >= topk, cand, thr)

    thr = lax.fori_loop(0, 32, bit_body, jnp.full((1, tq), INT_MIN, jnp.int32))
    n_above = count_ge(thr + 1)
    n_ties_kept = topk - n_above

    m_ref[...] = jnp.full(m_ref.shape, NEG, F32)
    l_ref[...] = jnp.zeros(l_ref.shape, F32)
    acc_ref[...] = jnp.zeros(acc_ref.shape, F32)
    eqc_ref[...] = jnp.zeros(eqc_ref.shape, F32)
    tri = tri_ref[...]
    hs = [slice(h * HEAD_DIM, (h + 1) * HEAD_DIM) for h in range(H_DSA)]

    def attend(first_blk, n_blk, bias_of_head, below_tile):
        width = n_blk * tk
        ks = pl.multiple_of(first_blk * tk, tk)
        key = jnp.concatenate([key_ref[first_blk + i] for i in range(n_blk)], axis=0)
        eq = key == thr
        eq_bf = jnp.where(eq, 1.0, 0.0).astype(BF)
        seen = eqc_ref[0:1, :]
        rank = [None] * n_blk
        for i in range(n_blk):
            counts = _dot(tri, eq_bf[i * tk:(i + 1) * tk])
            rank[i] = counts[0:tk] + seen
            seen = seen + counts[tk:tk + 1]
        eqc_ref[0:1, :] = seen
        sel = (key > thr) | (eq & (jnp.concatenate(rank, axis=0) <= n_ties_kept))
        if not below_tile:
            sel = sel & (ks + lax.broadcasted_iota(jnp.int32, (width, tq), 0) < limit)
        scores = [_dot(k_ref[pl.ds(ks, width), hs[h]], qt_ref[h]) for h in range(H_DSA)]
        probs, alphas = [], []
        for h in range(H_DSA):
            s = jnp.where(sel, scores[h] * scale + bias_of_head(h), NEG)
            m_prev = m_ref[h]
            m_new = jnp.maximum(m_prev, jnp.max(s, axis=0, keepdims=True))
            p = jnp.where(sel, jnp.exp(s - m_new), 0.0)
            alpha = jnp.exp(m_prev - m_new)
            l_ref[h] = alpha * l_ref[h] + jnp.sum(p, axis=0, keepdims=True)
            m_ref[h] = m_new
            probs.append(p.astype(BF))
            alphas.append(alpha)
        for h in range(H_DSA):
            vt = jnp.concatenate([vt_ref[h, first_blk + i] for i in range(n_blk)], axis=1)
            acc_ref[h] = alphas[h] * acc_ref[h] + _dot(vt, probs[h])

    far_bias = lambda h: bfar_ref[h][:, 0:1]
    n_far_wide = jnp.maximum(q0 - tk, 0) // wide

    def far_wide_body(c, carry):
        attend(c * big, big, far_bias, True)
        return carry

    lax.fori_loop(0, n_far_wide, far_wide_body, 0)

    def far_body(kc, carry):
        attend(kc, 1, far_bias, True)
        return carry

    lax.fori_loop(n_far_wide * big, jnp.maximum(diag - 1, n_far_wide * big), far_body, 0)

    @pl.when(diag >= 1)
    def _():
        attend(diag - 1, 1, lambda h: bnear_ref[0, h], True)

    for d in range(near_after):
        attend(diag + d, 1, lambda h, d=d: bnear_ref[d + 1, h], False)

    for h in range(H_DSA):
        o_ref[:, hs[h]] = (acc_ref[h] / l_ref[h]).T.astype(o_ref.dtype)


def t5_bucket(rel):
    half = T5_BUCKETS // 2
    max_exact = half // 2
    n = jnp.abs(rel)
    nf = jnp.maximum(n, 1).astype(F32)
    large = max_exact + (jnp.log(nf / max_exact) / math.log(T5_MAX_DIST / max_exact)
                         * (half - max_exact)).astype(jnp.int32)
    large = jnp.minimum(large, half - 1)
    return jnp.where(rel > 0, half, 0) + jnp.where(n < max_exact, n, large)


def _table_lookup(table, idx):
    onehot = jax.nn.one_hot(idx, table.shape[0], dtype=F32)
    out = jnp.einsum("...n,nh->...h", onehot, table.astype(F32), precision=lax.Precision.HIGHEST)
    return jnp.moveaxis(out, -1, 0)


def dsa_bias_tables(t5_table, tq, tk=LANES):
    j = jnp.arange(tk)[:, None]
    i = jnp.arange(tq)[None, :]
    near = jnp.stack([_table_lookup(t5_table, t5_bucket(d * tk + j - i)) for d in range(-1, -(-tq // tk))],
                     axis=0)
    far_rel = -jnp.ones((1, tk), jnp.int32) * (2 * tk)
    far = _table_lookup(t5_table, t5_bucket(far_rel))
    return near, far


def dsa_attention(idx_q, idxw, klo, khi, q, k, v, bias_tables, q_off, n_keys, tq, big, tk=LANES):
    b, sq, _ = q.shape
    sk = k.shape[1]
    assert tk >= T5_MAX_DIST and q_off % tk == 0 and (tq % tk == 0 or sq == tq) and sk % (big * tk) == 0
    topk = min(TOPK_MAX, n_keys // 4)
    near, far = bias_tables
    ss = lax.broadcasted_iota(jnp.int32, (tk + 16, tk), 0)
    jj = lax.broadcasted_iota(jnp.int32, (tk + 16, tk), 1)
    tri = ((jj <= ss) | (ss >= tk)).astype(BF)
    qi2t = jnp.transpose(idx_q.reshape(b, sq, H_IDX // 2, LANES), (0, 2, 3, 1))
    wt = jnp.transpose(idxw[:, :, :H_IDX], (0, 2, 1))
    qt = jnp.transpose(q.reshape(b, sq, H_DSA, HEAD_DIM), (0, 2, 3, 1))
    vt = _chunked_transpose(v, H_DSA, tk)
    whole = lambda n: pl.BlockSpec((None, sk, n), lambda bi, qi: (bi, 0, 0))
    heads_t = lambda h, d: pl.BlockSpec((None, h, d, tq), lambda bi, qi: (bi, 0, 0, qi))
    const = lambda a: pl.BlockSpec(a.shape, lambda bi, qi: (0,) * a.ndim)
    return pl.pallas_call(
        functools.partial(_dsa_kernel, tq=tq, tk=tk, big=big, q_off=q_off, topk=topk,
                          scale=HEAD_DIM ** -0.5, w_scale=H_IDX ** -0.5 * D_IDX ** -0.5),
        grid=(b, sq // tq),
        in_specs=[heads_t(H_IDX // 2, LANES), pl.BlockSpec((None, H_IDX, tq), lambda bi, qi: (bi, 0, qi)),
                  whole(LANES), whole(LANES), heads_t(H_DSA, HEAD_DIM), whole(H_DSA * HEAD_DIM),
                  pl.BlockSpec((None, H_DSA, sk // tk, HEAD_DIM, tk), lambda bi, qi: (bi, 0, 0, 0, 0)),
                  const(near), const(far), const(tri)],
        out_specs=pl.BlockSpec((None, tq, H_DSA * HEAD_DIM), lambda bi, qi: (bi, qi, 0)),
        out_shape=jax.ShapeDtypeStruct((b, sq, H_DSA * HEAD_DIM), BF),
        scratch_shapes=[pltpu.VMEM((sk // tk, tk, tq), jnp.int32),
                        pltpu.VMEM((H_DSA, 1, tq), F32), pltpu.VMEM((H_DSA, 1, tq), F32),
                        pltpu.VMEM((H_DSA, HEAD_DIM, tq), F32), pltpu.VMEM((8, tq), F32)],
        compiler_params=_cparams(2),
        name="dsa_attention",
    )(qi2t, wt, klo, khi, qt, k, vt, near, far, tri)


def _pad_cols(a, n):
    return jnp.pad(a, ((0, 0), (0, n - a.shape[1])))


def _rotate_half_cols(w):
    half = w.shape[1] // 2
    return jnp.concatenate([-w[:, half:], w[:, :half]], axis=1)


def prepare_layer_weights(w_in, q_norm, w_uq, kv_norm, w_ukv, w_branch):
    d = w_in.shape[0]
    q_lora, kv_lora = q_norm.shape[0], kv_norm.shape[0]
    sizes = (q_lora, kv_lora, ROPE_DIM, 3 * H_SB * HEAD_DIM, 3 * H_BAND * HEAD_DIM, 3 * H_DSA * HEAD_DIM,
             H_IDX * D_IDX, D_IDX, H_IDX, N_BRANCH * d)
    cols, start = [], 0
    for s in sizes:
        cols.append(w_in[:, start:start + s])
        start += s
    w_cq, w_ckv, w_kr, w_sb, w_bd, w_ds, w_iq, w_ik, w_iw, w_g = cols
    wm = jnp.concatenate([w_cq, w_ckv, _pad_cols(w_kr, LANES), _pad_cols(_rotate_half_cols(w_kr), LANES),
                          _pad_cols(w_ik, LANES), _pad_cols(w_iw, LANES)], axis=1).astype(BF)
    qa, qb = [], []
    hd = NOPE_DIM + ROPE_DIM
    for h in range(H_MLA):
        wh = w_uq[:, h * hd:(h + 1) * hd]
        qa += [wh[:, :NOPE_DIM], _pad_cols(wh[:, NOPE_DIM:], LANES)]
        qb.append(_pad_cols(_rotate_half_cols(wh[:, NOPE_DIM:]), LANES))
    wqa = jnp.concatenate(qa, axis=1).astype(BF)
    wqb = jnp.concatenate(qb, axis=1).astype(BF)
    kvd = NOPE_DIM + V_DIM
    wk = jnp.concatenate([w_ukv[:, h * kvd:h * kvd + NOPE_DIM] for h in range(H_MLA)], axis=1).astype(BF)
    wv = jnp.concatenate([w_ukv[:, h * kvd + NOPE_DIM:(h + 1) * kvd] for h in range(H_MLA)], axis=1).astype(BF)
    wg = jnp.transpose(w_g.reshape(d, N_BRANCH, d), (1, 0, 2)).astype(BF)
    return dict(wm=wm, wqa=wqa, wqb=wqb, wk=wk, wv=wv,
                w_sb=w_sb.astype(BF), w_bd=w_bd.astype(BF), w_ds=w_ds.astype(BF), w_iq=w_iq.astype(BF),
                wg=wg, wb=w_branch.astype(BF),
                qn=q_norm.reshape(1, -1), kvn=kv_norm.reshape(1, -1))


def rope_table(pos):
    half = ROPE_DIM // 2
    inv = ROPE_THETA ** (-jnp.arange(half, dtype=F32) / half)
    ang = pos.astype(F32)[:, None] * inv[None, :]
    cos, sin = jnp.cos(ang), jnp.sin(ang)
    z = jnp.zeros((pos.shape[0], LANES - ROPE_DIM), F32)
    return jnp.concatenate([cos, cos, z, sin, sin, z], axis=1)


def _kidx_pair(kidx):
    kb = kidx.astype(BF)
    return (jnp.pad(kb, ((0, 0), (0, 0), (0, LANES - D_IDX))),
            jnp.pad(kb, ((0, 0), (0, 0), (LANES - D_IDX, 0))))


def _with_past(past, new, pad_to):
    b = new.shape[0]
    a = jnp.concatenate([past.reshape(b, past.shape[1], -1).astype(BF), new.astype(BF)], axis=1)
    return jnp.pad(a, ((0, 0), (0, pad_to - a.shape[1]), (0, 0)))


def mixing_block(u, bsz, seq, lw, cs, past, band_mask, dsa_tables, w_out, h):
    t = bsz * seq
    ckv, krp, kidxp, idxw, qcat = proj_misc(u, lw["wm"], lw["qn"], lw["kvn"], lw["wqa"], lw["wqb"], cs)
    hw = H_SB * HEAD_DIM

    def qkv(w):
        q, = matmul(u, w[:, :hw], (BF,), tm=1024, name="proj_q")
        k32, kbf = matmul(u, w[:, hw:2 * hw], (F32, BF), tm=1024, name="proj_k")
        v32, vbf = matmul(u, w[:, 2 * hw:], (F32, BF), tm=1024, name="proj_v")
        return q, k32, kbf, v32, vbf

    sb_q, sb_k, sb_kb, sb_v, sb_vb = qkv(lw["w_sb"])
    bd_q, bd_k, bd_kb, bd_v, bd_vb = qkv(lw["w_bd"])
    ds_q, ds_k, ds_kb, ds_v, ds_vb = qkv(lw["w_ds"])
    idx_q, = matmul(u, lw["w_iq"], (BF,), tm=1024, name="proj_idxq")
    kr = krp[:, :ROPE_DIM]
    kidx = kidxp[:, :D_IDX]
    r3 = lambda a: a.reshape(bsz, seq, -1)
    band_pad = N_PREV_CHUNKS * CHUNK

    if past is None:
        kcat, vmla = mla_kv_up(ckv, krp, lw["wk"], lw["wv"])
        o_mla = mla_attention(r3(qcat), r3(kcat), r3(vmla), 0, PROMPT_TQ, PROMPT_TK)
        o_sb = sb_attention(r3(sb_q), r3(sb_kb), r3(sb_vb), 0, PROMPT_TQ, PROMPT_TK)
        front = ((0, 0), (band_pad, 0), (0, 0))
        o_bd = band_attention(r3(bd_q), jnp.pad(r3(bd_kb), front), jnp.pad(r3(bd_vb), front),
                              band_mask, PROMPT_TQ, -band_pad)
        klo, khi = _kidx_pair(r3(kidx))
        o_ds = dsa_attention(r3(idx_q), r3(idxw), klo, khi, r3(ds_q), r3(ds_kb), r3(ds_vb), dsa_tables,
                             0, seq, PROMPT_TQ, PROMPT_TK // LANES)
    else:
        (p_ckv, p_kr, p_sbk, p_sbv, p_bdk, p_bdv, p_dsk, p_dsv, p_kidx) = past
        past_len = p_sbk.shape[1]
        total = past_len + seq
        pad_to = -(-total // SAMPLE_TK) * SAMPLE_TK
        ckv_all = jnp.concatenate([p_ckv, r3(ckv)], axis=1)
        krp_all = jnp.concatenate([jnp.pad(p_kr, ((0, 0), (0, 0), (0, LANES - ROPE_DIM))), r3(krp)], axis=1)
        rows = pad_to - total
        ckv_all = jnp.pad(ckv_all, ((0, 0), (0, rows), (0, 0))).reshape(bsz * pad_to, -1)
        krp_all = jnp.pad(krp_all, ((0, 0), (0, rows), (0, 0))).reshape(bsz * pad_to, -1)
        kcat, vmla = mla_kv_up(ckv_all, krp_all, lw["wk"], lw["wv"], tm=pad_to)
        o_mla = mla_attention(r3(qcat), kcat.reshape(bsz, pad_to, -1), vmla.reshape(bsz, pad_to, -1),
                              past_len, seq, SAMPLE_TK)
        o_sb = sb_attention(r3(sb_q), _with_past(p_sbk, r3(sb_kb), pad_to), _with_past(p_sbv, r3(sb_vb), pad_to),
                            past_len, seq, SAMPLE_TK)
        band_len = _band_window(seq)
        o_bd = band_attention(r3(bd_q), _with_past(p_bdk, r3(bd_kb), band_len), _with_past(p_bdv, r3(bd_vb), band_len),
                              band_mask, seq, past_len - p_bdk.shape[1])
        kidx_all = jnp.pad(jnp.concatenate([p_kidx, r3(kidx)], axis=1), ((0, 0), (0, rows), (0, 0)))
        klo, khi = _kidx_pair(kidx_all)
        o_ds = dsa_attention(r3(idx_q), r3(idxw), klo, khi, r3(ds_q),
                             _with_past(p_dsk, r3(ds_kb), pad_to), _with_past(p_dsv, r3(ds_vb), pad_to),
                             dsa_tables, past_len, total, seq, SAMPLE_TK // LANES)

    f2 = lambda a: a.reshape(t, -1)
    merged = gate_merge(u, [f2(o_mla), f2(o_sb), f2(o_bd), f2(o_ds)], lw["wg"], lw["wb"])
    h_new, = matmul(merged, w_out, (F32,), residual=h, name="mix_out")
    return h_new, (ckv, kr, sb_k, sb_v, bd_k, bd_v, ds_k, ds_v, kidx)


def dense_ffn(h, un, w1, w3, w2):
    act = swiglu_up(un, w1, w3)
    out, = matmul(act, w2, (F32,), residual=h, name="ffn_down")
    return out


def moe_ffn(h, un, router_pad, w1, w3, w2):
    n_experts = w1.shape[0]
    gate = router_gate(un, router_pad, n_experts)
    for e in range(n_experts):
        act = swiglu_up(un, w1[e], w3[e])
        h, = matmul(act, w2[e], (F32,), residual=h, gate=gate, gate_col=e, name="moe_down")
    return h


def layer_step(h, p, bsz, seq, cs, past, lw, band_mask, dsa_tables, fw):
    u = rmsnorm(h, fw["g_mix"], BF)
    h, rows = mixing_block(u, bsz, seq, lw, cs, past, band_mask, dsa_tables, fw["w_out"], h)
    un = rmsnorm(h, fw["g_ffn"], BF)
    if fw["moe"]:
        h = moe_ffn(h, un, fw["router"], fw["w1"], fw["w3"], fw["w2"])
    else:
        h = dense_ffn(h, un, fw["w1"], fw["w3"], fw["w2"])
    un = rmsnorm(h, fw["g_ple"], BF)
    h = ple_update(h, p, un, fw["ple_w"], fw["ple_gate_w"])
    return h, rows


def kernel(x_prompt, x_sample, p_prompt, p_sample, cache_mla_ckv, cache_mla_krope, cache_sb_k, cache_sb_v, cache_band_k, cache_band_v, cache_dsa_k, cache_dsa_v, cache_dsa_kidx, norm_mix, w_in, mla_q_norm, mla_w_uq, mla_kv_norm, mla_w_ukv, band_rel_bias, t5_rel_bias, w_branch, w_out, norm_ffn, ffn_w1, ffn_w3, ffn_w2, moe_router, moe_w1, moe_w3, moe_w2, norm_ple, ple_w, ple_gate_w, norm_final):
    depth = w_in.shape[0]
    bp, sp, d = x_prompt.shape
    bs, ss, _ = x_sample.shape
    past_len = cache_sb_k.shape[2]
    tm = 512
    cs_p = rope_table(jnp.arange(sp))
    cs_s = rope_table(past_len + (jnp.arange(tm) % ss))
    dsa_tables_p = dsa_bias_tables(t5_rel_bias, PROMPT_TQ)
    dsa_tables_s = dsa_bias_tables(t5_rel_bias, ss)
    hp = x_prompt.reshape(bp * sp, d)
    hs = x_sample.reshape(bs * ss, d)
    rows_p, rows_s = [], []
    for i in range(depth):
        lw = prepare_layer_weights(w_in[i], mla_q_norm[i], mla_w_uq[i], mla_kv_norm[i], mla_w_ukv[i], w_branch[i])
        j = i // 2
        fw = dict(g_mix=norm_mix[i], g_ffn=norm_ffn[i], g_ple=norm_ple[i], w_out=w_out[i].astype(BF),
                  ple_w=ple_w[i].astype(BF), ple_gate_w=ple_gate_w[i].astype(BF), moe=i % 2 == 1)
        if i % 2 == 0:
            fw.update(w1=ffn_w1[j].astype(BF), w3=ffn_w3[j].astype(BF), w2=ffn_w2[j].astype(BF))
        else:
            fw.update(router=_pad_cols(moe_router[j], LANES).astype(BF),
                      w1=moe_w1[j].astype(BF), w3=moe_w3[j].astype(BF), w2=moe_w2[j].astype(BF))
        past_i = (cache_mla_ckv[i], cache_mla_krope[i], cache_sb_k[i], cache_sb_v[i], cache_band_k[i],
                  cache_band_v[i], cache_dsa_k[i], cache_dsa_v[i], cache_dsa_kidx[i])
        hp, rp = layer_step(hp, p_prompt[i].reshape(bp * sp, -1).astype(BF), bp, sp, cs_p, None,
                            lw, band_bias_mask(band_rel_bias[i], PROMPT_TQ), dsa_tables_p, fw)
        hs, rs = layer_step(hs, p_sample[i].reshape(bs * ss, -1).astype(BF), bs, ss, cs_s, past_i,
                            lw, band_bias_mask(band_rel_bias[i], ss), dsa_tables_s, fw)
        rows_p.append(rp)
        rows_s.append(rs)
    ones = norm_final
    y_prompt = rmsnorm(hp, ones, F32).reshape(bp, sp, d)
    y_sample = rmsnorm(hs, ones, F32).reshape(bs, ss, d)

    keep = min(N_PREV_CHUNKS * CHUNK, sp)

    def stacked(rows, n, bsz, seq, heads=None, tail=None):
        out = []
        for r in rows:
            a = r[n].reshape(bsz, seq, -1)
            if tail is not None:
                a = a[:, seq - tail:]
            if heads is not None:
                a = a.reshape(a.shape[0], a.shape[1], heads, HEAD_DIM)
            out.append(a)
        return jnp.stack(out, axis=0)

    res = [y_prompt, y_sample]
    for n, heads in ((0, None), (1, None), (2, H_SB), (3, H_SB), (4, H_BAND), (5, H_BAND),
                     (6, H_DSA), (7, H_DSA), (8, None)):
        tail = keep if n in (4, 5) else None
        res.append(stacked(rows_p, n, bp, sp, heads, tail))
        res.append(stacked(rows_s, n, bs, ss, heads))
    return tuple(res)
```

```python
import functools
import math

import jax
import jax.numpy as jnp
from jax import lax
from jax.experimental import pallas as pl
from jax.experimental.pallas import tpu as pltpu

BF = jnp.bfloat16
F32 = jnp.float32

CHUNK = 64
CHUNK_SHIFT = 6
HEAD_DIM = 128
N_BRANCH = 4
H_MLA = 4
NOPE_DIM = 128
ROPE_DIM = 64
V_DIM = 128
ROPE_THETA = 10000.0
H_SB = 4
H_BAND = 4
N_PREV_CHUNKS = 8
REL_CLIP = 128
H_DSA = 4
H_IDX = 16
D_IDX = 64
TOPK_MAX = 256
T5_BUCKETS = 32
T5_MAX_DIST = 128
TOP_K_EXPERTS = 2
EPS = 1e-6

LANES = 128
PROMPT_TQ = 256
PROMPT_TK = 512
SAMPLE_TK = 384
VMEM_LIMIT = 56 * 1024 * 1024
NEG = -1e30
INT_MIN = -2147483648


def _cparams(n_axes):
    return pltpu.CompilerParams(dimension_semantics=("arbitrary",) * n_axes,
                                vmem_limit_bytes=VMEM_LIMIT)


def _tile(n, preferred):
    if n <= preferred:
        return n
    t = preferred - preferred % LANES
    while n % t:
        t -= LANES
    assert t > 0
    return t


def _dot(a, b):
    return jnp.dot(a, b, preferred_element_type=F32)


def _dot_nt(a, b):
    return lax.dot_general(a, b, (((1,), (1,)), ((), ())), preferred_element_type=F32)


def _sigmoid(x):
    return 1.0 / (1.0 + jnp.exp(-x))


def _rms(x, g):
    return x * lax.rsqrt(jnp.mean(x * x, axis=-1, keepdims=True) + EPS) * g


def _rmsnorm_kernel(x_ref, g_ref, o_ref):
    o_ref[...] = _rms(x_ref[...], g_ref[...]).astype(o_ref.dtype)


def rmsnorm(x, g, out_dtype, tm=512):
    t, d = x.shape
    tm = _tile(t, tm)
    return pl.pallas_call(
        _rmsnorm_kernel,
        grid=(t // tm,),
        in_specs=[pl.BlockSpec((tm, d), lambda i: (i, 0)),
                  pl.BlockSpec((1, d), lambda i: (0, 0))],
        out_specs=pl.BlockSpec((tm, d), lambda i: (i, 0)),
        out_shape=jax.ShapeDtypeStruct((t, d), out_dtype),
        compiler_params=_cparams(1),
        name="rmsnorm",
    )(x, g.reshape(1, d))


def _mm_kernel(*refs, has_res, gate_col, n_out):
    a_ref, w_ref = refs[0], refs[1]
    pos = 2
    res_ref = gate_ref = None
    if has_res:
        res_ref = refs[pos]
        pos += 1
    if gate_col is not None:
        gate_ref = refs[pos]
        pos += 1
    r = _dot(a_ref[...], w_ref[...])
    if gate_ref is not None:
        r = r * gate_ref[:, gate_col:gate_col + 1]
    if res_ref is not None:
        r = res_ref[...] + r
    for o_ref in refs[pos:pos + n_out]:
        o_ref[...] = r.astype(o_ref.dtype)


WEIGHT_BLOCK_BYTES = 6 * 1024 * 1024


def matmul(a, w, out_dtypes, residual=None, gate=None, gate_col=None, tm=512, name="mm"):
    m, k = a.shape
    n = w.shape[1]
    tm, tn = _tile(m, tm), _tile(n, WEIGHT_BLOCK_BYTES // (2 * k))
    in_specs = [pl.BlockSpec((tm, k), lambda j, i: (i, 0)),
                pl.BlockSpec((k, tn), lambda j, i: (0, j))]
    args = [a, w]
    if residual is not None:
        in_specs.append(pl.BlockSpec((tm, tn), lambda j, i: (i, j)))
        args.append(residual)
    if gate is not None:
        in_specs.append(pl.BlockSpec((tm, gate.shape[1]), lambda j, i: (i, 0)))
        args.append(gate)
    outs = pl.pallas_call(
        functools.partial(_mm_kernel, has_res=residual is not None,
                          gate_col=gate_col if gate is not None else None, n_out=len(out_dtypes)),
        grid=(n // tn, m // tm),
        in_specs=in_specs,
        out_specs=[pl.BlockSpec((tm, tn), lambda j, i: (i, j)) for _ in out_dtypes],
        out_shape=[jax.ShapeDtypeStruct((m, n), dt) for dt in out_dtypes],
        compiler_params=_cparams(2),
        name=name,
    )(*args)
    return outs


def _project_kernel(a_ref, w_ref, *o_refs, kinds, d, blk):
    r = _dot(a_ref[...], w_ref[...])
    tm, n = r.shape
    for o_ref, kind in zip(o_refs, kinds):
        if kind == "rows":
            o_ref[...] = r.astype(o_ref.dtype)
        elif kind == "heads_t":
            for h in range(n // d):
                o_ref[h] = r[:, h * d:(h + 1) * d].T.astype(o_ref.dtype)
        else:
            for h in range(n // d):
                for j in range(tm // blk):
                    o_ref[h, j] = r[j * blk:(j + 1) * blk, h * d:(h + 1) * d].T.astype(o_ref.dtype)


def project(a, w, bsz, seq, outs, d=LANES, blk=LANES, tm=1024, name="project"):
    m, k = a.shape
    n = w.shape[1]
    tm = _tile(m, tm)
    spt = _seq_tiles(seq, tm)
    assert spt is not None and tm % blk == 0 and n % d == 0 and 2 * k * n <= 2 * WEIGHT_BLOCK_BYTES
    specs, shapes = [], []
    for kind, dt in outs:
        if kind == "rows":
            specs.append(pl.BlockSpec((tm, n), lambda i: (i, 0)))
            shapes.append(jax.ShapeDtypeStruct((m, n), dt))
        elif kind == "heads_t":
            specs.append(pl.BlockSpec((None, n // d, d, tm), lambda i: (i // spt, 0, 0, i % spt)))
            shapes.append(jax.ShapeDtypeStruct((bsz, n // d, d, seq), dt))
        else:
            specs.append(pl.BlockSpec((None, n // d, tm // blk, d, blk), lambda i: (i // spt, 0, i % spt, 0, 0)))
            shapes.append(jax.ShapeDtypeStruct((bsz, n // d, seq // blk, d, blk), dt))
    return pl.pallas_call(
        functools.partial(_project_kernel, kinds=tuple(kind for kind, _ in outs), d=d, blk=blk),
        grid=(m // tm,),
        in_specs=[pl.BlockSpec((tm, k), lambda i: (i, 0)), pl.BlockSpec((k, n), lambda i: (0, 0))],
        out_specs=specs,
        out_shape=shapes,
        compiler_params=_cparams(1),
        name=name,
    )(a, w)


def _swiglu_up_kernel(a_ref, w1_ref, w3_ref, o_ref):
    a = a_ref[...]
    x1 = _dot(a, w1_ref[...])
    x3 = _dot(a, w3_ref[...])
    o_ref[...] = (x1 * _sigmoid(x1) * x3).astype(o_ref.dtype)


def swiglu_up(a, w1, w3, tm=512):
    m, k = a.shape
    n = w1.shape[1]
    tm, tn = _tile(m, tm), _tile(n, WEIGHT_BLOCK_BYTES // (2 * k))
    return pl.pallas_call(
        _swiglu_up_kernel,
        grid=(n // tn, m // tm),
        in_specs=[pl.BlockSpec((tm, k), lambda j, i: (i, 0)),
                  pl.BlockSpec((k, tn), lambda j, i: (0, j)),
                  pl.BlockSpec((k, tn), lambda j, i: (0, j))],
        out_specs=pl.BlockSpec((tm, tn), lambda j, i: (i, j)),
        out_shape=jax.ShapeDtypeStruct((m, n), BF),
        compiler_params=_cparams(2),
        name="swiglu_up",
    )(a, w1, w3)


def _gate_merge_kernel(u_ref, o0_ref, o1_ref, o2_ref, o3_ref, wg_ref, wb_ref, out_ref):
    u = u_ref[...]
    acc = None
    for b, o_ref in enumerate((o0_ref, o1_ref, o2_ref, o3_ref)):
        t = _sigmoid(_dot(u, wg_ref[b])) * _dot(o_ref[...], wb_ref[b])
        acc = t if acc is None else acc + t
    out_ref[...] = acc.astype(out_ref.dtype)


def gate_merge(u, branches, wg, wb, tm=512, tn=512):
    t, d = u.shape
    bw = branches[0].shape[1]
    n = wg.shape[2]
    tm = min(tm, t)
    return pl.pallas_call(
        _gate_merge_kernel,
        grid=(n // tn, t // tm),
        in_specs=[pl.BlockSpec((tm, d), lambda j, i: (i, 0))]
        + [pl.BlockSpec((tm, bw), lambda j, i: (i, 0)) for _ in range(N_BRANCH)]
        + [pl.BlockSpec((N_BRANCH, d, tn), lambda j, i: (0, 0, j)),
           pl.BlockSpec((N_BRANCH, bw, tn), lambda j, i: (0, 0, j))],
        out_specs=pl.BlockSpec((tm, tn), lambda j, i: (i, j)),
        out_shape=jax.ShapeDtypeStruct((t, n), BF),
        compiler_params=_cparams(2),
        name="gate_merge",
    )(u, *branches, wg, wb)


def _ple_kernel(h_ref, p_ref, un_ref, wp_ref, wg_ref, o_ref):
    o_ref[...] = h_ref[...] + _dot(p_ref[...], wp_ref[...]) * _sigmoid(_dot(un_ref[...], wg_ref[...]))


def ple_update(h, p, un, wp, wg, tm=512, tn=1024):
    t, d = h.shape
    tm = min(tm, t)
    return pl.pallas_call(
        _ple_kernel,
        grid=(d // tn, t // tm),
        in_specs=[pl.BlockSpec((tm, tn), lambda j, i: (i, j)),
                  pl.BlockSpec((tm, p.shape[1]), lambda j, i: (i, 0)),
                  pl.BlockSpec((tm, d), lambda j, i: (i, 0)),
                  pl.BlockSpec((p.shape[1], tn), lambda j, i: (0, j)),
                  pl.BlockSpec((d, tn), lambda j, i: (0, j))],
        out_specs=pl.BlockSpec((tm, tn), lambda j, i: (i, j)),
        out_shape=jax.ShapeDtypeStruct((t, d), F32),
        compiler_params=_cparams(2),
        name="ple_update",
    )(h, p, un, wp, wg)


def _router_kernel(un_ref, wr_ref, g_ref, *, n_experts):
    logits = _dot(un_ref[...], wr_ref[...])
    lane = lax.broadcasted_iota(jnp.int32, logits.shape, 1).astype(F32)
    real = lane < n_experts
    logits = jnp.where(real, logits, NEG)
    e = jnp.where(real, jnp.exp(logits - jnp.max(logits, axis=1, keepdims=True)), 0.0)
    probs = e / jnp.sum(e, axis=1, keepdims=True)
    p1 = jnp.max(probs, axis=1, keepdims=True)
    i1 = jnp.min(jnp.where(probs == p1, lane, float(LANES)), axis=1, keepdims=True)
    first = lane == i1
    rest = jnp.where(first | ~real, -1.0, probs)
    p2 = jnp.max(rest, axis=1, keepdims=True)
    i2 = jnp.min(jnp.where(rest == p2, lane, float(LANES)), axis=1, keepdims=True)
    second = lane == i2
    denom = p1 + p2
    g_ref[...] = jnp.where(first, p1 / denom, 0.0) + jnp.where(second, p2 / denom, 0.0)


def router_gate(un, wr_pad, n_experts, tm=512):
    t, d = un.shape
    tm = min(tm, t)
    return pl.pallas_call(
        functools.partial(_router_kernel, n_experts=n_experts),
        grid=(t // tm,),
        in_specs=[pl.BlockSpec((tm, d), lambda i: (i, 0)),
                  pl.BlockSpec((d, LANES), lambda i: (0, 0))],
        out_specs=pl.BlockSpec((tm, LANES), lambda i: (i, 0)),
        out_shape=jax.ShapeDtypeStruct((t, LANES), F32),
        compiler_params=_cparams(1),
        name="router_gate",
    )(un, wr_pad)


Q_LORA_OFF = 0


def _proj_misc_kernel(u_ref, wm_ref, qn_ref, kvn_ref, wqa_ref, wqb_ref, cs_ref,
                      ckv_ref, krp_ref, kidx_ref, idxw_ref, qcat_ref, *, q_lora, kv_lora, lanes_are_tokens):
    x = _dot(u_ref[...], wm_ref[...])
    cos = cs_ref[:, 0:LANES]
    sin = cs_ref[:, LANES:2 * LANES]
    o = q_lora
    ckv_ref[...] = _rms(x[:, o:o + kv_lora], kvn_ref[...])
    o += kv_lora
    krp_ref[...] = x[:, o:o + LANES] * cos + x[:, o + LANES:o + 2 * LANES] * sin
    o += 2 * LANES
    kidx_ref[...] = x[:, o:o + LANES]
    idxw = x[:, o + LANES:o + 2 * LANES]
    if lanes_are_tokens:
        idxw_ref[...] = idxw.T[0:H_IDX]
    else:
        idxw_ref[...] = idxw
    cqn = _rms(x[:, 0:q_lora], qn_ref[...]).astype(BF)
    qa = _dot(cqn, wqa_ref[...])
    qb = _dot(cqn, wqb_ref[...])
    for h in range(H_MLA):
        lo = 2 * h * LANES
        nope = qa[:, lo:lo + LANES]
        rope = qa[:, lo + LANES:lo + 2 * LANES] * cos + qb[:, h * LANES:(h + 1) * LANES] * sin
        if lanes_are_tokens:
            qcat_ref[h, 0:LANES] = nope.T.astype(BF)
            qcat_ref[h, LANES:2 * LANES] = rope.T.astype(BF)
        else:
            qcat_ref[:, lo:lo + LANES] = nope.astype(BF)
            qcat_ref[:, lo + LANES:lo + 2 * LANES] = rope.astype(BF)


def _seq_tiles(seq, tm):
    return seq // tm if seq % tm == 0 else None


def proj_misc(u, bsz, seq, wm, qn, kvn, wqa, wqb, cs, tm=512):
    t, d = u.shape
    tm = min(tm, t)
    q_lora, kv_lora = qn.shape[1], kvn.shape[1]
    n_pos_blocks = cs.shape[0] // tm
    spt = _seq_tiles(seq, tm)
    row = lambda n: pl.BlockSpec((tm, n), lambda i: (i, 0))
    full = lambda a: pl.BlockSpec(a.shape, lambda i: (0, 0))
    if spt is None:
        q_specs = [row(LANES), row(2 * LANES * H_MLA)]
        q_shapes = [jax.ShapeDtypeStruct((t, LANES), F32), jax.ShapeDtypeStruct((t, 2 * LANES * H_MLA), BF)]
    else:
        q_specs = [pl.BlockSpec((None, H_IDX, tm), lambda i: (i // spt, 0, i % spt)),
                   pl.BlockSpec((None, H_MLA, 2 * LANES, tm), lambda i: (i // spt, 0, 0, i % spt))]
        q_shapes = [jax.ShapeDtypeStruct((bsz, H_IDX, seq), F32),
                    jax.ShapeDtypeStruct((bsz, H_MLA, 2 * LANES, seq), BF)]
    return pl.pallas_call(
        functools.partial(_proj_misc_kernel, q_lora=q_lora, kv_lora=kv_lora, lanes_are_tokens=spt is not None),
        grid=(t // tm,),
        in_specs=[row(d), full(wm), full(qn), full(kvn), full(wqa), full(wqb),
                  pl.BlockSpec((tm, 2 * LANES), lambda i: (i % n_pos_blocks, 0))],
        out_specs=[row(kv_lora), row(LANES), row(LANES)] + q_specs,
        out_shape=[jax.ShapeDtypeStruct((t, kv_lora), F32),
                   jax.ShapeDtypeStruct((t, LANES), F32),
                   jax.ShapeDtypeStruct((t, LANES), F32)] + q_shapes,
        compiler_params=_cparams(1),
        name="proj_misc",
    )(u, wm, qn, kvn, wqa, wqb, cs)


def _mla_kv_up_kernel(ckv_ref, krp_ref, wk_ref, wv_ref, kcat_ref, vt_ref):
    c = ckv_ref[...].astype(BF)
    kn = _dot(c, wk_ref[...])
    krp = krp_ref[...].astype(BF)
    for h in range(H_MLA):
        kcat_ref[:, 2 * h * LANES:(2 * h + 1) * LANES] = kn[:, h * LANES:(h + 1) * LANES].astype(BF)
        kcat_ref[:, (2 * h + 1) * LANES:(2 * h + 2) * LANES] = krp
    v = _dot(c, wv_ref[...])
    for h in range(H_MLA):
        vt_ref[h, 0] = v[:, h * V_DIM:(h + 1) * V_DIM].T.astype(BF)


def mla_kv_up(ckv, krp, wk, wv, bsz, seq, tm):
    t, c = ckv.shape
    spt = seq // tm
    assert seq % tm == 0
    row = lambda n: pl.BlockSpec((tm, n), lambda i: (i, 0))
    full = lambda a: pl.BlockSpec(a.shape, lambda i: (0, 0))
    return pl.pallas_call(
        _mla_kv_up_kernel,
        grid=(t // tm,),
        in_specs=[row(c), row(LANES), full(wk), full(wv)],
        out_specs=[row(2 * LANES * H_MLA),
                   pl.BlockSpec((None, H_MLA, 1, V_DIM, tm), lambda i: (i // spt, 0, i % spt, 0, 0))],
        out_shape=[jax.ShapeDtypeStruct((t, 2 * LANES * H_MLA), BF),
                   jax.ShapeDtypeStruct((bsz, H_MLA, spt, V_DIM, tm), BF)],
        compiler_params=_cparams(1),
        name="mla_kv_up",
    )(ckv, krp, wk, wv)


def _mla_attn_kernel(qt_ref, k_ref, vt_ref, o_ref, *, tq, tk, q_off, scale, heads):
    q0 = q_off + pl.program_id(2) * tq
    qpos = q0 + lax.broadcasted_iota(jnp.int32, (1, tq), 1)
    limit = (lax.shift_right_arithmetic(qpos, CHUNK_SHIFT) + 1) * CHUNK
    last_limit = ((q0 + tq - 1) // CHUNK + 1) * CHUNK
    n_chunks = (last_limit + tk - 1) // tk

    def body(kc, carry):
        ks = pl.multiple_of(kc * tk, tk)
        kpos = ks + lax.broadcasted_iota(jnp.int32, (tk, tq), 0)
        visible = kpos < limit
        scores = [_dot(k_ref[pl.ds(ks, tk), 2 * h * LANES:2 * (h + 1) * LANES], qt_ref[h]) for h in range(heads)]
        probs, stats = [], []
        for h in range(heads):
            m_prev, l_prev, _ = carry[h]
            s = jnp.where(visible, scores[h] * scale, NEG)
            m_new = jnp.maximum(m_prev, jnp.max(s, axis=0, keepdims=True))
            p = jnp.exp(s - m_new)
            alpha = jnp.exp(m_prev - m_new)
            stats.append((m_new, alpha * l_prev + jnp.sum(p, axis=0, keepdims=True), alpha))
            probs.append(p.astype(BF))
        out = []
        for h in range(heads):
            m_new, l_new, alpha = stats[h]
            out.append((m_new, l_new, alpha * carry[h][2] + _dot(vt_ref[h, kc], probs[h])))
        return tuple(out)

    init = tuple((jnp.full((1, tq), NEG, F32), jnp.zeros((1, tq), F32), jnp.zeros((V_DIM, tq), F32))
                 for _ in range(heads))
    final = lax.fori_loop(0, n_chunks, body, init)
    for h in range(heads):
        _, l_fin, acc = final[h]
        o_ref[:, h * V_DIM:(h + 1) * V_DIM] = (acc / l_fin).T.astype(o_ref.dtype)


def _chunked_transpose(v, heads, tk):
    b, s, hd = v.shape
    d = hd // heads
    return jnp.transpose(v.reshape(b, s // tk, tk, heads, d), (0, 3, 1, 4, 2))


def _heads_to_lanes(a, heads):
    b, s, hd = a.shape
    return jnp.transpose(a.reshape(b, s, heads, hd // heads), (0, 2, 3, 1))


def mla_attention(qt, kcat, vt, q_off, tq, tk, heads=4):
    b, _, _, sq = qt.shape
    sk = kcat.shape[1]
    scale = (NOPE_DIM + ROPE_DIM) ** -0.5
    return pl.pallas_call(
        functools.partial(_mla_attn_kernel, tq=tq, tk=tk, q_off=q_off, scale=scale, heads=heads),
        grid=(b, H_MLA // heads, sq // tq),
        in_specs=[pl.BlockSpec((None, heads, 2 * LANES, tq), lambda bi, h, qi: (bi, h, 0, qi)),
                  pl.BlockSpec((None, sk, heads * 2 * LANES), lambda bi, h, qi: (bi, 0, h)),
                  pl.BlockSpec((None, heads, sk // tk, V_DIM, tk), lambda bi, h, qi: (bi, h, 0, 0, 0))],
        out_specs=pl.BlockSpec((None, tq, heads * V_DIM), lambda bi, h, qi: (bi, qi, h)),
        out_shape=jax.ShapeDtypeStruct((b, sq, H_MLA * V_DIM), BF),
        compiler_params=_cparams(3),
        name="mla_attention",
    )(qt, kcat, vt)


def _split2(x):
    hi = x.astype(BF)
    return hi, (x - hi.astype(F32)).astype(BF)


def _sb_attn_kernel(q_ref, k_ref, v_ref, tri_ref, o_ref, *, tq, tk, q_off, scale, heads):
    q0 = q_off + pl.program_id(2) * tq
    qpos = q0 + lax.broadcasted_iota(jnp.int32, (tq, 1), 0)
    n_chunks = (q0 + tq - 1 + tk - 1) // tk
    tri2 = tri_ref[...]
    n_blk = tk // LANES
    hs = [slice(h * HEAD_DIM, (h + 1) * HEAD_DIM) for h in range(heads)]

    def step(kc, carry, diagonal):
        ks = pl.multiple_of(kc * tk, tk)
        if diagonal:
            strict = ks + lax.broadcasted_iota(jnp.int32, (1, tk), 1) < qpos
            causal = lambda x: jnp.where(strict, x, 0.0)
        else:
            causal = lambda x: x
        z = [_dot_nt(q_ref[:, hs[h]], k_ref[pl.ds(ks, tk), hs[h]]) for h in range(heads)]
        log_beta, parts = [], []
        for h in range(heads):
            zh = z[h] * scale
            sp = jnp.maximum(zh, 0.0) + jnp.log(1.0 + jnp.exp(-jnp.abs(zh)))
            log_beta.append(zh - sp)
            parts.append(_split2(causal(-sp)))
        after, later = [], []
        for h in range(heads):
            hi, lo = parts[h]
            run = carry[h][1]
            blocks = [None] * n_blk
            for blk in reversed(range(n_blk)):
                sl = slice(blk * LANES, (blk + 1) * LANES)
                sums = _dot(jnp.concatenate([hi[:, sl], lo[:, sl]], axis=1), tri2)
                blocks[blk] = sums[:, 0:LANES] + run
                run = run + sums[:, LANES:2 * LANES]
            after.append(jnp.concatenate(blocks, axis=1))
            later.append(run)
        weights = [causal(jnp.exp(log_beta[h] + after[h])).astype(BF) for h in range(heads)]
        return tuple((carry[h][0] + _dot(weights[h], v_ref[pl.ds(ks, tk), hs[h]]), later[h]) for h in range(heads))

    n_below = q0 // tk
    init = tuple((jnp.zeros((tq, HEAD_DIM), F32), jnp.zeros((tq, LANES), F32)) for _ in range(heads))
    state = lax.fori_loop(0, n_chunks - n_below, lambda it, c: step(n_chunks - 1 - it, c, True), init)
    final = lax.fori_loop(0, n_below, lambda it, c: step(n_below - 1 - it, c, False), state)
    for h in range(heads):
        o_ref[:, hs[h]] = final[h][0].astype(o_ref.dtype)


def sb_attention(q, k, v, q_off, tq, tk, heads=4):
    b, sq, _ = q.shape
    sk = k.shape[1]
    assert sk % tk == 0 and tk % LANES == 0 and H_SB % heads == 0
    j = lax.broadcasted_iota(jnp.int32, (LANES, LANES), 0)
    s = lax.broadcasted_iota(jnp.int32, (LANES, LANES), 1)
    tri = jnp.concatenate([(j > s).astype(BF), jnp.ones((LANES, LANES), BF)], axis=1)
    tri = jnp.concatenate([tri, tri], axis=0)
    width = heads * HEAD_DIM
    return pl.pallas_call(
        functools.partial(_sb_attn_kernel, tq=tq, tk=tk, q_off=q_off, scale=HEAD_DIM ** -0.5, heads=heads),
        grid=(b, H_SB // heads, sq // tq),
        in_specs=[pl.BlockSpec((None, tq, width), lambda bi, h, qi: (bi, qi, h)),
                  pl.BlockSpec((None, sk, width), lambda bi, h, qi: (bi, 0, h)),
                  pl.BlockSpec((None, sk, width), lambda bi, h, qi: (bi, 0, h)),
                  pl.BlockSpec((2 * LANES, 2 * LANES), lambda bi, h, qi: (0, 0))],
        out_specs=pl.BlockSpec((None, tq, width), lambda bi, h, qi: (bi, qi, h)),
        out_shape=jax.ShapeDtypeStruct((b, sq, H_SB * HEAD_DIM), BF),
        compiler_params=_cparams(3),
        name="sb_attention",
    )(q, k, v, tri)


def _band_attn_kernel(q_ref, k_ref, v_ref, bm_ref, o_ref, *, tq, win, kpos_base, scale):
    w0 = pl.multiple_of(pl.program_id(1) * tq, tq)
    kpos = kpos_base + w0 + lax.broadcasted_iota(jnp.int32, (1, win), 1)
    exists = kpos >= 0
    hs = [slice(h * HEAD_DIM, (h + 1) * HEAD_DIM) for h in range(H_BAND)]
    scores = [_dot_nt(q_ref[:, hs[h]], k_ref[pl.ds(w0, win), hs[h]]) for h in range(H_BAND)]
    probs, denoms = [], []
    for h in range(H_BAND):
        s = jnp.where(exists, scores[h] * scale + bm_ref[h], NEG)
        p = jnp.exp(s - jnp.max(s, axis=1, keepdims=True))
        denoms.append(jnp.sum(p, axis=1, keepdims=True))
        probs.append(p.astype(BF))
    for h in range(H_BAND):
        o = _dot(probs[h], v_ref[pl.ds(w0, win), hs[h]])
        o_ref[:, hs[h]] = (o / denoms[h]).astype(o_ref.dtype)


def _band_window(tq):
    return -(-(tq + N_PREV_CHUNKS * CHUNK) // LANES) * LANES


def band_bias_mask(rel_bias, tq):
    win = _band_window(tq)
    i = jnp.arange(tq)[:, None]
    j = jnp.arange(win)[None, :]
    rel = jnp.clip(i + N_PREV_CHUNKS * CHUNK - j, -REL_CLIP, REL_CLIP) + REL_CLIP
    ci, cj = i // CHUNK, j // CHUNK
    inside = (cj >= ci) & (cj <= ci + N_PREV_CHUNKS)
    return jnp.where(inside[None], _table_lookup(rel_bias, rel), NEG)


def band_attention(q, k_pad, v_pad, bias_mask, tq, kpos_base):
    b, sq, _ = q.shape
    skp = k_pad.shape[1]
    win = _band_window(tq)
    width = H_BAND * HEAD_DIM
    assert skp >= sq - tq + win
    return pl.pallas_call(
        functools.partial(_band_attn_kernel, tq=tq, win=win, kpos_base=kpos_base, scale=HEAD_DIM ** -0.5),
        grid=(b, sq // tq),
        in_specs=[pl.BlockSpec((None, tq, width), lambda bi, qi: (bi, qi, 0)),
                  pl.BlockSpec((None, skp, width), lambda bi, qi: (bi, 0, 0)),
                  pl.BlockSpec((None, skp, width), lambda bi, qi: (bi, 0, 0)),
                  pl.BlockSpec((H_BAND, tq, win), lambda bi, qi: (0, 0, 0))],
        out_specs=pl.BlockSpec((None, tq, width), lambda bi, qi: (bi, qi, 0)),
        out_shape=jax.ShapeDtypeStruct((b, sq, width), BF),
        compiler_params=_cparams(2),
        name="band_attention",
    )(q, k_pad, v_pad, bias_mask)


def _sortable(x):
    i = lax.bitcast_convert_type(x, jnp.int32)
    return i ^ (lax.shift_right_arithmetic(i, 31) & 0x7FFFFFFF)


def _dsa_kernel(qi2t_ref, wt_ref, klo_ref, khi_ref, qt_ref, k_ref, vt_ref, bnear_ref, bfar_ref, tri_ref,
                o_ref, key_ref, m_ref, l_ref, acc_ref, eqc_ref,
                *, tq, tk, big, q_off, topk, scale, w_scale):
    wide = big * tk
    near_after = -(-tq // tk)
    q0 = q_off + pl.program_id(1) * tq
    qpos = q0 + lax.broadcasted_iota(jnp.int32, (1, tq), 1)
    limit = (lax.shift_right_arithmetic(qpos, CHUNK_SHIFT) + 1) * CHUNK
    last_limit = ((q0 + tq - 1) // CHUNK + 1) * CHUNK
    n_wide = (last_limit + wide - 1) // wide
    diag = q0 // tk

    w = wt_ref[...] * w_scale

    def score_body(c, carry):
        ks = pl.multiple_of(c * wide, wide)
        klo = klo_ref[pl.ds(ks, wide), :]
        khi = khi_ref[pl.ds(ks, wide), :]
        acc = jnp.zeros((wide, tq), F32)
        for pair in range(H_IDX // 2):
            q2 = qi2t_ref[pair]
            acc = acc + w[2 * pair:2 * pair + 1] * jnp.maximum(_dot(klo, q2), 0.0)
            acc = acc + w[2 * pair + 1:2 * pair + 2] * jnp.maximum(_dot(khi, q2), 0.0)
        kpos = ks + lax.broadcasted_iota(jnp.int32, (wide, tq), 0)
        keys = _sortable(jnp.where(kpos < limit, acc + 0.0, -jnp.inf))
        for blk in range(big):
            key_ref[c * big + blk] = keys[blk * tk:(blk + 1) * tk]
        return carry

    lax.fori_loop(0, n_wide, score_body, 0)

    def count_ge(cand):
        def body(c, cnt):
            for blk in range(big):
                cnt = cnt + jnp.where(key_ref[c * big + blk] >= cand, 1.0, 0.0)
            return cnt
        cnt = lax.fori_loop(0, n_wide, body, jnp.zeros((tk, tq), F32))
        return jnp.sum(cnt, axis=0, keepdims=True)

    def bit_body(it, thr):
        cand = thr + lax.shift_left(jnp.int32(1), 31 - it)
        return jnp.where(count_ge(cand) >= topk, cand, thr)

    thr = lax.fori_loop(0, 32, bit_body, jnp.full((1, tq), INT_MIN, jnp.int32))
    n_above = count_ge(thr + 1)
    n_ties_kept = topk - n_above

    m_ref[...] = jnp.full(m_ref.shape, NEG, F32)
    l_ref[...] = jnp.zeros(l_ref.shape, F32)
    acc_ref[...] = jnp.zeros(acc_ref.shape, F32)
    eqc_ref[...] = jnp.zeros(eqc_ref.shape, F32)
    tri = tri_ref[...]
    hs = [slice(h * HEAD_DIM, (h + 1) * HEAD_DIM) for h in range(H_DSA)]

    def attend(first_blk, n_blk, bias_of_head, below_tile):
        width = n_blk * tk
        ks = pl.multiple_of(first_blk * tk, tk)
        key = jnp.concatenate([key_ref[first_blk + i] for i in range(n_blk)], axis=0)
        eq = key == thr
        eq_bf = jnp.where(eq, 1.0, 0.0).astype(BF)
        seen = eqc_ref[0:1, :]
        rank = [None] * n_blk
        for i in range(n_blk):
            counts = _dot(tri, eq_bf[i * tk:(i + 1) * tk])
            rank[i] = counts[0:tk] + seen
            seen = seen + counts[tk:tk + 1]
        eqc_ref[0:1, :] = seen
        sel = (key > thr) | (eq & (jnp.concatenate(rank, axis=0) <= n_ties_kept))
        if not below_tile:
            sel = sel & (ks + lax.broadcasted_iota(jnp.int32, (width, tq), 0) < limit)
        scores = [_dot(k_ref[pl.ds(ks, width), hs[h]], qt_ref[h]) for h in range(H_DSA)]
        probs, alphas = [], []
        for h in range(H_DSA):
            s = jnp.where(sel, scores[h] * scale + bias_of_head(h), NEG)
            m_prev = m_ref[h]
            m_new = jnp.maximum(m_prev, jnp.max(s, axis=0, keepdims=True))
            p = jnp.where(sel, jnp.exp(s - m_new), 0.0)
            alpha = jnp.exp(m_prev - m_new)
            l_ref[h] = alpha * l_ref[h] + jnp.sum(p, axis=0, keepdims=True)
            m_ref[h] = m_new
            probs.append(p.astype(BF))
            alphas.append(alpha)
        for h in range(H_DSA):
            vt = jnp.concatenate([vt_ref[h, first_blk + i] for i in range(n_blk)], axis=1)
            acc_ref[h] = alphas[h] * acc_ref[h] + _dot(vt, probs[h])

    far_bias = lambda h: bfar_ref[h][:, 0:1]
    n_far_wide = jnp.maximum(q0 - tk, 0) // wide

    def far_wide_body(c, carry):
        attend(c * big, big, far_bias, True)
        return carry

    lax.fori_loop(0, n_far_wide, far_wide_body, 0)

    def far_body(kc, carry):
        attend(kc, 1, far_bias, True)
        return carry

    lax.fori_loop(n_far_wide * big, jnp.maximum(diag - 1, n_far_wide * big), far_body, 0)

    @pl.when(diag >= 1)
    def _():
        attend(diag - 1, 1, lambda h: bnear_ref[0, h], True)

    for d in range(near_after):
        attend(diag + d, 1, lambda h, d=d: bnear_ref[d + 1, h], False)

    for h in range(H_DSA):
        o_ref[:, hs[h]] = (acc_ref[h] / l_ref[h]).T.astype(o_ref.dtype)


def t5_bucket(rel):
    half = T5_BUCKETS // 2
    max_exact = half // 2
    n = jnp.abs(rel)
    nf = jnp.maximum(n, 1).astype(F32)
    large = max_exact + (jnp.log(nf / max_exact) / math.log(T5_MAX_DIST / max_exact)
                         * (half - max_exact)).astype(jnp.int32)
    large = jnp.minimum(large, half - 1)
    return jnp.where(rel > 0, half, 0) + jnp.where(n < max_exact, n, large)


def _table_lookup(table, idx):
    onehot = jax.nn.one_hot(idx, table.shape[0], dtype=F32)
    out = jnp.einsum("...n,nh->...h", onehot, table.astype(F32), precision=lax.Precision.HIGHEST)
    return jnp.moveaxis(out, -1, 0)


def dsa_bias_tables(t5_table, tq, tk=LANES):
    j = jnp.arange(tk)[:, None]
    i = jnp.arange(tq)[None, :]
    near = jnp.stack([_table_lookup(t5_table, t5_bucket(d * tk + j - i)) for d in range(-1, -(-tq // tk))],
                     axis=0)
    far_rel = -jnp.ones((1, tk), jnp.int32) * (2 * tk)
    far = _table_lookup(t5_table, t5_bucket(far_rel))
    return near, far


def dsa_attention(qi2t, wt, klo, khi, qt, k, vt, bias_tables, q_off, n_keys, tq, big, tk=LANES):
    b, _, _, sq = qt.shape
    sk = k.shape[1]
    assert tk >= T5_MAX_DIST and q_off % tk == 0 and (tq % tk == 0 or sq == tq) and sk % (big * tk) == 0
    topk = min(TOPK_MAX, n_keys // 4)
    near, far = bias_tables
    ss = lax.broadcasted_iota(jnp.int32, (tk + 16, tk), 0)
    jj = lax.broadcasted_iota(jnp.int32, (tk + 16, tk), 1)
    tri = ((jj <= ss) | (ss >= tk)).astype(BF)
    whole = lambda n: pl.BlockSpec((None, sk, n), lambda bi, qi: (bi, 0, 0))
    heads_t = lambda h, d: pl.BlockSpec((None, h, d, tq), lambda bi, qi: (bi, 0, 0, qi))
    const = lambda a: pl.BlockSpec(a.shape, lambda bi, qi: (0,) * a.ndim)
    return pl.pallas_call(
        functools.partial(_dsa_kernel, tq=tq, tk=tk, big=big, q_off=q_off, topk=topk,
                          scale=HEAD_DIM ** -0.5, w_scale=H_IDX ** -0.5 * D_IDX ** -0.5),
        grid=(b, sq // tq),
        in_specs=[heads_t(H_IDX // 2, LANES), pl.BlockSpec((None, H_IDX, tq), lambda bi, qi: (bi, 0, qi)),
                  whole(LANES), whole(LANES), heads_t(H_DSA, HEAD_DIM), whole(H_DSA * HEAD_DIM),
                  pl.BlockSpec((None, H_DSA, sk // tk, HEAD_DIM, tk), lambda bi, qi: (bi, 0, 0, 0, 0)),
                  const(near), const(far), const(tri)],
        out_specs=pl.BlockSpec((None, tq, H_DSA * HEAD_DIM), lambda bi, qi: (bi, qi, 0)),
        out_shape=jax.ShapeDtypeStruct((b, sq, H_DSA * HEAD_DIM), BF),
        scratch_shapes=[pltpu.VMEM((sk // tk, tk, tq), jnp.int32),
                        pltpu.VMEM((H_DSA, 1, tq), F32), pltpu.VMEM((H_DSA, 1, tq), F32),
                        pltpu.VMEM((H_DSA, HEAD_DIM, tq), F32), pltpu.VMEM((8, tq), F32)],
        compiler_params=_cparams(2),
        name="dsa_attention",
    )(qi2t, wt, klo, khi, qt, k, vt, near, far, tri)


def _pad_cols(a, n):
    return jnp.pad(a, ((0, 0), (0, n - a.shape[1])))


def _rotate_half_cols(w):
    half = w.shape[1] // 2
    return jnp.concatenate([-w[:, half:], w[:, :half]], axis=1)


def prepare_layer_weights(w_in, q_norm, w_uq, kv_norm, w_ukv, w_branch):
    d = w_in.shape[0]
    q_lora, kv_lora = q_norm.shape[0], kv_norm.shape[0]
    sizes = (q_lora, kv_lora, ROPE_DIM, 3 * H_SB * HEAD_DIM, 3 * H_BAND * HEAD_DIM, 3 * H_DSA * HEAD_DIM,
             H_IDX * D_IDX, D_IDX, H_IDX, N_BRANCH * d)
    cols, start = [], 0
    for s in sizes:
        cols.append(w_in[:, start:start + s])
        start += s
    w_cq, w_ckv, w_kr, w_sb, w_bd, w_ds, w_iq, w_ik, w_iw, w_g = cols
    wm = jnp.concatenate([w_cq, w_ckv, _pad_cols(w_kr, LANES), _pad_cols(_rotate_half_cols(w_kr), LANES),
                          _pad_cols(w_ik, LANES), _pad_cols(w_iw, LANES)], axis=1).astype(BF)
    qa, qb = [], []
    hd = NOPE_DIM + ROPE_DIM
    for h in range(H_MLA):
        wh = w_uq[:, h * hd:(h + 1) * hd]
        qa += [wh[:, :NOPE_DIM], _pad_cols(wh[:, NOPE_DIM:], LANES)]
        qb.append(_pad_cols(_rotate_half_cols(wh[:, NOPE_DIM:]), LANES))
    wqa = jnp.concatenate(qa, axis=1).astype(BF)
    wqb = jnp.concatenate(qb, axis=1).astype(BF)
    kvd = NOPE_DIM + V_DIM
    wk = jnp.concatenate([w_ukv[:, h * kvd:h * kvd + NOPE_DIM] for h in range(H_MLA)], axis=1).astype(BF)
    wv = jnp.concatenate([w_ukv[:, h * kvd + NOPE_DIM:(h + 1) * kvd] for h in range(H_MLA)], axis=1).astype(BF)
    wg = jnp.transpose(w_g.reshape(d, N_BRANCH, d), (1, 0, 2)).astype(BF)
    return dict(wm=wm, wqa=wqa, wqb=wqb, wk=wk, wv=wv,
                w_sb=w_sb.astype(BF), w_bd=w_bd.astype(BF), w_ds=w_ds.astype(BF), w_iq=w_iq.astype(BF),
                wg=wg, wb=w_branch.astype(BF),
                qn=q_norm.reshape(1, -1), kvn=kv_norm.reshape(1, -1))


def rope_table(pos):
    half = ROPE_DIM // 2
    inv = ROPE_THETA ** (-jnp.arange(half, dtype=F32) / half)
    ang = pos.astype(F32)[:, None] * inv[None, :]
    cos, sin = jnp.cos(ang), jnp.sin(ang)
    z = jnp.zeros((pos.shape[0], LANES - ROPE_DIM), F32)
    return jnp.concatenate([cos, cos, z, sin, sin, z], axis=1)


def _kidx_pair(kidx):
    kb = kidx.astype(BF)
    return (jnp.pad(kb, ((0, 0), (0, 0), (0, LANES - D_IDX))),
            jnp.pad(kb, ((0, 0), (0, 0), (LANES - D_IDX, 0))))


def _with_past(past, new, pad_to):
    b = new.shape[0]
    a = jnp.concatenate([past.reshape(b, past.shape[1], -1).astype(BF), new.astype(BF)], axis=1)
    return jnp.pad(a, ((0, 0), (0, pad_to - a.shape[1]), (0, 0)))


def mixing_block(u, bsz, seq, lw, cs, past, band_mask, dsa_tables, w_out, h):
    t = bsz * seq
    ckv, krp, kidxp, idxw, qcat = proj_misc(u, bsz, seq, lw["wm"], lw["qn"], lw["kvn"], lw["wqa"], lw["wqb"], cs)
    hw = H_SB * HEAD_DIM

    def qkv(w):
        q, = matmul(u, w[:, :hw], (BF,), tm=1024, name="proj_q")
        k32, kbf = matmul(u, w[:, hw:2 * hw], (F32, BF), tm=1024, name="proj_k")
        v32, vbf = matmul(u, w[:, 2 * hw:], (F32, BF), tm=1024, name="proj_v")
        return q, k32, kbf, v32, vbf

    sb_q, sb_k, sb_kb, sb_v, sb_vb = qkv(lw["w_sb"])
    bd_q, bd_k, bd_kb, bd_v, bd_vb = qkv(lw["w_bd"])
    kr = krp[:, :ROPE_DIM]
    kidx = kidxp[:, :D_IDX]
    r3 = lambda a: a.reshape(bsz, seq, -1)
    band_pad = N_PREV_CHUNKS * CHUNK

    if past is None:
        w_ds = lw["w_ds"]
        ds_qt, = project(u, w_ds[:, :hw], bsz, seq, [("heads_t", BF)], name="proj_q")
        ds_k, ds_kb = matmul(u, w_ds[:, hw:2 * hw], (F32, BF), tm=1024, name="proj_k")
        ds_v, ds_vt = project(u, w_ds[:, 2 * hw:], bsz, seq, [("rows", F32), ("blocks_t", BF)], name="proj_v")
        idx_qt, = project(u, lw["w_iq"], bsz, seq, [("heads_t", BF)], name="proj_idxq")
        kcat, vmla_t = mla_kv_up(ckv, krp, lw["wk"], lw["wv"], bsz, seq, PROMPT_TK)
        o_mla = mla_attention(qcat, r3(kcat), vmla_t, 0, PROMPT_TQ, PROMPT_TK)
        o_sb = sb_attention(r3(sb_q), r3(sb_kb), r3(sb_vb), 0, PROMPT_TQ, PROMPT_TK)
        front = ((0, 0), (band_pad, 0), (0, 0))
        o_bd = band_attention(r3(bd_q), jnp.pad(r3(bd_kb), front), jnp.pad(r3(bd_vb), front),
                              band_mask, PROMPT_TQ, -band_pad)
        klo, khi = _kidx_pair(r3(kidx))
        o_ds = dsa_attention(idx_qt, idxw, klo, khi, ds_qt, r3(ds_kb), ds_vt, dsa_tables,
                             0, seq, PROMPT_TQ, PROMPT_TK // LANES)
    else:
        ds_q, ds_k, ds_kb, ds_v, ds_vb = qkv(lw["w_ds"])
        idx_q, = matmul(u, lw["w_iq"], (BF,), tm=1024, name="proj_idxq")
        (p_ckv, p_kr, p_sbk, p_sbv, p_bdk, p_bdv, p_dsk, p_dsv, p_kidx) = past
        past_len = p_sbk.shape[1]
        total = past_len + seq
        pad_to = -(-total // SAMPLE_TK) * SAMPLE_TK
        ckv_all = jnp.concatenate([p_ckv, r3(ckv)], axis=1)
        krp_all = jnp.concatenate([jnp.pad(p_kr, ((0, 0), (0, 0), (0, LANES - ROPE_DIM))), r3(krp)], axis=1)
        rows = pad_to - total
        ckv_all = jnp.pad(ckv_all, ((0, 0), (0, rows), (0, 0))).reshape(bsz * pad_to, -1)
        krp_all = jnp.pad(krp_all, ((0, 0), (0, rows), (0, 0))).reshape(bsz * pad_to, -1)
        kcat, vmla_t = mla_kv_up(ckv_all, krp_all, lw["wk"], lw["wv"], bsz, pad_to, SAMPLE_TK)
        o_mla = mla_attention(_heads_to_lanes(r3(qcat), H_MLA), kcat.reshape(bsz, pad_to, -1), vmla_t,
                              past_len, seq, SAMPLE_TK)
        o_sb = sb_attention(r3(sb_q), _with_past(p_sbk, r3(sb_kb), pad_to), _with_past(p_sbv, r3(sb_vb), pad_to),
                            past_len, seq, SAMPLE_TK)
        band_len = _band_window(seq)
        o_bd = band_attention(r3(bd_q), _with_past(p_bdk, r3(bd_kb), band_len), _with_past(p_bdv, r3(bd_vb), band_len),
                              band_mask, seq, past_len - p_bdk.shape[1])
        kidx_all = jnp.pad(jnp.concatenate([p_kidx, r3(kidx)], axis=1), ((0, 0), (0, rows), (0, 0)))
        klo, khi = _kidx_pair(kidx_all)
        o_ds = dsa_attention(_heads_to_lanes(r3(idx_q), H_IDX // 2),
                             jnp.transpose(r3(idxw)[:, :, :H_IDX], (0, 2, 1)), klo, khi,
                             _heads_to_lanes(r3(ds_q), H_DSA), _with_past(p_dsk, r3(ds_kb), pad_to),
                             _chunked_transpose(_with_past(p_dsv, r3(ds_vb), pad_to), H_DSA, LANES),
                             dsa_tables, past_len, total, seq, SAMPLE_TK // LANES)

    f2 = lambda a: a.reshape(t, -1)
    merged = gate_merge(u, [f2(o_mla), f2(o_sb), f2(o_bd), f2(o_ds)], lw["wg"], lw["wb"])
    h_new, = matmul(merged, w_out, (F32,), residual=h, name="mix_out")
    return h_new, (ckv, kr, sb_k, sb_v, bd_k, bd_v, ds_k, ds_v, kidx)


def dense_ffn(h, un, w1, w3, w2):
    act = swiglu_up(un, w1, w3)
    out, = matmul(act, w2, (F32,), residual=h, name="ffn_down")
    return out


def moe_ffn(h, un, router_pad, w1, w3, w2):
    n_experts = w1.shape[0]
    gate = router_gate(un, router_pad, n_experts)
    for e in range(n_experts):
        act = swiglu_up(un, w1[e], w3[e])
        h, = matmul(act, w2[e], (F32,), residual=h, gate=gate, gate_col=e, name="moe_down")
    return h


def layer_step(h, p, bsz, seq, cs, past, lw, band_mask, dsa_tables, fw):
    u = rmsnorm(h, fw["g_mix"], BF)
    h, rows = mixing_block(u, bsz, seq, lw, cs, past, band_mask, dsa_tables, fw["w_out"], h)
    un = rmsnorm(h, fw["g_ffn"], BF)
    if fw["moe"]:
        h = moe_ffn(h, un, fw["router"], fw["w1"], fw["w3"], fw["w2"])
    else:
        h = dense_ffn(h, un, fw["w1"], fw["w3"], fw["w2"])
    un = rmsnorm(h, fw["g_ple"], BF)
    h = ple_update(h, p, un, fw["ple_w"], fw["ple_gate_w"])
    return h, rows


def kernel(x_prompt, x_sample, p_prompt, p_sample, cache_mla_ckv, cache_mla_krope, cache_sb_k, cache_sb_v, cache_band_k, cache_band_v, cache_dsa_k, cache_dsa_v, cache_dsa_kidx, norm_mix, w_in, mla_q_norm, mla_w_uq, mla_kv_norm, mla_w_ukv, band_rel_bias, t5_rel_bias, w_branch, w_out, norm_ffn, ffn_w1, ffn_w3, ffn_w2, moe_router, moe_w1, moe_w3, moe_w2, norm_ple, ple_w, ple_gate_w, norm_final):
    depth = w_in.shape[0]
    bp, sp, d = x_prompt.shape
    bs, ss, _ = x_sample.shape
    past_len = cache_sb_k.shape[2]
    tm = 512
    cs_p = rope_table(jnp.arange(sp))
    cs_s = rope_table(past_len + (jnp.arange(tm) % ss))
    dsa_tables_p = dsa_bias_tables(t5_rel_bias, PROMPT_TQ)
    dsa_tables_s = dsa_bias_tables(t5_rel_bias, ss)
    hp = x_prompt.reshape(bp * sp, d)
    hs = x_sample.reshape(bs * ss, d)
    rows_p, rows_s = [], []
    for i in range(depth):
        lw = prepare_layer_weights(w_in[i], mla_q_norm[i], mla_w_uq[i], mla_kv_norm[i], mla_w_ukv[i], w_branch[i])
        j = i // 2
        fw = dict(g_mix=norm_mix[i], g_ffn=norm_ffn[i], g_ple=norm_ple[i], w_out=w_out[i].astype(BF),
                  ple_w=ple_w[i].astype(BF), ple_gate_w=ple_gate_w[i].astype(BF), moe=i % 2 == 1)
        if i % 2 == 0:
            fw.update(w1=ffn_w1[j].astype(BF), w3=ffn_w3[j].astype(BF), w2=ffn_w2[j].astype(BF))
        else:
            fw.update(router=_pad_cols(moe_router[j], LANES).astype(BF),
                      w1=moe_w1[j].astype(BF), w3=moe_w3[j].astype(BF), w2=moe_w2[j].astype(BF))
        past_i = (cache_mla_ckv[i], cache_mla_krope[i], cache_sb_k[i], cache_sb_v[i], cache_band_k[i],
                  cache_band_v[i], cache_dsa_k[i], cache_dsa_v[i], cache_dsa_kidx[i])
        hp, rp = layer_step(hp, p_prompt[i].reshape(bp * sp, -1).astype(BF), bp, sp, cs_p, None,
                            lw, band_bias_mask(band_rel_bias[i], PROMPT_TQ), dsa_tables_p, fw)
        hs, rs = layer_step(hs, p_sample[i].reshape(bs * ss, -1).astype(BF), bs, ss, cs_s, past_i,
                            lw, band_bias_mask(band_rel_bias[i], ss), dsa_tables_s, fw)
        rows_p.append(rp)
        rows_s.append(rs)
    ones = norm_final
    y_prompt = rmsnorm(hp, ones, F32).reshape(bp, sp, d)
    y_sample = rmsnorm(hs, ones, F32).reshape(bs, ss, d)

    keep = min(N_PREV_CHUNKS * CHUNK, sp)

    def stacked(rows, n, bsz, seq, heads=None, tail=None):
        out = []
        for r in rows:
            a = r[n].reshape(bsz, seq, -1)
            if tail is not None:
                a = a[:, seq - tail:]
            if heads is not None:
                a = a.reshape(a.shape[0], a.shape[1], heads, HEAD_DIM)
            out.append(a)
        return jnp.stack(out, axis=0)

    res = [y_prompt, y_sample]
    for n, heads in ((0, None), (1, None), (2, H_SB), (3, H_SB), (4, H_BAND), (5, H_BAND),
                     (6, H_DSA), (7, H_DSA), (8, None)):
        tail = keep if n in (4, 5) else None
        res.append(stacked(rows_p, n, bp, sp, heads, tail))
        res.append(stacked(rows_s, n, bs, ss, heads))
    return tuple(res)
```

```python
import functools
import math

import jax
import jax.numpy as jnp
from jax import lax
from jax.experimental import pallas as pl
from jax.experimental.pallas import tpu as pltpu

BF = jnp.bfloat16
F32 = jnp.float32

CHUNK = 64
CHUNK_SHIFT = 6
HEAD_DIM = 128
N_BRANCH = 4
H_MLA = 4
NOPE_DIM = 128
ROPE_DIM = 64
V_DIM = 128
ROPE_THETA = 10000.0
H_SB = 4
H_BAND = 4
N_PREV_CHUNKS = 8
REL_CLIP = 128
H_DSA = 4
H_IDX = 16
D_IDX = 64
TOPK_MAX = 256
T5_BUCKETS = 32
T5_MAX_DIST = 128
TOP_K_EXPERTS = 2
EPS = 1e-6

LANES = 128
PROMPT_TQ = 256
PROMPT_TK = 512
SAMPLE_TK = 384
VMEM_LIMIT = 56 * 1024 * 1024
NEG = -1e30
INT_MIN = -2147483648


def _cparams(n_axes):
    return pltpu.CompilerParams(dimension_semantics=("arbitrary",) * n_axes,
                                vmem_limit_bytes=VMEM_LIMIT)


def _tile(n, preferred):
    if n <= preferred:
        return n
    t = preferred - preferred % LANES
    while n % t:
        t -= LANES
    assert t > 0
    return t


def _dot(a, b):
    return jnp.dot(a, b, preferred_element_type=F32)


def _dot_nt(a, b):
    return lax.dot_general(a, b, (((1,), (1,)), ((), ())), preferred_element_type=F32)


def _sigmoid(x):
    return 1.0 / (1.0 + jnp.exp(-x))


def _rms(x, g):
    return x * lax.rsqrt(jnp.mean(x * x, axis=-1, keepdims=True) + EPS) * g


def _rmsnorm_kernel(x_ref, g_ref, o_ref):
    o_ref[...] = _rms(x_ref[...], g_ref[...]).astype(o_ref.dtype)


def rmsnorm(x, g, out_dtype, tm=512):
    t, d = x.shape
    tm = _tile(t, tm)
    return pl.pallas_call(
        _rmsnorm_kernel,
        grid=(t // tm,),
        in_specs=[pl.BlockSpec((tm, d), lambda i: (i, 0)),
                  pl.BlockSpec((1, d), lambda i: (0, 0))],
        out_specs=pl.BlockSpec((tm, d), lambda i: (i, 0)),
        out_shape=jax.ShapeDtypeStruct((t, d), out_dtype),
        compiler_params=_cparams(1),
        name="rmsnorm",
    )(x, g.reshape(1, d))


def _mm_kernel(*refs, has_res, gate_col, n_out):
    a_ref, w_ref = refs[0], refs[1]
    pos = 2
    res_ref = gate_ref = None
    if has_res:
        res_ref = refs[pos]
        pos += 1
    if gate_col is not None:
        gate_ref = refs[pos]
        pos += 1
    r = _dot(a_ref[...], w_ref[...])
    if gate_ref is not None:
        r = r * gate_ref[:, gate_col:gate_col + 1]
    if res_ref is not None:
        r = res_ref[...] + r
    for o_ref in refs[pos:pos + n_out]:
        o_ref[...] = r.astype(o_ref.dtype)


WEIGHT_BLOCK_BYTES = 6 * 1024 * 1024


def matmul(a, w, out_dtypes, residual=None, gate=None, gate_col=None, tm=512, name="mm"):
    m, k = a.shape
    n = w.shape[1]
    tm, tn = _tile(m, tm), _tile(n, WEIGHT_BLOCK_BYTES // (2 * k))
    in_specs = [pl.BlockSpec((tm, k), lambda j, i: (i, 0)),
                pl.BlockSpec((k, tn), lambda j, i: (0, j))]
    args = [a, w]
    if residual is not None:
        in_specs.append(pl.BlockSpec((tm, tn), lambda j, i: (i, j)))
        args.append(residual)
    if gate is not None:
        in_specs.append(pl.BlockSpec((tm, gate.shape[1]), lambda j, i: (i, 0)))
        args.append(gate)
    outs = pl.pallas_call(
        functools.partial(_mm_kernel, has_res=residual is not None,
                          gate_col=gate_col if gate is not None else None, n_out=len(out_dtypes)),
        grid=(n // tn, m // tm),
        in_specs=in_specs,
        out_specs=[pl.BlockSpec((tm, tn), lambda j, i: (i, j)) for _ in out_dtypes],
        out_shape=[jax.ShapeDtypeStruct((m, n), dt) for dt in out_dtypes],
        compiler_params=_cparams(2),
        name=name,
    )(*args)
    return outs


def _project_kernel(a_ref, w_ref, *refs, kinds, d, blk, n_prev, first_layer):
    o_refs = refs[n_prev:]
    r = _dot(a_ref[...], w_ref[...])
    tm, n = r.shape
    for o_ref, kind in zip(o_refs, kinds):
        if kind == "rows":
            o_ref[...] = r.astype(o_ref.dtype)
        elif kind == "stack_heads":
            slab = o_ref.at[0] if first_layer else o_ref
            for h in range(n // d):
                slab[:, h, :] = r[:, h * d:(h + 1) * d].astype(o_ref.dtype)
            if first_layer:
                for layer in range(1, o_ref.shape[0]):
                    o_ref[layer] = jnp.zeros(o_ref.shape[1:], o_ref.dtype)
        elif kind == "heads_t":
            for h in range(n // d):
                o_ref[h] = r[:, h * d:(h + 1) * d].T.astype(o_ref.dtype)
        else:
            for h in range(n // d):
                for j in range(tm // blk):
                    o_ref[h, j] = r[j * blk:(j + 1) * blk, h * d:(h + 1) * d].T.astype(o_ref.dtype)


def project(a, w, bsz, seq, outs, stack=None, d=LANES, blk=LANES, tm=1024, name="project"):
    m, k = a.shape
    n = w.shape[1]
    tm = _tile(m, tm)
    spt = _seq_tiles(seq, tm)
    kinds = tuple(kind for kind, _ in outs)
    assert n % d == 0 and 2 * k * n <= 2 * WEIGHT_BLOCK_BYTES
    assert (spt is not None and tm % blk == 0) or not {"heads_t", "blocks_t"} & set(kinds)
    layer, depth, prev = stack if stack is not None else (0, 1, None)
    specs, shapes, aliases, prev_args = [], [], {}, []
    for kind, dt in outs:
        if kind == "rows":
            specs.append(pl.BlockSpec((tm, n), lambda i: (i, 0)))
            shapes.append(jax.ShapeDtypeStruct((m, n), dt))
        elif kind == "stack_heads":
            if layer == 0:
                specs.append(pl.BlockSpec((depth, tm, n // d, d), lambda i: (0, i, 0, 0)))
            else:
                specs.append(pl.BlockSpec((None, tm, n // d, d), lambda i: (layer, i, 0, 0)))
                aliases[2 + len(prev_args)] = len(shapes)
                prev_args.append(prev)
            shapes.append(jax.ShapeDtypeStruct((depth, m, n // d, d), dt))
        elif kind == "heads_t":
            specs.append(pl.BlockSpec((None, n // d, d, tm), lambda i: (i // spt, 0, 0, i % spt)))
            shapes.append(jax.ShapeDtypeStruct((bsz, n // d, d, seq), dt))
        else:
            specs.append(pl.BlockSpec((None, n // d, tm // blk, d, blk), lambda i: (i // spt, 0, i % spt, 0, 0)))
            shapes.append(jax.ShapeDtypeStruct((bsz, n // d, seq // blk, d, blk), dt))
    return pl.pallas_call(
        functools.partial(_project_kernel, kinds=kinds, d=d, blk=blk, n_prev=len(prev_args),
                          first_layer=layer == 0),
        grid=(m // tm,),
        in_specs=[pl.BlockSpec((tm, k), lambda i: (i, 0)), pl.BlockSpec((k, n), lambda i: (0, 0))]
        + [pl.BlockSpec(memory_space=pl.ANY) for _ in prev_args],
        out_specs=specs,
        out_shape=shapes,
        input_output_aliases=aliases,
        compiler_params=_cparams(1),
        name=name,
    )(a, w, *prev_args)


def _swiglu_up_kernel(a_ref, w1_ref, w3_ref, o_ref):
    a = a_ref[...]
    x1 = _dot(a, w1_ref[...])
    x3 = _dot(a, w3_ref[...])
    o_ref[...] = (x1 * _sigmoid(x1) * x3).astype(o_ref.dtype)


def swiglu_up(a, w1, w3, tm=512):
    m, k = a.shape
    n = w1.shape[1]
    tm, tn = _tile(m, tm), _tile(n, WEIGHT_BLOCK_BYTES // (2 * k))
    return pl.pallas_call(
        _swiglu_up_kernel,
        grid=(n // tn, m // tm),
        in_specs=[pl.BlockSpec((tm, k), lambda j, i: (i, 0)),
                  pl.BlockSpec((k, tn), lambda j, i: (0, j)),
                  pl.BlockSpec((k, tn), lambda j, i: (0, j))],
        out_specs=pl.BlockSpec((tm, tn), lambda j, i: (i, j)),
        out_shape=jax.ShapeDtypeStruct((m, n), BF),
        compiler_params=_cparams(2),
        name="swiglu_up",
    )(a, w1, w3)


def _gate_merge_kernel(u_ref, o0_ref, o1_ref, o2_ref, o3_ref, wg_ref, wb_ref, out_ref):
    u = u_ref[...]
    acc = None
    for b, o_ref in enumerate((o0_ref, o1_ref, o2_ref, o3_ref)):
        t = _sigmoid(_dot(u, wg_ref[b])) * _dot(o_ref[...], wb_ref[b])
        acc = t if acc is None else acc + t
    out_ref[...] = acc.astype(out_ref.dtype)


def gate_merge(u, branches, wg, wb, tm=512, tn=512):
    t, d = u.shape
    bw = branches[0].shape[1]
    n = wg.shape[2]
    tm = min(tm, t)
    return pl.pallas_call(
        _gate_merge_kernel,
        grid=(n // tn, t // tm),
        in_specs=[pl.BlockSpec((tm, d), lambda j, i: (i, 0))]
        + [pl.BlockSpec((tm, bw), lambda j, i: (i, 0)) for _ in range(N_BRANCH)]
        + [pl.BlockSpec((N_BRANCH, d, tn), lambda j, i: (0, 0, j)),
           pl.BlockSpec((N_BRANCH, bw, tn), lambda j, i: (0, 0, j))],
        out_specs=pl.BlockSpec((tm, tn), lambda j, i: (i, j)),
        out_shape=jax.ShapeDtypeStruct((t, n), BF),
        compiler_params=_cparams(2),
        name="gate_merge",
    )(u, *branches, wg, wb)


def _ple_kernel(h_ref, p_ref, un_ref, wp_ref, wg_ref, o_ref):
    o_ref[...] = h_ref[...] + _dot(p_ref[...], wp_ref[...]) * _sigmoid(_dot(un_ref[...], wg_ref[...]))


def ple_update(h, p, un, wp, wg, tm=512, tn=1024):
    t, d = h.shape
    tm = min(tm, t)
    return pl.pallas_call(
        _ple_kernel,
        grid=(d // tn, t // tm),
        in_specs=[pl.BlockSpec((tm, tn), lambda j, i: (i, j)),
                  pl.BlockSpec((tm, p.shape[1]), lambda j, i: (i, 0)),
                  pl.BlockSpec((tm, d), lambda j, i: (i, 0)),
                  pl.BlockSpec((p.shape[1], tn), lambda j, i: (0, j)),
                  pl.BlockSpec((d, tn), lambda j, i: (0, j))],
        out_specs=pl.BlockSpec((tm, tn), lambda j, i: (i, j)),
        out_shape=jax.ShapeDtypeStruct((t, d), F32),
        compiler_params=_cparams(2),
        name="ple_update",
    )(h, p, un, wp, wg)


def _router_kernel(un_ref, wr_ref, g_ref, *, n_experts):
    logits = _dot(un_ref[...], wr_ref[...])
    lane = lax.broadcasted_iota(jnp.int32, logits.shape, 1).astype(F32)
    real = lane < n_experts
    logits = jnp.where(real, logits, NEG)
    e = jnp.where(real, jnp.exp(logits - jnp.max(logits, axis=1, keepdims=True)), 0.0)
    probs = e / jnp.sum(e, axis=1, keepdims=True)
    p1 = jnp.max(probs, axis=1, keepdims=True)
    i1 = jnp.min(jnp.where(probs == p1, lane, float(LANES)), axis=1, keepdims=True)
    first = lane == i1
    rest = jnp.where(first | ~real, -1.0, probs)
    p2 = jnp.max(rest, axis=1, keepdims=True)
    i2 = jnp.min(jnp.where(rest == p2, lane, float(LANES)), axis=1, keepdims=True)
    second = lane == i2
    denom = p1 + p2
    g_ref[...] = jnp.where(first, p1 / denom, 0.0) + jnp.where(second, p2 / denom, 0.0)


def router_gate(un, wr_pad, n_experts, tm=512):
    t, d = un.shape
    tm = min(tm, t)
    return pl.pallas_call(
        functools.partial(_router_kernel, n_experts=n_experts),
        grid=(t // tm,),
        in_specs=[pl.BlockSpec((tm, d), lambda i: (i, 0)),
                  pl.BlockSpec((d, LANES), lambda i: (0, 0))],
        out_specs=pl.BlockSpec((tm, LANES), lambda i: (i, 0)),
        out_shape=jax.ShapeDtypeStruct((t, LANES), F32),
        compiler_params=_cparams(1),
        name="router_gate",
    )(un, wr_pad)


Q_LORA_OFF = 0


def _proj_misc_kernel(u_ref, wm_ref, qn_ref, kvn_ref, wqa_ref, wqb_ref, cs_ref,
                      ckv_ref, krp_ref, kidx_ref, idxw_ref, qcat_ref, *, q_lora, kv_lora, lanes_are_tokens):
    x = _dot(u_ref[...], wm_ref[...])
    cos = cs_ref[:, 0:LANES]
    sin = cs_ref[:, LANES:2 * LANES]
    o = q_lora
    ckv_ref[...] = _rms(x[:, o:o + kv_lora], kvn_ref[...])
    o += kv_lora
    krp_ref[...] = x[:, o:o + LANES] * cos + x[:, o + LANES:o + 2 * LANES] * sin
    o += 2 * LANES
    kidx_ref[...] = x[:, o:o + LANES]
    idxw = x[:, o + LANES:o + 2 * LANES]
    if lanes_are_tokens:
        idxw_ref[...] = idxw.T[0:H_IDX]
    else:
        idxw_ref[...] = idxw
    cqn = _rms(x[:, 0:q_lora], qn_ref[...]).astype(BF)
    qa = _dot(cqn, wqa_ref[...])
    qb = _dot(cqn, wqb_ref[...])
    for h in range(H_MLA):
        lo = 2 * h * LANES
        nope = qa[:, lo:lo + LANES]
        rope = qa[:, lo + LANES:lo + 2 * LANES] * cos + qb[:, h * LANES:(h + 1) * LANES] * sin
        if lanes_are_tokens:
            qcat_ref[h, 0:LANES] = nope.T.astype(BF)
            qcat_ref[h, LANES:2 * LANES] = rope.T.astype(BF)
        else:
            qcat_ref[:, lo:lo + LANES] = nope.astype(BF)
            qcat_ref[:, lo + LANES:lo + 2 * LANES] = rope.astype(BF)


def _seq_tiles(seq, tm):
    return seq // tm if seq % tm == 0 else None


def proj_misc(u, bsz, seq, wm, qn, kvn, wqa, wqb, cs, tm=512):
    t, d = u.shape
    tm = min(tm, t)
    q_lora, kv_lora = qn.shape[1], kvn.shape[1]
    n_pos_blocks = cs.shape[0] // tm
    spt = _seq_tiles(seq, tm)
    row = lambda n: pl.BlockSpec((tm, n), lambda i: (i, 0))
    full = lambda a: pl.BlockSpec(a.shape, lambda i: (0, 0))
    if spt is None:
        q_specs = [row(LANES), row(2 * LANES * H_MLA)]
        q_shapes = [jax.ShapeDtypeStruct((t, LANES), F32), jax.ShapeDtypeStruct((t, 2 * LANES * H_MLA), BF)]
    else:
        q_specs = [pl.BlockSpec((None, H_IDX, tm), lambda i: (i // spt, 0, i % spt)),
                   pl.BlockSpec((None, H_MLA, 2 * LANES, tm), lambda i: (i // spt, 0, 0, i % spt))]
        q_shapes = [jax.ShapeDtypeStruct((bsz, H_IDX, seq), F32),
                    jax.ShapeDtypeStruct((bsz, H_MLA, 2 * LANES, seq), BF)]
    return pl.pallas_call(
        functools.partial(_proj_misc_kernel, q_lora=q_lora, kv_lora=kv_lora, lanes_are_tokens=spt is not None),
        grid=(t // tm,),
        in_specs=[row(d), full(wm), full(qn), full(kvn), full(wqa), full(wqb),
                  pl.BlockSpec((tm, 2 * LANES), lambda i: (i % n_pos_blocks, 0))],
        out_specs=[row(kv_lora), row(LANES), row(LANES)] + q_specs,
        out_shape=[jax.ShapeDtypeStruct((t, kv_lora), F32),
                   jax.ShapeDtypeStruct((t, LANES), F32),
                   jax.ShapeDtypeStruct((t, LANES), F32)] + q_shapes,
        compiler_params=_cparams(1),
        name="proj_misc",
    )(u, wm, qn, kvn, wqa, wqb, cs)


def _mla_kv_up_kernel(ckv_ref, krp_ref, wk_ref, wv_ref, kcat_ref, vt_ref):
    c = ckv_ref[...].astype(BF)
    kn = _dot(c, wk_ref[...])
    krp = krp_ref[...].astype(BF)
    for h in range(H_MLA):
        kcat_ref[:, 2 * h * LANES:(2 * h + 1) * LANES] = kn[:, h * LANES:(h + 1) * LANES].astype(BF)
        kcat_ref[:, (2 * h + 1) * LANES:(2 * h + 2) * LANES] = krp
    v = _dot(c, wv_ref[...])
    for h in range(H_MLA):
        vt_ref[h, 0] = v[:, h * V_DIM:(h + 1) * V_DIM].T.astype(BF)


def mla_kv_up(ckv, krp, wk, wv, bsz, seq, tm):
    t, c = ckv.shape
    spt = seq // tm
    assert seq % tm == 0
    row = lambda n: pl.BlockSpec((tm, n), lambda i: (i, 0))
    full = lambda a: pl.BlockSpec(a.shape, lambda i: (0, 0))
    return pl.pallas_call(
        _mla_kv_up_kernel,
        grid=(t // tm,),
        in_specs=[row(c), row(LANES), full(wk), full(wv)],
        out_specs=[row(2 * LANES * H_MLA),
                   pl.BlockSpec((None, H_MLA, 1, V_DIM, tm), lambda i: (i // spt, 0, i % spt, 0, 0))],
        out_shape=[jax.ShapeDtypeStruct((t, 2 * LANES * H_MLA), BF),
                   jax.ShapeDtypeStruct((bsz, H_MLA, spt, V_DIM, tm), BF)],
        compiler_params=_cparams(1),
        name="mla_kv_up",
    )(ckv, krp, wk, wv)


def _mla_attn_kernel(qt_ref, k_ref, vt_ref, o_ref, *, tq, tk, q_off, scale, heads):
    q0 = q_off + pl.program_id(2) * tq
    qpos = q0 + lax.broadcasted_iota(jnp.int32, (1, tq), 1)
    limit = (lax.shift_right_arithmetic(qpos, CHUNK_SHIFT) + 1) * CHUNK
    last_limit = ((q0 + tq - 1) // CHUNK + 1) * CHUNK
    n_chunks = (last_limit + tk - 1) // tk

    def body(kc, carry):
        ks = pl.multiple_of(kc * tk, tk)
        kpos = ks + lax.broadcasted_iota(jnp.int32, (tk, tq), 0)
        visible = kpos < limit
        scores = [_dot(k_ref[pl.ds(ks, tk), 2 * h * LANES:2 * (h + 1) * LANES], qt_ref[h]) for h in range(heads)]
        probs, stats = [], []
        for h in range(heads):
            m_prev, l_prev, _ = carry[h]
            s = jnp.where(visible, scores[h] * scale, NEG)
            m_new = jnp.maximum(m_prev, jnp.max(s, axis=0, keepdims=True))
            p = jnp.exp(s - m_new)
            alpha = jnp.exp(m_prev - m_new)
            stats.append((m_new, alpha * l_prev + jnp.sum(p, axis=0, keepdims=True), alpha))
            probs.append(p.astype(BF))
        out = []
        for h in range(heads):
            m_new, l_new, alpha = stats[h]
            out.append((m_new, l_new, alpha * carry[h][2] + _dot(vt_ref[h, kc], probs[h])))
        return tuple(out)

    init = tuple((jnp.full((1, tq), NEG, F32), jnp.zeros((1, tq), F32), jnp.zeros((V_DIM, tq), F32))
                 for _ in range(heads))
    final = lax.fori_loop(0, n_chunks, body, init)
    for h in range(heads):
        _, l_fin, acc = final[h]
        o_ref[:, h * V_DIM:(h + 1) * V_DIM] = (acc / l_fin).T.astype(o_ref.dtype)


def _chunked_transpose(v, heads, tk):
    b, s, hd = v.shape
    d = hd // heads
    return jnp.transpose(v.reshape(b, s // tk, tk, heads, d), (0, 3, 1, 4, 2))


def _heads_to_lanes(a, heads):
    b, s, hd = a.shape
    return jnp.transpose(a.reshape(b, s, heads, hd // heads), (0, 2, 3, 1))


def mla_attention(qt, kcat, vt, q_off, tq, tk, heads=4):
    b, _, _, sq = qt.shape
    sk = kcat.shape[1]
    scale = (NOPE_DIM + ROPE_DIM) ** -0.5
    return pl.pallas_call(
        functools.partial(_mla_attn_kernel, tq=tq, tk=tk, q_off=q_off, scale=scale, heads=heads),
        grid=(b, H_MLA // heads, sq // tq),
        in_specs=[pl.BlockSpec((None, heads, 2 * LANES, tq), lambda bi, h, qi: (bi, h, 0, qi)),
                  pl.BlockSpec((None, sk, heads * 2 * LANES), lambda bi, h, qi: (bi, 0, h)),
                  pl.BlockSpec((None, heads, sk // tk, V_DIM, tk), lambda bi, h, qi: (bi, h, 0, 0, 0))],
        out_specs=pl.BlockSpec((None, tq, heads * V_DIM), lambda bi, h, qi: (bi, qi, h)),
        out_shape=jax.ShapeDtypeStruct((b, sq, H_MLA * V_DIM), BF),
        compiler_params=_cparams(3),
        name="mla_attention",
    )(qt, kcat, vt)


def _split2(x):
    hi = x.astype(BF)
    return hi, (x - hi.astype(F32)).astype(BF)


def _sb_attn_kernel(q_ref, k_ref, v_ref, tri_ref, o_ref, *, tq, tk, q_off, scale, heads):
    q0 = q_off + pl.program_id(2) * tq
    qpos = q0 + lax.broadcasted_iota(jnp.int32, (tq, 1), 0)
    n_chunks = (q0 + tq - 1 + tk - 1) // tk
    tri2 = tri_ref[...]
    n_blk = tk // LANES
    hs = [slice(h * HEAD_DIM, (h + 1) * HEAD_DIM) for h in range(heads)]

    def step(kc, carry, diagonal):
        ks = pl.multiple_of(kc * tk, tk)
        if diagonal:
            strict = ks + lax.broadcasted_iota(jnp.int32, (1, tk), 1) < qpos
            causal = lambda x: jnp.where(strict, x, 0.0)
        else:
            causal = lambda x: x
        z = [_dot_nt(q_ref[:, hs[h]], k_ref[pl.ds(ks, tk), hs[h]]) for h in range(heads)]
        log_beta, parts = [], []
        for h in range(heads):
            zh = z[h] * scale
            sp = jnp.maximum(zh, 0.0) + jnp.log(1.0 + jnp.exp(-jnp.abs(zh)))
            log_beta.append(zh - sp)
            parts.append(_split2(causal(-sp)))
        after, later = [], []
        for h in range(heads):
            hi, lo = parts[h]
            run = carry[h][1]
            blocks = [None] * n_blk
            for blk in reversed(range(n_blk)):
                sl = slice(blk * LANES, (blk + 1) * LANES)
                sums = _dot(jnp.concatenate([hi[:, sl], lo[:, sl]], axis=1), tri2)
                blocks[blk] = sums[:, 0:LANES] + run
                run = run + sums[:, LANES:2 * LANES]
            after.append(jnp.concatenate(blocks, axis=1))
            later.append(run)
        weights = [causal(jnp.exp(log_beta[h] + after[h])).astype(BF) for h in range(heads)]
        return tuple((carry[h][0] + _dot(weights[h], v_ref[pl.ds(ks, tk), hs[h]]), later[h]) for h in range(heads))

    n_below = q0 // tk
    init = tuple((jnp.zeros((tq, HEAD_DIM), F32), jnp.zeros((tq, LANES), F32)) for _ in range(heads))
    state = lax.fori_loop(0, n_chunks - n_below, lambda it, c: step(n_chunks - 1 - it, c, True), init)
    final = lax.fori_loop(0, n_below, lambda it, c: step(n_below - 1 - it, c, False), state)
    for h in range(heads):
        o_ref[:, hs[h]] = final[h][0].astype(o_ref.dtype)


def sb_attention(q, k, v, q_off, tq, tk, heads=4):
    b, sq, _ = q.shape
    sk = k.shape[1]
    assert sk % tk == 0 and tk % LANES == 0 and H_SB % heads == 0
    j = lax.broadcasted_iota(jnp.int32, (LANES, LANES), 0)
    s = lax.broadcasted_iota(jnp.int32, (LANES, LANES), 1)
    tri = jnp.concatenate([(j > s).astype(BF), jnp.ones((LANES, LANES), BF)], axis=1)
    tri = jnp.concatenate([tri, tri], axis=0)
    width = heads * HEAD_DIM
    return pl.pallas_call(
        functools.partial(_sb_attn_kernel, tq=tq, tk=tk, q_off=q_off, scale=HEAD_DIM ** -0.5, heads=heads),
        grid=(b, H_SB // heads, sq // tq),
        in_specs=[pl.BlockSpec((None, tq, width), lambda bi, h, qi: (bi, qi, h)),
                  pl.BlockSpec((None, sk, width), lambda bi, h, qi: (bi, 0, h)),
                  pl.BlockSpec((None, sk, width), lambda bi, h, qi: (bi, 0, h)),
                  pl.BlockSpec((2 * LANES, 2 * LANES), lambda bi, h, qi: (0, 0))],
        out_specs=pl.BlockSpec((None, tq, width), lambda bi, h, qi: (bi, qi, h)),
        out_shape=jax.ShapeDtypeStruct((b, sq, H_SB * HEAD_DIM), BF),
        compiler_params=_cparams(3),
        name="sb_attention",
    )(q, k, v, tri)


def _band_attn_kernel(q_ref, k_ref, v_ref, bm_ref, o_ref, *, tq, win, kpos_base, scale):
    w0 = pl.multiple_of(pl.program_id(1) * tq, tq)
    kpos = kpos_base + w0 + lax.broadcasted_iota(jnp.int32, (1, win), 1)
    exists = kpos >= 0
    hs = [slice(h * HEAD_DIM, (h + 1) * HEAD_DIM) for h in range(H_BAND)]
    scores = [_dot_nt(q_ref[:, hs[h]], k_ref[pl.ds(w0, win), hs[h]]) for h in range(H_BAND)]
    probs, denoms = [], []
    for h in range(H_BAND):
        s = jnp.where(exists, scores[h] * scale + bm_ref[h], NEG)
        p = jnp.exp(s - jnp.max(s, axis=1, keepdims=True))
        denoms.append(jnp.sum(p, axis=1, keepdims=True))
        probs.append(p.astype(BF))
    for h in range(H_BAND):
        o = _dot(probs[h], v_ref[pl.ds(w0, win), hs[h]])
        o_ref[:, hs[h]] = (o / denoms[h]).astype(o_ref.dtype)


def _band_window(tq):
    return -(-(tq + N_PREV_CHUNKS * CHUNK) // LANES) * LANES


def band_bias_mask(rel_bias, tq):
    win = _band_window(tq)
    i = jnp.arange(tq)[:, None]
    j = jnp.arange(win)[None, :]
    rel = jnp.clip(i + N_PREV_CHUNKS * CHUNK - j, -REL_CLIP, REL_CLIP) + REL_CLIP
    ci, cj = i // CHUNK, j // CHUNK
    inside = (cj >= ci) & (cj <= ci + N_PREV_CHUNKS)
    return jnp.where(inside[None], _table_lookup(rel_bias, rel), NEG)


def band_attention(q, k_pad, v_pad, bias_mask, tq, kpos_base):
    b, sq, _ = q.shape
    skp = k_pad.shape[1]
    win = _band_window(tq)
    width = H_BAND * HEAD_DIM
    assert skp >= sq - tq + win
    return pl.pallas_call(
        functools.partial(_band_attn_kernel, tq=tq, win=win, kpos_base=kpos_base, scale=HEAD_DIM ** -0.5),
        grid=(b, sq // tq),
        in_specs=[pl.BlockSpec((None, tq, width), lambda bi, qi: (bi, qi, 0)),
                  pl.BlockSpec((None, skp, width), lambda bi, qi: (bi, 0, 0)),
                  pl.BlockSpec((None, skp, width), lambda bi, qi: (bi, 0, 0)),
                  pl.BlockSpec((H_BAND, tq, win), lambda bi, qi: (0, 0, 0))],
        out_specs=pl.BlockSpec((None, tq, width), lambda bi, qi: (bi, qi, 0)),
        out_shape=jax.ShapeDtypeStruct((b, sq, width), BF),
        compiler_params=_cparams(2),
        name="band_attention",
    )(q, k_pad, v_pad, bias_mask)


def _sortable(x):
    i = lax.bitcast_convert_type(x, jnp.int32)
    return i ^ (lax.shift_right_arithmetic(i, 31) & 0x7FFFFFFF)


def _dsa_kernel(qi2t_ref, wt_ref, klo_ref, khi_ref, qt_ref, k_ref, vt_ref, bnear_ref, bfar_ref, tri_ref,
                o_ref, key_ref, m_ref, l_ref, acc_ref, eqc_ref,
                *, tq, tk, big, q_off, topk, scale, w_scale):
    wide = big * tk
    near_after = -(-tq // tk)
    q0 = q_off + pl.program_id(1) * tq
    qpos = q0 + lax.broadcasted_iota(jnp.int32, (1, tq), 1)
    limit = (lax.shift_right_arithmetic(qpos, CHUNK_SHIFT) + 1) * CHUNK
    last_limit = ((q0 + tq - 1) // CHUNK + 1) * CHUNK
    n_wide = (last_limit + wide - 1) // wide
    diag = q0 // tk

    w = wt_ref[...] * w_scale

    def score_body(c, carry):
        ks = pl.multiple_of(c * wide, wide)
        klo = klo_ref[pl.ds(ks, wide), :]
        khi = khi_ref[pl.ds(ks, wide), :]
        acc = jnp.zeros((wide, tq), F32)
        for pair in range(H_IDX // 2):
            q2 = qi2t_ref[pair]
            acc = acc + w[2 * pair:2 * pair + 1] * jnp.maximum(_dot(klo, q2), 0.0)
            acc = acc + w[2 * pair + 1:2 * pair + 2] * jnp.maximum(_dot(khi, q2), 0.0)
        kpos = ks + lax.broadcasted_iota(jnp.int32, (wide, tq), 0)
        keys = _sortable(jnp.where(kpos < limit, acc + 0.0, -jnp.inf))
        for blk in range(big):
            key_ref[c * big + blk] = keys[blk * tk:(blk + 1) * tk]
        return carry

    lax.fori_loop(0, n_wide, score_body, 0)

    def count_ge(cand):
        def body(c, cnt):
            for blk in range(big):
                cnt = cnt + jnp.where(key_ref[c * big + blk] >= cand, 1.0, 0.0)
            return cnt
        cnt = lax.fori_loop(0, n_wide, body, jnp.zeros((tk, tq), F32))
        return jnp.sum(cnt, axis=0, keepdims=True)

    def bit_body(it, thr):
        cand = thr + lax.shift_left(jnp.int32(1), 31 - it)
        return jnp.where(count_ge(cand) >= topk, cand, thr)

    thr = lax.fori_loop(0, 32, bit_body, jnp.full((1, tq), INT_MIN, jnp.int32))
    n_above = count_ge(thr + 1)
    n_ties_kept = topk - n_above

    m_ref[...] = jnp.full(m_ref.shape, NEG, F32)
    l_ref[...] = jnp.zeros(l_ref.shape, F32)
    acc_ref[...] = jnp.zeros(acc_ref.shape, F32)
    eqc_ref[...] = jnp.zeros(eqc_ref.shape, F32)
    tri = tri_ref[...]
    hs = [slice(h * HEAD_DIM, (h + 1) * HEAD_DIM) for h in range(H_DSA)]

    def attend(first_blk, n_blk, bias_of_head, below_tile):
        width = n_blk * tk
        ks = pl.multiple_of(first_blk * tk, tk)
        key = jnp.concatenate([key_ref[first_blk + i] for i in range(n_blk)], axis=0)
        eq = key == thr
        eq_bf = jnp.where(eq, 1.0, 0.0).astype(BF)
        seen = eqc_ref[0:1, :]
        rank = [None] * n_blk
        for i in range(n_blk):
            counts = _dot(tri, eq_bf[i * tk:(i + 1) * tk])
            rank[i] = counts[0:tk] + seen
            seen = seen + counts[tk:tk + 1]
        eqc_ref[0:1, :] = seen
        sel = (key > thr) | (eq & (jnp.concatenate(rank, axis=0) <= n_ties_kept))
        if not below_tile:
            sel = sel & (ks + lax.broadcasted_iota(jnp.int32, (width, tq), 0) < limit)
        scores = [_dot(k_ref[pl.ds(ks, width), hs[h]], qt_ref[h]) for h in range(H_DSA)]
        probs, alphas = [], []
        for h in range(H_DSA):
            s = jnp.where(sel, scores[h] * scale + bias_of_head(h), NEG)
            m_prev = m_ref[h]
            m_new = jnp.maximum(m_prev, jnp.max(s, axis=0, keepdims=True))
            p = jnp.where(sel, jnp.exp(s - m_new), 0.0)
            alpha = jnp.exp(m_prev - m_new)
            l_ref[h] = alpha * l_ref[h] + jnp.sum(p, axis=0, keepdims=True)
            m_ref[h] = m_new
            probs.append(p.astype(BF))
            alphas.append(alpha)
        for h in range(H_DSA):
            vt = jnp.concatenate([vt_ref[h, first_blk + i] for i in range(n_blk)], axis=1)
            acc_ref[h] = alphas[h] * acc_ref[h] + _dot(vt, probs[h])

    far_bias = lambda h: bfar_ref[h][:, 0:1]
    n_far_wide = jnp.maximum(q0 - tk, 0) // wide

    def far_wide_body(c, carry):
        attend(c * big, big, far_bias, True)
        return carry

    lax.fori_loop(0, n_far_wide, far_wide_body, 0)

    def far_body(kc, carry):
        attend(kc, 1, far_bias, True)
        return carry

    lax.fori_loop(n_far_wide * big, jnp.maximum(diag - 1, n_far_wide * big), far_body, 0)

    @pl.when(diag >= 1)
    def _():
        attend(diag - 1, 1, lambda h: bnear_ref[0, h], True)

    for d in range(near_after):
        attend(diag + d, 1, lambda h, d=d: bnear_ref[d + 1, h], False)

    for h in range(H_DSA):
        o_ref[:, hs[h]] = (acc_ref[h] / l_ref[h]).T.astype(o_ref.dtype)


def t5_bucket(rel):
    half = T5_BUCKETS // 2
    max_exact = half // 2
    n = jnp.abs(rel)
    nf = jnp.maximum(n, 1).astype(F32)
    large = max_exact + (jnp.log(nf / max_exact) / math.log(T5_MAX_DIST / max_exact)
                         * (half - max_exact)).astype(jnp.int32)
    large = jnp.minimum(large, half - 1)
    return jnp.where(rel > 0, half, 0) + jnp.where(n < max_exact, n, large)


def _table_lookup(table, idx):
    onehot = jax.nn.one_hot(idx, table.shape[0], dtype=F32)
    out = jnp.einsum("...n,nh->...h", onehot, table.astype(F32), precision=lax.Precision.HIGHEST)
    return jnp.moveaxis(out, -1, 0)


def dsa_bias_tables(t5_table, tq, tk=LANES):
    j = jnp.arange(tk)[:, None]
    i = jnp.arange(tq)[None, :]
    near = jnp.stack([_table_lookup(t5_table, t5_bucket(d * tk + j - i)) for d in range(-1, -(-tq // tk))],
                     axis=0)
    far_rel = -jnp.ones((1, tk), jnp.int32) * (2 * tk)
    far = _table_lookup(t5_table, t5_bucket(far_rel))
    return near, far


def dsa_attention(qi2t, wt, klo, khi, qt, k, vt, bias_tables, q_off, n_keys, tq, big, tk=LANES):
    b, _, _, sq = qt.shape
    sk = k.shape[1]
    assert tk >= T5_MAX_DIST and q_off % tk == 0 and (tq % tk == 0 or sq == tq) and sk % (big * tk) == 0
    topk = min(TOPK_MAX, n_keys // 4)
    near, far = bias_tables
    ss = lax.broadcasted_iota(jnp.int32, (tk + 16, tk), 0)
    jj = lax.broadcasted_iota(jnp.int32, (tk + 16, tk), 1)
    tri = ((jj <= ss) | (ss >= tk)).astype(BF)
    whole = lambda n: pl.BlockSpec((None, sk, n), lambda bi, qi: (bi, 0, 0))
    heads_t = lambda h, d: pl.BlockSpec((None, h, d, tq), lambda bi, qi: (bi, 0, 0, qi))
    const = lambda a: pl.BlockSpec(a.shape, lambda bi, qi: (0,) * a.ndim)
    return pl.pallas_call(
        functools.partial(_dsa_kernel, tq=tq, tk=tk, big=big, q_off=q_off, topk=topk,
                          scale=HEAD_DIM ** -0.5, w_scale=H_IDX ** -0.5 * D_IDX ** -0.5),
        grid=(b, sq // tq),
        in_specs=[heads_t(H_IDX // 2, LANES), pl.BlockSpec((None, H_IDX, tq), lambda bi, qi: (bi, 0, qi)),
                  whole(LANES), whole(LANES), heads_t(H_DSA, HEAD_DIM), whole(H_DSA * HEAD_DIM),
                  pl.BlockSpec((None, H_DSA, sk // tk, HEAD_DIM, tk), lambda bi, qi: (bi, 0, 0, 0, 0)),
                  const(near), const(far), const(tri)],
        out_specs=pl.BlockSpec((None, tq, H_DSA * HEAD_DIM), lambda bi, qi: (bi, qi, 0)),
        out_shape=jax.ShapeDtypeStruct((b, sq, H_DSA * HEAD_DIM), BF),
        scratch_shapes=[pltpu.VMEM((sk // tk, tk, tq), jnp.int32),
                        pltpu.VMEM((H_DSA, 1, tq), F32), pltpu.VMEM((H_DSA, 1, tq), F32),
                        pltpu.VMEM((H_DSA, HEAD_DIM, tq), F32), pltpu.VMEM((8, tq), F32)],
        compiler_params=_cparams(2),
        name="dsa_attention",
    )(qi2t, wt, klo, khi, qt, k, vt, near, far, tri)


def _pad_cols(a, n):
    return jnp.pad(a, ((0, 0), (0, n - a.shape[1])))


def _rotate_half_cols(w):
    half = w.shape[1] // 2
    return jnp.concatenate([-w[:, half:], w[:, :half]], axis=1)


def prepare_layer_weights(w_in, q_norm, w_uq, kv_norm, w_ukv, w_branch):
    d = w_in.shape[0]
    q_lora, kv_lora = q_norm.shape[0], kv_norm.shape[0]
    sizes = (q_lora, kv_lora, ROPE_DIM, 3 * H_SB * HEAD_DIM, 3 * H_BAND * HEAD_DIM, 3 * H_DSA * HEAD_DIM,
             H_IDX * D_IDX, D_IDX, H_IDX, N_BRANCH * d)
    cols, start = [], 0
    for s in sizes:
        cols.append(w_in[:, start:start + s])
        start += s
    w_cq, w_ckv, w_kr, w_sb, w_bd, w_ds, w_iq, w_ik, w_iw, w_g = cols
    wm = jnp.concatenate([w_cq, w_ckv, _pad_cols(w_kr, LANES), _pad_cols(_rotate_half_cols(w_kr), LANES),
                          _pad_cols(w_ik, LANES), _pad_cols(w_iw, LANES)], axis=1).astype(BF)
    qa, qb = [], []
    hd = NOPE_DIM + ROPE_DIM
    for h in range(H_MLA):
        wh = w_uq[:, h * hd:(h + 1) * hd]
        qa += [wh[:, :NOPE_DIM], _pad_cols(wh[:, NOPE_DIM:], LANES)]
        qb.append(_pad_cols(_rotate_half_cols(wh[:, NOPE_DIM:]), LANES))
    wqa = jnp.concatenate(qa, axis=1).astype(BF)
    wqb = jnp.concatenate(qb, axis=1).astype(BF)
    kvd = NOPE_DIM + V_DIM
    wk = jnp.concatenate([w_ukv[:, h * kvd:h * kvd + NOPE_DIM] for h in range(H_MLA)], axis=1).astype(BF)
    wv = jnp.concatenate([w_ukv[:, h * kvd + NOPE_DIM:(h + 1) * kvd] for h in range(H_MLA)], axis=1).astype(BF)
    wg = jnp.transpose(w_g.reshape(d, N_BRANCH, d), (1, 0, 2)).astype(BF)
    return dict(wm=wm, wqa=wqa, wqb=wqb, wk=wk, wv=wv,
                w_sb=w_sb.astype(BF), w_bd=w_bd.astype(BF), w_ds=w_ds.astype(BF), w_iq=w_iq.astype(BF),
                wg=wg, wb=w_branch.astype(BF),
                qn=q_norm.reshape(1, -1), kvn=kv_norm.reshape(1, -1))


def rope_table(pos):
    half = ROPE_DIM // 2
    inv = ROPE_THETA ** (-jnp.arange(half, dtype=F32) / half)
    ang = pos.astype(F32)[:, None] * inv[None, :]
    cos, sin = jnp.cos(ang), jnp.sin(ang)
    z = jnp.zeros((pos.shape[0], LANES - ROPE_DIM), F32)
    return jnp.concatenate([cos, cos, z, sin, sin, z], axis=1)


def _kidx_pair(kidx):
    kb = kidx.astype(BF)
    return (jnp.pad(kb, ((0, 0), (0, 0), (0, LANES - D_IDX))),
            jnp.pad(kb, ((0, 0), (0, 0), (LANES - D_IDX, 0))))


def _with_past(past, new, pad_to):
    b = new.shape[0]
    a = jnp.concatenate([past.reshape(b, past.shape[1], -1).astype(BF), new.astype(BF)], axis=1)
    return jnp.pad(a, ((0, 0), (0, pad_to - a.shape[1]), (0, 0)))


def mixing_block(u, bsz, seq, lw, cs, past, band_mask, dsa_tables, w_out, h, stacks):
    t = bsz * seq
    ckv, krp, kidxp, idxw, qcat = proj_misc(u, bsz, seq, lw["wm"], lw["qn"], lw["kvn"], lw["wqa"], lw["wqb"], cs)
    hw = H_SB * HEAD_DIM

    layer, depth, prev = stacks
    prev = prev if prev is not None else (None,) * 4

    def stacked(w, name, prev_stack, other=("rows", BF)):
        return project(u, w, bsz, seq, [("stack_heads", F32), other], stack=(layer, depth, prev_stack), name=name)

    w_sb, w_bd = lw["w_sb"], lw["w_bd"]
    sb_q, = matmul(u, w_sb[:, :hw], (BF,), tm=1024, name="proj_q")
    sb_k, sb_kb = stacked(w_sb[:, hw:2 * hw], "proj_k", prev[0])
    sb_v, sb_vb = stacked(w_sb[:, 2 * hw:], "proj_v", prev[1])
    bd_q, = matmul(u, w_bd[:, :hw], (BF,), tm=1024, name="proj_q")
    bd_k, bd_kb = matmul(u, w_bd[:, hw:2 * hw], (F32, BF), tm=1024, name="proj_k")
    bd_v, bd_vb = matmul(u, w_bd[:, 2 * hw:], (F32, BF), tm=1024, name="proj_v")
    kr = krp[:, :ROPE_DIM]
    kidx = kidxp[:, :D_IDX]
    r3 = lambda a: a.reshape(bsz, seq, -1)
    band_pad = N_PREV_CHUNKS * CHUNK

    if past is None:
        w_ds = lw["w_ds"]
        ds_qt, = project(u, w_ds[:, :hw], bsz, seq, [("heads_t", BF)], name="proj_q")
        ds_k, ds_kb = stacked(w_ds[:, hw:2 * hw], "proj_k", prev[2])
        ds_v, ds_vt = stacked(w_ds[:, 2 * hw:], "proj_v", prev[3], ("blocks_t", BF))
        idx_qt, = project(u, lw["w_iq"], bsz, seq, [("heads_t", BF)], name="proj_idxq")
        kcat, vmla_t = mla_kv_up(ckv, krp, lw["wk"], lw["wv"], bsz, seq, PROMPT_TK)
        o_mla = mla_attention(qcat, r3(kcat), vmla_t, 0, PROMPT_TQ, PROMPT_TK)
        o_sb = sb_attention(r3(sb_q), r3(sb_kb), r3(sb_vb), 0, PROMPT_TQ, PROMPT_TK)
        front = ((0, 0), (band_pad, 0), (0, 0))
        o_bd = band_attention(r3(bd_q), jnp.pad(r3(bd_kb), front), jnp.pad(r3(bd_vb), front),
                              band_mask, PROMPT_TQ, -band_pad)
        klo, khi = _kidx_pair(r3(kidx))
        o_ds = dsa_attention(idx_qt, idxw, klo, khi, ds_qt, r3(ds_kb), ds_vt, dsa_tables,
                             0, seq, PROMPT_TQ, PROMPT_TK // LANES)
    else:
        w_ds = lw["w_ds"]
        ds_q, = matmul(u, w_ds[:, :hw], (BF,), tm=1024, name="proj_q")
        ds_k, ds_kb = stacked(w_ds[:, hw:2 * hw], "proj_k", prev[2])
        ds_v, ds_vb = stacked(w_ds[:, 2 * hw:], "proj_v", prev[3])
        idx_q, = matmul(u, lw["w_iq"], (BF,), tm=1024, name="proj_idxq")
        (p_ckv, p_kr, p_sbk, p_sbv, p_bdk, p_bdv, p_dsk, p_dsv, p_kidx) = past
        past_len = p_sbk.shape[1]
        total = past_len + seq
        pad_to = -(-total // SAMPLE_TK) * SAMPLE_TK
        ckv_all = jnp.concatenate([p_ckv, r3(ckv)], axis=1)
        krp_all = jnp.concatenate([jnp.pad(p_kr, ((0, 0), (0, 0), (0, LANES - ROPE_DIM))), r3(krp)], axis=1)
        rows = pad_to - total
        ckv_all = jnp.pad(ckv_all, ((0, 0), (0, rows), (0, 0))).reshape(bsz * pad_to, -1)
        krp_all = jnp.pad(krp_all, ((0, 0), (0, rows), (0, 0))).reshape(bsz * pad_to, -1)
        kcat, vmla_t = mla_kv_up(ckv_all, krp_all, lw["wk"], lw["wv"], bsz, pad_to, SAMPLE_TK)
        o_mla = mla_attention(_heads_to_lanes(r3(qcat), H_MLA), kcat.reshape(bsz, pad_to, -1), vmla_t,
                              past_len, seq, SAMPLE_TK)
        o_sb = sb_attention(r3(sb_q), _with_past(p_sbk, r3(sb_kb), pad_to), _with_past(p_sbv, r3(sb_vb), pad_to),
                            past_len, seq, SAMPLE_TK)
        band_len = _band_window(seq)
        o_bd = band_attention(r3(bd_q), _with_past(p_bdk, r3(bd_kb), band_len), _with_past(p_bdv, r3(bd_vb), band_len),
                              band_mask, seq, past_len - p_bdk.shape[1])
        kidx_all = jnp.pad(jnp.concatenate([p_kidx, r3(kidx)], axis=1), ((0, 0), (0, rows), (0, 0)))
        klo, khi = _kidx_pair(kidx_all)
        o_ds = dsa_attention(_heads_to_lanes(r3(idx_q), H_IDX // 2),
                             jnp.transpose(r3(idxw)[:, :, :H_IDX], (0, 2, 1)), klo, khi,
                             _heads_to_lanes(r3(ds_q), H_DSA), _with_past(p_dsk, r3(ds_kb), pad_to),
                             _chunked_transpose(_with_past(p_dsv, r3(ds_vb), pad_to), H_DSA, LANES),
                             dsa_tables, past_len, total, seq, SAMPLE_TK // LANES)

    f2 = lambda a: a.reshape(t, -1)
    merged = gate_merge(u, [f2(o_mla), f2(o_sb), f2(o_bd), f2(o_ds)], lw["wg"], lw["wb"])
    h_new, = matmul(merged, w_out, (F32,), residual=h, name="mix_out")
    return h_new, (ckv, kr, bd_k, bd_v, kidx), (sb_k, sb_v, ds_k, ds_v)


def dense_ffn(h, un, w1, w3, w2):
    act = swiglu_up(un, w1, w3)
    out, = matmul(act, w2, (F32,), residual=h, name="ffn_down")
    return out


def moe_ffn(h, un, router_pad, w1, w3, w2):
    n_experts = w1.shape[0]
    gate = router_gate(un, router_pad, n_experts)
    for e in range(n_experts):
        act = swiglu_up(un, w1[e], w3[e])
        h, = matmul(act, w2[e], (F32,), residual=h, gate=gate, gate_col=e, name="moe_down")
    return h


def layer_step(h, p, bsz, seq, cs, past, lw, band_mask, dsa_tables, fw, stacks):
    u = rmsnorm(h, fw["g_mix"], BF)
    h, rows, head_stacks = mixing_block(u, bsz, seq, lw, cs, past, band_mask, dsa_tables, fw["w_out"], h, stacks)
    un = rmsnorm(h, fw["g_ffn"], BF)
    if fw["moe"]:
        h = moe_ffn(h, un, fw["router"], fw["w1"], fw["w3"], fw["w2"])
    else:
        h = dense_ffn(h, un, fw["w1"], fw["w3"], fw["w2"])
    un = rmsnorm(h, fw["g_ple"], BF)
    h = ple_update(h, p, un, fw["ple_w"], fw["ple_gate_w"])
    return h, rows, head_stacks


def kernel(x_prompt, x_sample, p_prompt, p_sample, cache_mla_ckv, cache_mla_krope, cache_sb_k, cache_sb_v, cache_band_k, cache_band_v, cache_dsa_k, cache_dsa_v, cache_dsa_kidx, norm_mix, w_in, mla_q_norm, mla_w_uq, mla_kv_norm, mla_w_ukv, band_rel_bias, t5_rel_bias, w_branch, w_out, norm_ffn, ffn_w1, ffn_w3, ffn_w2, moe_router, moe_w1, moe_w3, moe_w2, norm_ple, ple_w, ple_gate_w, norm_final):
    depth = w_in.shape[0]
    bp, sp, d = x_prompt.shape
    bs, ss, _ = x_sample.shape
    past_len = cache_sb_k.shape[2]
    tm = 512
    cs_p = rope_table(jnp.arange(sp))
    cs_s = rope_table(past_len + (jnp.arange(tm) % ss))
    dsa_tables_p = dsa_bias_tables(t5_rel_bias, PROMPT_TQ)
    dsa_tables_s = dsa_bias_tables(t5_rel_bias, ss)
    hp = x_prompt.reshape(bp * sp, d)
    hs = x_sample.reshape(bs * ss, d)
    rows_p, rows_s = [], []
    stacks_p = stacks_s = None
    for i in range(depth):
        lw = prepare_layer_weights(w_in[i], mla_q_norm[i], mla_w_uq[i], mla_kv_norm[i], mla_w_ukv[i], w_branch[i])
        j = i // 2
        fw = dict(g_mix=norm_mix[i], g_ffn=norm_ffn[i], g_ple=norm_ple[i], w_out=w_out[i].astype(BF),
                  ple_w=ple_w[i].astype(BF), ple_gate_w=ple_gate_w[i].astype(BF), moe=i % 2 == 1)
        if i % 2 == 0:
            fw.update(w1=ffn_w1[j].astype(BF), w3=ffn_w3[j].astype(BF), w2=ffn_w2[j].astype(BF))
        else:
            fw.update(router=_pad_cols(moe_router[j], LANES).astype(BF),
                      w1=moe_w1[j].astype(BF), w3=moe_w3[j].astype(BF), w2=moe_w2[j].astype(BF))
        past_i = (cache_mla_ckv[i], cache_mla_krope[i], cache_sb_k[i], cache_sb_v[i], cache_band_k[i],
                  cache_band_v[i], cache_dsa_k[i], cache_dsa_v[i], cache_dsa_kidx[i])
        hp, rp, stacks_p = layer_step(hp, p_prompt[i].reshape(bp * sp, -1).astype(BF), bp, sp, cs_p, None,
                                      lw, band_bias_mask(band_rel_bias[i], PROMPT_TQ), dsa_tables_p, fw,
                                      (i, depth, stacks_p))
        hs, rs, stacks_s = layer_step(hs, p_sample[i].reshape(bs * ss, -1).astype(BF), bs, ss, cs_s, past_i,
                                      lw, band_bias_mask(band_rel_bias[i], ss), dsa_tables_s, fw,
                                      (i, depth, stacks_s))
        rows_p.append(rp)
        rows_s.append(rs)
    y_prompt = rmsnorm(hp, norm_final, F32).reshape(bp, sp, d)
    y_sample = rmsnorm(hs, norm_final, F32).reshape(bs, ss, d)

    keep = min(N_PREV_CHUNKS * CHUNK, sp)

    def stacked(rows, n, bsz, seq, heads=None, tail=None):
        out = []
        for r in rows:
            a = r[n].reshape(bsz, seq, -1)
            if tail is not None:
                a = a[:, seq - tail:]
            if heads is not None:
                a = a.reshape(a.shape[0], a.shape[1], heads, HEAD_DIM)
            out.append(a)
        return jnp.stack(out, axis=0)

    def both(n, heads=None, prompt_tail=None):
        return [stacked(rows_p, n, bp, sp, heads, prompt_tail), stacked(rows_s, n, bs, ss, heads)]

    def from_stacks(n):
        return [stacks_p[n].reshape(depth, bp, sp, -1, HEAD_DIM), stacks_s[n].reshape(depth, bs, ss, -1, HEAD_DIM)]

    res = [y_prompt, y_sample] + both(0) + both(1) + from_stacks(0) + from_stacks(1)
    res += both(2, H_BAND, keep) + both(3, H_BAND, keep) + from_stacks(2) + from_stacks(3) + both(4)
    return tuple(res)
```

```python
import functools
import math

import jax
import jax.numpy as jnp
from jax import lax
from jax.experimental import pallas as pl
from jax.experimental.pallas import tpu as pltpu

BF = jnp.bfloat16
F32 = jnp.float32

CHUNK = 64
CHUNK_SHIFT = 6
HEAD_DIM = 128
N_BRANCH = 4
H_MLA = 4
NOPE_DIM = 128
ROPE_DIM = 64
V_DIM = 128
ROPE_THETA = 10000.0
H_SB = 4
H_BAND = 4
N_PREV_CHUNKS = 8
REL_CLIP = 128
H_DSA = 4
H_IDX = 16
D_IDX = 64
TOPK_MAX = 256
T5_BUCKETS = 32
T5_MAX_DIST = 128
TOP_K_EXPERTS = 2
EPS = 1e-6

LANES = 128
PROMPT_TQ = 256
PROMPT_TK = 512
SAMPLE_TK = 384
VMEM_LIMIT = 56 * 1024 * 1024
NEG = -1e30
INT_MIN = -2147483648


def _cparams(n_axes):
    return pltpu.CompilerParams(dimension_semantics=("arbitrary",) * n_axes,
                                vmem_limit_bytes=VMEM_LIMIT)


def _tile(n, preferred):
    if n <= preferred:
        return n
    t = preferred - preferred % LANES
    while n % t:
        t -= LANES
    assert t > 0
    return t


def _dot(a, b):
    return jnp.dot(a, b, preferred_element_type=F32)


def _dot_nt(a, b):
    return lax.dot_general(a, b, (((1,), (1,)), ((), ())), preferred_element_type=F32)


def _sigmoid(x):
    return 1.0 / (1.0 + jnp.exp(-x))


def _rms(x, g):
    return x * lax.rsqrt(jnp.mean(x * x, axis=-1, keepdims=True) + EPS) * g


def _rmsnorm_kernel(x_ref, g_ref, o_ref):
    o_ref[...] = _rms(x_ref[...], g_ref[...]).astype(o_ref.dtype)


def rmsnorm(x, g, out_dtype, tm=512):
    t, d = x.shape
    tm = _tile(t, tm)
    return pl.pallas_call(
        _rmsnorm_kernel,
        grid=(t // tm,),
        in_specs=[pl.BlockSpec((tm, d), lambda i: (i, 0)),
                  pl.BlockSpec((1, d), lambda i: (0, 0))],
        out_specs=pl.BlockSpec((tm, d), lambda i: (i, 0)),
        out_shape=jax.ShapeDtypeStruct((t, d), out_dtype),
        compiler_params=_cparams(1),
        name="rmsnorm",
    )(x, g.reshape(1, d))


def _mm_kernel(*refs, has_res, gate_col, n_out):
    a_ref, w_ref = refs[0], refs[1]
    pos = 2
    res_ref = gate_ref = None
    if has_res:
        res_ref = refs[pos]
        pos += 1
    if gate_col is not None:
        gate_ref = refs[pos]
        pos += 1
    r = _dot(a_ref[...], w_ref[...])
    if gate_ref is not None:
        r = r * gate_ref[:, gate_col:gate_col + 1]
    if res_ref is not None:
        r = res_ref[...] + r
    for o_ref in refs[pos:pos + n_out]:
        o_ref[...] = r.astype(o_ref.dtype)


WEIGHT_BLOCK_BYTES = 6 * 1024 * 1024


def matmul(a, w, out_dtypes, residual=None, gate=None, gate_col=None, tm=512, name="mm"):
    m, k = a.shape
    n = w.shape[1]
    tm, tn = _tile(m, tm), _tile(n, WEIGHT_BLOCK_BYTES // (2 * k))
    in_specs = [pl.BlockSpec((tm, k), lambda j, i: (i, 0)),
                pl.BlockSpec((k, tn), lambda j, i: (0, j))]
    args = [a, w]
    if residual is not None:
        in_specs.append(pl.BlockSpec((tm, tn), lambda j, i: (i, j)))
        args.append(residual)
    if gate is not None:
        in_specs.append(pl.BlockSpec((tm, gate.shape[1]), lambda j, i: (i, 0)))
        args.append(gate)
    outs = pl.pallas_call(
        functools.partial(_mm_kernel, has_res=residual is not None,
                          gate_col=gate_col if gate is not None else None, n_out=len(out_dtypes)),
        grid=(n // tn, m // tm),
        in_specs=in_specs,
        out_specs=[pl.BlockSpec((tm, tn), lambda j, i: (i, j)) for _ in out_dtypes],
        out_shape=[jax.ShapeDtypeStruct((m, n), dt) for dt in out_dtypes],
        compiler_params=_cparams(2),
        name=name,
    )(*args)
    return outs


def _project_kernel(a_ref, w_ref, *refs, kinds, d, blk, n_prev, first_layer):
    o_refs = refs[n_prev:]
    r = _dot(a_ref[...], w_ref[...])
    tm, n = r.shape
    for o_ref, kind in zip(o_refs, kinds):
        if kind == "rows":
            o_ref[...] = r.astype(o_ref.dtype)
        elif kind == "stack_heads":
            slab = o_ref.at[0] if first_layer else o_ref
            for h in range(n // d):
                slab[:, h, :] = r[:, h * d:(h + 1) * d].astype(o_ref.dtype)
            if first_layer:
                for layer in range(1, o_ref.shape[0]):
                    o_ref[layer] = jnp.zeros(o_ref.shape[1:], o_ref.dtype)
        elif kind == "heads_t":
            for h in range(n // d):
                o_ref[h] = r[:, h * d:(h + 1) * d].T.astype(o_ref.dtype)
        else:
            for h in range(n // d):
                for j in range(tm // blk):
                    o_ref[h, j] = r[j * blk:(j + 1) * blk, h * d:(h + 1) * d].T.astype(o_ref.dtype)


def project(a, w, bsz, seq, outs, stack=None, d=LANES, blk=LANES, tm=1024, name="project"):
    m, k = a.shape
    n = w.shape[1]
    tm = _tile(m, tm)
    spt = _seq_tiles(seq, tm)
    kinds = tuple(kind for kind, _ in outs)
    assert n % d == 0 and 2 * k * n <= 2 * WEIGHT_BLOCK_BYTES
    assert (spt is not None and tm % blk == 0) or not {"heads_t", "blocks_t"} & set(kinds)
    layer, depth, prev = stack if stack is not None else (0, 1, None)
    specs, shapes, aliases, prev_args = [], [], {}, []
    for kind, dt in outs:
        if kind == "rows":
            specs.append(pl.BlockSpec((tm, n), lambda i: (i, 0)))
            shapes.append(jax.ShapeDtypeStruct((m, n), dt))
        elif kind == "stack_heads":
            if layer == 0:
                specs.append(pl.BlockSpec((depth, tm, n // d, d), lambda i: (0, i, 0, 0)))
            else:
                specs.append(pl.BlockSpec((None, tm, n // d, d), lambda i: (layer, i, 0, 0)))
                aliases[2 + len(prev_args)] = len(shapes)
                prev_args.append(prev)
            shapes.append(jax.ShapeDtypeStruct((depth, m, n // d, d), dt))
        elif kind == "heads_t":
            specs.append(pl.BlockSpec((None, n // d, d, tm), lambda i: (i // spt, 0, 0, i % spt)))
            shapes.append(jax.ShapeDtypeStruct((bsz, n // d, d, seq), dt))
        else:
            specs.append(pl.BlockSpec((None, n // d, tm // blk, d, blk), lambda i: (i // spt, 0, i % spt, 0, 0)))
            shapes.append(jax.ShapeDtypeStruct((bsz, n // d, seq // blk, d, blk), dt))
    return pl.pallas_call(
        functools.partial(_project_kernel, kinds=kinds, d=d, blk=blk, n_prev=len(prev_args),
                          first_layer=layer == 0),
        grid=(m // tm,),
        in_specs=[pl.BlockSpec((tm, k), lambda i: (i, 0)), pl.BlockSpec((k, n), lambda i: (0, 0))]
        + [pl.BlockSpec(memory_space=pl.ANY) for _ in prev_args],
        out_specs=specs,
        out_shape=shapes,
        input_output_aliases=aliases,
        compiler_params=_cparams(1),
        name=name,
    )(a, w, *prev_args)


def _swiglu_up_kernel(a_ref, w1_ref, w3_ref, o_ref):
    a = a_ref[...]
    x1 = _dot(a, w1_ref[...])
    x3 = _dot(a, w3_ref[...])
    o_ref[...] = (x1 * _sigmoid(x1) * x3).astype(o_ref.dtype)


def swiglu_up(a, w1, w3, tm=512):
    m, k = a.shape
    n = w1.shape[1]
    tm, tn = _tile(m, tm), _tile(n, WEIGHT_BLOCK_BYTES // (2 * k))
    return pl.pallas_call(
        _swiglu_up_kernel,
        grid=(n // tn, m // tm),
        in_specs=[pl.BlockSpec((tm, k), lambda j, i: (i, 0)),
                  pl.BlockSpec((k, tn), lambda j, i: (0, j)),
                  pl.BlockSpec((k, tn), lambda j, i: (0, j))],
        out_specs=pl.BlockSpec((tm, tn), lambda j, i: (i, j)),
        out_shape=jax.ShapeDtypeStruct((m, n), BF),
        compiler_params=_cparams(2),
        name="swiglu_up",
    )(a, w1, w3)


def _gate_merge_kernel(u_ref, o0_ref, o1_ref, o2_ref, o3_ref, wg_ref, wb_ref, out_ref):
    u = u_ref[...]
    acc = None
    for b, o_ref in enumerate((o0_ref, o1_ref, o2_ref, o3_ref)):
        t = _sigmoid(_dot(u, wg_ref[b])) * _dot(o_ref[...], wb_ref[b])
        acc = t if acc is None else acc + t
    out_ref[...] = acc.astype(out_ref.dtype)


def gate_merge(u, branches, wg, wb, tm=512, tn=512):
    t, d = u.shape
    bw = branches[0].shape[1]
    n = wg.shape[2]
    tm = min(tm, t)
    return pl.pallas_call(
        _gate_merge_kernel,
        grid=(n // tn, t // tm),
        in_specs=[pl.BlockSpec((tm, d), lambda j, i: (i, 0))]
        + [pl.BlockSpec((tm, bw), lambda j, i: (i, 0)) for _ in range(N_BRANCH)]
        + [pl.BlockSpec((N_BRANCH, d, tn), lambda j, i: (0, 0, j)),
           pl.BlockSpec((N_BRANCH, bw, tn), lambda j, i: (0, 0, j))],
        out_specs=pl.BlockSpec((tm, tn), lambda j, i: (i, j)),
        out_shape=jax.ShapeDtypeStruct((t, n), BF),
        compiler_params=_cparams(2),
        name="gate_merge",
    )(u, *branches, wg, wb)


def _ple_kernel(h_ref, p_ref, un_ref, wp_ref, wg_ref, g_ref, *o_refs):
    h = h_ref[...] + _dot(p_ref[...], wp_ref[...]) * _sigmoid(_dot(un_ref[...], wg_ref[...]))
    if len(o_refs) == 2:
        o_refs[0][...] = h
    o_refs[-1][...] = _rms(h, g_ref[...]).astype(o_refs[-1].dtype)


def ple_update(h, p, un, wp, wg, g_next, next_dtype, keep_h, tm=512):
    t, d = h.shape
    tm = min(tm, t)
    row = lambda n: pl.BlockSpec((tm, n), lambda i: (i, 0))
    full = lambda a: pl.BlockSpec(a.shape, lambda i: (0, 0))
    g_next = g_next.reshape(1, d)
    return pl.pallas_call(
        _ple_kernel,
        grid=(t // tm,),
        in_specs=[row(d), row(p.shape[1]), row(d), full(wp), full(wg), full(g_next)],
        out_specs=([row(d)] if keep_h else []) + [row(d)],
        out_shape=([jax.ShapeDtypeStruct((t, d), F32)] if keep_h else []) + [jax.ShapeDtypeStruct((t, d), next_dtype)],
        compiler_params=_cparams(1),
        name="ple_update",
    )(h, p, un, wp, wg, g_next)


def _mm_norm_kernel(a_ref, w_ref, res_ref, g_ref, h_ref, n_ref):
    h = res_ref[...] + _dot(a_ref[...], w_ref[...])
    h_ref[...] = h
    n_ref[...] = _rms(h, g_ref[...]).astype(n_ref.dtype)


def matmul_norm(a, w, residual, g, tm=512, name="mm_norm"):
    m, k = a.shape
    n = w.shape[1]
    tm = _tile(m, tm)
    row = lambda c: pl.BlockSpec((tm, c), lambda i: (i, 0))
    return pl.pallas_call(
        _mm_norm_kernel,
        grid=(m // tm,),
        in_specs=[row(k), pl.BlockSpec((k, n), lambda i: (0, 0)), row(n), pl.BlockSpec((1, n), lambda i: (0, 0))],
        out_specs=[row(n), row(n)],
        out_shape=[jax.ShapeDtypeStruct((m, n), F32), jax.ShapeDtypeStruct((m, n), BF)],
        compiler_params=_cparams(1),
        name=name,
    )(a, w, residual, g.reshape(1, n))


def _router_kernel(un_ref, wr_ref, g_ref, *, n_experts):
    logits = _dot(un_ref[...], wr_ref[...])
    lane = lax.broadcasted_iota(jnp.int32, logits.shape, 1).astype(F32)
    real = lane < n_experts
    logits = jnp.where(real, logits, NEG)
    e = jnp.where(real, jnp.exp(logits - jnp.max(logits, axis=1, keepdims=True)), 0.0)
    probs = e / jnp.sum(e, axis=1, keepdims=True)
    p1 = jnp.max(probs, axis=1, keepdims=True)
    i1 = jnp.min(jnp.where(probs == p1, lane, float(LANES)), axis=1, keepdims=True)
    first = lane == i1
    rest = jnp.where(first | ~real, -1.0, probs)
    p2 = jnp.max(rest, axis=1, keepdims=True)
    i2 = jnp.min(jnp.where(rest == p2, lane, float(LANES)), axis=1, keepdims=True)
    second = lane == i2
    denom = p1 + p2
    g_ref[...] = jnp.where(first, p1 / denom, 0.0) + jnp.where(second, p2 / denom, 0.0)


def router_gate(un, wr_pad, n_experts, tm=512):
    t, d = un.shape
    tm = min(tm, t)
    return pl.pallas_call(
        functools.partial(_router_kernel, n_experts=n_experts),
        grid=(t // tm,),
        in_specs=[pl.BlockSpec((tm, d), lambda i: (i, 0)),
                  pl.BlockSpec((d, LANES), lambda i: (0, 0))],
        out_specs=pl.BlockSpec((tm, LANES), lambda i: (i, 0)),
        out_shape=jax.ShapeDtypeStruct((t, LANES), F32),
        compiler_params=_cparams(1),
        name="router_gate",
    )(un, wr_pad)


Q_LORA_OFF = 0


def _proj_misc_kernel(u_ref, wm_ref, qn_ref, kvn_ref, wqa_ref, wqb_ref, cs_ref,
                      ckv_ref, krp_ref, kidx_ref, idxw_ref, qcat_ref, *, q_lora, kv_lora, lanes_are_tokens):
    x = _dot(u_ref[...], wm_ref[...])
    cos = cs_ref[:, 0:LANES]
    sin = cs_ref[:, LANES:2 * LANES]
    o = q_lora
    ckv_ref[...] = _rms(x[:, o:o + kv_lora], kvn_ref[...])
    o += kv_lora
    krp_ref[...] = x[:, o:o + LANES] * cos + x[:, o + LANES:o + 2 * LANES] * sin
    o += 2 * LANES
    kidx_ref[...] = x[:, o:o + LANES]
    idxw = x[:, o + LANES:o + 2 * LANES]
    if lanes_are_tokens:
        idxw_ref[...] = idxw.T[0:H_IDX]
    else:
        idxw_ref[...] = idxw
    cqn = _rms(x[:, 0:q_lora], qn_ref[...]).astype(BF)
    qa = _dot(cqn, wqa_ref[...])
    qb = _dot(cqn, wqb_ref[...])
    for h in range(H_MLA):
        lo = 2 * h * LANES
        nope = qa[:, lo:lo + LANES]
        rope = qa[:, lo + LANES:lo + 2 * LANES] * cos + qb[:, h * LANES:(h + 1) * LANES] * sin
        if lanes_are_tokens:
            qcat_ref[h, 0:LANES] = nope.T.astype(BF)
            qcat_ref[h, LANES:2 * LANES] = rope.T.astype(BF)
        else:
            qcat_ref[:, lo:lo + LANES] = nope.astype(BF)
            qcat_ref[:, lo + LANES:lo + 2 * LANES] = rope.astype(BF)


def _seq_tiles(seq, tm):
    return seq // tm if seq % tm == 0 else None


def proj_misc(u, bsz, seq, wm, qn, kvn, wqa, wqb, cs, tm=512):
    t, d = u.shape
    tm = min(tm, t)
    q_lora, kv_lora = qn.shape[1], kvn.shape[1]
    n_pos_blocks = cs.shape[0] // tm
    spt = _seq_tiles(seq, tm)
    row = lambda n: pl.BlockSpec((tm, n), lambda i: (i, 0))
    full = lambda a: pl.BlockSpec(a.shape, lambda i: (0, 0))
    if spt is None:
        q_specs = [row(LANES), row(2 * LANES * H_MLA)]
        q_shapes = [jax.ShapeDtypeStruct((t, LANES), F32), jax.ShapeDtypeStruct((t, 2 * LANES * H_MLA), BF)]
    else:
        q_specs = [pl.BlockSpec((None, H_IDX, tm), lambda i: (i // spt, 0, i % spt)),
                   pl.BlockSpec((None, H_MLA, 2 * LANES, tm), lambda i: (i // spt, 0, 0, i % spt))]
        q_shapes = [jax.ShapeDtypeStruct((bsz, H_IDX, seq), F32),
                    jax.ShapeDtypeStruct((bsz, H_MLA, 2 * LANES, seq), BF)]
    return pl.pallas_call(
        functools.partial(_proj_misc_kernel, q_lora=q_lora, kv_lora=kv_lora, lanes_are_tokens=spt is not None),
        grid=(t // tm,),
        in_specs=[row(d), full(wm), full(qn), full(kvn), full(wqa), full(wqb),
                  pl.BlockSpec((tm, 2 * LANES), lambda i: (i % n_pos_blocks, 0))],
        out_specs=[row(kv_lora), row(LANES), row(LANES)] + q_specs,
        out_shape=[jax.ShapeDtypeStruct((t, kv_lora), F32),
                   jax.ShapeDtypeStruct((t, LANES), F32),
                   jax.ShapeDtypeStruct((t, LANES), F32)] + q_shapes,
        compiler_params=_cparams(1),
        name="proj_misc",
    )(u, wm, qn, kvn, wqa, wqb, cs)


def _mla_kv_up_kernel(ckv_ref, krp_ref, wk_ref, wv_ref, kcat_ref, vt_ref):
    c = ckv_ref[...].astype(BF)
    kn = _dot(c, wk_ref[...])
    krp = krp_ref[...].astype(BF)
    for h in range(H_MLA):
        kcat_ref[:, 2 * h * LANES:(2 * h + 1) * LANES] = kn[:, h * LANES:(h + 1) * LANES].astype(BF)
        kcat_ref[:, (2 * h + 1) * LANES:(2 * h + 2) * LANES] = krp
    v = _dot(c, wv_ref[...])
    for h in range(H_MLA):
        vt_ref[h, 0] = v[:, h * V_DIM:(h + 1) * V_DIM].T.astype(BF)


def mla_kv_up(ckv, krp, wk, wv, bsz, seq, tm):
    t, c = ckv.shape
    spt = seq // tm
    assert seq % tm == 0
    row = lambda n: pl.BlockSpec((tm, n), lambda i: (i, 0))
    full = lambda a: pl.BlockSpec(a.shape, lambda i: (0, 0))
    return pl.pallas_call(
        _mla_kv_up_kernel,
        grid=(t // tm,),
        in_specs=[row(c), row(LANES), full(wk), full(wv)],
        out_specs=[row(2 * LANES * H_MLA),
                   pl.BlockSpec((None, H_MLA, 1, V_DIM, tm), lambda i: (i // spt, 0, i % spt, 0, 0))],
        out_shape=[jax.ShapeDtypeStruct((t, 2 * LANES * H_MLA), BF),
                   jax.ShapeDtypeStruct((bsz, H_MLA, spt, V_DIM, tm), BF)],
        compiler_params=_cparams(1),
        name="mla_kv_up",
    )(ckv, krp, wk, wv)


def _mla_attn_kernel(qt_ref, k_ref, vt_ref, o_ref, *, tq, tk, q_off, scale, heads):
    q0 = q_off + pl.program_id(2) * tq
    qpos = q0 + lax.broadcasted_iota(jnp.int32, (1, tq), 1)
    limit = (lax.shift_right_arithmetic(qpos, CHUNK_SHIFT) + 1) * CHUNK
    last_limit = ((q0 + tq - 1) // CHUNK + 1) * CHUNK
    n_chunks = (last_limit + tk - 1) // tk

    def body(kc, carry):
        ks = pl.multiple_of(kc * tk, tk)
        kpos = ks + lax.broadcasted_iota(jnp.int32, (tk, tq), 0)
        visible = kpos < limit
        scores = [_dot(k_ref[pl.ds(ks, tk), 2 * h * LANES:2 * (h + 1) * LANES], qt_ref[h]) for h in range(heads)]
        probs, stats = [], []
        for h in range(heads):
            m_prev, l_prev, _ = carry[h]
            s = jnp.where(visible, scores[h] * scale, NEG)
            m_new = jnp.maximum(m_prev, jnp.max(s, axis=0, keepdims=True))
            p = jnp.exp(s - m_new)
            alpha = jnp.exp(m_prev - m_new)
            stats.append((m_new, alpha * l_prev + jnp.sum(p, axis=0, keepdims=True), alpha))
            probs.append(p.astype(BF))
        out = []
        for h in range(heads):
            m_new, l_new, alpha = stats[h]
            out.append((m_new, l_new, alpha * carry[h][2] + _dot(vt_ref[h, kc], probs[h])))
        return tuple(out)

    init = tuple((jnp.full((1, tq), NEG, F32), jnp.zeros((1, tq), F32), jnp.zeros((V_DIM, tq), F32))
                 for _ in range(heads))
    final = lax.fori_loop(0, n_chunks, body, init)
    for h in range(heads):
        _, l_fin, acc = final[h]
        o_ref[:, h * V_DIM:(h + 1) * V_DIM] = (acc / l_fin).T.astype(o_ref.dtype)


def _chunked_transpose(v, heads, tk):
    b, s, hd = v.shape
    d = hd // heads
    return jnp.transpose(v.reshape(b, s // tk, tk, heads, d), (0, 3, 1, 4, 2))


def _heads_to_lanes(a, heads):
    b, s, hd = a.shape
    return jnp.transpose(a.reshape(b, s, heads, hd // heads), (0, 2, 3, 1))


def mla_attention(qt, kcat, vt, q_off, tq, tk, heads=4):
    b, _, _, sq = qt.shape
    sk = kcat.shape[1]
    scale = (NOPE_DIM + ROPE_DIM) ** -0.5
    return pl.pallas_call(
        functools.partial(_mla_attn_kernel, tq=tq, tk=tk, q_off=q_off, scale=scale, heads=heads),
        grid=(b, H_MLA // heads, sq // tq),
        in_specs=[pl.BlockSpec((None, heads, 2 * LANES, tq), lambda bi, h, qi: (bi, h, 0, qi)),
                  pl.BlockSpec((None, sk, heads * 2 * LANES), lambda bi, h, qi: (bi, 0, h)),
                  pl.BlockSpec((None, heads, sk // tk, V_DIM, tk), lambda bi, h, qi: (bi, h, 0, 0, 0))],
        out_specs=pl.BlockSpec((None, tq, heads * V_DIM), lambda bi, h, qi: (bi, qi, h)),
        out_shape=jax.ShapeDtypeStruct((b, sq, H_MLA * V_DIM), BF),
        compiler_params=_cparams(3),
        name="mla_attention",
    )(qt, kcat, vt)


def _split2(x):
    hi = x.astype(BF)
    return hi, (x - hi.astype(F32)).astype(BF)


def _sb_attn_kernel(q_ref, k_ref, v_ref, tri_ref, o_ref, *, tq, tk, q_off, scale, heads):
    q0 = q_off + pl.program_id(2) * tq
    qpos = q0 + lax.broadcasted_iota(jnp.int32, (tq, 1), 0)
    n_chunks = (q0 + tq - 1 + tk - 1) // tk
    tri2 = tri_ref[...]
    n_blk = tk // LANES
    hs = [slice(h * HEAD_DIM, (h + 1) * HEAD_DIM) for h in range(heads)]

    def step(kc, carry, diagonal):
        ks = pl.multiple_of(kc * tk, tk)
        if diagonal:
            strict = ks + lax.broadcasted_iota(jnp.int32, (1, tk), 1) < qpos
            causal = lambda x: jnp.where(strict, x, 0.0)
        else:
            causal = lambda x: x
        z = [_dot_nt(q_ref[:, hs[h]], k_ref[pl.ds(ks, tk), hs[h]]) for h in range(heads)]
        log_beta, parts = [], []
        for h in range(heads):
            zh = z[h] * scale
            sp = jnp.maximum(zh, 0.0) + jnp.log(1.0 + jnp.exp(-jnp.abs(zh)))
            log_beta.append(zh - sp)
            parts.append(_split2(causal(-sp)))
        after, later = [], []
        for h in range(heads):
            hi, lo = parts[h]
            run = carry[h][1]
            blocks = [None] * n_blk
            for blk in reversed(range(n_blk)):
                sl = slice(blk * LANES, (blk + 1) * LANES)
                sums = _dot(jnp.concatenate([hi[:, sl], lo[:, sl]], axis=1), tri2)
                blocks[blk] = sums[:, 0:LANES] + run
                run = run + sums[:, LANES:2 * LANES]
            after.append(jnp.concatenate(blocks, axis=1))
            later.append(run)
        weights = [causal(jnp.exp(log_beta[h] + after[h])).astype(BF) for h in range(heads)]
        return tuple((carry[h][0] + _dot(weights[h], v_ref[pl.ds(ks, tk), hs[h]]), later[h]) for h in range(heads))

    n_below = q0 // tk
    init = tuple((jnp.zeros((tq, HEAD_DIM), F32), jnp.zeros((tq, LANES), F32)) for _ in range(heads))
    state = lax.fori_loop(0, n_chunks - n_below, lambda it, c: step(n_chunks - 1 - it, c, True), init)
    final = lax.fori_loop(0, n_below, lambda it, c: step(n_below - 1 - it, c, False), state)
    for h in range(heads):
        o_ref[:, hs[h]] = final[h][0].astype(o_ref.dtype)


def sb_attention(q, k, v, q_off, tq, tk, heads=4):
    b, sq, _ = q.shape
    sk = k.shape[1]
    assert sk % tk == 0 and tk % LANES == 0 and H_SB % heads == 0
    j = lax.broadcasted_iota(jnp.int32, (LANES, LANES), 0)
    s = lax.broadcasted_iota(jnp.int32, (LANES, LANES), 1)
    tri = jnp.concatenate([(j > s).astype(BF), jnp.ones((LANES, LANES), BF)], axis=1)
    tri = jnp.concatenate([tri, tri], axis=0)
    width = heads * HEAD_DIM
    return pl.pallas_call(
        functools.partial(_sb_attn_kernel, tq=tq, tk=tk, q_off=q_off, scale=HEAD_DIM ** -0.5, heads=heads),
        grid=(b, H_SB // heads, sq // tq),
        in_specs=[pl.BlockSpec((None, tq, width), lambda bi, h, qi: (bi, qi, h)),
                  pl.BlockSpec((None, sk, width), lambda bi, h, qi: (bi, 0, h)),
                  pl.BlockSpec((None, sk, width), lambda bi, h, qi: (bi, 0, h)),
                  pl.BlockSpec((2 * LANES, 2 * LANES), lambda bi, h, qi: (0, 0))],
        out_specs=pl.BlockSpec((None, tq, width), lambda bi, h, qi: (bi, qi, h)),
        out_shape=jax.ShapeDtypeStruct((b, sq, H_SB * HEAD_DIM), BF),
        compiler_params=_cparams(3),
        name="sb_attention",
    )(q, k, v, tri)


def _band_attn_kernel(q_ref, k_ref, v_ref, bm_ref, o_ref, *, tq, win, kpos_base, scale):
    w0 = pl.multiple_of(pl.program_id(1) * tq, tq)
    kpos = kpos_base + w0 + lax.broadcasted_iota(jnp.int32, (1, win), 1)
    exists = kpos >= 0
    hs = [slice(h * HEAD_DIM, (h + 1) * HEAD_DIM) for h in range(H_BAND)]
    scores = [_dot_nt(q_ref[:, hs[h]], k_ref[pl.ds(w0, win), hs[h]]) for h in range(H_BAND)]
    probs, denoms = [], []
    for h in range(H_BAND):
        s = jnp.where(exists, scores[h] * scale + bm_ref[h], NEG)
        p = jnp.exp(s - jnp.max(s, axis=1, keepdims=True))
        denoms.append(jnp.sum(p, axis=1, keepdims=True))
        probs.append(p.astype(BF))
    for h in range(H_BAND):
        o = _dot(probs[h], v_ref[pl.ds(w0, win), hs[h]])
        o_ref[:, hs[h]] = (o / denoms[h]).astype(o_ref.dtype)


def _band_window(tq):
    return -(-(tq + N_PREV_CHUNKS * CHUNK) // LANES) * LANES


def band_bias_mask(rel_bias, tq):
    win = _band_window(tq)
    i = jnp.arange(tq)[:, None]
    j = jnp.arange(win)[None, :]
    rel = jnp.clip(i + N_PREV_CHUNKS * CHUNK - j, -REL_CLIP, REL_CLIP) + REL_CLIP
    ci, cj = i // CHUNK, j // CHUNK
    inside = (cj >= ci) & (cj <= ci + N_PREV_CHUNKS)
    return jnp.where(inside[None], _table_lookup(rel_bias, rel), NEG)


def band_attention(q, k_pad, v_pad, bias_mask, tq, kpos_base):
    b, sq, _ = q.shape
    skp = k_pad.shape[1]
    win = _band_window(tq)
    width = H_BAND * HEAD_DIM
    assert skp >= sq - tq + win
    return pl.pallas_call(
        functools.partial(_band_attn_kernel, tq=tq, win=win, kpos_base=kpos_base, scale=HEAD_DIM ** -0.5),
        grid=(b, sq // tq),
        in_specs=[pl.BlockSpec((None, tq, width), lambda bi, qi: (bi, qi, 0)),
                  pl.BlockSpec((None, skp, width), lambda bi, qi: (bi, 0, 0)),
                  pl.BlockSpec((None, skp, width), lambda bi, qi: (bi, 0, 0)),
                  pl.BlockSpec((H_BAND, tq, win), lambda bi, qi: (0, 0, 0))],
        out_specs=pl.BlockSpec((None, tq, width), lambda bi, qi: (bi, qi, 0)),
        out_shape=jax.ShapeDtypeStruct((b, sq, width), BF),
        compiler_params=_cparams(2),
        name="band_attention",
    )(q, k_pad, v_pad, bias_mask)


def _sortable(x):
    i = lax.bitcast_convert_type(x, jnp.int32)
    return i ^ (lax.shift_right_arithmetic(i, 31) & 0x7FFFFFFF)


def _dsa_kernel(qi2t_ref, wt_ref, klo_ref, khi_ref, qt_ref, k_ref, vt_ref, bnear_ref, bfar_ref, tri_ref,
                o_ref, key_ref, m_ref, l_ref, acc_ref, eqc_ref,
                *, tq, tk, big, q_off, topk, scale, w_scale):
    wide = big * tk
    near_after = -(-tq // tk)
    q0 = q_off + pl.program_id(1) * tq
    qpos = q0 + lax.broadcasted_iota(jnp.int32, (1, tq), 1)
    limit = (lax.shift_right_arithmetic(qpos, CHUNK_SHIFT) + 1) * CHUNK
    last_limit = ((q0 + tq - 1) // CHUNK + 1) * CHUNK
    n_wide = (last_limit + wide - 1) // wide
    diag = q0 // tk

    w = wt_ref[...] * w_scale

    def score_body(c, carry):
        ks = pl.multiple_of(c * wide, wide)
        klo = klo_ref[pl.ds(ks, wide), :]
        khi = khi_ref[pl.ds(ks, wide), :]
        acc = jnp.zeros((wide, tq), F32)
        for pair in range(H_IDX // 2):
            q2 = qi2t_ref[pair]
            acc = acc + w[2 * pair:2 * pair + 1] * jnp.maximum(_dot(klo, q2), 0.0)
            acc = acc + w[2 * pair + 1:2 * pair + 2] * jnp.maximum(_dot(khi, q2), 0.0)
        kpos = ks + lax.broadcasted_iota(jnp.int32, (wide, tq), 0)
        keys = _sortable(jnp.where(kpos < limit, acc + 0.0, -jnp.inf))
        for blk in range(big):
            key_ref[c * big + blk] = keys[blk * tk:(blk + 1) * tk]
        return carry

    lax.fori_loop(0, n_wide, score_body, 0)

    def count_ge(cand):
        def body(c, cnt):
            for blk in range(big):
                cnt = cnt + jnp.where(key_ref[c * big + blk] >= cand, 1.0, 0.0)
            return cnt
        cnt = lax.fori_loop(0, n_wide, body, jnp.zeros((tk, tq), F32))
        return jnp.sum(cnt, axis=0, keepdims=True)

    def bit_body(it, thr):
        cand = thr + lax.shift_left(jnp.int32(1), 31 - it)
        return jnp.where(count_ge(cand) >= topk, cand, thr)

    thr = lax.fori_loop(0, 32, bit_body, jnp.full((1, tq), INT_MIN, jnp.int32))
    n_above = count_ge(thr + 1)
    n_ties_kept = topk - n_above

    m_ref[...] = jnp.full(m_ref.shape, NEG, F32)
    l_ref[...] = jnp.zeros(l_ref.shape, F32)
    acc_ref[...] = jnp.zeros(acc_ref.shape, F32)
    eqc_ref[...] = jnp.zeros(eqc_ref.shape, F32)
    tri = tri_ref[...]
    hs = [slice(h * HEAD_DIM, (h + 1) * HEAD_DIM) for h in range(H_DSA)]

    def attend(first_blk, n_blk, bias_of_head, below_tile):
        width = n_blk * tk
        ks = pl.multiple_of(first_blk * tk, tk)
        key = jnp.concatenate([key_ref[first_blk + i] for i in range(n_blk)], axis=0)
        eq = key == thr
        eq_bf = jnp.where(eq, 1.0, 0.0).astype(BF)
        seen = eqc_ref[0:1, :]
        rank = [None] * n_blk
        for i in range(n_blk):
            counts = _dot(tri, eq_bf[i * tk:(i + 1) * tk])
            rank[i] = counts[0:tk] + seen
            seen = seen + counts[tk:tk + 1]
        eqc_ref[0:1, :] = seen
        sel = (key > thr) | (eq & (jnp.concatenate(rank, axis=0) <= n_ties_kept))
        if not below_tile:
            sel = sel & (ks + lax.broadcasted_iota(jnp.int32, (width, tq), 0) < limit)
        scores = [_dot(k_ref[pl.ds(ks, width), hs[h]], qt_ref[h]) for h in range(H_DSA)]
        probs, alphas = [], []
        for h in range(H_DSA):
            s = jnp.where(sel, scores[h] * scale + bias_of_head(h), NEG)
            m_prev = m_ref[h]
            m_new = jnp.maximum(m_prev, jnp.max(s, axis=0, keepdims=True))
            p = jnp.where(sel, jnp.exp(s - m_new), 0.0)
            alpha = jnp.exp(m_prev - m_new)
            l_ref[h] = alpha * l_ref[h] + jnp.sum(p, axis=0, keepdims=True)
            m_ref[h] = m_new
            probs.append(p.astype(BF))
            alphas.append(alpha)
        for h in range(H_DSA):
            vt = jnp.concatenate([vt_ref[h, first_blk + i] for i in range(n_blk)], axis=1)
            acc_ref[h] = alphas[h] * acc_ref[h] + _dot(vt, probs[h])

    far_bias = lambda h: bfar_ref[h][:, 0:1]
    n_far_wide = jnp.maximum(q0 - tk, 0) // wide

    def far_wide_body(c, carry):
        attend(c * big, big, far_bias, True)
        return carry

    lax.fori_loop(0, n_far_wide, far_wide_body, 0)

    def far_body(kc, carry):
        attend(kc, 1, far_bias, True)
        return carry

    lax.fori_loop(n_far_wide * big, jnp.maximum(diag - 1, n_far_wide * big), far_body, 0)

    @pl.when(diag >= 1)
    def _():
        attend(diag - 1, 1, lambda h: bnear_ref[0, h], True)

    for d in range(near_after):
        attend(diag + d, 1, lambda h, d=d: bnear_ref[d + 1, h], False)

    for h in range(H_DSA):
        o_ref[:, hs[h]] = (acc_ref[h] / l_ref[h]).T.astype(o_ref.dtype)


def t5_bucket(rel):
    half = T5_BUCKETS // 2
    max_exact = half // 2
    n = jnp.abs(rel)
    nf = jnp.maximum(n, 1).astype(F32)
    large = max_exact + (jnp.log(nf / max_exact) / math.log(T5_MAX_DIST / max_exact)
                         * (half - max_exact)).astype(jnp.int32)
    large = jnp.minimum(large, half - 1)
    return jnp.where(rel > 0, half, 0) + jnp.where(n < max_exact, n, large)


def _table_lookup(table, idx):
    onehot = jax.nn.one_hot(idx, table.shape[0], dtype=F32)
    out = jnp.einsum("...n,nh->...h", onehot, table.astype(F32), precision=lax.Precision.HIGHEST)
    return jnp.moveaxis(out, -1, 0)


def dsa_bias_tables(t5_table, tq, tk=LANES):
    j = jnp.arange(tk)[:, None]
    i = jnp.arange(tq)[None, :]
    near = jnp.stack([_table_lookup(t5_table, t5_bucket(d * tk + j - i)) for d in range(-1, -(-tq // tk))],
                     axis=0)
    far_rel = -jnp.ones((1, tk), jnp.int32) * (2 * tk)
    far = _table_lookup(t5_table, t5_bucket(far_rel))
    return near, far


def dsa_attention(qi2t, wt, klo, khi, qt, k, vt, bias_tables, q_off, n_keys, tq, big, tk=LANES):
    b, _, _, sq = qt.shape
    sk = k.shape[1]
    assert tk >= T5_MAX_DIST and q_off % tk == 0 and (tq % tk == 0 or sq == tq) and sk % (big * tk) == 0
    topk = min(TOPK_MAX, n_keys // 4)
    near, far = bias_tables
    ss = lax.broadcasted_iota(jnp.int32, (tk + 16, tk), 0)
    jj = lax.broadcasted_iota(jnp.int32, (tk + 16, tk), 1)
    tri = ((jj <= ss) | (ss >= tk)).astype(BF)
    whole = lambda n: pl.BlockSpec((None, sk, n), lambda bi, qi: (bi, 0, 0))
    heads_t = lambda h, d: pl.BlockSpec((None, h, d, tq), lambda bi, qi: (bi, 0, 0, qi))
    const = lambda a: pl.BlockSpec(a.shape, lambda bi, qi: (0,) * a.ndim)
    return pl.pallas_call(
        functools.partial(_dsa_kernel, tq=tq, tk=tk, big=big, q_off=q_off, topk=topk,
                          scale=HEAD_DIM ** -0.5, w_scale=H_IDX ** -0.5 * D_IDX ** -0.5),
        grid=(b, sq // tq),
        in_specs=[heads_t(H_IDX // 2, LANES), pl.BlockSpec((None, H_IDX, tq), lambda bi, qi: (bi, 0, qi)),
                  whole(LANES), whole(LANES), heads_t(H_DSA, HEAD_DIM), whole(H_DSA * HEAD_DIM),
                  pl.BlockSpec((None, H_DSA, sk // tk, HEAD_DIM, tk), lambda bi, qi: (bi, 0, 0, 0, 0)),
                  const(near), const(far), const(tri)],
        out_specs=pl.BlockSpec((None, tq, H_DSA * HEAD_DIM), lambda bi, qi: (bi, qi, 0)),
        out_shape=jax.ShapeDtypeStruct((b, sq, H_DSA * HEAD_DIM), BF),
        scratch_shapes=[pltpu.VMEM((sk // tk, tk, tq), jnp.int32),
                        pltpu.VMEM((H_DSA, 1, tq), F32), pltpu.VMEM((H_DSA, 1, tq), F32),
                        pltpu.VMEM((H_DSA, HEAD_DIM, tq), F32), pltpu.VMEM((8, tq), F32)],
        compiler_params=_cparams(2),
        name="dsa_attention",
    )(qi2t, wt, klo, khi, qt, k, vt, near, far, tri)


def _pad_cols(a, n):
    return jnp.pad(a, ((0, 0), (0, n - a.shape[1])))


def _rotate_half_cols(w):
    half = w.shape[1] // 2
    return jnp.concatenate([-w[:, half:], w[:, :half]], axis=1)


def prepare_layer_weights(w_in, q_norm, w_uq, kv_norm, w_ukv, w_branch):
    d = w_in.shape[0]
    q_lora, kv_lora = q_norm.shape[0], kv_norm.shape[0]
    sizes = (q_lora, kv_lora, ROPE_DIM, 3 * H_SB * HEAD_DIM, 3 * H_BAND * HEAD_DIM, 3 * H_DSA * HEAD_DIM,
             H_IDX * D_IDX, D_IDX, H_IDX, N_BRANCH * d)
    cols, start = [], 0
    for s in sizes:
        cols.append(w_in[:, start:start + s])
        start += s
    w_cq, w_ckv, w_kr, w_sb, w_bd, w_ds, w_iq, w_ik, w_iw, w_g = cols
    wm = jnp.concatenate([w_cq, w_ckv, _pad_cols(w_kr, LANES), _pad_cols(_rotate_half_cols(w_kr), LANES),
                          _pad_cols(w_ik, LANES), _pad_cols(w_iw, LANES)], axis=1).astype(BF)
    qa, qb = [], []
    hd = NOPE_DIM + ROPE_DIM
    for h in range(H_MLA):
        wh = w_uq[:, h * hd:(h + 1) * hd]
        qa += [wh[:, :NOPE_DIM], _pad_cols(wh[:, NOPE_DIM:], LANES)]
        qb.append(_pad_cols(_rotate_half_cols(wh[:, NOPE_DIM:]), LANES))
    wqa = jnp.concatenate(qa, axis=1).astype(BF)
    wqb = jnp.concatenate(qb, axis=1).astype(BF)
    kvd = NOPE_DIM + V_DIM
    wk = jnp.concatenate([w_ukv[:, h * kvd:h * kvd + NOPE_DIM] for h in range(H_MLA)], axis=1).astype(BF)
    wv = jnp.concatenate([w_ukv[:, h * kvd + NOPE_DIM:(h + 1) * kvd] for h in range(H_MLA)], axis=1).astype(BF)
    wg = jnp.transpose(w_g.reshape(d, N_BRANCH, d), (1, 0, 2)).astype(BF)
    return dict(wm=wm, wqa=wqa, wqb=wqb, wk=wk, wv=wv,
                w_sb=w_sb.astype(BF), w_bd=w_bd.astype(BF), w_ds=w_ds.astype(BF), w_iq=w_iq.astype(BF),
                wg=wg, wb=w_branch.astype(BF),
                qn=q_norm.reshape(1, -1), kvn=kv_norm.reshape(1, -1))


def rope_table(pos):
    half = ROPE_DIM // 2
    inv = ROPE_THETA ** (-jnp.arange(half, dtype=F32) / half)
    ang = pos.astype(F32)[:, None] * inv[None, :]
    cos, sin = jnp.cos(ang), jnp.sin(ang)
    z = jnp.zeros((pos.shape[0], LANES - ROPE_DIM), F32)
    return jnp.concatenate([cos, cos, z, sin, sin, z], axis=1)


def _kidx_pair(kidx):
    kb = kidx.astype(BF)
    return (jnp.pad(kb, ((0, 0), (0, 0), (0, LANES - D_IDX))),
            jnp.pad(kb, ((0, 0), (0, 0), (LANES - D_IDX, 0))))


def _with_past(past, new, pad_to):
    b = new.shape[0]
    a = jnp.concatenate([past.reshape(b, past.shape[1], -1).astype(BF), new.astype(BF)], axis=1)
    return jnp.pad(a, ((0, 0), (0, pad_to - a.shape[1]), (0, 0)))


def mixing_block(u, bsz, seq, lw, cs, past, band_mask, dsa_tables, w_out, h, stacks, g_ffn):
    t = bsz * seq
    ckv, krp, kidxp, idxw, qcat = proj_misc(u, bsz, seq, lw["wm"], lw["qn"], lw["kvn"], lw["wqa"], lw["wqb"], cs)
    hw = H_SB * HEAD_DIM

    layer, depth, prev = stacks
    prev = prev if prev is not None else (None,) * 4

    def stacked(w, name, prev_stack, other=("rows", BF)):
        return project(u, w, bsz, seq, [("stack_heads", F32), other], stack=(layer, depth, prev_stack), name=name)

    w_sb, w_bd = lw["w_sb"], lw["w_bd"]
    sb_q, = matmul(u, w_sb[:, :hw], (BF,), tm=1024, name="proj_q")
    sb_k, sb_kb = stacked(w_sb[:, hw:2 * hw], "proj_k", prev[0])
    sb_v, sb_vb = stacked(w_sb[:, 2 * hw:], "proj_v", prev[1])
    bd_q, = matmul(u, w_bd[:, :hw], (BF,), tm=1024, name="proj_q")
    bd_k, bd_kb = matmul(u, w_bd[:, hw:2 * hw], (F32, BF), tm=1024, name="proj_k")
    bd_v, bd_vb = matmul(u, w_bd[:, 2 * hw:], (F32, BF), tm=1024, name="proj_v")
    kr = krp[:, :ROPE_DIM]
    kidx = kidxp[:, :D_IDX]
    r3 = lambda a: a.reshape(bsz, seq, -1)
    band_pad = N_PREV_CHUNKS * CHUNK

    if past is None:
        w_ds = lw["w_ds"]
        ds_qt, = project(u, w_ds[:, :hw], bsz, seq, [("heads_t", BF)], name="proj_q")
        ds_k, ds_kb = stacked(w_ds[:, hw:2 * hw], "proj_k", prev[2])
        ds_v, ds_vt = stacked(w_ds[:, 2 * hw:], "proj_v", prev[3], ("blocks_t", BF))
        idx_qt, = project(u, lw["w_iq"], bsz, seq, [("heads_t", BF)], name="proj_idxq")
        kcat, vmla_t = mla_kv_up(ckv, krp, lw["wk"], lw["wv"], bsz, seq, PROMPT_TK)
        o_mla = mla_attention(qcat, r3(kcat), vmla_t, 0, PROMPT_TQ, PROMPT_TK)
        o_sb = sb_attention(r3(sb_q), r3(sb_kb), r3(sb_vb), 0, PROMPT_TQ, PROMPT_TK)
        front = ((0, 0), (band_pad, 0), (0, 0))
        o_bd = band_attention(r3(bd_q), jnp.pad(r3(bd_kb), front), jnp.pad(r3(bd_vb), front),
                              band_mask, PROMPT_TQ, -band_pad)
        klo, khi = _kidx_pair(r3(kidx))
        o_ds = dsa_attention(idx_qt, idxw, klo, khi, ds_qt, r3(ds_kb), ds_vt, dsa_tables,
                             0, seq, PROMPT_TQ, PROMPT_TK // LANES)
    else:
        w_ds = lw["w_ds"]
        ds_q, = matmul(u, w_ds[:, :hw], (BF,), tm=1024, name="proj_q")
        ds_k, ds_kb = stacked(w_ds[:, hw:2 * hw], "proj_k", prev[2])
        ds_v, ds_vb = stacked(w_ds[:, 2 * hw:], "proj_v", prev[3])
        idx_q, = matmul(u, lw["w_iq"], (BF,), tm=1024, name="proj_idxq")
        (p_ckv, p_kr, p_sbk, p_sbv, p_bdk, p_bdv, p_dsk, p_dsv, p_kidx) = past
        past_len = p_sbk.shape[1]
        total = past_len + seq
        pad_to = -(-total // SAMPLE_TK) * SAMPLE_TK
        ckv_all = jnp.concatenate([p_ckv, r3(ckv)], axis=1)
        krp_all = jnp.concatenate([jnp.pad(p_kr, ((0, 0), (0, 0), (0, LANES - ROPE_DIM))), r3(krp)], axis=1)
        rows = pad_to - total
        ckv_all = jnp.pad(ckv_all, ((0, 0), (0, rows), (0, 0))).reshape(bsz * pad_to, -1)
        krp_all = jnp.pad(krp_all, ((0, 0), (0, rows), (0, 0))).reshape(bsz * pad_to, -1)
        kcat, vmla_t = mla_kv_up(ckv_all, krp_all, lw["wk"], lw["wv"], bsz, pad_to, SAMPLE_TK)
        o_mla = mla_attention(_heads_to_lanes(r3(qcat), H_MLA), kcat.reshape(bsz, pad_to, -1), vmla_t,
                              past_len, seq, SAMPLE_TK)
        o_sb = sb_attention(r3(sb_q), _with_past(p_sbk, r3(sb_kb), pad_to), _with_past(p_sbv, r3(sb_vb), pad_to),
                            past_len, seq, SAMPLE_TK)
        band_len = _band_window(seq)
        o_bd = band_attention(r3(bd_q), _with_past(p_bdk, r3(bd_kb), band_len), _with_past(p_bdv, r3(bd_vb), band_len),
                              band_mask, seq, past_len - p_bdk.shape[1])
        kidx_all = jnp.pad(jnp.concatenate([p_kidx, r3(kidx)], axis=1), ((0, 0), (0, rows), (0, 0)))
        klo, khi = _kidx_pair(kidx_all)
        o_ds = dsa_attention(_heads_to_lanes(r3(idx_q), H_IDX // 2),
                             jnp.transpose(r3(idxw)[:, :, :H_IDX], (0, 2, 1)), klo, khi,
                             _heads_to_lanes(r3(ds_q), H_DSA), _with_past(p_dsk, r3(ds_kb), pad_to),
                             _chunked_transpose(_with_past(p_dsv, r3(ds_vb), pad_to), H_DSA, LANES),
                             dsa_tables, past_len, total, seq, SAMPLE_TK // LANES)

    f2 = lambda a: a.reshape(t, -1)
    merged = gate_merge(u, [f2(o_mla), f2(o_sb), f2(o_bd), f2(o_ds)], lw["wg"], lw["wb"])
    h_new, un = matmul_norm(merged, w_out, h, g_ffn, name="mix_out")
    return h_new, un, (ckv, kr, bd_k, bd_v, kidx), (sb_k, sb_v, ds_k, ds_v)


def dense_ffn(h, un, w1, w3, w2):
    act = swiglu_up(un, w1, w3)
    out, = matmul(act, w2, (F32,), residual=h, name="ffn_down")
    return out


def moe_ffn(h, un, router_pad, w1, w3, w2):
    n_experts = w1.shape[0]
    gate = router_gate(un, router_pad, n_experts)
    for e in range(n_experts):
        act = swiglu_up(un, w1[e], w3[e])
        h, = matmul(act, w2[e], (F32,), residual=h, gate=gate, gate_col=e, name="moe_down")
    return h


def layer_step(h, u, p, bsz, seq, cs, past, lw, band_mask, dsa_tables, fw, stacks, g_next, last):
    h, un, rows, head_stacks = mixing_block(u, bsz, seq, lw, cs, past, band_mask, dsa_tables, fw["w_out"], h,
                                            stacks, fw["g_ffn"])
    if fw["moe"]:
        h = moe_ffn(h, un, fw["router"], fw["w1"], fw["w3"], fw["w2"])
    else:
        h = dense_ffn(h, un, fw["w1"], fw["w3"], fw["w2"])
    un = rmsnorm(h, fw["g_ple"], BF)
    outs = ple_update(h, p, un, fw["ple_w"], fw["ple_gate_w"], g_next, F32 if last else BF, keep_h=not last)
    return (None if last else outs[0]), outs[-1], rows, head_stacks


def kernel(x_prompt, x_sample, p_prompt, p_sample, cache_mla_ckv, cache_mla_krope, cache_sb_k, cache_sb_v, cache_band_k, cache_band_v, cache_dsa_k, cache_dsa_v, cache_dsa_kidx, norm_mix, w_in, mla_q_norm, mla_w_uq, mla_kv_norm, mla_w_ukv, band_rel_bias, t5_rel_bias, w_branch, w_out, norm_ffn, ffn_w1, ffn_w3, ffn_w2, moe_router, moe_w1, moe_w3, moe_w2, norm_ple, ple_w, ple_gate_w, norm_final):
    depth = w_in.shape[0]
    bp, sp, d = x_prompt.shape
    bs, ss, _ = x_sample.shape
    past_len = cache_sb_k.shape[2]
    tm = 512
    cs_p = rope_table(jnp.arange(sp))
    cs_s = rope_table(past_len + (jnp.arange(tm) % ss))
    dsa_tables_p = dsa_bias_tables(t5_rel_bias, PROMPT_TQ)
    dsa_tables_s = dsa_bias_tables(t5_rel_bias, ss)
    hp = x_prompt.reshape(bp * sp, d)
    hs = x_sample.reshape(bs * ss, d)
    up = rmsnorm(hp, norm_mix[0], BF)
    us = rmsnorm(hs, norm_mix[0], BF)
    rows_p, rows_s = [], []
    stacks_p = stacks_s = None
    for i in range(depth):
        lw = prepare_layer_weights(w_in[i], mla_q_norm[i], mla_w_uq[i], mla_kv_norm[i], mla_w_ukv[i], w_branch[i])
        j = i // 2
        fw = dict(g_ffn=norm_ffn[i], g_ple=norm_ple[i], w_out=w_out[i].astype(BF),
                  ple_w=ple_w[i].astype(BF), ple_gate_w=ple_gate_w[i].astype(BF), moe=i % 2 == 1)
        if i % 2 == 0:
            fw.update(w1=ffn_w1[j].astype(BF), w3=ffn_w3[j].astype(BF), w2=ffn_w2[j].astype(BF))
        else:
            fw.update(router=_pad_cols(moe_router[j], LANES).astype(BF),
                      w1=moe_w1[j].astype(BF), w3=moe_w3[j].astype(BF), w2=moe_w2[j].astype(BF))
        past_i = (cache_mla_ckv[i], cache_mla_krope[i], cache_sb_k[i], cache_sb_v[i], cache_band_k[i],
                  cache_band_v[i], cache_dsa_k[i], cache_dsa_v[i], cache_dsa_kidx[i])
        last = i == depth - 1
        g_next = norm_final if last else norm_mix[i + 1]
        hp, up, rp, stacks_p = layer_step(hp, up, p_prompt[i].reshape(bp * sp, -1).astype(BF), bp, sp, cs_p, None,
                                          lw, band_bias_mask(band_rel_bias[i], PROMPT_TQ), dsa_tables_p, fw,
                                          (i, depth, stacks_p), g_next, last)
        hs, us, rs, stacks_s = layer_step(hs, us, p_sample[i].reshape(bs * ss, -1).astype(BF), bs, ss, cs_s, past_i,
                                          lw, band_bias_mask(band_rel_bias[i], ss), dsa_tables_s, fw,
                                          (i, depth, stacks_s), g_next, last)
        rows_p.append(rp)
        rows_s.append(rs)
    y_prompt = up.reshape(bp, sp, d)
    y_sample = us.reshape(bs, ss, d)

    keep = min(N_PREV_CHUNKS * CHUNK, sp)

    def stacked(rows, n, bsz, seq, heads=None, tail=None):
        out = []
        for r in rows:
            a = r[n].reshape(bsz, seq, -1)
            if tail is not None:
                a = a[:, seq - tail:]
            if heads is not None:
                a = a.reshape(a.shape[0], a.shape[1], heads, HEAD_DIM)
            out.append(a)
        return jnp.stack(out, axis=0)

    def both(n, heads=None, prompt_tail=None):
        return [stacked(rows_p, n, bp, sp, heads, prompt_tail), stacked(rows_s, n, bs, ss, heads)]

    def from_stacks(n):
        return [stacks_p[n].reshape(depth, bp, sp, -1, HEAD_DIM), stacks_s[n].reshape(depth, bs, ss, -1, HEAD_DIM)]

    res = [y_prompt, y_sample] + both(0) + both(1) + from_stacks(0) + from_stacks(1)
    res += both(2, H_BAND, keep) + both(3, H_BAND, keep) + from_stacks(2) + from_stacks(3) + both(4)
    return tuple(res)
```

```python
import functools
import math

import jax
import jax.numpy as jnp
from jax import lax
from jax.experimental import pallas as pl
from jax.experimental.pallas import tpu as pltpu

BF = jnp.bfloat16
F32 = jnp.float32

CHUNK = 64
CHUNK_SHIFT = 6
HEAD_DIM = 128
N_BRANCH = 4
H_MLA = 4
NOPE_DIM = 128
ROPE_DIM = 64
V_DIM = 128
ROPE_THETA = 10000.0
H_SB = 4
H_BAND = 4
N_PREV_CHUNKS = 8
REL_CLIP = 128
H_DSA = 4
H_IDX = 16
D_IDX = 64
TOPK_MAX = 256
T5_BUCKETS = 32
T5_MAX_DIST = 128
TOP_K_EXPERTS = 2
EPS = 1e-6

LANES = 128
PROMPT_TQ = 256
PROMPT_TK = 512
SAMPLE_TK = 384
VMEM_LIMIT = 56 * 1024 * 1024
NEG = -1e30
INT_MIN = -2147483648


def _cparams(n_axes):
    return pltpu.CompilerParams(dimension_semantics=("arbitrary",) * n_axes,
                                vmem_limit_bytes=VMEM_LIMIT)


def _tile(n, preferred):
    if n <= preferred:
        return n
    t = preferred - preferred % LANES
    while n % t:
        t -= LANES
    assert t > 0
    return t


def _dot(a, b):
    return jnp.dot(a, b, preferred_element_type=F32)


def _dot_nt(a, b):
    return lax.dot_general(a, b, (((1,), (1,)), ((), ())), preferred_element_type=F32)


def _sigmoid(x):
    return 1.0 / (1.0 + jnp.exp(-x))


def _rms(x, g):
    return x * lax.rsqrt(jnp.mean(x * x, axis=-1, keepdims=True) + EPS) * g


def _rmsnorm_kernel(x_ref, g_ref, o_ref):
    o_ref[...] = _rms(x_ref[...], g_ref[...]).astype(o_ref.dtype)


def rmsnorm(x, g, out_dtype, tm=512):
    t, d = x.shape
    tm = _tile(t, tm)
    return pl.pallas_call(
        _rmsnorm_kernel,
        grid=(t // tm,),
        in_specs=[pl.BlockSpec((tm, d), lambda i: (i, 0)),
                  pl.BlockSpec((1, d), lambda i: (0, 0))],
        out_specs=pl.BlockSpec((tm, d), lambda i: (i, 0)),
        out_shape=jax.ShapeDtypeStruct((t, d), out_dtype),
        compiler_params=_cparams(1),
        name="rmsnorm",
    )(x, g.reshape(1, d))


def _mm_kernel(*refs, has_res, gate_col, n_out):
    a_ref, w_ref = refs[0], refs[1]
    pos = 2
    res_ref = gate_ref = None
    if has_res:
        res_ref = refs[pos]
        pos += 1
    if gate_col is not None:
        gate_ref = refs[pos]
        pos += 1
    r = _dot(a_ref[...], w_ref[...])
    if gate_ref is not None:
        r = r * gate_ref[:, gate_col:gate_col + 1]
    if res_ref is not None:
        r = res_ref[...] + r
    for o_ref in refs[pos:pos + n_out]:
        o_ref[...] = r.astype(o_ref.dtype)


WEIGHT_BLOCK_BYTES = 6 * 1024 * 1024


def matmul(a, w, out_dtypes, residual=None, gate=None, gate_col=None, tm=512, name="mm"):
    m, k = a.shape
    n = w.shape[1]
    tm, tn = _tile(m, tm), _tile(n, WEIGHT_BLOCK_BYTES // (2 * k))
    in_specs = [pl.BlockSpec((tm, k), lambda j, i: (i, 0)),
                pl.BlockSpec((k, tn), lambda j, i: (0, j))]
    args = [a, w]
    if residual is not None:
        in_specs.append(pl.BlockSpec((tm, tn), lambda j, i: (i, j)))
        args.append(residual)
    if gate is not None:
        in_specs.append(pl.BlockSpec((tm, gate.shape[1]), lambda j, i: (i, 0)))
        args.append(gate)
    outs = pl.pallas_call(
        functools.partial(_mm_kernel, has_res=residual is not None,
                          gate_col=gate_col if gate is not None else None, n_out=len(out_dtypes)),
        grid=(n // tn, m // tm),
        in_specs=in_specs,
        out_specs=[pl.BlockSpec((tm, tn), lambda j, i: (i, j)) for _ in out_dtypes],
        out_shape=[jax.ShapeDtypeStruct((m, n), dt) for dt in out_dtypes],
        compiler_params=_cparams(2),
        name=name,
    )(*args)
    return outs


def _project_kernel(a_ref, w_ref, *refs, kinds, d, blk, n_prev, first_layer):
    o_refs = refs[n_prev:]
    r = _dot(a_ref[...], w_ref[...])
    tm, n = r.shape
    for o_ref, kind in zip(o_refs, kinds):
        if kind == "rows":
            o_ref[...] = r.astype(o_ref.dtype)
        elif kind == "stack_heads":
            slab = o_ref.at[0] if first_layer else o_ref
            for h in range(n // d):
                slab[:, h, :] = r[:, h * d:(h + 1) * d].astype(o_ref.dtype)
            if first_layer:
                for layer in range(1, o_ref.shape[0]):
                    o_ref[layer] = jnp.zeros(o_ref.shape[1:], o_ref.dtype)
        elif kind == "heads_t":
            for h in range(n // d):
                o_ref[h] = r[:, h * d:(h + 1) * d].T.astype(o_ref.dtype)
        else:
            for h in range(n // d):
                for j in range(tm // blk):
                    o_ref[h, j] = r[j * blk:(j + 1) * blk, h * d:(h + 1) * d].T.astype(o_ref.dtype)


def project(a, w, bsz, seq, outs, stack=None, d=LANES, blk=LANES, tm=1024, name="project"):
    m, k = a.shape
    n = w.shape[1]
    tm = _tile(m, tm)
    spt = _seq_tiles(seq, tm)
    kinds = tuple(kind for kind, _ in outs)
    assert n % d == 0 and 2 * k * n <= 2 * WEIGHT_BLOCK_BYTES
    assert (spt is not None and tm % blk == 0) or not {"heads_t", "blocks_t"} & set(kinds)
    layer, depth, prev = stack if stack is not None else (0, 1, None)
    specs, shapes, aliases, prev_args = [], [], {}, []
    for kind, dt in outs:
        if kind == "rows":
            specs.append(pl.BlockSpec((tm, n), lambda i: (i, 0)))
            shapes.append(jax.ShapeDtypeStruct((m, n), dt))
        elif kind == "stack_heads":
            if layer == 0:
                specs.append(pl.BlockSpec((depth, tm, n // d, d), lambda i: (0, i, 0, 0)))
            else:
                specs.append(pl.BlockSpec((None, tm, n // d, d), lambda i: (layer, i, 0, 0)))
                aliases[2 + len(prev_args)] = len(shapes)
                prev_args.append(prev)
            shapes.append(jax.ShapeDtypeStruct((depth, m, n // d, d), dt))
        elif kind == "heads_t":
            specs.append(pl.BlockSpec((None, n // d, d, tm), lambda i: (i // spt, 0, 0, i % spt)))
            shapes.append(jax.ShapeDtypeStruct((bsz, n // d, d, seq), dt))
        else:
            specs.append(pl.BlockSpec((None, n // d, tm // blk, d, blk), lambda i: (i // spt, 0, i % spt, 0, 0)))
            shapes.append(jax.ShapeDtypeStruct((bsz, n // d, seq // blk, d, blk), dt))
    return pl.pallas_call(
        functools.partial(_project_kernel, kinds=kinds, d=d, blk=blk, n_prev=len(prev_args),
                          first_layer=layer == 0),
        grid=(m // tm,),
        in_specs=[pl.BlockSpec((tm, k), lambda i: (i, 0)), pl.BlockSpec((k, n), lambda i: (0, 0))]
        + [pl.BlockSpec(memory_space=pl.ANY) for _ in prev_args],
        out_specs=specs,
        out_shape=shapes,
        input_output_aliases=aliases,
        compiler_params=_cparams(1),
        name=name,
    )(a, w, *prev_args)


def _swiglu_up_kernel(a_ref, w1_ref, w3_ref, o_ref):
    a = a_ref[...]
    x1 = _dot(a, w1_ref[...])
    x3 = _dot(a, w3_ref[...])
    o_ref[...] = (x1 * _sigmoid(x1) * x3).astype(o_ref.dtype)


def swiglu_up(a, w1, w3, tm=512):
    m, k = a.shape
    n = w1.shape[1]
    tm, tn = _tile(m, tm), _tile(n, WEIGHT_BLOCK_BYTES // (2 * k))
    return pl.pallas_call(
        _swiglu_up_kernel,
        grid=(n // tn, m // tm),
        in_specs=[pl.BlockSpec((tm, k), lambda j, i: (i, 0)),
                  pl.BlockSpec((k, tn), lambda j, i: (0, j)),
                  pl.BlockSpec((k, tn), lambda j, i: (0, j))],
        out_specs=pl.BlockSpec((tm, tn), lambda j, i: (i, j)),
        out_shape=jax.ShapeDtypeStruct((m, n), BF),
        compiler_params=_cparams(2),
        name="swiglu_up",
    )(a, w1, w3)


def _gate_merge_kernel(u_ref, o0_ref, o1_ref, o2_ref, o3_ref, wg_ref, wb_ref, out_ref):
    u = u_ref[...]
    acc = None
    for b, o_ref in enumerate((o0_ref, o1_ref, o2_ref, o3_ref)):
        t = _sigmoid(_dot(u, wg_ref[b])) * _dot(o_ref[...], wb_ref[b])
        acc = t if acc is None else acc + t
    out_ref[...] = acc.astype(out_ref.dtype)


def gate_merge(u, branches, wg, wb, tm=512, tn=512):
    t, d = u.shape
    bw = branches[0].shape[1]
    n = wg.shape[2]
    tm = min(tm, t)
    return pl.pallas_call(
        _gate_merge_kernel,
        grid=(n // tn, t // tm),
        in_specs=[pl.BlockSpec((tm, d), lambda j, i: (i, 0))]
        + [pl.BlockSpec((tm, bw), lambda j, i: (i, 0)) for _ in range(N_BRANCH)]
        + [pl.BlockSpec((N_BRANCH, d, tn), lambda j, i: (0, 0, j)),
           pl.BlockSpec((N_BRANCH, bw, tn), lambda j, i: (0, 0, j))],
        out_specs=pl.BlockSpec((tm, tn), lambda j, i: (i, j)),
        out_shape=jax.ShapeDtypeStruct((t, n), BF),
        compiler_params=_cparams(2),
        name="gate_merge",
    )(u, *branches, wg, wb)


def _ple_kernel(h_ref, p_ref, un_ref, wp_ref, wg_ref, g_ref, *o_refs):
    h = h_ref[...] + _dot(p_ref[...], wp_ref[...]) * _sigmoid(_dot(un_ref[...], wg_ref[...]))
    if len(o_refs) == 2:
        o_refs[0][...] = h
    o_refs[-1][...] = _rms(h, g_ref[...]).astype(o_refs[-1].dtype)


def ple_update(h, p, un, wp, wg, g_next, next_dtype, keep_h, tm=512):
    t, d = h.shape
    tm = min(tm, t)
    row = lambda n: pl.BlockSpec((tm, n), lambda i: (i, 0))
    full = lambda a: pl.BlockSpec(a.shape, lambda i: (0, 0))
    g_next = g_next.reshape(1, d)
    return pl.pallas_call(
        _ple_kernel,
        grid=(t // tm,),
        in_specs=[row(d), row(p.shape[1]), row(d), full(wp), full(wg), full(g_next)],
        out_specs=([row(d)] if keep_h else []) + [row(d)],
        out_shape=([jax.ShapeDtypeStruct((t, d), F32)] if keep_h else []) + [jax.ShapeDtypeStruct((t, d), next_dtype)],
        compiler_params=_cparams(1),
        name="ple_update",
    )(h, p, un, wp, wg, g_next)


def _mm_norm_kernel(a_ref, w_ref, res_ref, g_ref, h_ref, n_ref):
    h = res_ref[...] + _dot(a_ref[...], w_ref[...])
    h_ref[...] = h
    n_ref[...] = _rms(h, g_ref[...]).astype(n_ref.dtype)


def matmul_norm(a, w, residual, g, tm=512, name="mm_norm"):
    m, k = a.shape
    n = w.shape[1]
    tm = _tile(m, tm)
    row = lambda c: pl.BlockSpec((tm, c), lambda i: (i, 0))
    return pl.pallas_call(
        _mm_norm_kernel,
        grid=(m // tm,),
        in_specs=[row(k), pl.BlockSpec((k, n), lambda i: (0, 0)), row(n), pl.BlockSpec((1, n), lambda i: (0, 0))],
        out_specs=[row(n), row(n)],
        out_shape=[jax.ShapeDtypeStruct((m, n), F32), jax.ShapeDtypeStruct((m, n), BF)],
        compiler_params=_cparams(1),
        name=name,
    )(a, w, residual, g.reshape(1, n))


def _router_kernel(un_ref, wr_ref, g_ref, *, n_experts):
    logits = _dot(un_ref[...], wr_ref[...])
    lane = lax.broadcasted_iota(jnp.int32, logits.shape, 1).astype(F32)
    real = lane < n_experts
    logits = jnp.where(real, logits, NEG)
    e = jnp.where(real, jnp.exp(logits - jnp.max(logits, axis=1, keepdims=True)), 0.0)
    probs = e / jnp.sum(e, axis=1, keepdims=True)
    p1 = jnp.max(probs, axis=1, keepdims=True)
    i1 = jnp.min(jnp.where(probs == p1, lane, float(LANES)), axis=1, keepdims=True)
    first = lane == i1
    rest = jnp.where(first | ~real, -1.0, probs)
    p2 = jnp.max(rest, axis=1, keepdims=True)
    i2 = jnp.min(jnp.where(rest == p2, lane, float(LANES)), axis=1, keepdims=True)
    second = lane == i2
    denom = p1 + p2
    g_ref[...] = jnp.where(first, p1 / denom, 0.0) + jnp.where(second, p2 / denom, 0.0)


def router_gate(un, wr_pad, n_experts, tm=512):
    t, d = un.shape
    tm = min(tm, t)
    return pl.pallas_call(
        functools.partial(_router_kernel, n_experts=n_experts),
        grid=(t // tm,),
        in_specs=[pl.BlockSpec((tm, d), lambda i: (i, 0)),
                  pl.BlockSpec((d, LANES), lambda i: (0, 0))],
        out_specs=pl.BlockSpec((tm, LANES), lambda i: (i, 0)),
        out_shape=jax.ShapeDtypeStruct((t, LANES), F32),
        compiler_params=_cparams(1),
        name="router_gate",
    )(un, wr_pad)


Q_LORA_OFF = 0


def _proj_misc_kernel(u_ref, wm_ref, qn_ref, kvn_ref, wqa_ref, wqb_ref, cs_ref,
                      ckv_ref, krp_ref, kidx_ref, idxw_ref, qcat_ref, *, q_lora, kv_lora, lanes_are_tokens):
    x = _dot(u_ref[...], wm_ref[...])
    cos = cs_ref[:, 0:LANES]
    sin = cs_ref[:, LANES:2 * LANES]
    o = q_lora
    ckv_ref[...] = _rms(x[:, o:o + kv_lora], kvn_ref[...])
    o += kv_lora
    krp_ref[...] = x[:, o:o + LANES] * cos + x[:, o + LANES:o + 2 * LANES] * sin
    o += 2 * LANES
    kidx_ref[...] = x[:, o:o + LANES]
    idxw = x[:, o + LANES:o + 2 * LANES]
    if lanes_are_tokens:
        idxw_ref[...] = idxw.T[0:H_IDX]
    else:
        idxw_ref[...] = idxw
    cqn = _rms(x[:, 0:q_lora], qn_ref[...]).astype(BF)
    qa = _dot(cqn, wqa_ref[...])
    qb = _dot(cqn, wqb_ref[...])
    for h in range(H_MLA):
        lo = 2 * h * LANES
        nope = qa[:, lo:lo + LANES]
        rope = qa[:, lo + LANES:lo + 2 * LANES] * cos + qb[:, h * LANES:(h + 1) * LANES] * sin
        if lanes_are_tokens:
            qcat_ref[h, 0:LANES] = nope.T.astype(BF)
            qcat_ref[h, LANES:2 * LANES] = rope.T.astype(BF)
        else:
            qcat_ref[:, lo:lo + LANES] = nope.astype(BF)
            qcat_ref[:, lo + LANES:lo + 2 * LANES] = rope.astype(BF)


def _seq_tiles(seq, tm):
    return seq // tm if seq % tm == 0 else None


def proj_misc(u, bsz, seq, wm, qn, kvn, wqa, wqb, cs, tm=512):
    t, d = u.shape
    tm = min(tm, t)
    q_lora, kv_lora = qn.shape[1], kvn.shape[1]
    n_pos_blocks = cs.shape[0] // tm
    spt = _seq_tiles(seq, tm)
    row = lambda n: pl.BlockSpec((tm, n), lambda i: (i, 0))
    full = lambda a: pl.BlockSpec(a.shape, lambda i: (0, 0))
    if spt is None:
        q_specs = [row(LANES), row(2 * LANES * H_MLA)]
        q_shapes = [jax.ShapeDtypeStruct((t, LANES), F32), jax.ShapeDtypeStruct((t, 2 * LANES * H_MLA), BF)]
    else:
        q_specs = [pl.BlockSpec((None, H_IDX, tm), lambda i: (i // spt, 0, i % spt)),
                   pl.BlockSpec((None, H_MLA, 2 * LANES, tm), lambda i: (i // spt, 0, 0, i % spt))]
        q_shapes = [jax.ShapeDtypeStruct((bsz, H_IDX, seq), F32),
                    jax.ShapeDtypeStruct((bsz, H_MLA, 2 * LANES, seq), BF)]
    return pl.pallas_call(
        functools.partial(_proj_misc_kernel, q_lora=q_lora, kv_lora=kv_lora, lanes_are_tokens=spt is not None),
        grid=(t // tm,),
        in_specs=[row(d), full(wm), full(qn), full(kvn), full(wqa), full(wqb),
                  pl.BlockSpec((tm, 2 * LANES), lambda i: (i % n_pos_blocks, 0))],
        out_specs=[row(kv_lora), row(LANES), row(LANES)] + q_specs,
        out_shape=[jax.ShapeDtypeStruct((t, kv_lora), F32),
                   jax.ShapeDtypeStruct((t, LANES), F32),
                   jax.ShapeDtypeStruct((t, LANES), F32)] + q_shapes,
        compiler_params=_cparams(1),
        name="proj_misc",
    )(u, wm, qn, kvn, wqa, wqb, cs)


def _mla_kv_up_kernel(ckv_ref, krp_ref, wk_ref, wv_ref, kcat_ref, vt_ref):
    c = ckv_ref[...].astype(BF)
    kn = _dot(c, wk_ref[...])
    krp = krp_ref[...].astype(BF)
    for h in range(H_MLA):
        kcat_ref[:, 2 * h * LANES:(2 * h + 1) * LANES] = kn[:, h * LANES:(h + 1) * LANES].astype(BF)
        kcat_ref[:, (2 * h + 1) * LANES:(2 * h + 2) * LANES] = krp
    v = _dot(c, wv_ref[...])
    for h in range(H_MLA):
        vt_ref[h, 0] = v[:, h * V_DIM:(h + 1) * V_DIM].T.astype(BF)


def mla_kv_up(ckv, krp, wk, wv, bsz, seq, tm):
    t, c = ckv.shape
    spt = seq // tm
    assert seq % tm == 0
    row = lambda n: pl.BlockSpec((tm, n), lambda i: (i, 0))
    full = lambda a: pl.BlockSpec(a.shape, lambda i: (0, 0))
    return pl.pallas_call(
        _mla_kv_up_kernel,
        grid=(t // tm,),
        in_specs=[row(c), row(LANES), full(wk), full(wv)],
        out_specs=[row(2 * LANES * H_MLA),
                   pl.BlockSpec((None, H_MLA, 1, V_DIM, tm), lambda i: (i // spt, 0, i % spt, 0, 0))],
        out_shape=[jax.ShapeDtypeStruct((t, 2 * LANES * H_MLA), BF),
                   jax.ShapeDtypeStruct((bsz, H_MLA, spt, V_DIM, tm), BF)],
        compiler_params=_cparams(1),
        name="mla_kv_up",
    )(ckv, krp, wk, wv)


def _mla_attn_kernel(qt_ref, k_ref, vt_ref, o_ref, *, tq, tk, q_off, scale, heads):
    q0 = q_off + pl.program_id(2) * tq
    qpos = q0 + lax.broadcasted_iota(jnp.int32, (1, tq), 1)
    limit = (lax.shift_right_arithmetic(qpos, CHUNK_SHIFT) + 1) * CHUNK
    last_limit = ((q0 + tq - 1) // CHUNK + 1) * CHUNK
    n_chunks = (last_limit + tk - 1) // tk

    def step(kc, carry, masked):
        ks = pl.multiple_of(kc * tk, tk)
        if masked:
            visible = ks + lax.broadcasted_iota(jnp.int32, (tk, tq), 0) < limit
        scores = [_dot(k_ref[pl.ds(ks, tk), 2 * h * LANES:2 * (h + 1) * LANES], qt_ref[h]) for h in range(heads)]
        probs, stats = [], []
        for h in range(heads):
            m_prev, l_prev, _ = carry[h]
            s = scores[h] * scale
            if masked:
                s = jnp.where(visible, s, NEG)
            m_new = jnp.maximum(m_prev, jnp.max(s, axis=0, keepdims=True))
            p = jnp.exp(s - m_new)
            alpha = jnp.exp(m_prev - m_new)
            stats.append((m_new, alpha * l_prev + jnp.sum(p, axis=0, keepdims=True), alpha))
            probs.append(p.astype(BF))
        out = []
        for h in range(heads):
            m_new, l_new, alpha = stats[h]
            out.append((m_new, l_new, alpha * carry[h][2] + _dot(vt_ref[h, kc], probs[h])))
        return tuple(out)

    init = tuple((jnp.full((1, tq), NEG, F32), jnp.zeros((1, tq), F32), jnp.zeros((V_DIM, tq), F32))
                 for _ in range(heads))
    n_open = jnp.minimum(((q0 // CHUNK + 1) * CHUNK) // tk, n_chunks)
    state = lax.fori_loop(0, n_open, lambda kc, c: step(kc, c, False), init)
    final = lax.fori_loop(n_open, n_chunks, lambda kc, c: step(kc, c, True), state)
    for h in range(heads):
        _, l_fin, acc = final[h]
        o_ref[:, h * V_DIM:(h + 1) * V_DIM] = (acc / l_fin).T.astype(o_ref.dtype)


def _chunked_transpose(v, heads, tk):
    b, s, hd = v.shape
    d = hd // heads
    return jnp.transpose(v.reshape(b, s // tk, tk, heads, d), (0, 3, 1, 4, 2))


def _heads_to_lanes(a, heads):
    b, s, hd = a.shape
    return jnp.transpose(a.reshape(b, s, heads, hd // heads), (0, 2, 3, 1))


def mla_attention(qt, kcat, vt, q_off, tq, tk, heads=4):
    b, _, _, sq = qt.shape
    sk = kcat.shape[1]
    scale = (NOPE_DIM + ROPE_DIM) ** -0.5
    return pl.pallas_call(
        functools.partial(_mla_attn_kernel, tq=tq, tk=tk, q_off=q_off, scale=scale, heads=heads),
        grid=(b, H_MLA // heads, sq // tq),
        in_specs=[pl.BlockSpec((None, heads, 2 * LANES, tq), lambda bi, h, qi: (bi, h, 0, qi)),
                  pl.BlockSpec((None, sk, heads * 2 * LANES), lambda bi, h, qi: (bi, 0, h)),
                  pl.BlockSpec((None, heads, sk // tk, V_DIM, tk), lambda bi, h, qi: (bi, h, 0, 0, 0))],
        out_specs=pl.BlockSpec((None, tq, heads * V_DIM), lambda bi, h, qi: (bi, qi, h)),
        out_shape=jax.ShapeDtypeStruct((b, sq, H_MLA * V_DIM), BF),
        compiler_params=_cparams(3),
        name="mla_attention",
    )(qt, kcat, vt)


def _split2(x):
    hi = x.astype(BF)
    return hi, (x - hi.astype(F32)).astype(BF)


def _sb_attn_kernel(q_ref, k_ref, v_ref, tri_ref, o_ref, *, tq, tk, q_off, scale, heads):
    q0 = q_off + pl.program_id(2) * tq
    qpos = q0 + lax.broadcasted_iota(jnp.int32, (tq, 1), 0)
    n_chunks = (q0 + tq - 1 + tk - 1) // tk
    tri2 = tri_ref[...]
    n_blk = tk // LANES
    hs = [slice(h * HEAD_DIM, (h + 1) * HEAD_DIM) for h in range(heads)]

    def step(kc, carry, diagonal):
        ks = pl.multiple_of(kc * tk, tk)
        if diagonal:
            strict = ks + lax.broadcasted_iota(jnp.int32, (1, tk), 1) < qpos
            causal = lambda x: jnp.where(strict, x, 0.0)
        else:
            causal = lambda x: x
        z = [_dot_nt(q_ref[:, hs[h]], k_ref[pl.ds(ks, tk), hs[h]]) for h in range(heads)]
        log_beta, parts = [], []
        for h in range(heads):
            zh = z[h] * scale
            sp = jnp.maximum(zh, 0.0) + jnp.log(1.0 + jnp.exp(-jnp.abs(zh)))
            log_beta.append(zh - sp)
            parts.append(_split2(causal(-sp)))
        after, later = [], []
        for h in range(heads):
            hi, lo = parts[h]
            run = carry[h][1]
            blocks = [None] * n_blk
            for blk in reversed(range(n_blk)):
                sl = slice(blk * LANES, (blk + 1) * LANES)
                sums = _dot(jnp.concatenate([hi[:, sl], lo[:, sl]], axis=1), tri2)
                blocks[blk] = sums[:, 0:LANES] + run
                run = run + sums[:, LANES:2 * LANES]
            after.append(jnp.concatenate(blocks, axis=1))
            later.append(run)
        weights = [causal(jnp.exp(log_beta[h] + after[h])).astype(BF) for h in range(heads)]
        return tuple((carry[h][0] + _dot(weights[h], v_ref[pl.ds(ks, tk), hs[h]]), later[h]) for h in range(heads))

    n_below = q0 // tk
    init = tuple((jnp.zeros((tq, HEAD_DIM), F32), jnp.zeros((tq, LANES), F32)) for _ in range(heads))
    state = lax.fori_loop(0, n_chunks - n_below, lambda it, c: step(n_chunks - 1 - it, c, True), init)
    final = lax.fori_loop(0, n_below, lambda it, c: step(n_below - 1 - it, c, False), state)
    for h in range(heads):
        o_ref[:, hs[h]] = final[h][0].astype(o_ref.dtype)


def sb_attention(q, k, v, q_off, tq, tk, heads=4):
    b, sq, _ = q.shape
    sk = k.shape[1]
    assert sk % tk == 0 and tk % LANES == 0 and H_SB % heads == 0
    j = lax.broadcasted_iota(jnp.int32, (LANES, LANES), 0)
    s = lax.broadcasted_iota(jnp.int32, (LANES, LANES), 1)
    tri = jnp.concatenate([(j > s).astype(BF), jnp.ones((LANES, LANES), BF)], axis=1)
    tri = jnp.concatenate([tri, tri], axis=0)
    width = heads * HEAD_DIM
    return pl.pallas_call(
        functools.partial(_sb_attn_kernel, tq=tq, tk=tk, q_off=q_off, scale=HEAD_DIM ** -0.5, heads=heads),
        grid=(b, H_SB // heads, sq // tq),
        in_specs=[pl.BlockSpec((None, tq, width), lambda bi, h, qi: (bi, qi, h)),
                  pl.BlockSpec((None, sk, width), lambda bi, h, qi: (bi, 0, h)),
                  pl.BlockSpec((None, sk, width), lambda bi, h, qi: (bi, 0, h)),
                  pl.BlockSpec((2 * LANES, 2 * LANES), lambda bi, h, qi: (0, 0))],
        out_specs=pl.BlockSpec((None, tq, width), lambda bi, h, qi: (bi, qi, h)),
        out_shape=jax.ShapeDtypeStruct((b, sq, H_SB * HEAD_DIM), BF),
        compiler_params=_cparams(3),
        name="sb_attention",
    )(q, k, v, tri)


def _band_attn_kernel(q_ref, k_ref, v_ref, bm_ref, o_ref, *, tq, win, kpos_base, scale):
    w0 = pl.multiple_of(pl.program_id(1) * tq, tq)
    kpos = kpos_base + w0 + lax.broadcasted_iota(jnp.int32, (1, win), 1)
    exists = kpos >= 0
    hs = [slice(h * HEAD_DIM, (h + 1) * HEAD_DIM) for h in range(H_BAND)]
    scores = [_dot_nt(q_ref[:, hs[h]], k_ref[pl.ds(w0, win), hs[h]]) for h in range(H_BAND)]
    probs, denoms = [], []
    for h in range(H_BAND):
        s = jnp.where(exists, scores[h] * scale + bm_ref[h], NEG)
        p = jnp.exp(s - jnp.max(s, axis=1, keepdims=True))
        denoms.append(jnp.sum(p, axis=1, keepdims=True))
        probs.append(p.astype(BF))
    for h in range(H_BAND):
        o = _dot(probs[h], v_ref[pl.ds(w0, win), hs[h]])
        o_ref[:, hs[h]] = (o / denoms[h]).astype(o_ref.dtype)


def _band_window(tq):
    return -(-(tq + N_PREV_CHUNKS * CHUNK) // LANES) * LANES


def band_bias_mask(rel_bias, tq):
    win = _band_window(tq)
    i = jnp.arange(tq)[:, None]
    j = jnp.arange(win)[None, :]
    rel = jnp.clip(i + N_PREV_CHUNKS * CHUNK - j, -REL_CLIP, REL_CLIP) + REL_CLIP
    ci, cj = i // CHUNK, j // CHUNK
    inside = (cj >= ci) & (cj <= ci + N_PREV_CHUNKS)
    return jnp.where(inside[None], _table_lookup(rel_bias, rel), NEG)


def band_attention(q, k_pad, v_pad, bias_mask, tq, kpos_base):
    b, sq, _ = q.shape
    skp = k_pad.shape[1]
    win = _band_window(tq)
    width = H_BAND * HEAD_DIM
    assert skp >= sq - tq + win
    return pl.pallas_call(
        functools.partial(_band_attn_kernel, tq=tq, win=win, kpos_base=kpos_base, scale=HEAD_DIM ** -0.5),
        grid=(b, sq // tq),
        in_specs=[pl.BlockSpec((None, tq, width), lambda bi, qi: (bi, qi, 0)),
                  pl.BlockSpec((None, skp, width), lambda bi, qi: (bi, 0, 0)),
                  pl.BlockSpec((None, skp, width), lambda bi, qi: (bi, 0, 0)),
                  pl.BlockSpec((H_BAND, tq, win), lambda bi, qi: (0, 0, 0))],
        out_specs=pl.BlockSpec((None, tq, width), lambda bi, qi: (bi, qi, 0)),
        out_shape=jax.ShapeDtypeStruct((b, sq, width), BF),
        compiler_params=_cparams(2),
        name="band_attention",
    )(q, k_pad, v_pad, bias_mask)


def _sortable(x):
    i = lax.bitcast_convert_type(x, jnp.int32)
    return i ^ (lax.shift_right_arithmetic(i, 31) & 0x7FFFFFFF)


def _dsa_kernel(qi2t_ref, wt_ref, klo_ref, khi_ref, qt_ref, k_ref, vt_ref, bnear_ref, bfar_ref, tri_ref,
                o_ref, key_ref, m_ref, l_ref, acc_ref, eqc_ref,
                *, tq, tk, big, q_off, topk, scale, w_scale):
    wide = big * tk
    near_after = -(-tq // tk)
    q0 = q_off + pl.program_id(1) * tq
    qpos = q0 + lax.broadcasted_iota(jnp.int32, (1, tq), 1)
    limit = (lax.shift_right_arithmetic(qpos, CHUNK_SHIFT) + 1) * CHUNK
    last_limit = ((q0 + tq - 1) // CHUNK + 1) * CHUNK
    n_wide = (last_limit + wide - 1) // wide
    diag = q0 // tk

    w = wt_ref[...] * w_scale

    def score_body(c, carry):
        ks = pl.multiple_of(c * wide, wide)
        klo = klo_ref[pl.ds(ks, wide), :]
        khi = khi_ref[pl.ds(ks, wide), :]
        acc = jnp.zeros((wide, tq), F32)
        for pair in range(H_IDX // 2):
            q2 = qi2t_ref[pair]
            acc = acc + w[2 * pair:2 * pair + 1] * jnp.maximum(_dot(klo, q2), 0.0)
            acc = acc + w[2 * pair + 1:2 * pair + 2] * jnp.maximum(_dot(khi, q2), 0.0)
        kpos = ks + lax.broadcasted_iota(jnp.int32, (wide, tq), 0)
        keys = _sortable(jnp.where(kpos < limit, acc + 0.0, -jnp.inf))
        for blk in range(big):
            key_ref[c * big + blk] = keys[blk * tk:(blk + 1) * tk]
        return carry

    lax.fori_loop(0, n_wide, score_body, 0)

    def count_ge(cand):
        def body(c, cnt):
            for blk in range(big):
                cnt = cnt + jnp.where(key_ref[c * big + blk] >= cand, 1.0, 0.0)
            return cnt
        cnt = lax.fori_loop(0, n_wide, body, jnp.zeros((tk, tq), F32))
        return jnp.sum(cnt, axis=0, keepdims=True)

    bits_per_check = 4

    def bit_body(state):
        it, thr, at_thr = state
        for b in range(bits_per_check):
            cand = thr + lax.shift_left(jnp.int32(1), 31 - (it + b))
            cnt = count_ge(cand)
            take = cnt >= topk
            thr, at_thr = jnp.where(take, cand, thr), jnp.where(take, cnt, at_thr)
        return it + bits_per_check, thr, at_thr

    def bits_left(state):
        it, _, at_thr = state
        return (it < 32) & (jnp.max(at_thr) > topk)

    every_key = jnp.full((1, tq), 1.0, F32) * (n_wide * wide).astype(F32)
    _, thr, _ = lax.while_loop(bits_left, bit_body,
                               (jnp.int32(0), jnp.full((1, tq), INT_MIN, jnp.int32), every_key))
    n_above = count_ge(thr + 1)
    n_ties_kept = topk - n_above

    m_ref[...] = jnp.full(m_ref.shape, NEG, F32)
    l_ref[...] = jnp.zeros(l_ref.shape, F32)
    acc_ref[...] = jnp.zeros(acc_ref.shape, F32)
    eqc_ref[...] = jnp.zeros(eqc_ref.shape, F32)
    tri = tri_ref[...]
    hs = [slice(h * HEAD_DIM, (h + 1) * HEAD_DIM) for h in range(H_DSA)]

    def attend(first_blk, n_blk, bias_of_head, below_tile):
        width = n_blk * tk
        ks = pl.multiple_of(first_blk * tk, tk)
        key = jnp.concatenate([key_ref[first_blk + i] for i in range(n_blk)], axis=0)
        eq = key == thr
        eq_bf = jnp.where(eq, 1.0, 0.0).astype(BF)
        seen = eqc_ref[0:1, :]
        rank = [None] * n_blk
        for i in range(n_blk):
            counts = _dot(tri, eq_bf[i * tk:(i + 1) * tk])
            rank[i] = counts[0:tk] + seen
            seen = seen + counts[tk:tk + 1]
        eqc_ref[0:1, :] = seen
        sel = (key > thr) | (eq & (jnp.concatenate(rank, axis=0) <= n_ties_kept))
        if not below_tile:
            sel = sel & (ks + lax.broadcasted_iota(jnp.int32, (width, tq), 0) < limit)
        scores = [_dot(k_ref[pl.ds(ks, width), hs[h]], qt_ref[h]) for h in range(H_DSA)]
        probs, alphas = [], []
        for h in range(H_DSA):
            s = jnp.where(sel, scores[h] * scale + bias_of_head(h), NEG)
            m_prev = m_ref[h]
            m_new = jnp.maximum(m_prev, jnp.max(s, axis=0, keepdims=True))
            p = jnp.where(sel, jnp.exp(s - m_new), 0.0)
            alpha = jnp.exp(m_prev - m_new)
            l_ref[h] = alpha * l_ref[h] + jnp.sum(p, axis=0, keepdims=True)
            m_ref[h] = m_new
            probs.append(p.astype(BF))
            alphas.append(alpha)
        for h in range(H_DSA):
            vt = jnp.concatenate([vt_ref[h, first_blk + i] for i in range(n_blk)], axis=1)
            acc_ref[h] = alphas[h] * acc_ref[h] + _dot(vt, probs[h])

    far_bias = lambda h: bfar_ref[h][:, 0:1]
    n_far_wide = jnp.maximum(q0 - tk, 0) // wide

    def far_wide_body(c, carry):
        attend(c * big, big, far_bias, True)
        return carry

    lax.fori_loop(0, n_far_wide, far_wide_body, 0)

    def far_body(kc, carry):
        attend(kc, 1, far_bias, True)
        return carry

    lax.fori_loop(n_far_wide * big, jnp.maximum(diag - 1, n_far_wide * big), far_body, 0)

    @pl.when(diag >= 1)
    def _():
        attend(diag - 1, 1, lambda h: bnear_ref[0, h], True)

    for d in range(near_after):
        attend(diag + d, 1, lambda h, d=d: bnear_ref[d + 1, h], False)

    for h in range(H_DSA):
        o_ref[:, hs[h]] = (acc_ref[h] / l_ref[h]).T.astype(o_ref.dtype)


def t5_bucket(rel):
    half = T5_BUCKETS // 2
    max_exact = half // 2
    n = jnp.abs(rel)
    nf = jnp.maximum(n, 1).astype(F32)
    large = max_exact + (jnp.log(nf / max_exact) / math.log(T5_MAX_DIST / max_exact)
                         * (half - max_exact)).astype(jnp.int32)
    large = jnp.minimum(large, half - 1)
    return jnp.where(rel > 0, half, 0) + jnp.where(n < max_exact, n, large)


def _table_lookup(table, idx):
    onehot = jax.nn.one_hot(idx, table.shape[0], dtype=F32)
    out = jnp.einsum("...n,nh->...h", onehot, table.astype(F32), precision=lax.Precision.HIGHEST)
    return jnp.moveaxis(out, -1, 0)


def dsa_bias_tables(t5_table, tq, tk=LANES):
    j = jnp.arange(tk)[:, None]
    i = jnp.arange(tq)[None, :]
    near = jnp.stack([_table_lookup(t5_table, t5_bucket(d * tk + j - i)) for d in range(-1, -(-tq // tk))],
                     axis=0)
    far_rel = -jnp.ones((1, tk), jnp.int32) * (2 * tk)
    far = _table_lookup(t5_table, t5_bucket(far_rel))
    return near, far


def dsa_attention(qi2t, wt, klo, khi, qt, k, vt, bias_tables, q_off, n_keys, tq, big, tk=LANES):
    b, _, _, sq = qt.shape
    sk = k.shape[1]
    assert tk >= T5_MAX_DIST and q_off % tk == 0 and (tq % tk == 0 or sq == tq) and sk % (big * tk) == 0
    topk = min(TOPK_MAX, n_keys // 4)
    near, far = bias_tables
    ss = lax.broadcasted_iota(jnp.int32, (tk + 16, tk), 0)
    jj = lax.broadcasted_iota(jnp.int32, (tk + 16, tk), 1)
    tri = ((jj <= ss) | (ss >= tk)).astype(BF)
    whole = lambda n: pl.BlockSpec((None, sk, n), lambda bi, qi: (bi, 0, 0))
    heads_t = lambda h, d: pl.BlockSpec((None, h, d, tq), lambda bi, qi: (bi, 0, 0, qi))
    const = lambda a: pl.BlockSpec(a.shape, lambda bi, qi: (0,) * a.ndim)
    return pl.pallas_call(
        functools.partial(_dsa_kernel, tq=tq, tk=tk, big=big, q_off=q_off, topk=topk,
                          scale=HEAD_DIM ** -0.5, w_scale=H_IDX ** -0.5 * D_IDX ** -0.5),
        grid=(b, sq // tq),
        in_specs=[heads_t(H_IDX // 2, LANES), pl.BlockSpec((None, H_IDX, tq), lambda bi, qi: (bi, 0, qi)),
                  whole(LANES), whole(LANES), heads_t(H_DSA, HEAD_DIM), whole(H_DSA * HEAD_DIM),
                  pl.BlockSpec((None, H_DSA, sk // tk, HEAD_DIM, tk), lambda bi, qi: (bi, 0, 0, 0, 0)),
                  const(near), const(far), const(tri)],
        out_specs=pl.BlockSpec((None, tq, H_DSA * HEAD_DIM), lambda bi, qi: (bi, qi, 0)),
        out_shape=jax.ShapeDtypeStruct((b, sq, H_DSA * HEAD_DIM), BF),
        scratch_shapes=[pltpu.VMEM((sk // tk, tk, tq), jnp.int32),
                        pltpu.VMEM((H_DSA, 1, tq), F32), pltpu.VMEM((H_DSA, 1, tq), F32),
                        pltpu.VMEM((H_DSA, HEAD_DIM, tq), F32), pltpu.VMEM((8, tq), F32)],
        compiler_params=_cparams(2),
        name="dsa_attention",
    )(qi2t, wt, klo, khi, qt, k, vt, near, far, tri)


def _pad_cols(a, n):
    return jnp.pad(a, ((0, 0), (0, n - a.shape[1])))


def _rotate_half_cols(w):
    half = w.shape[1] // 2
    return jnp.concatenate([-w[:, half:], w[:, :half]], axis=1)


def prepare_layer_weights(w_in, q_norm, w_uq, kv_norm, w_ukv, w_branch):
    d = w_in.shape[0]
    q_lora, kv_lora = q_norm.shape[0], kv_norm.shape[0]
    sizes = (q_lora, kv_lora, ROPE_DIM, 3 * H_SB * HEAD_DIM, 3 * H_BAND * HEAD_DIM, 3 * H_DSA * HEAD_DIM,
             H_IDX * D_IDX, D_IDX, H_IDX, N_BRANCH * d)
    cols, start = [], 0
    for s in sizes:
        cols.append(w_in[:, start:start + s])
        start += s
    w_cq, w_ckv, w_kr, w_sb, w_bd, w_ds, w_iq, w_ik, w_iw, w_g = cols
    wm = jnp.concatenate([w_cq, w_ckv, _pad_cols(w_kr, LANES), _pad_cols(_rotate_half_cols(w_kr), LANES),
                          _pad_cols(w_ik, LANES), _pad_cols(w_iw, LANES)], axis=1).astype(BF)
    qa, qb = [], []
    hd = NOPE_DIM + ROPE_DIM
    for h in range(H_MLA):
        wh = w_uq[:, h * hd:(h + 1) * hd]
        qa += [wh[:, :NOPE_DIM], _pad_cols(wh[:, NOPE_DIM:], LANES)]
        qb.append(_pad_cols(_rotate_half_cols(wh[:, NOPE_DIM:]), LANES))
    wqa = jnp.concatenate(qa, axis=1).astype(BF)
    wqb = jnp.concatenate(qb, axis=1).astype(BF)
    kvd = NOPE_DIM + V_DIM
    wk = jnp.concatenate([w_ukv[:, h * kvd:h * kvd + NOPE_DIM] for h in range(H_MLA)], axis=1).astype(BF)
    wv = jnp.concatenate([w_ukv[:, h * kvd + NOPE_DIM:(h + 1) * kvd] for h in range(H_MLA)], axis=1).astype(BF)
    wg = jnp.transpose(w_g.reshape(d, N_BRANCH, d), (1, 0, 2)).astype(BF)
    return dict(wm=wm, wqa=wqa, wqb=wqb, wk=wk, wv=wv,
                w_sb=w_sb.astype(BF), w_bd=w_bd.astype(BF), w_ds=w_ds.astype(BF), w_iq=w_iq.astype(BF),
                wg=wg, wb=w_branch.astype(BF),
                qn=q_norm.reshape(1, -1), kvn=kv_norm.reshape(1, -1))


def rope_table(pos):
    half = ROPE_DIM // 2
    inv = ROPE_THETA ** (-jnp.arange(half, dtype=F32) / half)
    ang = pos.astype(F32)[:, None] * inv[None, :]
    cos, sin = jnp.cos(ang), jnp.sin(ang)
    z = jnp.zeros((pos.shape[0], LANES - ROPE_DIM), F32)
    return jnp.concatenate([cos, cos, z, sin, sin, z], axis=1)


def _kidx_pair(kidx):
    kb = kidx.astype(BF)
    return (jnp.pad(kb, ((0, 0), (0, 0), (0, LANES - D_IDX))),
            jnp.pad(kb, ((0, 0), (0, 0), (LANES - D_IDX, 0))))


def _with_past(past, new, pad_to):
    b = new.shape[0]
    a = jnp.concatenate([past.reshape(b, past.shape[1], -1).astype(BF), new.astype(BF)], axis=1)
    return jnp.pad(a, ((0, 0), (0, pad_to - a.shape[1]), (0, 0)))


def mixing_block(u, bsz, seq, lw, cs, past, band_mask, dsa_tables, w_out, h, stacks, g_ffn):
    t = bsz * seq
    ckv, krp, kidxp, idxw, qcat = proj_misc(u, bsz, seq, lw["wm"], lw["qn"], lw["kvn"], lw["wqa"], lw["wqb"], cs)
    hw = H_SB * HEAD_DIM

    layer, depth, prev = stacks
    prev = prev if prev is not None else (None,) * 4

    def stacked(w, name, prev_stack, other=("rows", BF)):
        return project(u, w, bsz, seq, [("stack_heads", F32), other], stack=(layer, depth, prev_stack), name=name)

    w_sb, w_bd = lw["w_sb"], lw["w_bd"]
    sb_q, = matmul(u, w_sb[:, :hw], (BF,), tm=1024, name="proj_q")
    sb_k, sb_kb = stacked(w_sb[:, hw:2 * hw], "proj_k", prev[0])
    sb_v, sb_vb = stacked(w_sb[:, 2 * hw:], "proj_v", prev[1])
    bd_q, = matmul(u, w_bd[:, :hw], (BF,), tm=1024, name="proj_q")
    bd_k, bd_kb = matmul(u, w_bd[:, hw:2 * hw], (F32, BF), tm=1024, name="proj_k")
    bd_v, bd_vb = matmul(u, w_bd[:, 2 * hw:], (F32, BF), tm=1024, name="proj_v")
    kr = krp[:, :ROPE_DIM]
    kidx = kidxp[:, :D_IDX]
    r3 = lambda a: a.reshape(bsz, seq, -1)
    band_pad = N_PREV_CHUNKS * CHUNK

    if past is None:
        w_ds = lw["w_ds"]
        ds_qt, = project(u, w_ds[:, :hw], bsz, seq, [("heads_t", BF)], name="proj_q")
        ds_k, ds_kb = stacked(w_ds[:, hw:2 * hw], "proj_k", prev[2])
        ds_v, ds_vt = stacked(w_ds[:, 2 * hw:], "proj_v", prev[3], ("blocks_t", BF))
        idx_qt, = project(u, lw["w_iq"], bsz, seq, [("heads_t", BF)], name="proj_idxq")
        kcat, vmla_t = mla_kv_up(ckv, krp, lw["wk"], lw["wv"], bsz, seq, PROMPT_TK)
        o_mla = mla_attention(qcat, r3(kcat), vmla_t, 0, PROMPT_TQ, PROMPT_TK)
        o_sb = sb_attention(r3(sb_q), r3(sb_kb), r3(sb_vb), 0, PROMPT_TQ, PROMPT_TK)
        front = ((0, 0), (band_pad, 0), (0, 0))
        o_bd = band_attention(r3(bd_q), jnp.pad(r3(bd_kb), front), jnp.pad(r3(bd_vb), front),
                              band_mask, PROMPT_TQ, -band_pad)
        klo, khi = _kidx_pair(r3(kidx))
        o_ds = dsa_attention(idx_qt, idxw, klo, khi, ds_qt, r3(ds_kb), ds_vt, dsa_tables,
                             0, seq, PROMPT_TQ, PROMPT_TK // LANES)
    else:
        w_ds = lw["w_ds"]
        ds_q, = matmul(u, w_ds[:, :hw], (BF,), tm=1024, name="proj_q")
        ds_k, ds_kb = stacked(w_ds[:, hw:2 * hw], "proj_k", prev[2])
        ds_v, ds_vb = stacked(w_ds[:, 2 * hw:], "proj_v", prev[3])
        idx_q, = matmul(u, lw["w_iq"], (BF,), tm=1024, name="proj_idxq")
        (p_ckv, p_kr, p_sbk, p_sbv, p_bdk, p_bdv, p_dsk, p_dsv, p_kidx) = past
        past_len = p_sbk.shape[1]
        total = past_len + seq
        pad_to = -(-total // SAMPLE_TK) * SAMPLE_TK
        ckv_all = jnp.concatenate([p_ckv, r3(ckv)], axis=1)
        krp_all = jnp.concatenate([jnp.pad(p_kr, ((0, 0), (0, 0), (0, LANES - ROPE_DIM))), r3(krp)], axis=1)
        rows = pad_to - total
        ckv_all = jnp.pad(ckv_all, ((0, 0), (0, rows), (0, 0))).reshape(bsz * pad_to, -1)
        krp_all = jnp.pad(krp_all, ((0, 0), (0, rows), (0, 0))).reshape(bsz * pad_to, -1)
        kcat, vmla_t = mla_kv_up(ckv_all, krp_all, lw["wk"], lw["wv"], bsz, pad_to, SAMPLE_TK)
        o_mla = mla_attention(_heads_to_lanes(r3(qcat), H_MLA), kcat.reshape(bsz, pad_to, -1), vmla_t,
                              past_len, seq, SAMPLE_TK)
        o_sb = sb_attention(r3(sb_q), _with_past(p_sbk, r3(sb_kb), pad_to), _with_past(p_sbv, r3(sb_vb), pad_to),
                            past_len, seq, SAMPLE_TK)
        band_len = _band_window(seq)
        o_bd = band_attention(r3(bd_q), _with_past(p_bdk, r3(bd_kb), band_len), _with_past(p_bdv, r3(bd_vb), band_len),
                              band_mask, seq, past_len - p_bdk.shape[1])
        kidx_all = jnp.pad(jnp.concatenate([p_kidx, r3(kidx)], axis=1), ((0, 0), (0, rows), (0, 0)))
        klo, khi = _kidx_pair(kidx_all)
        o_ds = dsa_attention(_heads_to_lanes(r3(idx_q), H_IDX // 2),
                             jnp.transpose(r3(idxw)[:, :, :H_IDX], (0, 2, 1)), klo, khi,
                             _heads_to_lanes(r3(ds_q), H_DSA), _with_past(p_dsk, r3(ds_kb), pad_to),
                             _chunked_transpose(_with_past(p_dsv, r3(ds_vb), pad_to), H_DSA, LANES),
                             dsa_tables, past_len, total, seq, SAMPLE_TK // LANES)

    f2 = lambda a: a.reshape(t, -1)
    merged = gate_merge(u, [f2(o_mla), f2(o_sb), f2(o_bd), f2(o_ds)], lw["wg"], lw["wb"])
    h_new, un = matmul_norm(merged, w_out, h, g_ffn, name="mix_out")
    return h_new, un, (ckv, kr, bd_k, bd_v, kidx), (sb_k, sb_v, ds_k, ds_v)


def dense_ffn(h, un, w1, w3, w2):
    act = swiglu_up(un, w1, w3)
    out, = matmul(act, w2, (F32,), residual=h, name="ffn_down")
    return out


def moe_ffn(h, un, router_pad, w1, w3, w2):
    n_experts = w1.shape[0]
    gate = router_gate(un, router_pad, n_experts)
    for e in range(n_experts):
        act = swiglu_up(un, w1[e], w3[e])
        h, = matmul(act, w2[e], (F32,), residual=h, gate=gate, gate_col=e, name="moe_down")
    return h


def layer_step(h, u, p, bsz, seq, cs, past, lw, band_mask, dsa_tables, fw, stacks, g_next, last):
    h, un, rows, head_stacks = mixing_block(u, bsz, seq, lw, cs, past, band_mask, dsa_tables, fw["w_out"], h,
                                            stacks, fw["g_ffn"])
    if fw["moe"]:
        h = moe_ffn(h, un, fw["router"], fw["w1"], fw["w3"], fw["w2"])
    else:
        h = dense_ffn(h, un, fw["w1"], fw["w3"], fw["w2"])
    un = rmsnorm(h, fw["g_ple"], BF)
    outs = ple_update(h, p, un, fw["ple_w"], fw["ple_gate_w"], g_next, F32 if last else BF, keep_h=not last)
    return (None if last else outs[0]), outs[-1], rows, head_stacks


def kernel(x_prompt, x_sample, p_prompt, p_sample, cache_mla_ckv, cache_mla_krope, cache_sb_k, cache_sb_v, cache_band_k, cache_band_v, cache_dsa_k, cache_dsa_v, cache_dsa_kidx, norm_mix, w_in, mla_q_norm, mla_w_uq, mla_kv_norm, mla_w_ukv, band_rel_bias, t5_rel_bias, w_branch, w_out, norm_ffn, ffn_w1, ffn_w3, ffn_w2, moe_router, moe_w1, moe_w3, moe_w2, norm_ple, ple_w, ple_gate_w, norm_final):
    depth = w_in.shape[0]
    bp, sp, d = x_prompt.shape
    bs, ss, _ = x_sample.shape
    past_len = cache_sb_k.shape[2]
    tm = 512
    cs_p = rope_table(jnp.arange(sp))
    cs_s = rope_table(past_len + (jnp.arange(tm) % ss))
    dsa_tables_p = dsa_bias_tables(t5_rel_bias, PROMPT_TQ)
    dsa_tables_s = dsa_bias_tables(t5_rel_bias, ss)
    hp = x_prompt.reshape(bp * sp, d)
    hs = x_sample.reshape(bs * ss, d)
    up = rmsnorm(hp, norm_mix[0], BF)
    us = rmsnorm(hs, norm_mix[0], BF)
    rows_p, rows_s = [], []
    stacks_p = stacks_s = None
    for i in range(depth):
        lw = prepare_layer_weights(w_in[i], mla_q_norm[i], mla_w_uq[i], mla_kv_norm[i], mla_w_ukv[i], w_branch[i])
        j = i // 2
        fw = dict(g_ffn=norm_ffn[i], g_ple=norm_ple[i], w_out=w_out[i].astype(BF),
                  ple_w=ple_w[i].astype(BF), ple_gate_w=ple_gate_w[i].astype(BF), moe=i % 2 == 1)
        if i % 2 == 0:
            fw.update(w1=ffn_w1[j].astype(BF), w3=ffn_w3[j].astype(BF), w2=ffn_w2[j].astype(BF))
        else:
            fw.update(router=_pad_cols(moe_router[j], LANES).astype(BF),
                      w1=moe_w1[j].astype(BF), w3=moe_w3[j].astype(BF), w2=moe_w2[j].astype(BF))
        past_i = (cache_mla_ckv[i], cache_mla_krope[i], cache_sb_k[i], cache_sb_v[i], cache_band_k[i],
                  cache_band_v[i], cache_dsa_k[i], cache_dsa_v[i], cache_dsa_kidx[i])
        last = i == depth - 1
        g_next = norm_final if last else norm_mix[i + 1]
        hp, up, rp, stacks_p = layer_step(hp, up, p_prompt[i].reshape(bp * sp, -1).astype(BF), bp, sp, cs_p, None,
                                          lw, band_bias_mask(band_rel_bias[i], PROMPT_TQ), dsa_tables_p, fw,
                                          (i, depth, stacks_p), g_next, last)
        hs, us, rs, stacks_s = layer_step(hs, us, p_sample[i].reshape(bs * ss, -1).astype(BF), bs, ss, cs_s, past_i,
                                          lw, band_bias_mask(band_rel_bias[i], ss), dsa_tables_s, fw,
                                          (i, depth, stacks_s), g_next, last)
        rows_p.append(rp)
        rows_s.append(rs)
    y_prompt = up.reshape(bp, sp, d)
    y_sample = us.reshape(bs, ss, d)

    keep = min(N_PREV_CHUNKS * CHUNK, sp)

    def stacked(rows, n, bsz, seq, heads=None, tail=None):
        out = []
        for r in rows:
            a = r[n].reshape(bsz, seq, -1)
            if tail is not None:
                a = a[:, seq - tail:]
            if heads is not None:
                a = a.reshape(a.shape[0], a.shape[1], heads, HEAD_DIM)
            out.append(a)
        return jnp.stack(out, axis=0)

    def both(n, heads=None, prompt_tail=None):
        return [stacked(rows_p, n, bp, sp, heads, prompt_tail), stacked(rows_s, n, bs, ss, heads)]

    def from_stacks(n):
        return [stacks_p[n].reshape(depth, bp, sp, -1, HEAD_DIM), stacks_s[n].reshape(depth, bs, ss, -1, HEAD_DIM)]

    res = [y_prompt, y_sample] + both(0) + both(1) + from_stacks(0) + from_stacks(1)
    res += both(2, H_BAND, keep) + both(3, H_BAND, keep) + from_stacks(2) + from_stacks(3) + both(4)
    return tuple(res)
```

```python
import functools
import math

import jax
import jax.numpy as jnp
from jax import lax
from jax.experimental import pallas as pl
from jax.experimental.pallas import tpu as pltpu

BF = jnp.bfloat16
F32 = jnp.float32

CHUNK = 64
CHUNK_SHIFT = 6
HEAD_DIM = 128
N_BRANCH = 4
H_MLA = 4
NOPE_DIM = 128
ROPE_DIM = 64
V_DIM = 128
ROPE_THETA = 10000.0
H_SB = 4
H_BAND = 4
N_PREV_CHUNKS = 8
REL_CLIP = 128
H_DSA = 4
H_IDX = 16
D_IDX = 64
TOPK_MAX = 256
T5_BUCKETS = 32
T5_MAX_DIST = 128
TOP_K_EXPERTS = 2
EPS = 1e-6

LANES = 128
PROMPT_TQ = 256
PROMPT_TK = 512
SAMPLE_TK = 384
VMEM_LIMIT = 56 * 1024 * 1024
NEG = -1e30
INT_MIN = -2147483648


def _cparams(n_axes):
    return pltpu.CompilerParams(dimension_semantics=("arbitrary",) * n_axes,
                                vmem_limit_bytes=VMEM_LIMIT)


def _tile(n, preferred):
    if n <= preferred:
        return n
    t = preferred - preferred % LANES
    while n % t:
        t -= LANES
    assert t > 0
    return t


def _dot(a, b):
    return jnp.dot(a, b, preferred_element_type=F32)


def _dot_nt(a, b):
    return lax.dot_general(a, b, (((1,), (1,)), ((), ())), preferred_element_type=F32)


def _sigmoid(x):
    return 1.0 / (1.0 + jnp.exp(-x))


def _rms(x, g):
    return x * lax.rsqrt(jnp.mean(x * x, axis=-1, keepdims=True) + EPS) * g


def _rmsnorm_kernel(x_ref, g_ref, o_ref):
    o_ref[...] = _rms(x_ref[...], g_ref[...]).astype(o_ref.dtype)


def rmsnorm(x, g, out_dtype, tm=512):
    t, d = x.shape
    tm = _tile(t, tm)
    return pl.pallas_call(
        _rmsnorm_kernel,
        grid=(t // tm,),
        in_specs=[pl.BlockSpec((tm, d), lambda i: (i, 0)),
                  pl.BlockSpec((1, d), lambda i: (0, 0))],
        out_specs=pl.BlockSpec((tm, d), lambda i: (i, 0)),
        out_shape=jax.ShapeDtypeStruct((t, d), out_dtype),
        compiler_params=_cparams(1),
        name="rmsnorm",
    )(x, g.reshape(1, d))


def _mm_kernel(*refs, has_res, gate_col, n_out):
    a_ref, w_ref = refs[0], refs[1]
    pos = 2
    res_ref = gate_ref = None
    if has_res:
        res_ref = refs[pos]
        pos += 1
    if gate_col is not None:
        gate_ref = refs[pos]
        pos += 1
    r = _dot(a_ref[...], w_ref[...])
    if gate_ref is not None:
        r = r * gate_ref[:, gate_col:gate_col + 1]
    if res_ref is not None:
        r = res_ref[...] + r
    for o_ref in refs[pos:pos + n_out]:
        o_ref[...] = r.astype(o_ref.dtype)


WEIGHT_BLOCK_BYTES = 6 * 1024 * 1024


def matmul(a, w, out_dtypes, residual=None, gate=None, gate_col=None, tm=512, name="mm"):
    m, k = a.shape
    n = w.shape[1]
    tm, tn = _tile(m, tm), _tile(n, WEIGHT_BLOCK_BYTES // (2 * k))
    in_specs = [pl.BlockSpec((tm, k), lambda j, i: (i, 0)),
                pl.BlockSpec((k, tn), lambda j, i: (0, j))]
    args = [a, w]
    if residual is not None:
        in_specs.append(pl.BlockSpec((tm, tn), lambda j, i: (i, j)))
        args.append(residual)
    if gate is not None:
        in_specs.append(pl.BlockSpec((tm, gate.shape[1]), lambda j, i: (i, 0)))
        args.append(gate)
    outs = pl.pallas_call(
        functools.partial(_mm_kernel, has_res=residual is not None,
                          gate_col=gate_col if gate is not None else None, n_out=len(out_dtypes)),
        grid=(n // tn, m // tm),
        in_specs=in_specs,
        out_specs=[pl.BlockSpec((tm, tn), lambda j, i: (i, j)) for _ in out_dtypes],
        out_shape=[jax.ShapeDtypeStruct((m, n), dt) for dt in out_dtypes],
        compiler_params=_cparams(2),
        name=name,
    )(*args)
    return outs


def _project_kernel(a_ref, w_ref, *refs, kinds, d, blk, n_prev, first_layer):
    o_refs = refs[n_prev:]
    r = _dot(a_ref[...], w_ref[...])
    tm, n = r.shape
    for o_ref, kind in zip(o_refs, kinds):
        if kind == "rows":
            o_ref[...] = r.astype(o_ref.dtype)
        elif kind == "stack_heads":
            slab = o_ref.at[0] if first_layer else o_ref
            for h in range(n // d):
                slab[:, h, :] = r[:, h * d:(h + 1) * d].astype(o_ref.dtype)
            if first_layer:
                for layer in range(1, o_ref.shape[0]):
                    o_ref[layer] = jnp.zeros(o_ref.shape[1:], o_ref.dtype)
        elif kind == "heads_t":
            for h in range(n // d):
                o_ref[h] = r[:, h * d:(h + 1) * d].T.astype(o_ref.dtype)
        else:
            for h in range(n // d):
                for j in range(tm // blk):
                    o_ref[h, j] = r[j * blk:(j + 1) * blk, h * d:(h + 1) * d].T.astype(o_ref.dtype)


def project(a, w, bsz, seq, outs, stack=None, d=LANES, blk=LANES, tm=1024, name="project"):
    m, k = a.shape
    n = w.shape[1]
    tm = _tile(m, tm)
    spt = _seq_tiles(seq, tm)
    kinds = tuple(kind for kind, _ in outs)
    assert n % d == 0 and 2 * k * n <= 2 * WEIGHT_BLOCK_BYTES
    assert (spt is not None and tm % blk == 0) or not {"heads_t", "blocks_t"} & set(kinds)
    layer, depth, prev = stack if stack is not None else (0, 1, None)
    specs, shapes, aliases, prev_args = [], [], {}, []
    for kind, dt in outs:
        if kind == "rows":
            specs.append(pl.BlockSpec((tm, n), lambda i: (i, 0)))
            shapes.append(jax.ShapeDtypeStruct((m, n), dt))
        elif kind == "stack_heads":
            if layer == 0:
                specs.append(pl.BlockSpec((depth, tm, n // d, d), lambda i: (0, i, 0, 0)))
            else:
                specs.append(pl.BlockSpec((None, tm, n // d, d), lambda i: (layer, i, 0, 0)))
                aliases[2 + len(prev_args)] = len(shapes)
                prev_args.append(prev)
            shapes.append(jax.ShapeDtypeStruct((depth, m, n // d, d), dt))
        elif kind == "heads_t":
            specs.append(pl.BlockSpec((None, n // d, d, tm), lambda i: (i // spt, 0, 0, i % spt)))
            shapes.append(jax.ShapeDtypeStruct((bsz, n // d, d, seq), dt))
        else:
            specs.append(pl.BlockSpec((None, n // d, tm // blk, d, blk), lambda i: (i // spt, 0, i % spt, 0, 0)))
            shapes.append(jax.ShapeDtypeStruct((bsz, n // d, seq // blk, d, blk), dt))
    return pl.pallas_call(
        functools.partial(_project_kernel, kinds=kinds, d=d, blk=blk, n_prev=len(prev_args),
                          first_layer=layer == 0),
        grid=(m // tm,),
        in_specs=[pl.BlockSpec((tm, k), lambda i: (i, 0)), pl.BlockSpec((k, n), lambda i: (0, 0))]
        + [pl.BlockSpec(memory_space=pl.ANY) for _ in prev_args],
        out_specs=specs,
        out_shape=shapes,
        input_output_aliases=aliases,
        compiler_params=_cparams(1),
        name=name,
    )(a, w, *prev_args)


def _swiglu_up_kernel(a_ref, w1_ref, w3_ref, o_ref):
    a = a_ref[...]
    x1 = _dot(a, w1_ref[...])
    x3 = _dot(a, w3_ref[...])
    o_ref[...] = (x1 * _sigmoid(x1) * x3).astype(o_ref.dtype)


def swiglu_up(a, w1, w3, tm=512):
    m, k = a.shape
    n = w1.shape[1]
    tm, tn = _tile(m, tm), _tile(n, WEIGHT_BLOCK_BYTES // (2 * k))
    return pl.pallas_call(
        _swiglu_up_kernel,
        grid=(n // tn, m // tm),
        in_specs=[pl.BlockSpec((tm, k), lambda j, i: (i, 0)),
                  pl.BlockSpec((k, tn), lambda j, i: (0, j)),
                  pl.BlockSpec((k, tn), lambda j, i: (0, j))],
        out_specs=pl.BlockSpec((tm, tn), lambda j, i: (i, j)),
        out_shape=jax.ShapeDtypeStruct((m, n), BF),
        compiler_params=_cparams(2),
        name="swiglu_up",
    )(a, w1, w3)


def _gate_merge_kernel(u_ref, o0_ref, o1_ref, o2_ref, o3_ref, wg_ref, wb_ref, out_ref):
    u = u_ref[...]
    acc = None
    for b, o_ref in enumerate((o0_ref, o1_ref, o2_ref, o3_ref)):
        t = _sigmoid(_dot(u, wg_ref[b])) * _dot(o_ref[...], wb_ref[b])
        acc = t if acc is None else acc + t
    out_ref[...] = acc.astype(out_ref.dtype)


def gate_merge(u, branches, wg, wb, tm=512, tn=512):
    t, d = u.shape
    bw = branches[0].shape[1]
    n = wg.shape[2]
    tm = min(tm, t)
    return pl.pallas_call(
        _gate_merge_kernel,
        grid=(n // tn, t // tm),
        in_specs=[pl.BlockSpec((tm, d), lambda j, i: (i, 0))]
        + [pl.BlockSpec((tm, bw), lambda j, i: (i, 0)) for _ in range(N_BRANCH)]
        + [pl.BlockSpec((N_BRANCH, d, tn), lambda j, i: (0, 0, j)),
           pl.BlockSpec((N_BRANCH, bw, tn), lambda j, i: (0, 0, j))],
        out_specs=pl.BlockSpec((tm, tn), lambda j, i: (i, j)),
        out_shape=jax.ShapeDtypeStruct((t, n), BF),
        compiler_params=_cparams(2),
        name="gate_merge",
    )(u, *branches, wg, wb)


def _ple_kernel(h_ref, p_ref, un_ref, wp_ref, wg_ref, g_ref, *o_refs):
    h = h_ref[...] + _dot(p_ref[...], wp_ref[...]) * _sigmoid(_dot(un_ref[...], wg_ref[...]))
    if len(o_refs) == 2:
        o_refs[0][...] = h
    o_refs[-1][...] = _rms(h, g_ref[...]).astype(o_refs[-1].dtype)


def ple_update(h, p, un, wp, wg, g_next, next_dtype, keep_h, tm=512):
    t, d = h.shape
    tm = min(tm, t)
    row = lambda n: pl.BlockSpec((tm, n), lambda i: (i, 0))
    full = lambda a: pl.BlockSpec(a.shape, lambda i: (0, 0))
    g_next = g_next.reshape(1, d)
    return pl.pallas_call(
        _ple_kernel,
        grid=(t // tm,),
        in_specs=[row(d), row(p.shape[1]), row(d), full(wp), full(wg), full(g_next)],
        out_specs=([row(d)] if keep_h else []) + [row(d)],
        out_shape=([jax.ShapeDtypeStruct((t, d), F32)] if keep_h else []) + [jax.ShapeDtypeStruct((t, d), next_dtype)],
        compiler_params=_cparams(1),
        name="ple_update",
    )(h, p, un, wp, wg, g_next)


def _mm_norm_kernel(a_ref, w_ref, res_ref, g_ref, h_ref, n_ref):
    h = res_ref[...] + _dot(a_ref[...], w_ref[...])
    h_ref[...] = h
    n_ref[...] = _rms(h, g_ref[...]).astype(n_ref.dtype)


def matmul_norm(a, w, residual, g, tm=512, name="mm_norm"):
    m, k = a.shape
    n = w.shape[1]
    tm = _tile(m, tm)
    row = lambda c: pl.BlockSpec((tm, c), lambda i: (i, 0))
    return pl.pallas_call(
        _mm_norm_kernel,
        grid=(m // tm,),
        in_specs=[row(k), pl.BlockSpec((k, n), lambda i: (0, 0)), row(n), pl.BlockSpec((1, n), lambda i: (0, 0))],
        out_specs=[row(n), row(n)],
        out_shape=[jax.ShapeDtypeStruct((m, n), F32), jax.ShapeDtypeStruct((m, n), BF)],
        compiler_params=_cparams(1),
        name=name,
    )(a, w, residual, g.reshape(1, n))


def _router_kernel(un_ref, wr_ref, g_ref, *, n_experts):
    logits = _dot(un_ref[...], wr_ref[...])
    lane = lax.broadcasted_iota(jnp.int32, logits.shape, 1).astype(F32)
    real = lane < n_experts
    logits = jnp.where(real, logits, NEG)
    e = jnp.where(real, jnp.exp(logits - jnp.max(logits, axis=1, keepdims=True)), 0.0)
    probs = e / jnp.sum(e, axis=1, keepdims=True)
    p1 = jnp.max(probs, axis=1, keepdims=True)
    i1 = jnp.min(jnp.where(probs == p1, lane, float(LANES)), axis=1, keepdims=True)
    first = lane == i1
    rest = jnp.where(first | ~real, -1.0, probs)
    p2 = jnp.max(rest, axis=1, keepdims=True)
    i2 = jnp.min(jnp.where(rest == p2, lane, float(LANES)), axis=1, keepdims=True)
    second = lane == i2
    denom = p1 + p2
    g_ref[...] = jnp.where(first, p1 / denom, 0.0) + jnp.where(second, p2 / denom, 0.0)


def router_gate(un, wr_pad, n_experts, tm=512):
    t, d = un.shape
    tm = min(tm, t)
    return pl.pallas_call(
        functools.partial(_router_kernel, n_experts=n_experts),
        grid=(t // tm,),
        in_specs=[pl.BlockSpec((tm, d), lambda i: (i, 0)),
                  pl.BlockSpec((d, LANES), lambda i: (0, 0))],
        out_specs=pl.BlockSpec((tm, LANES), lambda i: (i, 0)),
        out_shape=jax.ShapeDtypeStruct((t, LANES), F32),
        compiler_params=_cparams(1),
        name="router_gate",
    )(un, wr_pad)


Q_LORA_OFF = 0


def _proj_misc_kernel(u_ref, wm_ref, qn_ref, kvn_ref, wqa_ref, wqb_ref, cs_ref,
                      ckv_ref, krp_ref, kidx_ref, idxw_ref, qcat_ref, *, q_lora, kv_lora, lanes_are_tokens):
    x = _dot(u_ref[...], wm_ref[...])
    cos = cs_ref[:, 0:LANES]
    sin = cs_ref[:, LANES:2 * LANES]
    o = q_lora
    ckv_ref[...] = _rms(x[:, o:o + kv_lora], kvn_ref[...])
    o += kv_lora
    krp_ref[...] = x[:, o:o + LANES] * cos + x[:, o + LANES:o + 2 * LANES] * sin
    o += 2 * LANES
    kidx_ref[...] = x[:, o:o + LANES]
    idxw = x[:, o + LANES:o + 2 * LANES]
    if lanes_are_tokens:
        idxw_ref[...] = idxw.T[0:H_IDX]
    else:
        idxw_ref[...] = idxw
    cqn = _rms(x[:, 0:q_lora], qn_ref[...]).astype(BF)
    qa = _dot(cqn, wqa_ref[...])
    qb = _dot(cqn, wqb_ref[...])
    for h in range(H_MLA):
        lo = 2 * h * LANES
        nope = qa[:, lo:lo + LANES]
        rope = qa[:, lo + LANES:lo + 2 * LANES] * cos + qb[:, h * LANES:(h + 1) * LANES] * sin
        if lanes_are_tokens:
            qcat_ref[h, 0:LANES] = nope.T.astype(BF)
            qcat_ref[h, LANES:2 * LANES] = rope.T.astype(BF)
        else:
            qcat_ref[:, lo:lo + LANES] = nope.astype(BF)
            qcat_ref[:, lo + LANES:lo + 2 * LANES] = rope.astype(BF)


def _seq_tiles(seq, tm):
    return seq // tm if seq % tm == 0 else None


def proj_misc(u, bsz, seq, wm, qn, kvn, wqa, wqb, cs, tm=512):
    t, d = u.shape
    tm = min(tm, t)
    q_lora, kv_lora = qn.shape[1], kvn.shape[1]
    n_pos_blocks = cs.shape[0] // tm
    spt = _seq_tiles(seq, tm)
    row = lambda n: pl.BlockSpec((tm, n), lambda i: (i, 0))
    full = lambda a: pl.BlockSpec(a.shape, lambda i: (0, 0))
    if spt is None:
        q_specs = [row(LANES), row(2 * LANES * H_MLA)]
        q_shapes = [jax.ShapeDtypeStruct((t, LANES), F32), jax.ShapeDtypeStruct((t, 2 * LANES * H_MLA), BF)]
    else:
        q_specs = [pl.BlockSpec((None, H_IDX, tm), lambda i: (i // spt, 0, i % spt)),
                   pl.BlockSpec((None, H_MLA, 2 * LANES, tm), lambda i: (i // spt, 0, 0, i % spt))]
        q_shapes = [jax.ShapeDtypeStruct((bsz, H_IDX, seq), F32),
                    jax.ShapeDtypeStruct((bsz, H_MLA, 2 * LANES, seq), BF)]
    return pl.pallas_call(
        functools.partial(_proj_misc_kernel, q_lora=q_lora, kv_lora=kv_lora, lanes_are_tokens=spt is not None),
        grid=(t // tm,),
        in_specs=[row(d), full(wm), full(qn), full(kvn), full(wqa), full(wqb),
                  pl.BlockSpec((tm, 2 * LANES), lambda i: (i % n_pos_blocks, 0))],
        out_specs=[row(kv_lora), row(LANES), row(LANES)] + q_specs,
        out_shape=[jax.ShapeDtypeStruct((t, kv_lora), F32),
                   jax.ShapeDtypeStruct((t, LANES), F32),
                   jax.ShapeDtypeStruct((t, LANES), F32)] + q_shapes,
        compiler_params=_cparams(1),
        name="proj_misc",
    )(u, wm, qn, kvn, wqa, wqb, cs)


def _mla_kv_up_kernel(ckv_ref, krp_ref, wk_ref, wv_ref, kcat_ref, vt_ref):
    c = ckv_ref[...].astype(BF)
    kn = _dot(c, wk_ref[...])
    krp = krp_ref[...].astype(BF)
    for h in range(H_MLA):
        kcat_ref[:, 2 * h * LANES:(2 * h + 1) * LANES] = kn[:, h * LANES:(h + 1) * LANES].astype(BF)
        kcat_ref[:, (2 * h + 1) * LANES:(2 * h + 2) * LANES] = krp
    v = _dot(c, wv_ref[...])
    for h in range(H_MLA):
        vt_ref[h, 0] = v[:, h * V_DIM:(h + 1) * V_DIM].T.astype(BF)


def mla_kv_up(ckv, krp, wk, wv, bsz, seq, tm):
    t, c = ckv.shape
    spt = seq // tm
    assert seq % tm == 0
    row = lambda n: pl.BlockSpec((tm, n), lambda i: (i, 0))
    full = lambda a: pl.BlockSpec(a.shape, lambda i: (0, 0))
    return pl.pallas_call(
        _mla_kv_up_kernel,
        grid=(t // tm,),
        in_specs=[row(c), row(LANES), full(wk), full(wv)],
        out_specs=[row(2 * LANES * H_MLA),
                   pl.BlockSpec((None, H_MLA, 1, V_DIM, tm), lambda i: (i // spt, 0, i % spt, 0, 0))],
        out_shape=[jax.ShapeDtypeStruct((t, 2 * LANES * H_MLA), BF),
                   jax.ShapeDtypeStruct((bsz, H_MLA, spt, V_DIM, tm), BF)],
        compiler_params=_cparams(1),
        name="mla_kv_up",
    )(ckv, krp, wk, wv)


def _mla_attn_kernel(qt_ref, k_ref, vt_ref, o_ref, *, tq, tk, q_off, scale, heads):
    q0 = q_off + pl.program_id(2) * tq
    qpos = q0 + lax.broadcasted_iota(jnp.int32, (1, tq), 1)
    limit = (lax.shift_right_arithmetic(qpos, CHUNK_SHIFT) + 1) * CHUNK
    last_limit = ((q0 + tq - 1) // CHUNK + 1) * CHUNK
    n_chunks = (last_limit + tk - 1) // tk

    def step(kc, carry, masked):
        ks = pl.multiple_of(kc * tk, tk)
        if masked:
            visible = ks + lax.broadcasted_iota(jnp.int32, (tk, tq), 0) < limit
        scores = [_dot(k_ref[pl.ds(ks, tk), 2 * h * LANES:2 * (h + 1) * LANES], qt_ref[h]) for h in range(heads)]
        probs, stats = [], []
        for h in range(heads):
            m_prev, l_prev, _ = carry[h]
            s = scores[h] * scale
            if masked:
                s = jnp.where(visible, s, NEG)
            m_new = jnp.maximum(m_prev, jnp.max(s, axis=0, keepdims=True))
            p = jnp.exp(s - m_new)
            alpha = jnp.exp(m_prev - m_new)
            stats.append((m_new, alpha * l_prev + jnp.sum(p, axis=0, keepdims=True), alpha))
            probs.append(p.astype(BF))
        out = []
        for h in range(heads):
            m_new, l_new, alpha = stats[h]
            out.append((m_new, l_new, alpha * carry[h][2] + _dot(vt_ref[h, kc], probs[h])))
        return tuple(out)

    init = tuple((jnp.full((1, tq), NEG, F32), jnp.zeros((1, tq), F32), jnp.zeros((V_DIM, tq), F32))
                 for _ in range(heads))
    n_open = jnp.minimum(((q0 // CHUNK + 1) * CHUNK) // tk, n_chunks)
    state = lax.fori_loop(0, n_open, lambda kc, c: step(kc, c, False), init)
    final = lax.fori_loop(n_open, n_chunks, lambda kc, c: step(kc, c, True), state)
    for h in range(heads):
        _, l_fin, acc = final[h]
        o_ref[:, h * V_DIM:(h + 1) * V_DIM] = (acc / l_fin).T.astype(o_ref.dtype)


def _chunked_transpose(v, heads, tk):
    b, s, hd = v.shape
    d = hd // heads
    return jnp.transpose(v.reshape(b, s // tk, tk, heads, d), (0, 3, 1, 4, 2))


def _heads_to_lanes(a, heads):
    b, s, hd = a.shape
    return jnp.transpose(a.reshape(b, s, heads, hd // heads), (0, 2, 3, 1))


def mla_attention(qt, kcat, vt, q_off, tq, tk, heads=4):
    b, _, _, sq = qt.shape
    sk = kcat.shape[0] // b
    nq = sq // tq
    scale = (NOPE_DIM + ROPE_DIM) ** -0.5
    return pl.pallas_call(
        functools.partial(_mla_attn_kernel, tq=tq, tk=tk, q_off=q_off, scale=scale, heads=heads),
        grid=(b, H_MLA // heads, nq),
        in_specs=[pl.BlockSpec((None, heads, 2 * LANES, tq), lambda bi, h, qi: (bi, h, 0, qi)),
                  pl.BlockSpec((sk, heads * 2 * LANES), lambda bi, h, qi: (bi, h)),
                  pl.BlockSpec((None, heads, sk // tk, V_DIM, tk), lambda bi, h, qi: (bi, h, 0, 0, 0))],
        out_specs=pl.BlockSpec((tq, heads * V_DIM), lambda bi, h, qi: (bi * nq + qi, h)),
        out_shape=jax.ShapeDtypeStruct((b * sq, H_MLA * V_DIM), BF),
        compiler_params=_cparams(3),
        name="mla_attention",
    )(qt, kcat, vt)


def _split2(x):
    hi = x.astype(BF)
    return hi, (x - hi.astype(F32)).astype(BF)


def _sb_attn_kernel(q_ref, k_ref, v_ref, tri_ref, o_ref, *, tq, tk, q_off, scale, heads):
    q0 = q_off + pl.program_id(2) * tq
    qpos = q0 + lax.broadcasted_iota(jnp.int32, (tq, 1), 0)
    n_chunks = (q0 + tq - 1 + tk - 1) // tk
    tri2 = tri_ref[...]
    n_blk = tk // LANES
    hs = [slice(h * HEAD_DIM, (h + 1) * HEAD_DIM) for h in range(heads)]

    def step(kc, carry, diagonal):
        ks = pl.multiple_of(kc * tk, tk)
        if diagonal:
            strict = ks + lax.broadcasted_iota(jnp.int32, (1, tk), 1) < qpos
            causal = lambda x: jnp.where(strict, x, 0.0)
        else:
            causal = lambda x: x
        z = [_dot_nt(q_ref[:, hs[h]], k_ref[pl.ds(ks, tk), hs[h]]) for h in range(heads)]
        log_beta, parts = [], []
        for h in range(heads):
            zh = z[h] * scale
            sp = jnp.maximum(zh, 0.0) + jnp.log(1.0 + jnp.exp(-jnp.abs(zh)))
            log_beta.append(zh - sp)
            parts.append(_split2(causal(-sp)))
        after, later = [], []
        for h in range(heads):
            hi, lo = parts[h]
            run = carry[h][1]
            blocks = [None] * n_blk
            for blk in reversed(range(n_blk)):
                sl = slice(blk * LANES, (blk + 1) * LANES)
                sums = _dot(jnp.concatenate([hi[:, sl], lo[:, sl]], axis=1), tri2)
                blocks[blk] = sums[:, 0:LANES] + run
                run = run + sums[:, LANES:2 * LANES]
            after.append(jnp.concatenate(blocks, axis=1))
            later.append(run)
        weights = [causal(jnp.exp(log_beta[h] + after[h])).astype(BF) for h in range(heads)]
        return tuple((carry[h][0] + _dot(weights[h], v_ref[pl.ds(ks, tk), hs[h]]), later[h]) for h in range(heads))

    n_below = q0 // tk
    init = tuple((jnp.zeros((tq, HEAD_DIM), F32), jnp.zeros((tq, LANES), F32)) for _ in range(heads))
    state = lax.fori_loop(0, n_chunks - n_below, lambda it, c: step(n_chunks - 1 - it, c, True), init)
    final = lax.fori_loop(0, n_below, lambda it, c: step(n_below - 1 - it, c, False), state)
    for h in range(heads):
        o_ref[:, hs[h]] = final[h][0].astype(o_ref.dtype)


def sb_attention(q, k, v, b, q_off, tq, tk, heads=4):
    sq, sk = q.shape[0] // b, k.shape[0] // b
    nq = sq // tq
    assert sk % tk == 0 and tk % LANES == 0 and H_SB % heads == 0
    j = lax.broadcasted_iota(jnp.int32, (LANES, LANES), 0)
    s = lax.broadcasted_iota(jnp.int32, (LANES, LANES), 1)
    tri = jnp.concatenate([(j > s).astype(BF), jnp.ones((LANES, LANES), BF)], axis=1)
    tri = jnp.concatenate([tri, tri], axis=0)
    width = heads * HEAD_DIM
    return pl.pallas_call(
        functools.partial(_sb_attn_kernel, tq=tq, tk=tk, q_off=q_off, scale=HEAD_DIM ** -0.5, heads=heads),
        grid=(b, H_SB // heads, nq),
        in_specs=[pl.BlockSpec((tq, width), lambda bi, h, qi: (bi * nq + qi, h)),
                  pl.BlockSpec((sk, width), lambda bi, h, qi: (bi, h)),
                  pl.BlockSpec((sk, width), lambda bi, h, qi: (bi, h)),
                  pl.BlockSpec((2 * LANES, 2 * LANES), lambda bi, h, qi: (0, 0))],
        out_specs=pl.BlockSpec((tq, width), lambda bi, h, qi: (bi * nq + qi, h)),
        out_shape=jax.ShapeDtypeStruct((b * sq, H_SB * HEAD_DIM), BF),
        compiler_params=_cparams(3),
        name="sb_attention",
    )(q, k, v, tri)


def _band_attn_kernel(q_ref, k_ref, v_ref, bm_ref, o_ref, *, tq, win, kpos_base, scale):
    w0 = pl.multiple_of(pl.program_id(1) * tq, tq)
    kpos = kpos_base + w0 + lax.broadcasted_iota(jnp.int32, (1, win), 1)
    exists = kpos >= 0
    hs = [slice(h * HEAD_DIM, (h + 1) * HEAD_DIM) for h in range(H_BAND)]
    scores = [_dot_nt(q_ref[:, hs[h]], k_ref[pl.ds(w0, win), hs[h]]) for h in range(H_BAND)]
    probs, denoms = [], []
    for h in range(H_BAND):
        s = jnp.where(exists, scores[h] * scale + bm_ref[h], NEG)
        p = jnp.exp(s - jnp.max(s, axis=1, keepdims=True))
        denoms.append(jnp.sum(p, axis=1, keepdims=True))
        probs.append(p.astype(BF))
    for h in range(H_BAND):
        o = _dot(probs[h], v_ref[pl.ds(w0, win), hs[h]])
        o_ref[:, hs[h]] = (o / denoms[h]).astype(o_ref.dtype)


def _band_window(tq):
    return -(-(tq + N_PREV_CHUNKS * CHUNK) // LANES) * LANES


def band_bias_mask(rel_bias, tq):
    win = _band_window(tq)
    i = jnp.arange(tq)[:, None]
    j = jnp.arange(win)[None, :]
    rel = jnp.clip(i + N_PREV_CHUNKS * CHUNK - j, -REL_CLIP, REL_CLIP) + REL_CLIP
    ci, cj = i // CHUNK, j // CHUNK
    inside = (cj >= ci) & (cj <= ci + N_PREV_CHUNKS)
    return jnp.where(inside[None], _table_lookup(rel_bias, rel), NEG)


def band_attention(q, k_pad, v_pad, bias_mask, tq, kpos_base):
    b, skp, _ = k_pad.shape
    sq = q.shape[0] // b
    nq = sq // tq
    win = _band_window(tq)
    width = H_BAND * HEAD_DIM
    assert skp >= sq - tq + win
    return pl.pallas_call(
        functools.partial(_band_attn_kernel, tq=tq, win=win, kpos_base=kpos_base, scale=HEAD_DIM ** -0.5),
        grid=(b, nq),
        in_specs=[pl.BlockSpec((tq, width), lambda bi, qi: (bi * nq + qi, 0)),
                  pl.BlockSpec((None, skp, width), lambda bi, qi: (bi, 0, 0)),
                  pl.BlockSpec((None, skp, width), lambda bi, qi: (bi, 0, 0)),
                  pl.BlockSpec((H_BAND, tq, win), lambda bi, qi: (0, 0, 0))],
        out_specs=pl.BlockSpec((tq, width), lambda bi, qi: (bi * nq + qi, 0)),
        out_shape=jax.ShapeDtypeStruct((b * sq, width), BF),
        compiler_params=_cparams(2),
        name="band_attention",
    )(q, k_pad, v_pad, bias_mask)


def _sortable(x):
    i = lax.bitcast_convert_type(x, jnp.int32)
    return i ^ (lax.shift_right_arithmetic(i, 31) & 0x7FFFFFFF)


def _dsa_kernel(qi2t_ref, wt_ref, klo_ref, khi_ref, qt_ref, k_ref, vt_ref, bnear_ref, bfar_ref, tri_ref,
                o_ref, key_ref, m_ref, l_ref, acc_ref, eqc_ref,
                *, tq, tk, big, q_off, topk, scale, w_scale):
    wide = big * tk
    near_after = -(-tq // tk)
    q0 = q_off + pl.program_id(1) * tq
    qpos = q0 + lax.broadcasted_iota(jnp.int32, (1, tq), 1)
    limit = (lax.shift_right_arithmetic(qpos, CHUNK_SHIFT) + 1) * CHUNK
    last_limit = ((q0 + tq - 1) // CHUNK + 1) * CHUNK
    n_wide = (last_limit + wide - 1) // wide
    diag = q0 // tk

    w = wt_ref[...] * w_scale

    def score_body(c, carry):
        ks = pl.multiple_of(c * wide, wide)
        klo = klo_ref[pl.ds(ks, wide), :]
        khi = khi_ref[pl.ds(ks, wide), :]
        acc = jnp.zeros((wide, tq), F32)
        for pair in range(H_IDX // 2):
            q2 = qi2t_ref[pair]
            acc = acc + w[2 * pair:2 * pair + 1] * jnp.maximum(_dot(klo, q2), 0.0)
            acc = acc + w[2 * pair + 1:2 * pair + 2] * jnp.maximum(_dot(khi, q2), 0.0)
        kpos = ks + lax.broadcasted_iota(jnp.int32, (wide, tq), 0)
        keys = _sortable(jnp.where(kpos < limit, acc + 0.0, -jnp.inf))
        for blk in range(big):
            key_ref[c * big + blk] = keys[blk * tk:(blk + 1) * tk]
        return carry

    lax.fori_loop(0, n_wide, score_body, 0)

    def count_ge(cand):
        def body(c, cnt):
            for blk in range(big):
                cnt = cnt + jnp.where(key_ref[c * big + blk] >= cand, 1.0, 0.0)
            return cnt
        cnt = lax.fori_loop(0, n_wide, body, jnp.zeros((tk, tq), F32))
        return jnp.sum(cnt, axis=0, keepdims=True)

    bits_per_check = 4

    def bit_body(state):
        it, thr, at_thr = state
        for b in range(bits_per_check):
            cand = thr + lax.shift_left(jnp.int32(1), 31 - (it + b))
            cnt = count_ge(cand)
            take = cnt >= topk
            thr, at_thr = jnp.where(take, cand, thr), jnp.where(take, cnt, at_thr)
        return it + bits_per_check, thr, at_thr

    def bits_left(state):
        it, _, at_thr = state
        return (it < 32) & (jnp.max(at_thr) > topk)

    every_key = jnp.full((1, tq), 1.0, F32) * (n_wide * wide).astype(F32)
    _, thr, _ = lax.while_loop(bits_left, bit_body,
                               (jnp.int32(0), jnp.full((1, tq), INT_MIN, jnp.int32), every_key))
    n_above = count_ge(thr + 1)
    n_ties_kept = topk - n_above

    m_ref[...] = jnp.full(m_ref.shape, NEG, F32)
    l_ref[...] = jnp.zeros(l_ref.shape, F32)
    acc_ref[...] = jnp.zeros(acc_ref.shape, F32)
    eqc_ref[...] = jnp.zeros(eqc_ref.shape, F32)
    tri = tri_ref[...]
    hs = [slice(h * HEAD_DIM, (h + 1) * HEAD_DIM) for h in range(H_DSA)]

    def attend(first_blk, n_blk, bias_of_head, below_tile):
        width = n_blk * tk
        ks = pl.multiple_of(first_blk * tk, tk)
        key = jnp.concatenate([key_ref[first_blk + i] for i in range(n_blk)], axis=0)
        eq = key == thr
        eq_bf = jnp.where(eq, 1.0, 0.0).astype(BF)
        seen = eqc_ref[0:1, :]
        rank = [None] * n_blk
        for i in range(n_blk):
            counts = _dot(tri, eq_bf[i * tk:(i + 1) * tk])
            rank[i] = counts[0:tk] + seen
            seen = seen + counts[tk:tk + 1]
        eqc_ref[0:1, :] = seen
        sel = (key > thr) | (eq & (jnp.concatenate(rank, axis=0) <= n_ties_kept))
        if not below_tile:
            sel = sel & (ks + lax.broadcasted_iota(jnp.int32, (width, tq), 0) < limit)
        scores = [_dot(k_ref[pl.ds(ks, width), hs[h]], qt_ref[h]) for h in range(H_DSA)]
        probs, alphas = [], []
        for h in range(H_DSA):
            s = jnp.where(sel, scores[h] * scale + bias_of_head(h), NEG)
            m_prev = m_ref[h]
            m_new = jnp.maximum(m_prev, jnp.max(s, axis=0, keepdims=True))
            p = jnp.where(sel, jnp.exp(s - m_new), 0.0)
            alpha = jnp.exp(m_prev - m_new)
            l_ref[h] = alpha * l_ref[h] + jnp.sum(p, axis=0, keepdims=True)
            m_ref[h] = m_new
            probs.append(p.astype(BF))
            alphas.append(alpha)
        for h in range(H_DSA):
            vt = jnp.concatenate([vt_ref[h, first_blk + i] for i in range(n_blk)], axis=1)
            acc_ref[h] = alphas[h] * acc_ref[h] + _dot(vt, probs[h])

    far_bias = lambda h: bfar_ref[h][:, 0:1]
    n_far_wide = jnp.maximum(q0 - tk, 0) // wide

    def far_wide_body(c, carry):
        attend(c * big, big, far_bias, True)
        return carry

    lax.fori_loop(0, n_far_wide, far_wide_body, 0)

    def far_body(kc, carry):
        attend(kc, 1, far_bias, True)
        return carry

    lax.fori_loop(n_far_wide * big, jnp.maximum(diag - 1, n_far_wide * big), far_body, 0)

    @pl.when(diag >= 1)
    def _():
        attend(diag - 1, 1, lambda h: bnear_ref[0, h], True)

    for d in range(near_after):
        attend(diag + d, 1, lambda h, d=d: bnear_ref[d + 1, h], False)

    for h in range(H_DSA):
        o_ref[:, hs[h]] = (acc_ref[h] / l_ref[h]).T.astype(o_ref.dtype)


def t5_bucket(rel):
    half = T5_BUCKETS // 2
    max_exact = half // 2
    n = jnp.abs(rel)
    nf = jnp.maximum(n, 1).astype(F32)
    large = max_exact + (jnp.log(nf / max_exact) / math.log(T5_MAX_DIST / max_exact)
                         * (half - max_exact)).astype(jnp.int32)
    large = jnp.minimum(large, half - 1)
    return jnp.where(rel > 0, half, 0) + jnp.where(n < max_exact, n, large)


def _table_lookup(table, idx):
    onehot = jax.nn.one_hot(idx, table.shape[0], dtype=F32)
    out = jnp.einsum("...n,nh->...h", onehot, table.astype(F32), precision=lax.Precision.HIGHEST)
    return jnp.moveaxis(out, -1, 0)


def dsa_bias_tables(t5_table, tq, tk=LANES):
    j = jnp.arange(tk)[:, None]
    i = jnp.arange(tq)[None, :]
    near = jnp.stack([_table_lookup(t5_table, t5_bucket(d * tk + j - i)) for d in range(-1, -(-tq // tk))],
                     axis=0)
    far_rel = -jnp.ones((1, tk), jnp.int32) * (2 * tk)
    far = _table_lookup(t5_table, t5_bucket(far_rel))
    return near, far


def dsa_attention(qi2t, wt, klo, khi, qt, k, vt, bias_tables, q_off, n_keys, tq, big, tk=LANES):
    b, _, _, sq = qt.shape
    sk = k.shape[0] // b
    nq = sq // tq
    assert tk >= T5_MAX_DIST and q_off % tk == 0 and (tq % tk == 0 or sq == tq) and sk % (big * tk) == 0
    topk = min(TOPK_MAX, n_keys // 4)
    near, far = bias_tables
    ss = lax.broadcasted_iota(jnp.int32, (tk + 16, tk), 0)
    jj = lax.broadcasted_iota(jnp.int32, (tk + 16, tk), 1)
    tri = ((jj <= ss) | (ss >= tk)).astype(BF)
    whole = lambda n: pl.BlockSpec((None, sk, n), lambda bi, qi: (bi, 0, 0))
    heads_t = lambda h, d: pl.BlockSpec((None, h, d, tq), lambda bi, qi: (bi, 0, 0, qi))
    const = lambda a: pl.BlockSpec(a.shape, lambda bi, qi: (0,) * a.ndim)
    return pl.pallas_call(
        functools.partial(_dsa_kernel, tq=tq, tk=tk, big=big, q_off=q_off, topk=topk,
                          scale=HEAD_DIM ** -0.5, w_scale=H_IDX ** -0.5 * D_IDX ** -0.5),
        grid=(b, nq),
        in_specs=[heads_t(H_IDX // 2, LANES), pl.BlockSpec((None, H_IDX, tq), lambda bi, qi: (bi, 0, qi)),
                  whole(LANES), whole(LANES), heads_t(H_DSA, HEAD_DIM),
                  pl.BlockSpec((sk, H_DSA * HEAD_DIM), lambda bi, qi: (bi, 0)),
                  pl.BlockSpec((None, H_DSA, sk // tk, HEAD_DIM, tk), lambda bi, qi: (bi, 0, 0, 0, 0)),
                  const(near), const(far), const(tri)],
        out_specs=pl.BlockSpec((tq, H_DSA * HEAD_DIM), lambda bi, qi: (bi * nq + qi, 0)),
        out_shape=jax.ShapeDtypeStruct((b * sq, H_DSA * HEAD_DIM), BF),
        scratch_shapes=[pltpu.VMEM((sk // tk, tk, tq), jnp.int32),
                        pltpu.VMEM((H_DSA, 1, tq), F32), pltpu.VMEM((H_DSA, 1, tq), F32),
                        pltpu.VMEM((H_DSA, HEAD_DIM, tq), F32), pltpu.VMEM((8, tq), F32)],
        compiler_params=_cparams(2),
        name="dsa_attention",
    )(qi2t, wt, klo, khi, qt, k, vt, near, far, tri)


def _pad_cols(a, n):
    return jnp.pad(a, ((0, 0), (0, n - a.shape[1])))


def _rotate_half_cols(w):
    half = w.shape[1] // 2
    return jnp.concatenate([-w[:, half:], w[:, :half]], axis=1)


def prepare_layer_weights(w_in, q_norm, w_uq, kv_norm, w_ukv, w_branch):
    d = w_in.shape[0]
    q_lora, kv_lora = q_norm.shape[0], kv_norm.shape[0]
    sizes = (q_lora, kv_lora, ROPE_DIM, 3 * H_SB * HEAD_DIM, 3 * H_BAND * HEAD_DIM, 3 * H_DSA * HEAD_DIM,
             H_IDX * D_IDX, D_IDX, H_IDX, N_BRANCH * d)
    cols, start = [], 0
    for s in sizes:
        cols.append(w_in[:, start:start + s])
        start += s
    w_cq, w_ckv, w_kr, w_sb, w_bd, w_ds, w_iq, w_ik, w_iw, w_g = cols
    wm = jnp.concatenate([w_cq, w_ckv, _pad_cols(w_kr, LANES), _pad_cols(_rotate_half_cols(w_kr), LANES),
                          _pad_cols(w_ik, LANES), _pad_cols(w_iw, LANES)], axis=1).astype(BF)
    qa, qb = [], []
    hd = NOPE_DIM + ROPE_DIM
    for h in range(H_MLA):
        wh = w_uq[:, h * hd:(h + 1) * hd]
        qa += [wh[:, :NOPE_DIM], _pad_cols(wh[:, NOPE_DIM:], LANES)]
        qb.append(_pad_cols(_rotate_half_cols(wh[:, NOPE_DIM:]), LANES))
    wqa = jnp.concatenate(qa, axis=1).astype(BF)
    wqb = jnp.concatenate(qb, axis=1).astype(BF)
    kvd = NOPE_DIM + V_DIM
    wk = jnp.concatenate([w_ukv[:, h * kvd:h * kvd + NOPE_DIM] for h in range(H_MLA)], axis=1).astype(BF)
    wv = jnp.concatenate([w_ukv[:, h * kvd + NOPE_DIM:(h + 1) * kvd] for h in range(H_MLA)], axis=1).astype(BF)
    wg = jnp.transpose(w_g.reshape(d, N_BRANCH, d), (1, 0, 2)).astype(BF)
    return dict(wm=wm, wqa=wqa, wqb=wqb, wk=wk, wv=wv,
                w_sb=w_sb.astype(BF), w_bd=w_bd.astype(BF), w_ds=w_ds.astype(BF), w_iq=w_iq.astype(BF),
                wg=wg, wb=w_branch.astype(BF),
                qn=q_norm.reshape(1, -1), kvn=kv_norm.reshape(1, -1))


def rope_table(pos):
    half = ROPE_DIM // 2
    inv = ROPE_THETA ** (-jnp.arange(half, dtype=F32) / half)
    ang = pos.astype(F32)[:, None] * inv[None, :]
    cos, sin = jnp.cos(ang), jnp.sin(ang)
    z = jnp.zeros((pos.shape[0], LANES - ROPE_DIM), F32)
    return jnp.concatenate([cos, cos, z, sin, sin, z], axis=1)


def _kidx_pair(kidx):
    kb = kidx.astype(BF)
    return (jnp.pad(kb, ((0, 0), (0, 0), (0, LANES - D_IDX))),
            jnp.pad(kb, ((0, 0), (0, 0), (LANES - D_IDX, 0))))


def _with_past(past, new, pad_to):
    b = new.shape[0]
    a = jnp.concatenate([past.reshape(b, past.shape[1], -1).astype(BF), new.astype(BF)], axis=1)
    return jnp.pad(a, ((0, 0), (0, pad_to - a.shape[1]), (0, 0)))


def mixing_block(u, bsz, seq, lw, cs, past, band_mask, dsa_tables, w_out, h, stacks, g_ffn):
    t = bsz * seq
    ckv, krp, kidxp, idxw, qcat = proj_misc(u, bsz, seq, lw["wm"], lw["qn"], lw["kvn"], lw["wqa"], lw["wqb"], cs)
    hw = H_SB * HEAD_DIM

    layer, depth, prev = stacks
    prev = prev if prev is not None else (None,) * 4

    def stacked(w, name, prev_stack, other=("rows", BF)):
        return project(u, w, bsz, seq, [("stack_heads", F32), other], stack=(layer, depth, prev_stack), name=name)

    w_sb, w_bd = lw["w_sb"], lw["w_bd"]
    sb_q, = matmul(u, w_sb[:, :hw], (BF,), tm=1024, name="proj_q")
    sb_k, sb_kb = stacked(w_sb[:, hw:2 * hw], "proj_k", prev[0])
    sb_v, sb_vb = stacked(w_sb[:, 2 * hw:], "proj_v", prev[1])
    bd_q, = matmul(u, w_bd[:, :hw], (BF,), tm=1024, name="proj_q")
    bd_k, bd_kb = matmul(u, w_bd[:, hw:2 * hw], (F32, BF), tm=1024, name="proj_k")
    bd_v, bd_vb = matmul(u, w_bd[:, 2 * hw:], (F32, BF), tm=1024, name="proj_v")
    kr = krp[:, :ROPE_DIM]
    kidx = kidxp[:, :D_IDX]
    r3 = lambda a: a.reshape(bsz, seq, -1)
    band_pad = N_PREV_CHUNKS * CHUNK

    if past is None:
        w_ds = lw["w_ds"]
        ds_qt, = project(u, w_ds[:, :hw], bsz, seq, [("heads_t", BF)], name="proj_q")
        ds_k, ds_kb = stacked(w_ds[:, hw:2 * hw], "proj_k", prev[2])
        ds_v, ds_vt = stacked(w_ds[:, 2 * hw:], "proj_v", prev[3], ("blocks_t", BF))
        idx_qt, = project(u, lw["w_iq"], bsz, seq, [("heads_t", BF)], name="proj_idxq")
        kcat, vmla_t = mla_kv_up(ckv, krp, lw["wk"], lw["wv"], bsz, seq, PROMPT_TK)
        o_mla = mla_attention(qcat, kcat, vmla_t, 0, PROMPT_TQ, PROMPT_TK)
        o_sb = sb_attention(sb_q, sb_kb, sb_vb, bsz, 0, PROMPT_TQ, PROMPT_TK)
        front = ((0, 0), (band_pad, 0), (0, 0))
        o_bd = band_attention(bd_q, jnp.pad(r3(bd_kb), front), jnp.pad(r3(bd_vb), front),
                              band_mask, PROMPT_TQ, -band_pad)
        klo, khi = _kidx_pair(r3(kidx))
        o_ds = dsa_attention(idx_qt, idxw, klo, khi, ds_qt, ds_kb, ds_vt, dsa_tables,
                             0, seq, PROMPT_TQ, PROMPT_TK // LANES)
    else:
        w_ds = lw["w_ds"]
        ds_q, = matmul(u, w_ds[:, :hw], (BF,), tm=1024, name="proj_q")
        ds_k, ds_kb = stacked(w_ds[:, hw:2 * hw], "proj_k", prev[2])
        ds_v, ds_vb = stacked(w_ds[:, 2 * hw:], "proj_v", prev[3])
        idx_q, = matmul(u, lw["w_iq"], (BF,), tm=1024, name="proj_idxq")
        (p_ckv, p_kr, p_sbk, p_sbv, p_bdk, p_bdv, p_dsk, p_dsv, p_kidx) = past
        past_len = p_sbk.shape[1]
        total = past_len + seq
        pad_to = -(-total // SAMPLE_TK) * SAMPLE_TK
        ckv_all = jnp.concatenate([p_ckv, r3(ckv)], axis=1)
        krp_all = jnp.concatenate([jnp.pad(p_kr, ((0, 0), (0, 0), (0, LANES - ROPE_DIM))), r3(krp)], axis=1)
        rows = pad_to - total
        ckv_all = jnp.pad(ckv_all, ((0, 0), (0, rows), (0, 0))).reshape(bsz * pad_to, -1)
        krp_all = jnp.pad(krp_all, ((0, 0), (0, rows), (0, 0))).reshape(bsz * pad_to, -1)
        kcat, vmla_t = mla_kv_up(ckv_all, krp_all, lw["wk"], lw["wv"], bsz, pad_to, SAMPLE_TK)
        rows2 = lambda a: a.reshape(bsz * a.shape[1], -1)
        o_mla = mla_attention(_heads_to_lanes(r3(qcat), H_MLA), kcat, vmla_t, past_len, seq, SAMPLE_TK)
        o_sb = sb_attention(sb_q, rows2(_with_past(p_sbk, r3(sb_kb), pad_to)),
                            rows2(_with_past(p_sbv, r3(sb_vb), pad_to)), bsz, past_len, seq, SAMPLE_TK)
        band_len = _band_window(seq)
        o_bd = band_attention(bd_q, _with_past(p_bdk, r3(bd_kb), band_len), _with_past(p_bdv, r3(bd_vb), band_len),
                              band_mask, seq, past_len - p_bdk.shape[1])
        kidx_all = jnp.pad(jnp.concatenate([p_kidx, r3(kidx)], axis=1), ((0, 0), (0, rows), (0, 0)))
        klo, khi = _kidx_pair(kidx_all)
        o_ds = dsa_attention(_heads_to_lanes(r3(idx_q), H_IDX // 2),
                             jnp.transpose(r3(idxw)[:, :, :H_IDX], (0, 2, 1)), klo, khi,
                             _heads_to_lanes(r3(ds_q), H_DSA), rows2(_with_past(p_dsk, r3(ds_kb), pad_to)),
                             _chunked_transpose(_with_past(p_dsv, r3(ds_vb), pad_to), H_DSA, LANES),
                             dsa_tables, past_len, total, seq, SAMPLE_TK // LANES)

    merged = gate_merge(u, [o_mla, o_sb, o_bd, o_ds], lw["wg"], lw["wb"])
    h_new, un = matmul_norm(merged, w_out, h, g_ffn, name="mix_out")
    return h_new, un, (ckv, kr, bd_k, bd_v, kidx), (sb_k, sb_v, ds_k, ds_v)


def dense_ffn(h, un, w1, w3, w2):
    act = swiglu_up(un, w1, w3)
    out, = matmul(act, w2, (F32,), residual=h, name="ffn_down")
    return out


def moe_ffn(h, un, router_pad, w1, w3, w2):
    n_experts = w1.shape[0]
    gate = router_gate(un, router_pad, n_experts)
    for e in range(n_experts):
        act = swiglu_up(un, w1[e], w3[e])
        h, = matmul(act, w2[e], (F32,), residual=h, gate=gate, gate_col=e, name="moe_down")
    return h


def layer_step(h, u, p, bsz, seq, cs, past, lw, band_mask, dsa_tables, fw, stacks, g_next, last):
    h, un, rows, head_stacks = mixing_block(u, bsz, seq, lw, cs, past, band_mask, dsa_tables, fw["w_out"], h,
                                            stacks, fw["g_ffn"])
    if fw["moe"]:
        h = moe_ffn(h, un, fw["router"], fw["w1"], fw["w3"], fw["w2"])
    else:
        h = dense_ffn(h, un, fw["w1"], fw["w3"], fw["w2"])
    un = rmsnorm(h, fw["g_ple"], BF)
    outs = ple_update(h, p, un, fw["ple_w"], fw["ple_gate_w"], g_next, F32 if last else BF, keep_h=not last)
    return (None if last else outs[0]), outs[-1], rows, head_stacks


def kernel(x_prompt, x_sample, p_prompt, p_sample, cache_mla_ckv, cache_mla_krope, cache_sb_k, cache_sb_v, cache_band_k, cache_band_v, cache_dsa_k, cache_dsa_v, cache_dsa_kidx, norm_mix, w_in, mla_q_norm, mla_w_uq, mla_kv_norm, mla_w_ukv, band_rel_bias, t5_rel_bias, w_branch, w_out, norm_ffn, ffn_w1, ffn_w3, ffn_w2, moe_router, moe_w1, moe_w3, moe_w2, norm_ple, ple_w, ple_gate_w, norm_final):
    depth = w_in.shape[0]
    bp, sp, d = x_prompt.shape
    bs, ss, _ = x_sample.shape
    past_len = cache_sb_k.shape[2]
    tm = 512
    cs_p = rope_table(jnp.arange(sp))
    cs_s = rope_table(past_len + (jnp.arange(tm) % ss))
    dsa_tables_p = dsa_bias_tables(t5_rel_bias, PROMPT_TQ)
    dsa_tables_s = dsa_bias_tables(t5_rel_bias, ss)
    hp = x_prompt.reshape(bp * sp, d)
    hs = x_sample.reshape(bs * ss, d)
    up = rmsnorm(hp, norm_mix[0], BF)
    us = rmsnorm(hs, norm_mix[0], BF)
    rows_p, rows_s = [], []
    stacks_p = stacks_s = None
    for i in range(depth):
        lw = prepare_layer_weights(w_in[i], mla_q_norm[i], mla_w_uq[i], mla_kv_norm[i], mla_w_ukv[i], w_branch[i])
        j = i // 2
        fw = dict(g_ffn=norm_ffn[i], g_ple=norm_ple[i], w_out=w_out[i].astype(BF),
                  ple_w=ple_w[i].astype(BF), ple_gate_w=ple_gate_w[i].astype(BF), moe=i % 2 == 1)
        if i % 2 == 0:
            fw.update(w1=ffn_w1[j].astype(BF), w3=ffn_w3[j].astype(BF), w2=ffn_w2[j].astype(BF))
        else:
            fw.update(router=_pad_cols(moe_router[j], LANES).astype(BF),
                      w1=moe_w1[j].astype(BF), w3=moe_w3[j].astype(BF), w2=moe_w2[j].astype(BF))
        past_i = (cache_mla_ckv[i], cache_mla_krope[i], cache_sb_k[i], cache_sb_v[i], cache_band_k[i],
                  cache_band_v[i], cache_dsa_k[i], cache_dsa_v[i], cache_dsa_kidx[i])
        last = i == depth - 1
        g_next = norm_final if last else norm_mix[i + 1]
        hp, up, rp, stacks_p = layer_step(hp, up, p_prompt[i].reshape(bp * sp, -1).astype(BF), bp, sp, cs_p, None,
                                          lw, band_bias_mask(band_rel_bias[i], PROMPT_TQ), dsa_tables_p, fw,
                                          (i, depth, stacks_p), g_next, last)
        hs, us, rs, stacks_s = layer_step(hs, us, p_sample[i].reshape(bs * ss, -1).astype(BF), bs, ss, cs_s, past_i,
                                          lw, band_bias_mask(band_rel_bias[i], ss), dsa_tables_s, fw,
                                          (i, depth, stacks_s), g_next, last)
        rows_p.append(rp)
        rows_s.append(rs)
    y_prompt = up.reshape(bp, sp, d)
    y_sample = us.reshape(bs, ss, d)

    keep = min(N_PREV_CHUNKS * CHUNK, sp)

    def stacked(rows, n, bsz, seq, heads=None, tail=None):
        out = []
        for r in rows:
            a = r[n].reshape(bsz, seq, -1)
            if tail is not None:
                a = a[:, seq - tail:]
            if heads is not None:
                a = a.reshape(a.shape[0], a.shape[1], heads, HEAD_DIM)
            out.append(a)
        return jnp.stack(out, axis=0)

    def both(n, heads=None, prompt_tail=None):
        return [stacked(rows_p, n, bp, sp, heads, prompt_tail), stacked(rows_s, n, bs, ss, heads)]

    def from_stacks(n):
        return [stacks_p[n].reshape(depth, bp, sp, -1, HEAD_DIM), stacks_s[n].reshape(depth, bs, ss, -1, HEAD_DIM)]

    res = [y_prompt, y_sample] + both(0) + both(1) + from_stacks(0) + from_stacks(1)
    res += both(2, H_BAND, keep) + both(3, H_BAND, keep) + from_stacks(2) + from_stacks(3) + both(4)
    return tuple(res)
```

```python
import functools
import math

import jax
import jax.numpy as jnp
from jax import lax
from jax.experimental import pallas as pl
from jax.experimental.pallas import tpu as pltpu

BF = jnp.bfloat16
F32 = jnp.float32

CHUNK = 64
CHUNK_SHIFT = 6
HEAD_DIM = 128
N_BRANCH = 4
H_MLA = 4
NOPE_DIM = 128
ROPE_DIM = 64
V_DIM = 128
ROPE_THETA = 10000.0
H_SB = 4
H_BAND = 4
N_PREV_CHUNKS = 8
REL_CLIP = 128
H_DSA = 4
H_IDX = 16
D_IDX = 64
TOPK_MAX = 256
T5_BUCKETS = 32
T5_MAX_DIST = 128
TOP_K_EXPERTS = 2
EPS = 1e-6

LANES = 128
PROMPT_TQ = 256
PROMPT_TK = 512
SAMPLE_TK = 384
VMEM_LIMIT = 56 * 1024 * 1024
NEG = -1e30
INT_MIN = -2147483648


def _cparams(n_axes):
    return pltpu.CompilerParams(dimension_semantics=("arbitrary",) * n_axes,
                                vmem_limit_bytes=VMEM_LIMIT)


def _tile(n, preferred):
    if n <= preferred:
        return n
    t = preferred - preferred % LANES
    while n % t:
        t -= LANES
    assert t > 0
    return t


def _dot(a, b):
    return jnp.dot(a, b, preferred_element_type=F32)


def _dot_nt(a, b):
    return lax.dot_general(a, b, (((1,), (1,)), ((), ())), preferred_element_type=F32)


def _sigmoid(x):
    return 1.0 / (1.0 + jnp.exp(-x))


def _rms(x, g):
    return x * lax.rsqrt(jnp.mean(x * x, axis=-1, keepdims=True) + EPS) * g


def _rmsnorm_kernel(x_ref, g_ref, o_ref):
    o_ref[...] = _rms(x_ref[...], g_ref[...]).astype(o_ref.dtype)


def rmsnorm(x, g, out_dtype, tm=512):
    t, d = x.shape
    tm = _tile(t, tm)
    return pl.pallas_call(
        _rmsnorm_kernel,
        grid=(t // tm,),
        in_specs=[pl.BlockSpec((tm, d), lambda i: (i, 0)),
                  pl.BlockSpec((1, d), lambda i: (0, 0))],
        out_specs=pl.BlockSpec((tm, d), lambda i: (i, 0)),
        out_shape=jax.ShapeDtypeStruct((t, d), out_dtype),
        compiler_params=_cparams(1),
        name="rmsnorm",
    )(x, g.reshape(1, d))


def _mm_kernel(*refs, has_res, gate_col, n_out):
    a_ref, w_ref = refs[0], refs[1]
    pos = 2
    res_ref = gate_ref = None
    if has_res:
        res_ref = refs[pos]
        pos += 1
    if gate_col is not None:
        gate_ref = refs[pos]
        pos += 1
    r = _dot(a_ref[...], w_ref[...])
    if gate_ref is not None:
        r = r * gate_ref[:, gate_col:gate_col + 1]
    if res_ref is not None:
        r = res_ref[...] + r
    for o_ref in refs[pos:pos + n_out]:
        o_ref[...] = r.astype(o_ref.dtype)


WEIGHT_BLOCK_BYTES = 6 * 1024 * 1024


def matmul(a, w, out_dtypes, residual=None, gate=None, gate_col=None, tm=512, name="mm"):
    m, k = a.shape
    n = w.shape[1]
    tm, tn = _tile(m, tm), _tile(n, WEIGHT_BLOCK_BYTES // (2 * k))
    in_specs = [pl.BlockSpec((tm, k), lambda j, i: (i, 0)),
                pl.BlockSpec((k, tn), lambda j, i: (0, j))]
    args = [a, w]
    if residual is not None:
        in_specs.append(pl.BlockSpec((tm, tn), lambda j, i: (i, j)))
        args.append(residual)
    if gate is not None:
        in_specs.append(pl.BlockSpec((tm, gate.shape[1]), lambda j, i: (i, 0)))
        args.append(gate)
    outs = pl.pallas_call(
        functools.partial(_mm_kernel, has_res=residual is not None,
                          gate_col=gate_col if gate is not None else None, n_out=len(out_dtypes)),
        grid=(n // tn, m // tm),
        in_specs=in_specs,
        out_specs=[pl.BlockSpec((tm, tn), lambda j, i: (i, j)) for _ in out_dtypes],
        out_shape=[jax.ShapeDtypeStruct((m, n), dt) for dt in out_dtypes],
        compiler_params=_cparams(2),
        name=name,
    )(*args)
    return outs


def _project_kernel(a_ref, w_ref, *refs, kinds, d, blk, n_prev, first_layer):
    o_refs = refs[n_prev:]
    r = _dot(a_ref[...], w_ref[...])
    tm, n = r.shape
    for o_ref, kind in zip(o_refs, kinds):
        if kind == "rows":
            o_ref[...] = r.astype(o_ref.dtype)
        elif kind == "stack_heads":
            slab = o_ref.at[0] if first_layer else o_ref
            slab[...] = r.reshape(tm, n // d, d).astype(o_ref.dtype)
            if first_layer:
                for layer in range(1, o_ref.shape[0]):
                    o_ref[layer] = jnp.zeros(o_ref.shape[1:], o_ref.dtype)
        elif kind == "heads_t":
            for h in range(n // d):
                o_ref[h] = r[:, h * d:(h + 1) * d].T.astype(o_ref.dtype)
        else:
            for h in range(n // d):
                for j in range(tm // blk):
                    o_ref[h, j] = r[j * blk:(j + 1) * blk, h * d:(h + 1) * d].T.astype(o_ref.dtype)


def project(a, w, bsz, seq, outs, stack=None, d=LANES, blk=LANES, tm=1024, name="project"):
    m, k = a.shape
    n = w.shape[1]
    tm = _tile(m, tm)
    spt = _seq_tiles(seq, tm)
    kinds = tuple(kind for kind, _ in outs)
    assert n % d == 0 and 2 * k * n <= 2 * WEIGHT_BLOCK_BYTES
    assert (spt is not None and tm % blk == 0) or not {"heads_t", "blocks_t"} & set(kinds)
    layer, depth, prev = stack if stack is not None else (0, 1, None)
    specs, shapes, aliases, prev_args = [], [], {}, []
    for kind, dt in outs:
        if kind == "rows":
            specs.append(pl.BlockSpec((tm, n), lambda i: (i, 0)))
            shapes.append(jax.ShapeDtypeStruct((m, n), dt))
        elif kind == "stack_heads":
            if layer == 0:
                specs.append(pl.BlockSpec((depth, tm, n // d, d), lambda i: (0, i, 0, 0)))
            else:
                specs.append(pl.BlockSpec((None, tm, n // d, d), lambda i: (layer, i, 0, 0)))
                aliases[2 + len(prev_args)] = len(shapes)
                prev_args.append(prev)
            shapes.append(jax.ShapeDtypeStruct((depth, m, n // d, d), dt))
        elif kind == "heads_t":
            specs.append(pl.BlockSpec((None, n // d, d, tm), lambda i: (i // spt, 0, 0, i % spt)))
            shapes.append(jax.ShapeDtypeStruct((bsz, n // d, d, seq), dt))
        else:
            specs.append(pl.BlockSpec((None, n // d, tm // blk, d, blk), lambda i: (i // spt, 0, i % spt, 0, 0)))
            shapes.append(jax.ShapeDtypeStruct((bsz, n // d, seq // blk, d, blk), dt))
    return pl.pallas_call(
        functools.partial(_project_kernel, kinds=kinds, d=d, blk=blk, n_prev=len(prev_args),
                          first_layer=layer == 0),
        grid=(m // tm,),
        in_specs=[pl.BlockSpec((tm, k), lambda i: (i, 0)), pl.BlockSpec((k, n), lambda i: (0, 0))]
        + [pl.BlockSpec(memory_space=pl.ANY) for _ in prev_args],
        out_specs=specs,
        out_shape=shapes,
        input_output_aliases=aliases,
        compiler_params=_cparams(1),
        name=name,
    )(a, w, *prev_args)


def _swiglu_up_kernel(a_ref, w1_ref, w3_ref, o_ref):
    a = a_ref[...]
    x1 = _dot(a, w1_ref[...])
    x3 = _dot(a, w3_ref[...])
    o_ref[...] = (x1 * _sigmoid(x1) * x3).astype(o_ref.dtype)


def swiglu_up(a, w1, w3, tm=512):
    m, k = a.shape
    n = w1.shape[1]
    tm, tn = _tile(m, tm), _tile(n, WEIGHT_BLOCK_BYTES // (2 * k))
    return pl.pallas_call(
        _swiglu_up_kernel,
        grid=(n // tn, m // tm),
        in_specs=[pl.BlockSpec((tm, k), lambda j, i: (i, 0)),
                  pl.BlockSpec((k, tn), lambda j, i: (0, j)),
                  pl.BlockSpec((k, tn), lambda j, i: (0, j))],
        out_specs=pl.BlockSpec((tm, tn), lambda j, i: (i, j)),
        out_shape=jax.ShapeDtypeStruct((m, n), BF),
        compiler_params=_cparams(2),
        name="swiglu_up",
    )(a, w1, w3)


def _gate_merge_kernel(u_ref, o0_ref, o1_ref, o2_ref, o3_ref, wg_ref, wb_ref, out_ref):
    u = u_ref[...]
    acc = None
    for b, o_ref in enumerate((o0_ref, o1_ref, o2_ref, o3_ref)):
        t = _sigmoid(_dot(u, wg_ref[b])) * _dot(o_ref[...], wb_ref[b])
        acc = t if acc is None else acc + t
    out_ref[...] = acc.astype(out_ref.dtype)


def gate_merge(u, branches, wg, wb, tm=512, tn=512):
    t, d = u.shape
    bw = branches[0].shape[1]
    n = wg.shape[2]
    tm = min(tm, t)
    return pl.pallas_call(
        _gate_merge_kernel,
        grid=(n // tn, t // tm),
        in_specs=[pl.BlockSpec((tm, d), lambda j, i: (i, 0))]
        + [pl.BlockSpec((tm, bw), lambda j, i: (i, 0)) for _ in range(N_BRANCH)]
        + [pl.BlockSpec((N_BRANCH, d, tn), lambda j, i: (0, 0, j)),
           pl.BlockSpec((N_BRANCH, bw, tn), lambda j, i: (0, 0, j))],
        out_specs=pl.BlockSpec((tm, tn), lambda j, i: (i, j)),
        out_shape=jax.ShapeDtypeStruct((t, n), BF),
        compiler_params=_cparams(2),
        name="gate_merge",
    )(u, *branches, wg, wb)


def _ple_kernel(h_ref, p_ref, un_ref, wp_ref, wg_ref, g_ref, *o_refs):
    h = h_ref[...] + _dot(p_ref[...], wp_ref[...]) * _sigmoid(_dot(un_ref[...], wg_ref[...]))
    if len(o_refs) == 2:
        o_refs[0][...] = h
    o_refs[-1][...] = _rms(h, g_ref[...]).astype(o_refs[-1].dtype)


def ple_update(h, p, un, wp, wg, g_next, next_dtype, keep_h, tm=512):
    t, d = h.shape
    tm = min(tm, t)
    row = lambda n: pl.BlockSpec((tm, n), lambda i: (i, 0))
    full = lambda a: pl.BlockSpec(a.shape, lambda i: (0, 0))
    g_next = g_next.reshape(1, d)
    return pl.pallas_call(
        _ple_kernel,
        grid=(t // tm,),
        in_specs=[row(d), row(p.shape[1]), row(d), full(wp), full(wg), full(g_next)],
        out_specs=([row(d)] if keep_h else []) + [row(d)],
        out_shape=([jax.ShapeDtypeStruct((t, d), F32)] if keep_h else []) + [jax.ShapeDtypeStruct((t, d), next_dtype)],
        compiler_params=_cparams(1),
        name="ple_update",
    )(h, p, un, wp, wg, g_next)


def _mm_norm_kernel(a_ref, w_ref, res_ref, g_ref, h_ref, n_ref):
    h = res_ref[...] + _dot(a_ref[...], w_ref[...])
    h_ref[...] = h
    n_ref[...] = _rms(h, g_ref[...]).astype(n_ref.dtype)


def matmul_norm(a, w, residual, g, tm=512, name="mm_norm"):
    m, k = a.shape
    n = w.shape[1]
    tm = _tile(m, tm)
    row = lambda c: pl.BlockSpec((tm, c), lambda i: (i, 0))
    return pl.pallas_call(
        _mm_norm_kernel,
        grid=(m // tm,),
        in_specs=[row(k), pl.BlockSpec((k, n), lambda i: (0, 0)), row(n), pl.BlockSpec((1, n), lambda i: (0, 0))],
        out_specs=[row(n), row(n)],
        out_shape=[jax.ShapeDtypeStruct((m, n), F32), jax.ShapeDtypeStruct((m, n), BF)],
        compiler_params=_cparams(1),
        name=name,
    )(a, w, residual, g.reshape(1, n))


def _router_kernel(un_ref, wr_ref, g_ref, *, n_experts):
    logits = _dot(un_ref[...], wr_ref[...])
    lane = lax.broadcasted_iota(jnp.int32, logits.shape, 1).astype(F32)
    real = lane < n_experts
    logits = jnp.where(real, logits, NEG)
    e = jnp.where(real, jnp.exp(logits - jnp.max(logits, axis=1, keepdims=True)), 0.0)
    probs = e / jnp.sum(e, axis=1, keepdims=True)
    p1 = jnp.max(probs, axis=1, keepdims=True)
    i1 = jnp.min(jnp.where(probs == p1, lane, float(LANES)), axis=1, keepdims=True)
    first = lane == i1
    rest = jnp.where(first | ~real, -1.0, probs)
    p2 = jnp.max(rest, axis=1, keepdims=True)
    i2 = jnp.min(jnp.where(rest == p2, lane, float(LANES)), axis=1, keepdims=True)
    second = lane == i2
    denom = p1 + p2
    g_ref[...] = jnp.where(first, p1 / denom, 0.0) + jnp.where(second, p2 / denom, 0.0)


def router_gate(un, wr_pad, n_experts, tm=512):
    t, d = un.shape
    tm = min(tm, t)
    return pl.pallas_call(
        functools.partial(_router_kernel, n_experts=n_experts),
        grid=(t // tm,),
        in_specs=[pl.BlockSpec((tm, d), lambda i: (i, 0)),
                  pl.BlockSpec((d, LANES), lambda i: (0, 0))],
        out_specs=pl.BlockSpec((tm, LANES), lambda i: (i, 0)),
        out_shape=jax.ShapeDtypeStruct((t, LANES), F32),
        compiler_params=_cparams(1),
        name="router_gate",
    )(un, wr_pad)


def _proj_misc_kernel(u_ref, wm_ref, qn_ref, kvn_ref, wqa_ref, wqb_ref, cs_ref,
                      ckv_ref, krp_ref, kidx_ref, idxw_ref, qcat_ref, *, q_lora, kv_lora, lanes_are_tokens):
    x = _dot(u_ref[...], wm_ref[...])
    cos = cs_ref[:, 0:LANES]
    sin = cs_ref[:, LANES:2 * LANES]
    o = q_lora
    ckv_ref[...] = _rms(x[:, o:o + kv_lora], kvn_ref[...])
    o += kv_lora
    krp_ref[...] = x[:, o:o + LANES] * cos + x[:, o + LANES:o + 2 * LANES] * sin
    o += 2 * LANES
    kidx_ref[...] = x[:, o:o + LANES]
    idxw = x[:, o + LANES:o + 2 * LANES]
    if lanes_are_tokens:
        idxw_ref[...] = idxw.T[0:H_IDX]
    else:
        idxw_ref[...] = idxw
    cqn = _rms(x[:, 0:q_lora], qn_ref[...]).astype(BF)
    qa = _dot(cqn, wqa_ref[...])
    qb = _dot(cqn, wqb_ref[...])
    for h in range(H_MLA):
        lo = 2 * h * LANES
        nope = qa[:, lo:lo + LANES]
        rope = qa[:, lo + LANES:lo + 2 * LANES] * cos + qb[:, h * LANES:(h + 1) * LANES] * sin
        if lanes_are_tokens:
            qcat_ref[h, 0:LANES] = nope.T.astype(BF)
            qcat_ref[h, LANES:2 * LANES] = rope.T.astype(BF)
        else:
            qcat_ref[:, lo:lo + LANES] = nope.astype(BF)
            qcat_ref[:, lo + LANES:lo + 2 * LANES] = rope.astype(BF)


def _seq_tiles(seq, tm):
    return seq // tm if seq % tm == 0 else None


def proj_misc(u, bsz, seq, wm, qn, kvn, wqa, wqb, cs, tm=512):
    t, d = u.shape
    tm = min(tm, t)
    q_lora, kv_lora = qn.shape[1], kvn.shape[1]
    n_pos_blocks = cs.shape[0] // tm
    spt = _seq_tiles(seq, tm)
    row = lambda n: pl.BlockSpec((tm, n), lambda i: (i, 0))
    full = lambda a: pl.BlockSpec(a.shape, lambda i: (0, 0))
    if spt is None:
        q_specs = [row(LANES), row(2 * LANES * H_MLA)]
        q_shapes = [jax.ShapeDtypeStruct((t, LANES), F32), jax.ShapeDtypeStruct((t, 2 * LANES * H_MLA), BF)]
    else:
        q_specs = [pl.BlockSpec((None, H_IDX, tm), lambda i: (i // spt, 0, i % spt)),
                   pl.BlockSpec((None, H_MLA, 2 * LANES, tm), lambda i: (i // spt, 0, 0, i % spt))]
        q_shapes = [jax.ShapeDtypeStruct((bsz, H_IDX, seq), F32),
                    jax.ShapeDtypeStruct((bsz, H_MLA, 2 * LANES, seq), BF)]
    return pl.pallas_call(
        functools.partial(_proj_misc_kernel, q_lora=q_lora, kv_lora=kv_lora, lanes_are_tokens=spt is not None),
        grid=(t // tm,),
        in_specs=[row(d), full(wm), full(qn), full(kvn), full(wqa), full(wqb),
                  pl.BlockSpec((tm, 2 * LANES), lambda i: (i % n_pos_blocks, 0))],
        out_specs=[row(kv_lora), row(LANES), row(LANES)] + q_specs,
        out_shape=[jax.ShapeDtypeStruct((t, kv_lora), F32),
                   jax.ShapeDtypeStruct((t, LANES), F32),
                   jax.ShapeDtypeStruct((t, LANES), F32)] + q_shapes,
        compiler_params=_cparams(1),
        name="proj_misc",
    )(u, wm, qn, kvn, wqa, wqb, cs)


def _mla_kv_up_kernel(ckv_ref, krp_ref, wk_ref, wv_ref, kcat_ref, vt_ref):
    c = ckv_ref[...].astype(BF)
    kn = _dot(c, wk_ref[...])
    krp = krp_ref[...].astype(BF)
    for h in range(H_MLA):
        kcat_ref[:, 2 * h * LANES:(2 * h + 1) * LANES] = kn[:, h * LANES:(h + 1) * LANES].astype(BF)
        kcat_ref[:, (2 * h + 1) * LANES:(2 * h + 2) * LANES] = krp
    v = _dot(c, wv_ref[...])
    for h in range(H_MLA):
        vt_ref[h, 0] = v[:, h * V_DIM:(h + 1) * V_DIM].T.astype(BF)


def mla_kv_up(ckv, krp, wk, wv, bsz, seq, tm):
    t, c = ckv.shape
    spt = seq // tm
    assert seq % tm == 0
    row = lambda n: pl.BlockSpec((tm, n), lambda i: (i, 0))
    full = lambda a: pl.BlockSpec(a.shape, lambda i: (0, 0))
    return pl.pallas_call(
        _mla_kv_up_kernel,
        grid=(t // tm,),
        in_specs=[row(c), row(LANES), full(wk), full(wv)],
        out_specs=[row(2 * LANES * H_MLA),
                   pl.BlockSpec((None, H_MLA, 1, V_DIM, tm), lambda i: (i // spt, 0, i % spt, 0, 0))],
        out_shape=[jax.ShapeDtypeStruct((t, 2 * LANES * H_MLA), BF),
                   jax.ShapeDtypeStruct((bsz, H_MLA, spt, V_DIM, tm), BF)],
        compiler_params=_cparams(1),
        name="mla_kv_up",
    )(ckv, krp, wk, wv)


def _mla_attn_kernel(qt_ref, k_ref, vt_ref, o_ref, *, tq, tk, q_off, scale, heads):
    q0 = q_off + pl.program_id(2) * tq
    qpos = q0 + lax.broadcasted_iota(jnp.int32, (1, tq), 1)
    limit = (lax.shift_right_arithmetic(qpos, CHUNK_SHIFT) + 1) * CHUNK
    last_limit = ((q0 + tq - 1) // CHUNK + 1) * CHUNK
    n_chunks = (last_limit + tk - 1) // tk

    def step(kc, carry, masked):
        ks = pl.multiple_of(kc * tk, tk)
        if masked:
            visible = ks + lax.broadcasted_iota(jnp.int32, (tk, tq), 0) < limit
        scores = [_dot(k_ref[pl.ds(ks, tk), 2 * h * LANES:2 * (h + 1) * LANES], qt_ref[h]) for h in range(heads)]
        probs, stats = [], []
        for h in range(heads):
            m_prev, l_prev, _ = carry[h]
            s = scores[h] * scale
            if masked:
                s = jnp.where(visible, s, NEG)
            m_new = jnp.maximum(m_prev, jnp.max(s, axis=0, keepdims=True))
            p = jnp.exp(s - m_new)
            alpha = jnp.exp(m_prev - m_new)
            stats.append((m_new, alpha * l_prev + jnp.sum(p, axis=0, keepdims=True), alpha))
            probs.append(p.astype(BF))
        out = []
        for h in range(heads):
            m_new, l_new, alpha = stats[h]
            out.append((m_new, l_new, alpha * carry[h][2] + _dot(vt_ref[h, kc], probs[h])))
        return tuple(out)

    init = tuple((jnp.full((1, tq), NEG, F32), jnp.zeros((1, tq), F32), jnp.zeros((V_DIM, tq), F32))
                 for _ in range(heads))
    n_open = jnp.minimum(((q0 // CHUNK + 1) * CHUNK) // tk, n_chunks)
    state = lax.fori_loop(0, n_open, lambda kc, c: step(kc, c, False), init)
    final = lax.fori_loop(n_open, n_chunks, lambda kc, c: step(kc, c, True), state)
    for h in range(heads):
        _, l_fin, acc = final[h]
        o_ref[:, h * V_DIM:(h + 1) * V_DIM] = (acc / l_fin).T.astype(o_ref.dtype)


def _chunked_transpose(v, heads, tk):
    b, s, hd = v.shape
    d = hd // heads
    return jnp.transpose(v.reshape(b, s // tk, tk, heads, d), (0, 3, 1, 4, 2))


def _heads_to_lanes(a, heads):
    b, s, hd = a.shape
    return jnp.transpose(a.reshape(b, s, heads, hd // heads), (0, 2, 3, 1))


def mla_attention(qt, kcat, vt, q_off, tq, tk, heads=4):
    b, _, _, sq = qt.shape
    sk = kcat.shape[0] // b
    nq = sq // tq
    scale = (NOPE_DIM + ROPE_DIM) ** -0.5
    return pl.pallas_call(
        functools.partial(_mla_attn_kernel, tq=tq, tk=tk, q_off=q_off, scale=scale, heads=heads),
        grid=(b, H_MLA // heads, nq),
        in_specs=[pl.BlockSpec((None, heads, 2 * LANES, tq), lambda bi, h, qi: (bi, h, 0, qi)),
                  pl.BlockSpec((sk, heads * 2 * LANES), lambda bi, h, qi: (bi, h)),
                  pl.BlockSpec((None, heads, sk // tk, V_DIM, tk), lambda bi, h, qi: (bi, h, 0, 0, 0))],
        out_specs=pl.BlockSpec((tq, heads * V_DIM), lambda bi, h, qi: (bi * nq + qi, h)),
        out_shape=jax.ShapeDtypeStruct((b * sq, H_MLA * V_DIM), BF),
        compiler_params=_cparams(3),
        name="mla_attention",
    )(qt, kcat, vt)


def _split2(x):
    hi = x.astype(BF)
    return hi, (x - hi.astype(F32)).astype(BF)


def _sb_attn_kernel(q_ref, k_ref, v_ref, tri_ref, o_ref, *, tq, tk, q_off, scale, heads):
    q0 = q_off + pl.program_id(2) * tq
    qpos = q0 + lax.broadcasted_iota(jnp.int32, (tq, 1), 0)
    n_chunks = (q0 + tq - 1 + tk - 1) // tk
    tri2 = tri_ref[...]
    n_blk = tk // LANES
    hs = [slice(h * HEAD_DIM, (h + 1) * HEAD_DIM) for h in range(heads)]

    def step(kc, carry, diagonal):
        ks = pl.multiple_of(kc * tk, tk)
        if diagonal:
            strict = ks + lax.broadcasted_iota(jnp.int32, (1, tk), 1) < qpos
            causal = lambda x: jnp.where(strict, x, 0.0)
        else:
            causal = lambda x: x
        z = [_dot_nt(q_ref[:, hs[h]], k_ref[pl.ds(ks, tk), hs[h]]) for h in range(heads)]
        log_beta, parts = [], []
        for h in range(heads):
            zh = z[h] * scale
            sp = jnp.maximum(zh, 0.0) + jnp.log(1.0 + jnp.exp(-jnp.abs(zh)))
            log_beta.append(zh - sp)
            parts.append(_split2(causal(-sp)))
        after, later = [], []
        for h in range(heads):
            hi, lo = parts[h]
            run = carry[h][1]
            blocks = [None] * n_blk
            for blk in reversed(range(n_blk)):
                sl = slice(blk * LANES, (blk + 1) * LANES)
                sums = _dot(jnp.concatenate([hi[:, sl], lo[:, sl]], axis=1), tri2)
                blocks[blk] = sums[:, 0:LANES] + run
                run = run + sums[:, LANES:2 * LANES]
            after.append(jnp.concatenate(blocks, axis=1))
            later.append(run)
        weights = [causal(jnp.exp(log_beta[h] + after[h])).astype(BF) for h in range(heads)]
        return tuple((carry[h][0] + _dot(weights[h], v_ref[pl.ds(ks, tk), hs[h]]), later[h]) for h in range(heads))

    n_below = q0 // tk
    init = tuple((jnp.zeros((tq, HEAD_DIM), F32), jnp.zeros((tq, LANES), F32)) for _ in range(heads))
    state = lax.fori_loop(0, n_chunks - n_below, lambda it, c: step(n_chunks - 1 - it, c, True), init)
    final = lax.fori_loop(0, n_below, lambda it, c: step(n_below - 1 - it, c, False), state)
    for h in range(heads):
        o_ref[:, hs[h]] = final[h][0].astype(o_ref.dtype)


def sb_attention(q, k, v, b, q_off, tq, tk, heads=4):
    sq, sk = q.shape[0] // b, k.shape[0] // b
    nq = sq // tq
    assert sk % tk == 0 and tk % LANES == 0 and H_SB % heads == 0
    j = lax.broadcasted_iota(jnp.int32, (LANES, LANES), 0)
    s = lax.broadcasted_iota(jnp.int32, (LANES, LANES), 1)
    tri = jnp.concatenate([(j > s).astype(BF), jnp.ones((LANES, LANES), BF)], axis=1)
    tri = jnp.concatenate([tri, tri], axis=0)
    width = heads * HEAD_DIM
    return pl.pallas_call(
        functools.partial(_sb_attn_kernel, tq=tq, tk=tk, q_off=q_off, scale=HEAD_DIM ** -0.5, heads=heads),
        grid=(b, H_SB // heads, nq),
        in_specs=[pl.BlockSpec((tq, width), lambda bi, h, qi: (bi * nq + qi, h)),
                  pl.BlockSpec((sk, width), lambda bi, h, qi: (bi, h)),
                  pl.BlockSpec((sk, width), lambda bi, h, qi: (bi, h)),
                  pl.BlockSpec((2 * LANES, 2 * LANES), lambda bi, h, qi: (0, 0))],
        out_specs=pl.BlockSpec((tq, width), lambda bi, h, qi: (bi * nq + qi, h)),
        out_shape=jax.ShapeDtypeStruct((b * sq, H_SB * HEAD_DIM), BF),
        compiler_params=_cparams(3),
        name="sb_attention",
    )(q, k, v, tri)


def _band_attn_kernel(q_ref, k_ref, v_ref, bm_ref, o_ref, *, tq, win, kpos_base, scale):
    w0 = pl.multiple_of(pl.program_id(1) * tq, tq)
    kpos = kpos_base + w0 + lax.broadcasted_iota(jnp.int32, (1, win), 1)
    exists = kpos >= 0
    hs = [slice(h * HEAD_DIM, (h + 1) * HEAD_DIM) for h in range(H_BAND)]
    scores = [_dot_nt(q_ref[:, hs[h]], k_ref[pl.ds(w0, win), hs[h]]) for h in range(H_BAND)]
    probs, denoms = [], []
    for h in range(H_BAND):
        s = jnp.where(exists, scores[h] * scale + bm_ref[h], NEG)
        p = jnp.exp(s - jnp.max(s, axis=1, keepdims=True))
        denoms.append(jnp.sum(p, axis=1, keepdims=True))
        probs.append(p.astype(BF))
    for h in range(H_BAND):
        o = _dot(probs[h], v_ref[pl.ds(w0, win), hs[h]])
        o_ref[:, hs[h]] = (o / denoms[h]).astype(o_ref.dtype)


def _band_window(tq):
    return -(-(tq + N_PREV_CHUNKS * CHUNK) // LANES) * LANES


def band_bias_mask(rel_bias, tq):
    win = _band_window(tq)
    i = jnp.arange(tq)[:, None]
    j = jnp.arange(win)[None, :]
    rel = jnp.clip(i + N_PREV_CHUNKS * CHUNK - j, -REL_CLIP, REL_CLIP) + REL_CLIP
    ci, cj = i // CHUNK, j // CHUNK
    inside = (cj >= ci) & (cj <= ci + N_PREV_CHUNKS)
    return jnp.where(inside[None], _table_lookup(rel_bias, rel), NEG)


def band_attention(q, k_pad, v_pad, bias_mask, tq, kpos_base):
    b, skp, _ = k_pad.shape
    sq = q.shape[0] // b
    nq = sq // tq
    win = _band_window(tq)
    width = H_BAND * HEAD_DIM
    assert skp >= sq - tq + win
    return pl.pallas_call(
        functools.partial(_band_attn_kernel, tq=tq, win=win, kpos_base=kpos_base, scale=HEAD_DIM ** -0.5),
        grid=(b, nq),
        in_specs=[pl.BlockSpec((tq, width), lambda bi, qi: (bi * nq + qi, 0)),
                  pl.BlockSpec((None, skp, width), lambda bi, qi: (bi, 0, 0)),
                  pl.BlockSpec((None, skp, width), lambda bi, qi: (bi, 0, 0)),
                  pl.BlockSpec((H_BAND, tq, win), lambda bi, qi: (0, 0, 0))],
        out_specs=pl.BlockSpec((tq, width), lambda bi, qi: (bi * nq + qi, 0)),
        out_shape=jax.ShapeDtypeStruct((b * sq, width), BF),
        compiler_params=_cparams(2),
        name="band_attention",
    )(q, k_pad, v_pad, bias_mask)


def _sortable(x):
    i = lax.bitcast_convert_type(x, jnp.int32)
    return i ^ (lax.shift_right_arithmetic(i, 31) & 0x7FFFFFFF)


def _dsa_kernel(qi2t_ref, wt_ref, klo_ref, khi_ref, qt_ref, k_ref, vt_ref, bnear_ref, bfar_ref, tri_ref,
                o_ref, key_ref, m_ref, l_ref, acc_ref, eqc_ref,
                *, tq, tk, big, q_off, topk, scale, w_scale):
    wide = big * tk
    near_after = -(-tq // tk)
    q0 = q_off + pl.program_id(1) * tq
    qpos = q0 + lax.broadcasted_iota(jnp.int32, (1, tq), 1)
    limit = (lax.shift_right_arithmetic(qpos, CHUNK_SHIFT) + 1) * CHUNK
    last_limit = ((q0 + tq - 1) // CHUNK + 1) * CHUNK
    n_wide = (last_limit + wide - 1) // wide
    diag = q0 // tk

    w = wt_ref[...] * w_scale

    def score_body(c, carry):
        ks = pl.multiple_of(c * wide, wide)
        klo = klo_ref[pl.ds(ks, wide), :]
        khi = khi_ref[pl.ds(ks, wide), :]
        acc = jnp.zeros((wide, tq), F32)
        for pair in range(H_IDX // 2):
            q2 = qi2t_ref[pair]
            acc = acc + w[2 * pair:2 * pair + 1] * jnp.maximum(_dot(klo, q2), 0.0)
            acc = acc + w[2 * pair + 1:2 * pair + 2] * jnp.maximum(_dot(khi, q2), 0.0)
        kpos = ks + lax.broadcasted_iota(jnp.int32, (wide, tq), 0)
        keys = _sortable(jnp.where(kpos < limit, acc + 0.0, -jnp.inf))
        for blk in range(big):
            key_ref[c * big + blk] = keys[blk * tk:(blk + 1) * tk]
        return carry

    lax.fori_loop(0, n_wide, score_body, 0)

    def count_ge(cand):
        def body(c, cnt):
            for blk in range(big):
                cnt = cnt + jnp.where(key_ref[c * big + blk] >= cand, 1.0, 0.0)
            return cnt
        cnt = lax.fori_loop(0, n_wide, body, jnp.zeros((tk, tq), F32))
        return jnp.sum(cnt, axis=0, keepdims=True)

    bits_per_check = 4

    def bit_body(state):
        it, thr, at_thr = state
        for b in range(bits_per_check):
            cand = thr + lax.shift_left(jnp.int32(1), 31 - (it + b))
            cnt = count_ge(cand)
            take = cnt >= topk
            thr, at_thr = jnp.where(take, cand, thr), jnp.where(take, cnt, at_thr)
        return it + bits_per_check, thr, at_thr

    def bits_left(state):
        it, _, at_thr = state
        return (it < 32) & (jnp.max(at_thr) > topk)

    every_key = jnp.full((1, tq), 1.0, F32) * (n_wide * wide).astype(F32)
    _, thr, _ = lax.while_loop(bits_left, bit_body,
                               (jnp.int32(0), jnp.full((1, tq), INT_MIN, jnp.int32), every_key))
    n_above = count_ge(thr + 1)
    n_ties_kept = topk - n_above

    m_ref[...] = jnp.full(m_ref.shape, NEG, F32)
    l_ref[...] = jnp.zeros(l_ref.shape, F32)
    acc_ref[...] = jnp.zeros(acc_ref.shape, F32)
    eqc_ref[...] = jnp.zeros(eqc_ref.shape, F32)
    tri = tri_ref[...]
    hs = [slice(h * HEAD_DIM, (h + 1) * HEAD_DIM) for h in range(H_DSA)]

    def attend(first_blk, n_blk, bias_of_head, below_tile):
        width = n_blk * tk
        ks = pl.multiple_of(first_blk * tk, tk)
        key = jnp.concatenate([key_ref[first_blk + i] for i in range(n_blk)], axis=0)
        eq = key == thr
        eq_bf = jnp.where(eq, 1.0, 0.0).astype(BF)
        seen = eqc_ref[0:1, :]
        rank = [None] * n_blk
        for i in range(n_blk):
            counts = _dot(tri, eq_bf[i * tk:(i + 1) * tk])
            rank[i] = counts[0:tk] + seen
            seen = seen + counts[tk:tk + 1]
        eqc_ref[0:1, :] = seen
        sel = (key > thr) | (eq & (jnp.concatenate(rank, axis=0) <= n_ties_kept))
        if not below_tile:
            sel = sel & (ks + lax.broadcasted_iota(jnp.int32, (width, tq), 0) < limit)
        scores = [_dot(k_ref[pl.ds(ks, width), hs[h]], qt_ref[h]) for h in range(H_DSA)]
        probs, alphas = [], []
        for h in range(H_DSA):
            s = jnp.where(sel, scores[h] * scale + bias_of_head(h), NEG)
            m_prev = m_ref[h]
            m_new = jnp.maximum(m_prev, jnp.max(s, axis=0, keepdims=True))
            p = jnp.where(sel, jnp.exp(s - m_new), 0.0)
            alpha = jnp.exp(m_prev - m_new)
            l_ref[h] = alpha * l_ref[h] + jnp.sum(p, axis=0, keepdims=True)
            m_ref[h] = m_new
            probs.append(p.astype(BF))
            alphas.append(alpha)
        for h in range(H_DSA):
            vt = jnp.concatenate([vt_ref[h, first_blk + i] for i in range(n_blk)], axis=1)
            acc_ref[h] = alphas[h] * acc_ref[h] + _dot(vt, probs[h])

    far_bias = lambda h: bfar_ref[h][:, 0:1]
    n_far_wide = jnp.maximum(q0 - tk, 0) // wide

    def far_wide_body(c, carry):
        attend(c * big, big, far_bias, True)
        return carry

    lax.fori_loop(0, n_far_wide, far_wide_body, 0)

    def far_body(kc, carry):
        attend(kc, 1, far_bias, True)
        return carry

    lax.fori_loop(n_far_wide * big, jnp.maximum(diag - 1, n_far_wide * big), far_body, 0)

    @pl.when(diag >= 1)
    def _():
        attend(diag - 1, 1, lambda h: bnear_ref[0, h], True)

    for d in range(near_after):
        attend(diag + d, 1, lambda h, d=d: bnear_ref[d + 1, h], False)

    for h in range(H_DSA):
        o_ref[:, hs[h]] = (acc_ref[h] / l_ref[h]).T.astype(o_ref.dtype)


def t5_bucket(rel):
    half = T5_BUCKETS // 2
    max_exact = half // 2
    n = jnp.abs(rel)
    nf = jnp.maximum(n, 1).astype(F32)
    large = max_exact + (jnp.log(nf / max_exact) / math.log(T5_MAX_DIST / max_exact)
                         * (half - max_exact)).astype(jnp.int32)
    large = jnp.minimum(large, half - 1)
    return jnp.where(rel > 0, half, 0) + jnp.where(n < max_exact, n, large)


def _table_lookup(table, idx):
    onehot = jax.nn.one_hot(idx, table.shape[0], dtype=F32)
    out = jnp.einsum("...n,nh->...h", onehot, table.astype(F32), precision=lax.Precision.HIGHEST)
    return jnp.moveaxis(out, -1, 0)


def dsa_bias_tables(t5_table, tq, tk=LANES):
    j = jnp.arange(tk)[:, None]
    i = jnp.arange(tq)[None, :]
    near = jnp.stack([_table_lookup(t5_table, t5_bucket(d * tk + j - i)) for d in range(-1, -(-tq // tk))],
                     axis=0)
    far_rel = -jnp.ones((1, tk), jnp.int32) * (2 * tk)
    far = _table_lookup(t5_table, t5_bucket(far_rel))
    return near, far


def dsa_attention(qi2t, wt, klo, khi, qt, k, vt, bias_tables, q_off, n_keys, tq, big, tk=LANES):
    b, _, _, sq = qt.shape
    sk = k.shape[0] // b
    nq = sq // tq
    assert tk >= T5_MAX_DIST and q_off % tk == 0 and (tq % tk == 0 or sq == tq) and sk % (big * tk) == 0
    topk = min(TOPK_MAX, n_keys // 4)
    near, far = bias_tables
    ss = lax.broadcasted_iota(jnp.int32, (tk + 16, tk), 0)
    jj = lax.broadcasted_iota(jnp.int32, (tk + 16, tk), 1)
    tri = ((jj <= ss) | (ss >= tk)).astype(BF)
    whole = lambda n: pl.BlockSpec((None, sk, n), lambda bi, qi: (bi, 0, 0))
    heads_t = lambda h, d: pl.BlockSpec((None, h, d, tq), lambda bi, qi: (bi, 0, 0, qi))
    const = lambda a: pl.BlockSpec(a.shape, lambda bi, qi: (0,) * a.ndim)
    return pl.pallas_call(
        functools.partial(_dsa_kernel, tq=tq, tk=tk, big=big, q_off=q_off, topk=topk,
                          scale=HEAD_DIM ** -0.5, w_scale=H_IDX ** -0.5 * D_IDX ** -0.5),
        grid=(b, nq),
        in_specs=[heads_t(H_IDX // 2, LANES), pl.BlockSpec((None, H_IDX, tq), lambda bi, qi: (bi, 0, qi)),
                  whole(LANES), whole(LANES), heads_t(H_DSA, HEAD_DIM),
                  pl.BlockSpec((sk, H_DSA * HEAD_DIM), lambda bi, qi: (bi, 0)),
                  pl.BlockSpec((None, H_DSA, sk // tk, HEAD_DIM, tk), lambda bi, qi: (bi, 0, 0, 0, 0)),
                  const(near), const(far), const(tri)],
        out_specs=pl.BlockSpec((tq, H_DSA * HEAD_DIM), lambda bi, qi: (bi * nq + qi, 0)),
        out_shape=jax.ShapeDtypeStruct((b * sq, H_DSA * HEAD_DIM), BF),
        scratch_shapes=[pltpu.VMEM((sk // tk, tk, tq), jnp.int32),
                        pltpu.VMEM((H_DSA, 1, tq), F32), pltpu.VMEM((H_DSA, 1, tq), F32),
                        pltpu.VMEM((H_DSA, HEAD_DIM, tq), F32), pltpu.VMEM((8, tq), F32)],
        compiler_params=_cparams(2),
        name="dsa_attention",
    )(qi2t, wt, klo, khi, qt, k, vt, near, far, tri)


def _pad_cols(a, n):
    return jnp.pad(a, ((0, 0), (0, n - a.shape[1])))


def _rotate_half_cols(w):
    half = w.shape[1] // 2
    return jnp.concatenate([-w[:, half:], w[:, :half]], axis=1)


def prepare_layer_weights(w_in, q_norm, w_uq, kv_norm, w_ukv, w_branch):
    d = w_in.shape[0]
    q_lora, kv_lora = q_norm.shape[0], kv_norm.shape[0]
    sizes = (q_lora, kv_lora, ROPE_DIM, 3 * H_SB * HEAD_DIM, 3 * H_BAND * HEAD_DIM, 3 * H_DSA * HEAD_DIM,
             H_IDX * D_IDX, D_IDX, H_IDX, N_BRANCH * d)
    cols, start = [], 0
    for s in sizes:
        cols.append(w_in[:, start:start + s])
        start += s
    w_cq, w_ckv, w_kr, w_sb, w_bd, w_ds, w_iq, w_ik, w_iw, w_g = cols
    wm = jnp.concatenate([w_cq, w_ckv, _pad_cols(w_kr, LANES), _pad_cols(_rotate_half_cols(w_kr), LANES),
                          _pad_cols(w_ik, LANES), _pad_cols(w_iw, LANES)], axis=1).astype(BF)
    qa, qb = [], []
    hd = NOPE_DIM + ROPE_DIM
    for h in range(H_MLA):
        wh = w_uq[:, h * hd:(h + 1) * hd]
        qa += [wh[:, :NOPE_DIM], _pad_cols(wh[:, NOPE_DIM:], LANES)]
        qb.append(_pad_cols(_rotate_half_cols(wh[:, NOPE_DIM:]), LANES))
    wqa = jnp.concatenate(qa, axis=1).astype(BF)
    wqb = jnp.concatenate(qb, axis=1).astype(BF)
    kvd = NOPE_DIM + V_DIM
    wk = jnp.concatenate([w_ukv[:, h * kvd:h * kvd + NOPE_DIM] for h in range(H_MLA)], axis=1).astype(BF)
    wv = jnp.concatenate([w_ukv[:, h * kvd + NOPE_DIM:(h + 1) * kvd] for h in range(H_MLA)], axis=1).astype(BF)
    wg = jnp.transpose(w_g.reshape(d, N_BRANCH, d), (1, 0, 2)).astype(BF)
    return dict(wm=wm, wqa=wqa, wqb=wqb, wk=wk, wv=wv,
                w_sb=w_sb.astype(BF), w_bd=w_bd.astype(BF), w_ds=w_ds.astype(BF), w_iq=w_iq.astype(BF),
                wg=wg, wb=w_branch.astype(BF),
                qn=q_norm.reshape(1, -1), kvn=kv_norm.reshape(1, -1))


def rope_table(pos):
    half = ROPE_DIM // 2
    inv = ROPE_THETA ** (-jnp.arange(half, dtype=F32) / half)
    ang = pos.astype(F32)[:, None] * inv[None, :]
    cos, sin = jnp.cos(ang), jnp.sin(ang)
    z = jnp.zeros((pos.shape[0], LANES - ROPE_DIM), F32)
    return jnp.concatenate([cos, cos, z, sin, sin, z], axis=1)


def _kidx_pair(kidx):
    kb = kidx.astype(BF)
    return (jnp.pad(kb, ((0, 0), (0, 0), (0, LANES - D_IDX))),
            jnp.pad(kb, ((0, 0), (0, 0), (LANES - D_IDX, 0))))


def _with_past(past, new, pad_to):
    b = new.shape[0]
    a = jnp.concatenate([past.reshape(b, past.shape[1], -1).astype(BF), new.astype(BF)], axis=1)
    return jnp.pad(a, ((0, 0), (0, pad_to - a.shape[1]), (0, 0)))


def mixing_block(u, bsz, seq, lw, cs, past, band_mask, dsa_tables, w_out, h, stacks, g_ffn):
    ckv, krp, kidxp, idxw, qcat = proj_misc(u, bsz, seq, lw["wm"], lw["qn"], lw["kvn"], lw["wqa"], lw["wqb"], cs)
    hw = H_SB * HEAD_DIM

    layer, depth, prev = stacks
    prev = prev if prev is not None else (None,) * 4

    def stacked(w, name, prev_stack, other=("rows", BF)):
        return project(u, w, bsz, seq, [("stack_heads", F32), other], stack=(layer, depth, prev_stack), name=name)

    w_sb, w_bd = lw["w_sb"], lw["w_bd"]
    sb_q, = matmul(u, w_sb[:, :hw], (BF,), tm=1024, name="proj_q")
    sb_k, sb_kb = stacked(w_sb[:, hw:2 * hw], "proj_k", prev[0])
    sb_v, sb_vb = stacked(w_sb[:, 2 * hw:], "proj_v", prev[1])
    bd_q, = matmul(u, w_bd[:, :hw], (BF,), tm=1024, name="proj_q")
    bd_k, bd_kb = matmul(u, w_bd[:, hw:2 * hw], (F32, BF), tm=1024, name="proj_k")
    bd_v, bd_vb = matmul(u, w_bd[:, 2 * hw:], (F32, BF), tm=1024, name="proj_v")
    kr = krp[:, :ROPE_DIM]
    kidx = kidxp[:, :D_IDX]
    r3 = lambda a: a.reshape(bsz, seq, -1)
    band_pad = N_PREV_CHUNKS * CHUNK

    if past is None:
        w_ds = lw["w_ds"]
        ds_qt, = project(u, w_ds[:, :hw], bsz, seq, [("heads_t", BF)], name="proj_q")
        ds_k, ds_kb = stacked(w_ds[:, hw:2 * hw], "proj_k", prev[2])
        ds_v, ds_vt = stacked(w_ds[:, 2 * hw:], "proj_v", prev[3], ("blocks_t", BF))
        idx_qt, = project(u, lw["w_iq"], bsz, seq, [("heads_t", BF)], name="proj_idxq")
        kcat, vmla_t = mla_kv_up(ckv, krp, lw["wk"], lw["wv"], bsz, seq, PROMPT_TK)
        o_mla = mla_attention(qcat, kcat, vmla_t, 0, PROMPT_TQ, PROMPT_TK)
        o_sb = sb_attention(sb_q, sb_kb, sb_vb, bsz, 0, PROMPT_TQ, PROMPT_TK)
        front = ((0, 0), (band_pad, 0), (0, 0))
        o_bd = band_attention(bd_q, jnp.pad(r3(bd_kb), front), jnp.pad(r3(bd_vb), front),
                              band_mask, PROMPT_TQ, -band_pad)
        klo, khi = _kidx_pair(r3(kidx))
        o_ds = dsa_attention(idx_qt, idxw, klo, khi, ds_qt, ds_kb, ds_vt, dsa_tables,
                             0, seq, PROMPT_TQ, PROMPT_TK // LANES)
    else:
        w_ds = lw["w_ds"]
        ds_q, = matmul(u, w_ds[:, :hw], (BF,), tm=1024, name="proj_q")
        ds_k, ds_kb = stacked(w_ds[:, hw:2 * hw], "proj_k", prev[2])
        ds_v, ds_vb = stacked(w_ds[:, 2 * hw:], "proj_v", prev[3])
        idx_q, = matmul(u, lw["w_iq"], (BF,), tm=1024, name="proj_idxq")
        (p_ckv, p_kr, p_sbk, p_sbv, p_bdk, p_bdv, p_dsk, p_dsv, p_kidx) = past
        past_len = p_sbk.shape[1]
        total = past_len + seq
        pad_to = -(-total // SAMPLE_TK) * SAMPLE_TK
        ckv_all = jnp.concatenate([p_ckv, r3(ckv)], axis=1)
        krp_all = jnp.concatenate([jnp.pad(p_kr, ((0, 0), (0, 0), (0, LANES - ROPE_DIM))), r3(krp)], axis=1)
        rows = pad_to - total
        ckv_all = jnp.pad(ckv_all, ((0, 0), (0, rows), (0, 0))).reshape(bsz * pad_to, -1)
        krp_all = jnp.pad(krp_all, ((0, 0), (0, rows), (0, 0))).reshape(bsz * pad_to, -1)
        kcat, vmla_t = mla_kv_up(ckv_all, krp_all, lw["wk"], lw["wv"], bsz, pad_to, SAMPLE_TK)
        rows2 = lambda a: a.reshape(bsz * a.shape[1], -1)
        o_mla = mla_attention(_heads_to_lanes(r3(qcat), H_MLA), kcat, vmla_t, past_len, seq, SAMPLE_TK)
        o_sb = sb_attention(sb_q, rows2(_with_past(p_sbk, r3(sb_kb), pad_to)),
                            rows2(_with_past(p_sbv, r3(sb_vb), pad_to)), bsz, past_len, seq, SAMPLE_TK)
        band_len = _band_window(seq)
        o_bd = band_attention(bd_q, _with_past(p_bdk, r3(bd_kb), band_len), _with_past(p_bdv, r3(bd_vb), band_len),
                              band_mask, seq, past_len - p_bdk.shape[1])
        kidx_all = jnp.pad(jnp.concatenate([p_kidx, r3(kidx)], axis=1), ((0, 0), (0, rows), (0, 0)))
        klo, khi = _kidx_pair(kidx_all)
        o_ds = dsa_attention(_heads_to_lanes(r3(idx_q), H_IDX // 2),
                             jnp.transpose(r3(idxw)[:, :, :H_IDX], (0, 2, 1)), klo, khi,
                             _heads_to_lanes(r3(ds_q), H_DSA), rows2(_with_past(p_dsk, r3(ds_kb), pad_to)),
                             _chunked_transpose(_with_past(p_dsv, r3(ds_vb), pad_to), H_DSA, LANES),
                             dsa_tables, past_len, total, seq, SAMPLE_TK // LANES)

    merged = gate_merge(u, [o_mla, o_sb, o_bd, o_ds], lw["wg"], lw["wb"])
    h_new, un = matmul_norm(merged, w_out, h, g_ffn, name="mix_out")
    return h_new, un, (ckv, kr, bd_k, bd_v, kidx), (sb_k, sb_v, ds_k, ds_v)


def dense_ffn(h, un, w1, w3, w2):
    act = swiglu_up(un, w1, w3)
    out, = matmul(act, w2, (F32,), residual=h, name="ffn_down")
    return out


def moe_ffn(h, un, router_pad, w1, w3, w2):
    n_experts = w1.shape[0]
    gate = router_gate(un, router_pad, n_experts)
    for e in range(n_experts):
        act = swiglu_up(un, w1[e], w3[e])
        h, = matmul(act, w2[e], (F32,), residual=h, gate=gate, gate_col=e, name="moe_down")
    return h


def layer_step(h, u, p, bsz, seq, cs, past, lw, band_mask, dsa_tables, fw, stacks, g_next, last):
    h, un, rows, head_stacks = mixing_block(u, bsz, seq, lw, cs, past, band_mask, dsa_tables, fw["w_out"], h,
                                            stacks, fw["g_ffn"])
    if fw["moe"]:
        h = moe_ffn(h, un, fw["router"], fw["w1"], fw["w3"], fw["w2"])
    else:
        h = dense_ffn(h, un, fw["w1"], fw["w3"], fw["w2"])
    un = rmsnorm(h, fw["g_ple"], BF)
    outs = ple_update(h, p, un, fw["ple_w"], fw["ple_gate_w"], g_next, F32 if last else BF, keep_h=not last)
    return (None if last else outs[0]), outs[-1], rows, head_stacks


def kernel(x_prompt, x_sample, p_prompt, p_sample, cache_mla_ckv, cache_mla_krope, cache_sb_k, cache_sb_v, cache_band_k, cache_band_v, cache_dsa_k, cache_dsa_v, cache_dsa_kidx, norm_mix, w_in, mla_q_norm, mla_w_uq, mla_kv_norm, mla_w_ukv, band_rel_bias, t5_rel_bias, w_branch, w_out, norm_ffn, ffn_w1, ffn_w3, ffn_w2, moe_router, moe_w1, moe_w3, moe_w2, norm_ple, ple_w, ple_gate_w, norm_final):
    depth = w_in.shape[0]
    bp, sp, d = x_prompt.shape
    bs, ss, _ = x_sample.shape
    past_len = cache_sb_k.shape[2]
    tm = 512
    cs_p = rope_table(jnp.arange(sp))
    cs_s = rope_table(past_len + (jnp.arange(tm) % ss))
    dsa_tables_p = dsa_bias_tables(t5_rel_bias, PROMPT_TQ)
    dsa_tables_s = dsa_bias_tables(t5_rel_bias, ss)
    hp = x_prompt.reshape(bp * sp, d)
    hs = x_sample.reshape(bs * ss, d)
    up = rmsnorm(hp, norm_mix[0], BF)
    us = rmsnorm(hs, norm_mix[0], BF)
    rows_p, rows_s = [], []
    stacks_p = stacks_s = None
    for i in range(depth):
        lw = prepare_layer_weights(w_in[i], mla_q_norm[i], mla_w_uq[i], mla_kv_norm[i], mla_w_ukv[i], w_branch[i])
        j = i // 2
        fw = dict(g_ffn=norm_ffn[i], g_ple=norm_ple[i], w_out=w_out[i].astype(BF),
                  ple_w=ple_w[i].astype(BF), ple_gate_w=ple_gate_w[i].astype(BF), moe=i % 2 == 1)
        if i % 2 == 0:
            fw.update(w1=ffn_w1[j].astype(BF), w3=ffn_w3[j].astype(BF), w2=ffn_w2[j].astype(BF))
        else:
            fw.update(router=_pad_cols(moe_router[j], LANES).astype(BF),
                      w1=moe_w1[j].astype(BF), w3=moe_w3[j].astype(BF), w2=moe_w2[j].astype(BF))
        past_i = (cache_mla_ckv[i], cache_mla_krope[i], cache_sb_k[i], cache_sb_v[i], cache_band_k[i],
                  cache_band_v[i], cache_dsa_k[i], cache_dsa_v[i], cache_dsa_kidx[i])
        last = i == depth - 1
        g_next = norm_final if last else norm_mix[i + 1]
        hp, up, rp, stacks_p = layer_step(hp, up, p_prompt[i].reshape(bp * sp, -1).astype(BF), bp, sp, cs_p, None,
                                          lw, band_bias_mask(band_rel_bias[i], PROMPT_TQ), dsa_tables_p, fw,
                                          (i, depth, stacks_p), g_next, last)
        hs, us, rs, stacks_s = layer_step(hs, us, p_sample[i].reshape(bs * ss, -1).astype(BF), bs, ss, cs_s, past_i,
                                          lw, band_bias_mask(band_rel_bias[i], ss), dsa_tables_s, fw,
                                          (i, depth, stacks_s), g_next, last)
        rows_p.append(rp)
        rows_s.append(rs)
    y_prompt = up.reshape(bp, sp, d)
    y_sample = us.reshape(bs, ss, d)

    keep = min(N_PREV_CHUNKS * CHUNK, sp)

    def stacked(rows, n, bsz, seq, heads=None, tail=None):
        out = []
        for r in rows:
            a = r[n].reshape(bsz, seq, -1)
            if tail is not None:
                a = a[:, seq - tail:]
            if heads is not None:
                a = a.reshape(a.shape[0], a.shape[1], heads, HEAD_DIM)
            out.append(a)
        return jnp.stack(out, axis=0)

    def both(n, heads=None, prompt_tail=None):
        return [stacked(rows_p, n, bp, sp, heads, prompt_tail), stacked(rows_s, n, bs, ss, heads)]

    def from_stacks(n):
        return [stacks_p[n].reshape(depth, bp, sp, -1, HEAD_DIM), stacks_s[n].reshape(depth, bs, ss, -1, HEAD_DIM)]

    res = [y_prompt, y_sample] + both(0) + both(1) + from_stacks(0) + from_stacks(1)
    res += both(2, H_BAND, keep) + both(3, H_BAND, keep) + from_stacks(2) + from_stacks(3) + both(4)
    return tuple(res)
```
